```python
import math
import jax
import jax.numpy as jnp
from jax import lax
import numpy as np

D_MODEL = 2048
BATCH = 4
SEQ = 2048
DEPTH = 2

GRID_W = 64
CTX_LEN = 256
EPS = 1e-6
F32 = jnp.float32

ML_HEADS = 4
ML_HD = 128
ML_W = ML_HEADS * ML_HD
ML_CHUNK = 64

SSD_HEADS = 8
SSD_HD = 64
SSD_W = SSD_HEADS * SSD_HD
SSD_GROUPS = 2
SSD_STATE = 128
SSD_CONV = 5
SSD_CONV_CH = SSD_W + 2 * SSD_GROUPS * SSD_STATE
SSD_CHUNK = 64

MLA_HEADS = 4
MLA_NOPE = 128
MLA_ROPE = 64
MLA_QK = MLA_NOPE + MLA_ROPE
MLA_V = 128
MLA_Q_RANK = 448
MLA_KV_RANK = 160
MLA_W = MLA_HEADS * MLA_V
ROPE_THETA = 10000.0
ATT_QBLOCK = 128

NA_HEADS = 4
NA_HD = 128
NA_W = NA_HEADS * NA_HD
NA_KH = 8
NA_KW = 16

MIX_W = ML_W + SSD_W + MLA_W + NA_W

IN_SPLITS = (ML_W, ML_W, ML_W, ML_W, 2 * ML_HEADS, 2 * ML_HEADS,
             SSD_W, SSD_CONV_CH, 2 * SSD_HEADS,
             MLA_Q_RANK, MLA_KV_RANK, MLA_ROPE,
             NA_W, NA_W, NA_W)
D_IN = sum(IN_SPLITS)

FFN_DENSE = 5632
N_EXPERTS = 8
TOP_K = 2
FFN_EXPERT = 7168
MOE_BLOCK = 128

kernel_name = 'hybrid_dit_mlstm_ssd_mla_natten_moe'


def rms_norm(x, g):
    xf = x.astype(F32)
    y = xf * lax.rsqrt(jnp.mean(xf * xf, axis=-1, keepdims=True) + EPS)
    return (y * g.astype(F32)).astype(x.dtype)


def split_cols(p):
    offs = np.cumsum(IN_SPLITS)[:-1].tolist()
    return jnp.split(p, offs, axis=-1)


def axial_rope(n_tok, rot_dim):
    t = jnp.arange(n_tok)
    row = (t // GRID_W).astype(F32)
    col = (t % GRID_W).astype(F32)
    n_freq = rot_dim // 4
    freqs = ROPE_THETA ** (-jnp.arange(n_freq, dtype=F32) / n_freq)
    ang = jnp.concatenate([row[:, None] * freqs, col[:, None] * freqs], axis=-1)
    return jnp.cos(ang), jnp.sin(ang)


def apply_rope(u, cos, sin):
    uf = u.astype(F32)
    half = u.shape[-1] // 2
    u1, u2 = uf[..., :half], uf[..., half:]
    cs, sn = cos[:, None, :], sin[:, None, :]
    return jnp.concatenate([u1 * cs - u2 * sn, u1 * sn + u2 * cs], axis=-1).astype(u.dtype)


def dense_attention(q, k, v, scale):
    s = jnp.einsum('bhqd,bhkd->bhqk', q, k).astype(F32) * scale
    p = jax.nn.softmax(s, axis=-1).astype(v.dtype)
    return jnp.einsum('bhqk,bhkd->bhqd', p, v)


def blocked_attention(q, k, v, scale):
    B, H, T, dk = q.shape
    nb = T // ATT_QBLOCK
    qb = jnp.moveaxis(q.reshape(B, H, nb, ATT_QBLOCK, dk), 2, 0)
    ob = lax.map(lambda qq: dense_attention(qq, k, v, scale), qb)
    return jnp.moveaxis(ob, 0, 2).reshape(B, H, T, v.shape[-1])


def mlstm_chunked(q, k, v, log_i, log_f, state):
    B, H, T, d = q.shape
    L = ML_CHUNK
    nc = T // L

    def chunks(u):
        return jnp.moveaxis(u.reshape(B, H, nc, L, *u.shape[3:]), 2, 0)

    causal = jnp.tril(jnp.ones((L, L), bool))

    def step(carry, inp):
        C, n, m = carry
        qc, kc, vc, ic, fc = inp
        b = jnp.cumsum(fc, axis=-1)
        dmat = jnp.where(causal, b[..., :, None] - b[..., None, :] + ic[..., None, :], -jnp.inf)
        inter = b + m[..., None]
        mt = jnp.maximum(inter, jnp.max(dmat, axis=-1))
        w_intra = jnp.exp(dmat - mt[..., None])
        w_state = jnp.exp(inter - mt)
        s = jnp.einsum('bhtd,bhsd->bhts', qc, kc) * w_intra
        num = jnp.einsum('bhts,bhsd->bhtd', s, vc) + w_state[..., None] * jnp.einsum('bhde,bhte->bhtd', C, qc)
        den = jnp.sum(s, axis=-1) + w_state * jnp.einsum('bhd,bhtd->bht', n, qc)
        h = num / jnp.maximum(jnp.abs(den), jnp.exp(-mt))[..., None]
        bl = b[..., -1]
        g = bl[..., None] - b + ic
        m_new = jnp.maximum(bl + m, jnp.max(g, axis=-1))
        wg = jnp.exp(g - m_new[..., None])
        wc = jnp.exp(bl + m - m_new)
        C_new = wc[..., None, None] * C + jnp.einsum('bhs,bhsd,bhse->bhde', wg, vc, kc)
        n_new = wc[..., None] * n + jnp.einsum('bhs,bhsd->bhd', wg, kc)
        return (C_new, n_new, m_new), h

    state, hs = lax.scan(step, state, (chunks(q), chunks(k), chunks(v), chunks(log_i), chunks(log_f)))
    return state, jnp.moveaxis(hs, 0, 2).reshape(B, H, T, d)


def mlstm_mixer(lat, ctx, i_bias, f_bias, norm_g, with_ctx_out):
    def prep(parts):
        q, k, v, o, ig, fg = parts
        B, T, _ = q.shape
        heads = lambda u: u.astype(F32).reshape(B, T, ML_HEADS, ML_HD).transpose(0, 2, 1, 3)
        gate = lambda u, bias: jnp.transpose(u.astype(F32).reshape(B, T, 2, ML_HEADS) + bias.astype(F32), (2, 0, 3, 1))
        return (heads(q) * ML_HD ** -0.5, heads(k), heads(v), o,
                gate(ig, i_bias), jax.nn.log_sigmoid(gate(fg, f_bias)))

    ql, kl, vl, ol, il, fl = prep(lat)
    qc, kc, vc, oc, ic, fc = prep(ctx)
    B = ql.shape[0]
    zero = (jnp.zeros((B, ML_HEADS, ML_HD, ML_HD), F32), jnp.zeros((B, ML_HEADS, ML_HD), F32),
            jnp.zeros((B, ML_HEADS), F32))
    rev = lambda u: jnp.flip(u, axis=2)
    st_f, hc_f = mlstm_chunked(qc, kc, vc, ic[0], fc[0], zero)
    _, hl_f = mlstm_chunked(ql, kl, vl, il[0], fl[0], st_f)
    st_b, hc_b = mlstm_chunked(rev(qc), rev(kc), rev(vc), rev(ic[1]), rev(fc[1]), zero)
    _, hl_b = mlstm_chunked(rev(ql), rev(kl), rev(vl), rev(il[1]), rev(fl[1]), st_b)

    def finish(h, o):
        B_, H_, T_, d_ = h.shape
        h = rms_norm(h.transpose(0, 2, 1, 3), norm_g.reshape(ML_HEADS, ML_HD))
        out = h * jax.nn.sigmoid(o.astype(F32)).reshape(B_, T_, ML_HEADS, ML_HD)
        return out.reshape(B_, T_, ML_W).astype(o.dtype)

    out_l = finish(hl_f + rev(hl_b), ol)
    out_c = finish(hc_f + rev(hc_b), oc) if with_ctx_out else None
    return out_l, out_c


def ssd_chunked(x, dt, A, Bm, Cm, state):
    B, T, H, P = x.shape
    L = SSD_CHUNK
    nc = T // L

    def chunks(u):
        return jnp.moveaxis(u.reshape(B, nc, L, *u.shape[2:]), 1, 0)

    causal = jnp.tril(jnp.ones((L, L), bool))[None, :, :, None]

    def step(h, inp):
        xc, dtc, bc, cc = inp
        cum = jnp.cumsum(dtc * A, axis=1)
        decay = jnp.exp(jnp.where(causal, cum[:, :, None, :] - cum[:, None, :, :], -jnp.inf))
        w = jnp.einsum('bthn,bshn->btsh', cc, bc) * decay * dtc[:, None, :, :]
        y = (jnp.einsum('btsh,bshp->bthp', w, xc)
             + jnp.einsum('bthn,bhpn->bthp', cc, h) * jnp.exp(cum)[..., None])
        tail = jnp.exp(cum[:, -1:, :] - cum) * dtc
        h_new = h * jnp.exp(cum[:, -1, :])[..., None, None] + jnp.einsum('bsh,bshp,bshn->bhpn', tail, xc, bc)
        return h_new, y

    state, ys = lax.scan(step, state, (chunks(x), chunks(dt), chunks(Bm), chunks(Cm)))
    return state, jnp.moveaxis(ys, 0, 1).reshape(B, T, H, P)


def dwconv_centered(u, w, b):
    K = w.shape[0]
    out = lax.conv_general_dilated(u, w[:, None, :].astype(u.dtype), (1,), [(K // 2, K // 2)],
                                   dimension_numbers=('NWC', 'WIO', 'NWC'), feature_group_count=u.shape[-1])
    return out + b.astype(u.dtype)


def ssd_mixer(lat, ctx, conv_w, conv_b, dt_bias, A_log, D_skip, norm_g, with_ctx_out):
    A = -jnp.exp(A_log.astype(F32))
    rep = SSD_HEADS // SSD_GROUPS

    def prep(parts):
        z, xbc, dtr = parts
        B, T, _ = z.shape
        xbc = jax.nn.silu(dwconv_centered(xbc, conv_w, conv_b)).astype(F32)
        xs, bm, cm = jnp.split(xbc, [SSD_W, SSD_W + SSD_GROUPS * SSD_STATE], axis=-1)
        grp = lambda u: jnp.repeat(u.reshape(B, T, SSD_GROUPS, SSD_STATE), rep, axis=2)
        dt = jax.nn.softplus(dtr.astype(F32).reshape(B, T, 2, SSD_HEADS) + dt_bias.astype(F32))
        return z, xs.reshape(B, T, SSD_HEADS, SSD_HD), grp(bm), grp(cm), dt

    zl, xl, bl, cl, dtl = prep(lat)
    zc, xc, bc, cc, dtc = prep(ctx)
    B = xl.shape[0]
    zero = jnp.zeros((B, SSD_HEADS, SSD_HD, SSD_STATE), F32)
    rev = lambda u: jnp.flip(u, axis=1)
    st_f, yc_f = ssd_chunked(xc, dtc[:, :, 0], A[0], bc, cc, zero)
    _, yl_f = ssd_chunked(xl, dtl[:, :, 0], A[0], bl, cl, st_f)
    st_b, yc_b = ssd_chunked(rev(xc), rev(dtc[:, :, 1]), A[1], rev(bc), rev(cc), zero)
    _, yl_b = ssd_chunked(rev(xl), rev(dtl[:, :, 1]), A[1], rev(bl), rev(cl), st_b)

    def finish(yf, yb, xs, z):
        B_, T_ = z.shape[:2]
        y = yf + rev(yb) + D_skip.astype(F32)[:, None] * xs
        y = y.reshape(B_, T_, SSD_W) * jax.nn.silu(z.astype(F32))
        return rms_norm(y, norm_g).astype(z.dtype)

    out_l = finish(yl_f, yl_b, xl, zl)
    out_c = finish(yc_f, yc_b, xc, zc) if with_ctx_out else None
    return out_l, out_c


def mla_mixer(lat, ctx, q_norm, w_qb, kv_norm, w_kvb, gq, gk, rope, with_ctx_out):
    def prep(parts, rot):
        cq, ckv, kr = parts
        B, T, _ = cq.shape
        q = jnp.dot(rms_norm(cq, q_norm), w_qb).reshape(B, T, MLA_HEADS, MLA_QK)
        kv = jnp.dot(rms_norm(ckv, kv_norm), w_kvb).reshape(B, T, MLA_HEADS, MLA_NOPE + MLA_V)
        k_nope, v = jnp.split(kv, [MLA_NOPE], axis=-1)
        k = jnp.concatenate([k_nope, jnp.broadcast_to(kr[:, :, None, :], (B, T, MLA_HEADS, MLA_ROPE))], axis=-1)
        q = rms_norm(q, gq)
        k = rms_norm(k, gk)
        if rot is not None:
            cos, sin = rot
            q = jnp.concatenate([q[..., :MLA_NOPE], apply_rope(q[..., MLA_NOPE:], cos, sin)], axis=-1)
            k = jnp.concatenate([k[..., :MLA_NOPE], apply_rope(k[..., MLA_NOPE:], cos, sin)], axis=-1)
        t = lambda u: u.transpose(0, 2, 1, 3)
        return t(q), t(k), t(v)

    ql, kl, vl = prep(lat, rope)
    qc, kc, vc = prep(ctx, None)
    scale = MLA_QK ** -0.5
    k_all = jnp.concatenate([kl, kc], axis=2)
    v_all = jnp.concatenate([vl, vc], axis=2)
    merge = lambda o: o.transpose(0, 2, 1, 3).reshape(o.shape[0], o.shape[2], MLA_W)
    out_l = merge(blocked_attention(ql, k_all, v_all, scale))
    out_c = merge(dense_attention(qc, kc, vc, scale)) if with_ctx_out else None
    return out_l, out_c


def na_indices(rows, kh, kw):
    r = jnp.arange(rows)
    cidx = jnp.arange(GRID_W)
    r0 = jnp.clip(r - kh // 2, 0, rows - kh)
    c0 = jnp.clip(cidx - kw // 2, 0, GRID_W - kw)
    key_r = r0[:, None] + jnp.arange(kh)
    key_c = c0[:, None] + jnp.arange(kw)
    idx = key_r[:, None, :, None] * GRID_W + key_c[None, :, None, :]
    dr = key_r - r[:, None] + NA_KH - 1
    dc = key_c - cidx[:, None] + NA_KW - 1
    return idx.reshape(rows, GRID_W, kh * kw), dr, dc


def na_mixer(lat, ctx, gq, gk, rpb, with_ctx_out):
    def prep(parts):
        q, k, v = parts
        B, T, _ = q.shape
        heads = lambda u: u.reshape(B, T, NA_HEADS, NA_HD)
        t = lambda u: u.transpose(0, 2, 1, 3)
        return t(rms_norm(heads(q), gq)), t(rms_norm(heads(k), gk)), t(heads(v))

    ql, kl, vl = prep(lat)
    qc, kc, vc = prep(ctx)
    B, H, T, d = ql.shape
    rows = T // GRID_W
    kh = min(NA_KH, rows)
    nk = kh * NA_KW
    idx, dr, dc = na_indices(rows, kh, NA_KW)
    bias = rpb[:, dr[:, None, :, None], dc[None, :, None, :]]
    bias = bias.reshape(H, rows, GRID_W, nk).transpose(1, 0, 2, 3).astype(F32)
    scale = NA_HD ** -0.5
    q_rows = jnp.moveaxis(ql.reshape(B, H, rows, GRID_W, d), 2, 0)

    def row_step(args):
        q_row, idx_row, b_row = args
        k_nb = kl[:, :, idx_row]
        v_nb = vl[:, :, idx_row]
        s_nb = jnp.einsum('bhwd,bhwkd->bhwk', q_row, k_nb).astype(F32) * scale + b_row
        s_cx = jnp.einsum('bhwd,bhcd->bhwc', q_row, kc).astype(F32) * scale
        p = jax.nn.softmax(jnp.concatenate([s_nb, s_cx], axis=-1), axis=-1).astype(v_nb.dtype)
        return (jnp.einsum('bhwk,bhwkd->bhwd', p[..., :nk], v_nb)
                + jnp.einsum('bhwc,bhcd->bhwd', p[..., nk:], vc))

    o_rows = lax.map(row_step, (q_rows, idx, bias))
    merge = lambda o: o.transpose(0, 2, 1, 3).reshape(o.shape[0], o.shape[2], NA_W)
    out_l = merge(jnp.moveaxis(o_rows, 0, 2).reshape(B, H, T, d))
    out_c = merge(dense_attention(qc, kc, vc, scale)) if with_ctx_out else None
    return out_l, out_c


def swiglu(h, w1, w3, w2):
    return jnp.dot(jax.nn.silu(jnp.dot(h, w1)) * jnp.dot(h, w3), w2)


def moe_swiglu(h, router, w1, w3, w2):
    n_tok, d = h.shape
    n_exp = w1.shape[0]
    logits = jnp.dot(h, router).astype(F32)
    top_v, top_e = lax.top_k(logits, TOP_K)
    gates = jax.nn.softmax(top_v, axis=-1)
    flat_e = top_e.reshape(-1)
    flat_t = jnp.repeat(jnp.arange(n_tok, dtype=jnp.int32), TOP_K)
    flat_g = gates.reshape(-1)
    order = jnp.argsort(flat_e)
    se = flat_e[order]
    counts = jnp.bincount(flat_e, length=n_exp)
    padded = (counts + MOE_BLOCK - 1) // MOE_BLOCK * MOE_BLOCK
    pend = jnp.cumsum(padded)
    pstart = pend - padded
    sstart = jnp.cumsum(counts) - counts
    slot = pstart[se] + jnp.arange(flat_e.shape[0]) - sstart[se]
    n_blk = -(-flat_e.shape[0] // MOE_BLOCK) + n_exp
    n_slot = n_blk * MOE_BLOCK
    slot_tok = jnp.full((n_slot,), n_tok, jnp.int32).at[slot].set(flat_t[order])
    slot_gate = jnp.zeros((n_slot,), F32).at[slot].set(flat_g[order])
    blk_e = jnp.minimum(jnp.searchsorted(pend, jnp.arange(n_blk) * MOE_BLOCK, side='right'), n_exp - 1)
    h_pad = jnp.concatenate([h, jnp.zeros((1, d), h.dtype)], axis=0)
    xb = h_pad[slot_tok].reshape(n_blk, MOE_BLOCK, d)

    def expert_block(args):
        xe, e = args
        return jnp.dot(jax.nn.silu(jnp.dot(xe, w1[e])) * jnp.dot(xe, w3[e]), w2[e])

    yb = lax.map(expert_block, (xb, blk_e)).reshape(n_slot, d)
    out = jax.ops.segment_sum(yb * slot_gate[:, None].astype(yb.dtype), slot_tok, num_segments=n_tok + 1)
    return out[:n_tok]


def setup_inputs(seed: int = 0) -> dict:
    key = jax.random.key(seed)
    ks = iter(jax.random.split(key, 48))
    nrm = lambda shape, s: jax.random.normal(next(ks), shape, F32) * s
    L, D = DEPTH, D_MODEL
    nd, nm = (DEPTH + 1) // 2, DEPTH // 2
    x = nrm((BATCH, SEQ, D), 1.0)
    c = nrm((BATCH, D), 1.0)
    ctx = nrm((BATCH, CTX_LEN, D), 1.0)
    c_ctx = nrm((D,), 1.0)
    mod_w = nrm((L, D, 6 * D), 0.5 * D ** -0.5)
    mod_b = nrm((L, 6 * D), 0.02)
    norm1 = 1.0 + nrm((L, D), 0.05)
    w_in = nrm((L, D, D_IN), D ** -0.5)
    w_out = nrm((L, MIX_W, D), MIX_W ** -0.5)
    ml_i_bias = nrm((L, 2, ML_HEADS), 0.1)
    ml_f_bias = 3.0 + 3.0 * jax.random.uniform(next(ks), (L, 2, ML_HEADS), F32)
    ml_norm = 1.0 + nrm((L, ML_W), 0.05)
    ssd_conv_w = nrm((L, SSD_CONV, SSD_CONV_CH), SSD_CONV ** -0.5)
    ssd_conv_b = nrm((L, SSD_CONV_CH), 0.02)
    dt0 = jnp.exp(jax.random.uniform(next(ks), (L, 2, SSD_HEADS), F32, math.log(1e-3), math.log(1e-1)))
    ssd_dt_bias = dt0 + jnp.log(-jnp.expm1(-dt0))
    ssd_A_log = jnp.log(jax.random.uniform(next(ks), (L, 2, SSD_HEADS), F32, 1.0, 16.0))
    ssd_D = 1.0 + nrm((L, SSD_HEADS), 0.1)
    ssd_norm = 1.0 + nrm((L, SSD_W), 0.05)
    mla_q_norm = 1.0 + nrm((L, MLA_Q_RANK), 0.05)
    mla_w_qb = nrm((L, MLA_Q_RANK, MLA_HEADS * MLA_QK), MLA_Q_RANK ** -0.5)
    mla_kv_norm = 1.0 + nrm((L, MLA_KV_RANK), 0.05)
    mla_w_kvb = nrm((L, MLA_KV_RANK, MLA_HEADS * (MLA_NOPE + MLA_V)), MLA_KV_RANK ** -0.5)
    mla_gq = 1.0 + nrm((L, MLA_QK), 0.05)
    mla_gk = 1.0 + nrm((L, MLA_QK), 0.05)
    na_gq = 1.0 + nrm((L, NA_HD), 0.05)
    na_gk = 1.0 + nrm((L, NA_HD), 0.05)
    na_rpb = nrm((L, NA_HEADS, 2 * NA_KH - 1, 2 * NA_KW - 1), 0.1)
    norm2 = 1.0 + nrm((L, D), 0.05)
    ffn_w1 = nrm((nd, D, FFN_DENSE), D ** -0.5)
    ffn_w3 = nrm((nd, D, FFN_DENSE), D ** -0.5)
    ffn_w2 = nrm((nd, FFN_DENSE, D), FFN_DENSE ** -0.5)
    moe_router = nrm((nm, D, N_EXPERTS), D ** -0.5)
    moe_w1 = nrm((nm, N_EXPERTS, D, FFN_EXPERT), D ** -0.5)
    moe_w3 = nrm((nm, N_EXPERTS, D, FFN_EXPERT), D ** -0.5)
    moe_w2 = nrm((nm, N_EXPERTS, FFN_EXPERT, D), FFN_EXPERT ** -0.5)
    return {'x': x, 'c': c, 'ctx': ctx, 'c_ctx': c_ctx, 'mod_w': mod_w, 'mod_b': mod_b,
            'norm1': norm1, 'w_in': w_in, 'w_out': w_out,
            'ml_i_bias': ml_i_bias, 'ml_f_bias': ml_f_bias, 'ml_norm': ml_norm,
            'ssd_conv_w': ssd_conv_w, 'ssd_conv_b': ssd_conv_b, 'ssd_dt_bias': ssd_dt_bias,
            'ssd_A_log': ssd_A_log, 'ssd_D': ssd_D, 'ssd_norm': ssd_norm,
            'mla_q_norm': mla_q_norm, 'mla_w_qb': mla_w_qb, 'mla_kv_norm': mla_kv_norm,
            'mla_w_kvb': mla_w_kvb, 'mla_gq': mla_gq, 'mla_gk': mla_gk,
            'na_gq': na_gq, 'na_gk': na_gk, 'na_rpb': na_rpb, 'norm2': norm2,
            'ffn_w1': ffn_w1, 'ffn_w3': ffn_w3, 'ffn_w2': ffn_w2,
            'moe_router': moe_router, 'moe_w1': moe_w1, 'moe_w3': moe_w3, 'moe_w2': moe_w2}


def reference(x, c, ctx, c_ctx, mod_w, mod_b, norm1, w_in, w_out,
              ml_i_bias, ml_f_bias, ml_norm,
              ssd_conv_w, ssd_conv_b, ssd_dt_bias, ssd_A_log, ssd_D, ssd_norm,
              mla_q_norm, mla_w_qb, mla_kv_norm, mla_w_kvb, mla_gq, mla_gk,
              na_gq, na_gk, na_rpb, norm2,
              ffn_w1, ffn_w3, ffn_w2, moe_router, moe_w1, moe_w3, moe_w2):
    B, T, D = x.shape
    n_ctx = ctx.shape[1]
    rope = axial_rope(T, MLA_ROPE)
    silu_c = jax.nn.silu(c)
    silu_cc = jax.nn.silu(c_ctx)
    xc = ctx
    for l in range(DEPTH):
        with_ctx = l < DEPTH - 1
        mod = (jnp.dot(silu_c, mod_w[l]) + mod_b[l])[:, None, :]
        modc = (jnp.dot(silu_cc, mod_w[l]) + mod_b[l])[None, None, :]
        sh1, sc1, g1, sh2, sc2, g2 = jnp.split(mod, 6, axis=-1)
        sh1c, sc1c, g1c, sh2c, sc2c, g2c = jnp.split(modc, 6, axis=-1)

        h = rms_norm(x, norm1[l]) * (1.0 + sc1) + sh1
        hc = rms_norm(xc, norm1[l]) * (1.0 + sc1c) + sh1c
        pl = split_cols(jnp.dot(h, w_in[l]))
        pc = split_cols(jnp.dot(hc, w_in[l]))
        ml_l, ml_c = mlstm_mixer(pl[0:6], pc[0:6], ml_i_bias[l], ml_f_bias[l], ml_norm[l], with_ctx)
        ss_l, ss_c = ssd_mixer(pl[6:9], pc[6:9], ssd_conv_w[l], ssd_conv_b[l], ssd_dt_bias[l],
                               ssd_A_log[l], ssd_D[l], ssd_norm[l], with_ctx)
        la_l, la_c = mla_mixer(pl[9:12], pc[9:12], mla_q_norm[l], mla_w_qb[l], mla_kv_norm[l],
                               mla_w_kvb[l], mla_gq[l], mla_gk[l], rope, with_ctx)
        na_l, na_c = na_mixer(pl[12:15], pc[12:15], na_gq[l], na_gk[l], na_rpb[l], with_ctx)
        y = jnp.dot(jnp.concatenate([ml_l, ss_l, la_l, na_l], axis=-1), w_out[l])
        x = x + g1 * y
        if with_ctx:
            yc = jnp.dot(jnp.concatenate([ml_c, ss_c, la_c, na_c], axis=-1), w_out[l])
            xc = xc + g1c * yc

        h2 = rms_norm(x, norm2[l]) * (1.0 + sc2) + sh2
        tokens = h2.reshape(B * T, D)
        if with_ctx:
            h2c = rms_norm(xc, norm2[l]) * (1.0 + sc2c) + sh2c
            tokens = jnp.concatenate([tokens, h2c.reshape(B * n_ctx, D)], axis=0)
        if l % 2 == 0:
            f = swiglu(tokens, ffn_w1[l // 2], ffn_w3[l // 2], ffn_w2[l // 2])
        else:
            f = moe_swiglu(tokens, moe_router[l // 2], moe_w1[l // 2], moe_w3[l // 2], moe_w2[l // 2])
        x = x + g2 * f[:B * T].reshape(B, T, D)
        if with_ctx:
            xc = xc + g2c * f[B * T:].reshape(B, n_ctx, D)
    return x
```

```python
from functools import partial

import numpy as np
import jax
import jax.numpy as jnp
from jax import lax
from jax.experimental import pallas as pl
from jax.experimental.pallas import tpu as pltpu

F32 = jnp.float32
BF16 = jnp.bfloat16
HI = lax.Precision.HIGHEST

D = 2048
NB = 4
T_LAT = 2048
T_CTX = 256
S_ALL = T_LAT + T_CTX
N_ROWS = NB * S_ALL
DEPTH = 2
GRID_W = 64
EPS = 1e-6

ROWBLK = 256
BLK_PER_B = S_ALL // ROWBLK
CHUNK = 64
CH_PER_B = S_ALL // CHUNK
CH_CTX = T_CTX // CHUNK

ML_HEADS, ML_HD = 4, 128
SSD_HEADS, SSD_HD, SSD_GROUPS, SSD_STATE, SSD_CONV = 8, 64, 2, 128, 5
SSD_W = SSD_HEADS * SSD_HD
MLA_HEADS, MLA_NOPE, MLA_ROPE, MLA_V = 4, 128, 64, 128
MLA_QK = MLA_NOPE + MLA_ROPE
MLA_Q_RANK, MLA_KV_RANK = 448, 160
MLA_HP = 256
ROPE_THETA = 10000.0
NA_HEADS, NA_HD, NA_KH, NA_KW = 4, 128, 8, 16
N_EXPERTS, TOP_K, MOE_BLOCK = 8, 2, 128
FFN_EXPERT = 7168

C_Q, C_K, C_V, C_O = 0, 512, 1024, 1536
C_Z = 2048
C_XBC = 2560
C_CQ, C_CKV, C_KR = 3584, 4096, 4352
C_NAQ, C_NAK, C_NAV = 4480, 4992, 5504
C_GATE = 6016
D_INP = 6144
G_I, G_F, G_DT, G_USED = 0, 8, 16, 32

VMEM_LIMIT = 56 * 1024 * 1024


def _cparams(sem):
    return pltpu.CompilerParams(dimension_semantics=sem, vmem_limit_bytes=VMEM_LIMIT)


def _mod_index(blk256):
    return jnp.where(blk256 % BLK_PER_B == 0, NB, blk256 // BLK_PER_B)


def _dot_nt(a, b):
    return lax.dot_general(a, b, (((1,), (1,)), ((), ())), preferred_element_type=F32)


def _dot_tn(a, b):
    return lax.dot_general(a, b, (((0,), (0,)), ((), ())), preferred_element_type=F32)


def _silu(x):
    return x * jax.nn.sigmoid(x)


def _softplus(x):
    return jnp.maximum(x, 0.0) + jnp.log(1.0 + jnp.exp(-jnp.abs(x)))


def _mod_kernel(c_ref, w_ref, b_ref, o_ref):
    o_ref[0] = jnp.dot(_silu(c_ref[...]), w_ref[0], preferred_element_type=F32) + b_ref[0]


def modulation(c_all, mod_w, mod_b):
    tn = 1024
    n_out = mod_w.shape[-1]
    return pl.pallas_call(
        _mod_kernel,
        grid=(DEPTH, n_out // tn),
        in_specs=[pl.BlockSpec((8, D), lambda l, j: (0, 0)),
                  pl.BlockSpec((1, D, tn), lambda l, j: (l, 0, j)),
                  pl.BlockSpec((1, 1, tn), lambda l, j: (l, 0, j))],
        out_specs=pl.BlockSpec((1, 8, tn), lambda l, j: (l, 0, j)),
        out_shape=jax.ShapeDtypeStruct((DEPTH, 8, n_out), F32),
        compiler_params=_cparams(("arbitrary", "arbitrary")),
        name="modulation",
    )(c_all, mod_w, mod_b.reshape(DEPTH, 1, n_out))


def _rmsmod_kernel(x_ref, g_ref, sh_ref, sc_ref, *rest, with_router):
    midx = _mod_index(pl.program_id(0))
    x = x_ref[...]
    y = x * lax.rsqrt(jnp.mean(x * x, axis=-1, keepdims=True) + EPS) * g_ref[...]
    h = y * (1.0 + sc_ref[pl.ds(midx, 1), :]) + sh_ref[pl.ds(midx, 1), :]
    if with_router:
        r_ref, o_ref, lg_ref = rest
        o_ref[...] = h
        lg_ref[...] = jnp.dot(h, r_ref[...], precision=HI, preferred_element_type=F32)
    else:
        (o_ref,) = rest
        o_ref[...] = h.astype(BF16)


def rms_modulate(x, g, modtab, which_shift, router=None):
    with_router = router is not None
    in_specs = [pl.BlockSpec((ROWBLK, D), lambda i: (i, 0)),
                pl.BlockSpec((1, D), lambda i: (0, 0)),
                pl.BlockSpec((8, D), lambda i: (0, which_shift)),
                pl.BlockSpec((8, D), lambda i: (0, which_shift + 1))]
    args = [x, g.reshape(1, D), modtab, modtab]
    if with_router:
        in_specs.append(pl.BlockSpec((D, 128), lambda i: (0, 0)))
        args.append(router)
        out_shape = [jax.ShapeDtypeStruct((N_ROWS, D), F32), jax.ShapeDtypeStruct((N_ROWS, 128), F32)]
        out_specs = [pl.BlockSpec((ROWBLK, D), lambda i: (i, 0)), pl.BlockSpec((ROWBLK, 128), lambda i: (i, 0))]
    else:
        out_shape = jax.ShapeDtypeStruct((N_ROWS, D), BF16)
        out_specs = pl.BlockSpec((ROWBLK, D), lambda i: (i, 0))
    return pl.pallas_call(
        partial(_rmsmod_kernel, with_router=with_router),
        grid=(N_ROWS // ROWBLK,),
        in_specs=in_specs, out_specs=out_specs, out_shape=out_shape,
        compiler_params=_cparams(("arbitrary",)),
        name="rms_modulate_router" if with_router else "rms_modulate",
    )(*args)


def _mm_kernel(*refs, mode, nk, tm):
    nw = 2 if mode == "swiglu" else 1
    a_ref = refs[0]
    w_refs = refs[1:1 + nw]
    pos = 1 + nw
    if mode == "resid":
        res_ref, gate_ref = refs[pos:pos + 2]
        pos += 2
    o_ref = refs[pos]
    acc_refs = refs[pos + 1:]
    i = pl.program_id(0)
    k = pl.program_id(2)
    a = a_ref[...]
    parts = [jnp.dot(a, w_ref[...].astype(BF16), preferred_element_type=F32) for w_ref in w_refs]

    def finish(accs):
        if mode == "plain":
            o_ref[...] = accs[0].astype(o_ref.dtype)
        elif mode == "swiglu":
            o_ref[...] = (_silu(accs[0]) * accs[1]).astype(o_ref.dtype)
        else:
            for s in range(tm // ROWBLK):
                midx = _mod_index(i * (tm // ROWBLK) + s)
                rows = slice(s * ROWBLK, (s + 1) * ROWBLK)
                o_ref[rows, :] = res_ref[rows, :] + gate_ref[pl.ds(midx, 1), :] * accs[0][rows, :]

    if nk == 1:
        finish(parts)
    else:
        @pl.when(k == 0)
        def _():
            for acc_ref, p in zip(acc_refs, parts):
                acc_ref[...] = p

        @pl.when(k > 0)
        def _():
            for acc_ref, p in zip(acc_refs, parts):
                acc_ref[...] += p

        @pl.when(k == nk - 1)
        def _():
            finish([acc_ref[...] for acc_ref in acc_refs])


def matmul(a, ws, mode="plain", out_dtype=F32, res=None, modtab=None, gate_col=None,
           tm=768, tn=512, tk=None, name="matmul"):
    m, kdim = a.shape
    n = ws[0].shape[1]
    tk = kdim if tk is None else tk
    nk = kdim // tk
    in_specs = [pl.BlockSpec((tm, tk), lambda i, j, k: (i, k))]
    in_specs += [pl.BlockSpec((tk, tn), lambda i, j, k: (k, j)) for _ in ws]
    args = [a, *ws]
    if mode == "resid":
        in_specs += [pl.BlockSpec((tm, tn), lambda i, j, k: (i, j)),
                     pl.BlockSpec((8, tn), lambda i, j, k: (0, gate_col * (D // tn) + j))]
        args += [res, modtab]
    scratch = [] if nk == 1 else [pltpu.VMEM((tm, tn), F32) for _ in ws]
    return pl.pallas_call(
        partial(_mm_kernel, mode=mode, nk=nk, tm=tm),
        grid=(m // tm, n // tn, nk),
        in_specs=in_specs,
        out_specs=pl.BlockSpec((tm, tn), lambda i, j, k: (i, j)),
        out_shape=jax.ShapeDtypeStruct((m, n), out_dtype),
        scratch_shapes=scratch,
        compiler_params=_cparams(("arbitrary", "arbitrary", "arbitrary")),
        name=name,
    )(*args)


def _fwd_chunk(i):
    return i


def _bwd_chunk(i):
    return jnp.where(i < CH_CTX, CH_CTX - 1 - i, CH_PER_B + CH_CTX - 1 - i)


def _tri_masks():
    r = lax.broadcasted_iota(jnp.int32, (CHUNK, CHUNK), 0)
    c = lax.broadcasted_iota(jnp.int32, (CHUNK, CHUNK), 1)
    return c <= r, c >= r


def _mlstm_kernel(qf_ref, kf_ref, vf_ref, gcf_ref, grf_ref,
                  qb_ref, kb_ref, vb_ref, gcb_ref, grb_ref,
                  biasc_ref, biasr_ref, hf_ref, hb_ref, c_ref, n_ref, m_ref):
    @pl.when(pl.program_id(1) == 0)
    def _():
        c_ref[...] = jnp.zeros_like(c_ref)
        n_ref[...] = jnp.zeros_like(n_ref)
        m_ref[...] = jnp.zeros_like(m_ref)

    low, upp = _tri_masks()
    lowf, uppf = low.astype(F32), upp.astype(F32)
    streams = ((0, qf_ref, kf_ref, vf_ref, gcf_ref, grf_ref, hf_ref, low, lowf, uppf, CHUNK - 1),
               (1, qb_ref, kb_ref, vb_ref, gcb_ref, grb_ref, hb_ref, upp, uppf, lowf, 0))
    for d, q_ref, k_ref, v_ref, gc_ref, gr_ref, h_ref, mask, tri_c, tri_r, last in streams:
        ac = gc_ref[...] + biasc_ref[...]
        ar = gr_ref[0] + biasr_ref[...]
        lfc = -_softplus(-ac)
        lfr = -_softplus(-ar)
        bc_all = jnp.dot(tri_c, lfc, precision=HI, preferred_element_type=F32)
        br_all = jnp.dot(lfr, tri_r, precision=HI, preferred_element_type=F32)
        for h in range(ML_HEADS):
            s_idx = d * ML_HEADS + h
            ji, jf = G_I + s_idx, G_F + s_idx
            cols = slice(h * ML_HD, (h + 1) * ML_HD)
            q = q_ref[:, cols] * (ML_HD ** -0.5)
            k = k_ref[:, cols]
            v = v_ref[:, cols]
            bc = bc_all[:, jf:jf + 1]
            br = br_all[jf:jf + 1, :]
            icc = ac[:, ji:ji + 1]
            icr = ar[ji:ji + 1, :]
            cmat = c_ref[s_idx]
            nvec = n_ref[s_idx]
            m_prev = m_ref[s_idx][:, 0:1]
            dmat = jnp.where(mask, bc - br + icr, -1e30)
            inter = bc + m_prev
            mt = jnp.maximum(inter, jnp.max(dmat, axis=1, keepdims=True))
            w_intra = jnp.exp(dmat - mt)
            w_state = jnp.exp(inter - mt)
            s = _dot_nt(q, k) * w_intra
            num = jnp.dot(s, v, preferred_element_type=F32) + w_state * _dot_nt(q, cmat)
            den = jnp.sum(s, axis=1, keepdims=True) + w_state * jnp.sum(q * nvec, axis=1, keepdims=True)
            h_ref[:, cols] = num / jnp.maximum(jnp.abs(den), jnp.exp(-mt))
            bl = bc[last:last + 1, :]
            g = bl - bc + icc
            m_new = jnp.maximum(bl + m_prev, jnp.max(g, axis=0, keepdims=True))
            wg = jnp.exp(g - m_new)
            wc = jnp.exp(bl + m_prev - m_new)
            c_ref[s_idx] = wc * cmat + _dot_tn(wg * v, k)
            n_ref[s_idx] = wc * nvec + jnp.sum(wg * k, axis=0, keepdims=True)
            m_ref[s_idx] = jnp.broadcast_to(m_new, (1, 128))


def mlstm_scan(p, gates_t, biasc, biasr):
    def rows(chunk_fn):
        return lambda b, i: b * CH_PER_B + chunk_fn(i)

    def stream_specs(chunk_fn):
        r = rows(chunk_fn)
        return [pl.BlockSpec((CHUNK, 512), lambda b, i: (r(b, i), C_Q // 512)),
                pl.BlockSpec((CHUNK, 512), lambda b, i: (r(b, i), C_K // 512)),
                pl.BlockSpec((CHUNK, 512), lambda b, i: (r(b, i), C_V // 512)),
                pl.BlockSpec((CHUNK, 128), lambda b, i: (r(b, i), C_GATE // 128)),
                pl.BlockSpec((1, G_USED, CHUNK), lambda b, i: (r(b, i), 0, 0))]

    rf, rb = rows(_fwd_chunk), rows(_bwd_chunk)
    n_streams = 2 * ML_HEADS
    return pl.pallas_call(
        _mlstm_kernel,
        grid=(NB, CH_PER_B),
        in_specs=stream_specs(_fwd_chunk) + stream_specs(_bwd_chunk) + [
            pl.BlockSpec((1, 128), lambda b, i: (0, 0)),
            pl.BlockSpec((G_USED, 1), lambda b, i: (0, 0))],
        out_specs=[pl.BlockSpec((CHUNK, 512), lambda b, i: (rf(b, i), 0)),
                   pl.BlockSpec((CHUNK, 512), lambda b, i: (rb(b, i), 0))],
        out_shape=[jax.ShapeDtypeStruct((N_ROWS, 512), F32)] * 2,
        scratch_shapes=[pltpu.VMEM((n_streams, ML_HD, ML_HD), F32),
                        pltpu.VMEM((n_streams, 1, ML_HD), F32),
                        pltpu.VMEM((n_streams, 1, 128), F32)],
        compiler_params=_cparams(("arbitrary", "arbitrary")),
        name="mlstm_scan",
    )(p, p, p, p, gates_t, p, p, p, p, gates_t, biasc, biasr)


def _mlstm_finish_kernel(hf_ref, hb_ref, o_ref, g_ref, out_ref):
    for h in range(ML_HEADS):
        cols = slice(h * ML_HD, (h + 1) * ML_HD)
        x = hf_ref[:, cols] + hb_ref[:, cols]
        y = x * lax.rsqrt(jnp.mean(x * x, axis=-1, keepdims=True) + EPS) * g_ref[:, cols]
        out_ref[:, cols] = (y * jax.nn.sigmoid(o_ref[:, cols])).astype(BF16)


def mlstm_finish(hf, hb, p, norm_g):
    blk = lambda c: pl.BlockSpec((ROWBLK, 512), lambda i: (i, c))
    return pl.pallas_call(
        _mlstm_finish_kernel,
        grid=(N_ROWS // ROWBLK,),
        in_specs=[blk(0), blk(0), blk(C_O // 512), pl.BlockSpec((1, 512), lambda i: (0, 0))],
        out_specs=blk(0),
        out_shape=jax.ShapeDtypeStruct((N_ROWS, 512), BF16),
        compiler_params=_cparams(("arbitrary",)),
        name="mlstm_finish",
    )(hf, hb, p, norm_g.reshape(1, 512))


_PAD = 8


def _ssd_conv_kernel(u_ref, w_ref, b_ref, o_ref, pad_ref):
    zeros = jnp.zeros((_PAD, u_ref.shape[1]), F32)
    segs = ((0, T_CTX, _PAD), (T_CTX, T_LAT, 2 * _PAD + T_CTX))
    pad_ref[0:_PAD] = zeros
    pad_ref[_PAD + T_CTX:2 * _PAD + T_CTX] = zeros
    pad_ref[2 * _PAD + S_ALL:3 * _PAD + S_ALL] = zeros
    for src, n, dst in segs:
        pad_ref[dst:dst + n] = u_ref[src:src + n]
    for src, n, dst in segs:
        acc = jnp.broadcast_to(b_ref[...], (n, u_ref.shape[1]))
        for j in range(SSD_CONV):
            lo = dst + j - SSD_CONV // 2
            acc = acc + w_ref[j:j + 1, :] * pad_ref[lo:lo + n]
        o_ref[src:src + n] = _silu(acc)


def ssd_conv(p, conv_w, conv_b):
    tc = 256
    return pl.pallas_call(
        _ssd_conv_kernel,
        grid=(NB, 1024 // tc),
        in_specs=[pl.BlockSpec((S_ALL, tc), lambda b, c: (b, C_XBC // tc + c)),
                  pl.BlockSpec((SSD_CONV, tc), lambda b, c: (0, c)),
                  pl.BlockSpec((1, tc), lambda b, c: (0, c))],
        out_specs=pl.BlockSpec((S_ALL, tc), lambda b, c: (b, c)),
        out_shape=jax.ShapeDtypeStruct((N_ROWS, 1024), F32),
        scratch_shapes=[pltpu.VMEM((S_ALL + 3 * _PAD, tc), F32)],
        compiler_params=_cparams(("arbitrary", "arbitrary")),
        name="ssd_conv",
    )(p, conv_w, conv_b.reshape(1, 1024))


def _ssd_kernel(xf_ref, bf_ref, cf_ref, gcf_ref, grf_ref,
                xb_ref, bb_ref, cb_ref, gcb_ref, grb_ref,
                biasc_ref, biasr_ref, alogc_ref, alogr_ref, yf_ref, yb_ref, st_ref):
    @pl.when(pl.program_id(1) == 0)
    def _():
        st_ref[...] = jnp.zeros_like(st_ref)

    low, upp = _tri_masks()
    lowf, uppf = low.astype(F32), upp.astype(F32)
    a_c = -jnp.exp(alogc_ref[...])
    a_r = -jnp.exp(alogr_ref[...])
    streams = ((0, xf_ref, bf_ref, cf_ref, gcf_ref, grf_ref, yf_ref, low, lowf, uppf, CHUNK - 1),
               (1, xb_ref, bb_ref, cb_ref, gcb_ref, grb_ref, yb_ref, upp, uppf, lowf, 0))
    for d, x_ref, b_ref, c_ref, gc_ref, gr_ref, y_ref, mask, tri_c, tri_r, last in streams:
        dtc_all = _softplus(gc_ref[...] + biasc_ref[...])
        dtr_all = _softplus(gr_ref[0] + biasr_ref[...])
        cumc_all = jnp.dot(tri_c, dtc_all * a_c, precision=HI, preferred_element_type=F32)
        cumr_all = jnp.dot(dtr_all * a_r, tri_r, precision=HI, preferred_element_type=F32)
        for g in range(SSD_GROUPS):
            gcols = slice(g * SSD_STATE, (g + 1) * SSD_STATE)
            bm = b_ref[:, gcols]
            cm = c_ref[:, gcols]
            cb = _dot_nt(cm, bm)
            for hh in range(SSD_HEADS // SSD_GROUPS):
                h = g * (SSD_HEADS // SSD_GROUPS) + hh
                j = G_DT + d * SSD_HEADS + h
                s_idx = d * SSD_HEADS + h
                cols = slice(h * SSD_HD, (h + 1) * SSD_HD)
                x = x_ref[:, cols]
                cc = cumc_all[:, j:j + 1]
                cr = cumr_all[j:j + 1, :]
                dtc = dtc_all[:, j:j + 1]
                dtr = dtr_all[j:j + 1, :]
                st = st_ref[s_idx]
                decay = jnp.exp(jnp.where(mask, cc - cr, -1e30))
                w = cb * decay * dtr
                y_ref[:, cols] = (jnp.dot(w, x, preferred_element_type=F32)
                                  + _dot_nt(cm, st) * jnp.exp(cc))
                cl = cc[last:last + 1, :]
                tail = jnp.exp(cl - cc) * dtc
                st_ref[s_idx] = st * jnp.exp(cl) + _dot_tn(tail * x, bm)


def ssd_scan(xbc, p, gates_t, biasc, biasr, alogc, alogr):
    def rows(chunk_fn):
        return lambda b, i: b * CH_PER_B + chunk_fn(i)

    def stream_specs(chunk_fn):
        r = rows(chunk_fn)
        return [pl.BlockSpec((CHUNK, 512), lambda b, i: (r(b, i), 0)),
                pl.BlockSpec((CHUNK, 256), lambda b, i: (r(b, i), 2)),
                pl.BlockSpec((CHUNK, 256), lambda b, i: (r(b, i), 3)),
                pl.BlockSpec((CHUNK, 128), lambda b, i: (r(b, i), C_GATE // 128)),
                pl.BlockSpec((1, G_USED, CHUNK), lambda b, i: (r(b, i), 0, 0))]

    rf, rb = rows(_fwd_chunk), rows(_bwd_chunk)
    const = lambda shape: pl.BlockSpec(shape, lambda b, i: (0, 0))
    return pl.pallas_call(
        _ssd_kernel,
        grid=(NB, CH_PER_B),
        in_specs=stream_specs(_fwd_chunk) + stream_specs(_bwd_chunk) + [
            const((1, 128)), const((G_USED, 1)), const((1, 128)), const((G_USED, 1))],
        out_specs=[pl.BlockSpec((CHUNK, 512), lambda b, i: (rf(b, i), 0)),
                   pl.BlockSpec((CHUNK, 512), lambda b, i: (rb(b, i), 0))],
        out_shape=[jax.ShapeDtypeStruct((N_ROWS, 512), F32)] * 2,
        scratch_shapes=[pltpu.VMEM((2 * SSD_HEADS, SSD_HD, SSD_STATE), F32)],
        compiler_params=_cparams(("arbitrary", "arbitrary")),
        name="ssd_scan",
    )(xbc, xbc, xbc, p, gates_t, xbc, xbc, xbc, p, gates_t, biasc, biasr, alogc, alogr)


def _ssd_finish_kernel(yf_ref, yb_ref, xs_ref, z_ref, dsk_ref, g_ref, out_ref):
    y = yf_ref[...] + yb_ref[...] + dsk_ref[...] * xs_ref[...]
    y = y * _silu(z_ref[...])
    out_ref[...] = (y * lax.rsqrt(jnp.mean(y * y, axis=-1, keepdims=True) + EPS) * g_ref[...]).astype(BF16)


def ssd_finish(yf, yb, xbc, p, d_skip, norm_g):
    blk = lambda c: pl.BlockSpec((ROWBLK, 512), lambda i: (i, c))
    vec = pl.BlockSpec((1, 512), lambda i: (0, 0))
    return pl.pallas_call(
        _ssd_finish_kernel,
        grid=(N_ROWS // ROWBLK,),
        in_specs=[blk(0), blk(0), blk(0), blk(C_Z // 512), vec, vec],
        out_specs=blk(0),
        out_shape=jax.ShapeDtypeStruct((N_ROWS, 512), BF16),
        compiler_params=_cparams(("arbitrary",)),
        name="ssd_finish",
    )(yf, yb, xbc, p, jnp.repeat(d_skip, SSD_HD).reshape(1, 512), norm_g.reshape(1, 512))


def _rope_tile(r, cs_ref, s1_ref, s2_ref):
    return (r * cs_ref[...] + pltpu.roll(r, 128 - MLA_ROPE // 2, 1) * s1_ref[...]
            + pltpu.roll(r, MLA_ROPE // 2, 1) * s2_ref[...])


def _mla_prep_kernel(cq_ref, ckv_ref, kr_ref, wq_ref, wkv_ref, qn_ref, kvn_ref, gq_ref, gk_ref,
                     cs_ref, s1_ref, s2_ref, q_out, k_out, v_out):
    cq = cq_ref[...]
    cqn = cq * lax.rsqrt(jnp.sum(cq * cq, axis=-1, keepdims=True) / MLA_Q_RANK + EPS) * qn_ref[...]
    q_raw = jnp.dot(cqn, wq_ref[...], preferred_element_type=F32)
    ckv = ckv_ref[...]
    ckvn = ckv * lax.rsqrt(jnp.sum(ckv * ckv, axis=-1, keepdims=True) / MLA_KV_RANK + EPS) * kvn_ref[...]
    kv_raw = jnp.dot(ckvn, wkv_ref[...], preferred_element_type=F32)
    kr = kr_ref[...]
    kr_ss = jnp.sum(kr * kr, axis=-1, keepdims=True)
    for h in range(MLA_HEADS):
        base = h * MLA_HP
        qa = q_raw[:, base:base + 128]
        qb = q_raw[:, base + 128:base + 256]
        q_scale = lax.rsqrt((jnp.sum(qa * qa, axis=-1, keepdims=True)
                             + jnp.sum(qb * qb, axis=-1, keepdims=True)) / MLA_QK + EPS)
        q_out[:, base:base + 128] = (qa * q_scale * gq_ref[:, 0:128]).astype(BF16)
        q_out[:, base + 128:base + 256] = _rope_tile(qb * q_scale * gq_ref[:, 128:256],
                                                     cs_ref, s1_ref, s2_ref).astype(BF16)
        kn = kv_raw[:, base:base + 128]
        k_scale = lax.rsqrt((jnp.sum(kn * kn, axis=-1, keepdims=True) + kr_ss) / MLA_QK + EPS)
        k_out[:, base:base + 128] = (kn * k_scale * gk_ref[:, 0:128]).astype(BF16)
        k_out[:, base + 128:base + 256] = _rope_tile(kr * k_scale * gk_ref[:, 128:256],
                                                     cs_ref, s1_ref, s2_ref).astype(BF16)
        v_out[:, h * MLA_V:(h + 1) * MLA_V] = kv_raw[:, base + 128:base + 256].astype(BF16)


def mla_prep(p, wq, wkv, qn, kvn, gq, gk, rope_tabs):
    const = lambda shape: pl.BlockSpec(shape, lambda i: (0, 0))
    tab = pl.BlockSpec((ROWBLK, 128), lambda i: (i % BLK_PER_B, 0))
    row = lambda w: pl.BlockSpec((ROWBLK, w), lambda i: (i, 0))
    return pl.pallas_call(
        _mla_prep_kernel,
        grid=(N_ROWS // ROWBLK,),
        in_specs=[pl.BlockSpec((ROWBLK, 512), lambda i: (i, C_CQ // 512)),
                  pl.BlockSpec((ROWBLK, 256), lambda i: (i, C_CKV // 256)),
                  pl.BlockSpec((ROWBLK, 128), lambda i: (i, C_KR // 128)),
                  const((512, MLA_HEADS * MLA_HP)), const((256, MLA_HEADS * MLA_HP)),
                  const((1, 512)), const((1, 256)), const((1, MLA_HP)), const((1, MLA_HP)),
                  tab, tab, tab],
        out_specs=[row(MLA_HEADS * MLA_HP), row(MLA_HEADS * MLA_HP), row(MLA_HEADS * MLA_V)],
        out_shape=[jax.ShapeDtypeStruct((N_ROWS, MLA_HEADS * MLA_HP), BF16),
                   jax.ShapeDtypeStruct((N_ROWS, MLA_HEADS * MLA_HP), BF16),
                   jax.ShapeDtypeStruct((N_ROWS, MLA_HEADS * MLA_V), BF16)],
        compiler_params=_cparams(("arbitrary",)),
        name="mla_prep",
    )(p, p, p, wq, wkv, qn, kvn, gq, gk, *rope_tabs)


def _softmax_pv(s_list, v_list):
    m = s_list[0].max(axis=-1, keepdims=True)
    for s in s_list[1:]:
        m = jnp.maximum(m, s.max(axis=-1, keepdims=True))
    ps = [jnp.exp(s - m) for s in s_list]
    l = sum(p.sum(axis=-1, keepdims=True) for p in ps)
    o = sum(jnp.dot(p.astype(BF16), v, preferred_element_type=F32) for p, v in zip(ps, v_list))
    return o / l


def _mla_attn_kernel(q_ref, k_ref, v_ref, o_ref):
    scale = MLA_QK ** -0.5
    q = q_ref[...]

    @pl.when(pl.program_id(2) == 0)
    def _():
        s = _dot_nt(q, k_ref[0:T_CTX, :]) * scale
        o_ref[...] = _softmax_pv([s], [v_ref[0:T_CTX, :]]).astype(BF16)

    @pl.when(pl.program_id(2) > 0)
    def _():
        s = _dot_nt(q, k_ref[...]) * scale
        o_ref[...] = _softmax_pv([s], [v_ref[...]]).astype(BF16)


def mla_attention(q, k, v):
    return pl.pallas_call(
        _mla_attn_kernel,
        grid=(NB, MLA_HEADS, BLK_PER_B),
        in_specs=[pl.BlockSpec((ROWBLK, MLA_HP), lambda b, h, i: (b * BLK_PER_B + i, h)),
                  pl.BlockSpec((S_ALL, MLA_HP), lambda b, h, i: (b, h)),
                  pl.BlockSpec((S_ALL, MLA_V), lambda b, h, i: (b, h))],
        out_specs=pl.BlockSpec((ROWBLK, MLA_V), lambda b, h, i: (b * BLK_PER_B + i, h)),
        out_shape=jax.ShapeDtypeStruct((N_ROWS, MLA_HEADS * MLA_V), BF16),
        compiler_params=_cparams(("arbitrary", "arbitrary", "arbitrary")),
        name="mla_attention",
    )(q, k, v)


N_GRID_ROWS = T_LAT // GRID_W
NA_WIN = NA_KH * GRID_W


def _na_first_row(r):
    return jnp.clip(r - NA_KH // 2, 0, N_GRID_ROWS - NA_KH)


def _na_kernel(q_ref, k_ref, v_ref, gq_ref, gk_ref, bias_ref, o_ref, kn_ref, vb_ref):
    st = pl.program_id(2)
    scale = NA_HD ** -0.5

    @pl.when(st == 0)
    def _():
        k = k_ref[...]
        kn_ref[...] = (k * lax.rsqrt(jnp.mean(k * k, axis=-1, keepdims=True) + EPS) * gk_ref[...]).astype(BF16)
        vb_ref[...] = v_ref[...].astype(BF16)

    q = q_ref[...]
    qn = (q * lax.rsqrt(jnp.mean(q * q, axis=-1, keepdims=True) + EPS) * gq_ref[...]).astype(BF16)
    s_cx = _dot_nt(qn, kn_ref[0:T_CTX, :]) * scale
    v_cx = vb_ref[0:T_CTX, :]

    @pl.when(st < CH_CTX)
    def _():
        o_ref[...] = _softmax_pv([s_cx], [v_cx]).astype(BF16)

    @pl.when(st >= CH_CTX)
    def _():
        start = pl.multiple_of(T_CTX + _na_first_row(st - CH_CTX) * GRID_W, GRID_W)
        s_nb = _dot_nt(qn, kn_ref[pl.ds(start, NA_WIN), :]) * scale + bias_ref[0, 0]
        o_ref[...] = _softmax_pv([s_nb, s_cx], [vb_ref[pl.ds(start, NA_WIN), :], v_cx]).astype(BF16)


def na_attention(p, gq, gk, bias_tab):
    def bias_map(b, h, st):
        r = jnp.maximum(st - CH_CTX, 0)
        return (h, r - _na_first_row(r), 0, 0)

    return pl.pallas_call(
        _na_kernel,
        grid=(NB, NA_HEADS, CH_PER_B),
        in_specs=[pl.BlockSpec((GRID_W, NA_HD), lambda b, h, st: (b * CH_PER_B + st, C_NAQ // NA_HD + h)),
                  pl.BlockSpec((S_ALL, NA_HD), lambda b, h, st: (b, C_NAK // NA_HD + h)),
                  pl.BlockSpec((S_ALL, NA_HD), lambda b, h, st: (b, C_NAV // NA_HD + h)),
                  pl.BlockSpec((1, NA_HD), lambda b, h, st: (0, 0)),
                  pl.BlockSpec((1, NA_HD), lambda b, h, st: (0, 0)),
                  pl.BlockSpec((1, 1, GRID_W, NA_WIN), bias_map)],
        out_specs=pl.BlockSpec((GRID_W, NA_HD), lambda b, h, st: (b * CH_PER_B + st, h)),
        out_shape=jax.ShapeDtypeStruct((N_ROWS, NA_HEADS * NA_HD), BF16),
        scratch_shapes=[pltpu.VMEM((S_ALL, NA_HD), BF16), pltpu.VMEM((S_ALL, NA_HD), BF16)],
        compiler_params=_cparams(("arbitrary", "arbitrary", "arbitrary")),
        name="na_attention",
    )(p, p, p, gq.reshape(1, NA_HD), gk.reshape(1, NA_HD), bias_tab)


def na_bias_table(rpb):
    delta = np.arange(NA_KH)[:, None]
    j = np.arange(NA_KH)[None, :]
    dr = j - delta + NA_KH - 1
    c = np.arange(GRID_W)[:, None]
    kc = np.arange(GRID_W)[None, :]
    c0 = np.clip(c - NA_KW // 2, 0, GRID_W - NA_KW)
    valid = (kc >= c0) & (kc < c0 + NA_KW)
    dc = np.clip(kc - c + NA_KW - 1, 0, 2 * NA_KW - 2)
    t = rpb[:, dr]
    t = t[:, :, :, dc]
    t = jnp.where(valid[None, None, None], t, -1e30)
    return t.transpose(0, 1, 3, 2, 4).reshape(NA_HEADS, NA_KH, GRID_W, NA_WIN).astype(F32)


N_TOK = NB * T_LAT
N_BLK = N_TOK * TOP_K // MOE_BLOCK + N_EXPERTS
N_SLOT = N_BLK * MOE_BLOCK


def _gather_kernel(idx_ref, h_hbm, o_ref, buf_ref, sem):
    base = pl.program_id(0) * MOE_BLOCK

    def row_copy(r):
        return pltpu.make_async_copy(h_hbm.at[pl.ds(idx_ref[base + r], 1)], buf_ref.at[pl.ds(r, 1)], sem)

    def start(r, _):
        row_copy(r).start()
        return 0

    def wait(r, _):
        row_copy(r).wait()
        return 0

    lax.fori_loop(0, MOE_BLOCK, start, 0)
    lax.fori_loop(0, MOE_BLOCK, wait, 0)
    o_ref[...] = buf_ref[...].astype(BF16)


def moe_gather(h2, slot_row):
    return pl.pallas_call(
        _gather_kernel,
        grid_spec=pltpu.PrefetchScalarGridSpec(
            num_scalar_prefetch=1,
            grid=(N_BLK,),
            in_specs=[pl.BlockSpec(memory_space=pl.ANY)],
            out_specs=pl.BlockSpec((MOE_BLOCK, D), lambda i, idx: (i, 0)),
            scratch_shapes=[pltpu.VMEM((MOE_BLOCK, D), F32), pltpu.SemaphoreType.DMA(())]),
        out_shape=jax.ShapeDtypeStruct((N_SLOT, D), BF16),
        compiler_params=_cparams(("arbitrary",)),
        name="moe_gather",
    )(slot_row, h2)


def _moe_up_kernel(be_ref, x_ref, w1_ref, w3_ref, o_ref):
    x = x_ref[...]
    a = jnp.dot(x, w1_ref[0].astype(BF16), preferred_element_type=F32)
    g = jnp.dot(x, w3_ref[0].astype(BF16), preferred_element_type=F32)
    o_ref[...] = (_silu(a) * g).astype(BF16)


def moe_up(xb, w1, w3, blk_e):
    tn = 512
    wspec = pl.BlockSpec((1, D, tn), lambda j, i, be: (be[i], 0, j))
    return pl.pallas_call(
        _moe_up_kernel,
        grid_spec=pltpu.PrefetchScalarGridSpec(
            num_scalar_prefetch=1,
            grid=(FFN_EXPERT // tn, N_BLK),
            in_specs=[pl.BlockSpec((MOE_BLOCK, D), lambda j, i, be: (i, 0)), wspec, wspec],
            out_specs=pl.BlockSpec((MOE_BLOCK, tn), lambda j, i, be: (i, j))),
        out_shape=jax.ShapeDtypeStruct((N_SLOT, FFN_EXPERT), BF16),
        compiler_params=_cparams(("arbitrary", "arbitrary")),
        name="moe_up",
    )(blk_e, xb, w1, w3)


def _moe_down_kernel(be_ref, h_ref, w2_ref, o_ref):
    o_ref[...] = jnp.dot(h_ref[...], w2_ref[0].astype(BF16), preferred_element_type=F32)


def moe_down(hb, w2, blk_e):
    tn = 256
    return pl.pallas_call(
        _moe_down_kernel,
        grid_spec=pltpu.PrefetchScalarGridSpec(
            num_scalar_prefetch=1,
            grid=(D // tn, N_BLK),
            in_specs=[pl.BlockSpec((MOE_BLOCK, FFN_EXPERT), lambda j, i, be: (i, 0)),
                      pl.BlockSpec((1, FFN_EXPERT, tn), lambda j, i, be: (be[i], 0, j))],
            out_specs=pl.BlockSpec((MOE_BLOCK, tn), lambda j, i, be: (i, j))),
        out_shape=jax.ShapeDtypeStruct((N_SLOT, D), F32),
        compiler_params=_cparams(("arbitrary", "arbitrary")),
        name="moe_down",
    )(blk_e, hb, w2)


def _combine_kernel(slot_ref, y_hbm, x_ref, gate_ref, g2_ref, o_ref, buf_ref, sem):
    i = pl.program_id(0)
    base = i * MOE_BLOCK * TOP_K

    def row_copy(r, kk):
        return pltpu.make_async_copy(y_hbm.at[pl.ds(slot_ref[base + r * TOP_K + kk], 1)],
                                     buf_ref.at[kk, pl.ds(r, 1)], sem)

    def start(r, _):
        for kk in range(TOP_K):
            row_copy(r, kk).start()
        return 0

    def wait(r, _):
        for kk in range(TOP_K):
            row_copy(r, kk).wait()
        return 0

    lax.fori_loop(0, MOE_BLOCK, start, 0)
    lax.fori_loop(0, MOE_BLOCK, wait, 0)
    b = i // (T_LAT // MOE_BLOCK)
    f = gate_ref[:, 0:1] * buf_ref[0] + gate_ref[:, 1:2] * buf_ref[1]
    o_ref[...] = x_ref[...] + g2_ref[pl.ds(b, 1), :] * f


def moe_combine(yb, x, slots, gates, modtab):
    per_b = T_LAT // MOE_BLOCK
    x_map = lambda i, s: ((i // per_b) * (S_ALL // MOE_BLOCK) + T_CTX // MOE_BLOCK + i % per_b, 0)
    return pl.pallas_call(
        _combine_kernel,
        grid_spec=pltpu.PrefetchScalarGridSpec(
            num_scalar_prefetch=1,
            grid=(N_TOK // MOE_BLOCK,),
            in_specs=[pl.BlockSpec(memory_space=pl.ANY),
                      pl.BlockSpec((MOE_BLOCK, D), x_map),
                      pl.BlockSpec((MOE_BLOCK, TOP_K), lambda i, s: (i, 0)),
                      pl.BlockSpec((8, D), lambda i, s: (0, 5))],
            out_specs=pl.BlockSpec((MOE_BLOCK, D), lambda i, s: (i, 0)),
            scratch_shapes=[pltpu.VMEM((TOP_K, MOE_BLOCK, D), F32), pltpu.SemaphoreType.DMA(())]),
        out_shape=jax.ShapeDtypeStruct((N_TOK, D), F32),
        compiler_params=_cparams(("arbitrary",)),
        name="moe_combine",
    )(slots, yb, x, gates, modtab)


def moe_routing(logits):
    top_v, top_e = lax.top_k(logits, TOP_K)
    gates = jax.nn.softmax(top_v, axis=-1)
    flat_e = top_e.reshape(-1)
    onehot = (flat_e[:, None] == jnp.arange(N_EXPERTS)[None, :]).astype(jnp.int32)
    rank = jnp.take_along_axis(jnp.cumsum(onehot, axis=0) - onehot, flat_e[:, None], axis=1)[:, 0]
    counts = onehot.sum(axis=0)
    padded = (counts + MOE_BLOCK - 1) // MOE_BLOCK * MOE_BLOCK
    pend = jnp.cumsum(padded)
    slot = (pend - padded)[flat_e] + rank
    tok = jnp.arange(N_TOK, dtype=jnp.int32)
    tok_row = (tok // T_LAT) * S_ALL + T_CTX + tok % T_LAT
    slot_row = jnp.zeros((N_SLOT,), jnp.int32).at[slot].set(jnp.repeat(tok_row, TOP_K))
    blk_e = jnp.minimum(jnp.searchsorted(pend, jnp.arange(N_BLK) * MOE_BLOCK, side='right'),
                        N_EXPERTS - 1).astype(jnp.int32)
    return slot_row, blk_e, slot.astype(jnp.int32), gates


def _pad_cols(w, width):
    return jnp.pad(w, ((0, 0), (0, width - w.shape[1])))


def layout_w_in(w):
    offs = np.cumsum((512, 512, 512, 512, 8, 8, 512, 1024, 16, 448, 160, 64, 512, 512, 512))[:-1].tolist()
    q, k, v, o, ig, fg, z, xbc, dt, cq, ckv, kr, naq, nak, nav = jnp.split(w, offs, axis=1)
    gate = _pad_cols(jnp.concatenate([ig, fg, dt], axis=1), 128)
    return jnp.concatenate([q, k, v, o, z, xbc, _pad_cols(cq, 512), _pad_cols(ckv, 256), _pad_cols(kr, 128),
                            naq, nak, nav, gate], axis=1)


def rope_tables():
    t = np.arange(T_LAT)
    n_freq = MLA_ROPE // 4
    freqs = ROPE_THETA ** (-jnp.arange(n_freq, dtype=F32) / n_freq)
    row = jnp.asarray(t // GRID_W, F32)
    col = jnp.asarray(t % GRID_W, F32)
    ang = jnp.concatenate([row[:, None] * freqs, col[:, None] * freqs], axis=-1)
    cos, sin = jnp.cos(ang), jnp.sin(ang)
    half = MLA_ROPE // 2
    zc = jnp.zeros((T_LAT, 128 - MLA_ROPE), F32)
    zh = jnp.zeros((T_LAT, half), F32)
    cs = jnp.concatenate([cos, cos, zc], axis=1)
    s1 = jnp.concatenate([-sin, zh, zc], axis=1)
    s2 = jnp.concatenate([zh, sin, zc], axis=1)
    ident = jnp.concatenate([jnp.ones((T_CTX, MLA_ROPE), F32), jnp.zeros((T_CTX, 128 - MLA_ROPE), F32)], axis=1)
    zeros = jnp.zeros((T_CTX, 128), F32)
    return (jnp.concatenate([ident, cs], axis=0), jnp.concatenate([zeros, s1], axis=0),
            jnp.concatenate([zeros, s2], axis=0))


def _gate_vectors(i_bias, f_bias, dt_bias, a_log):
    used = jnp.concatenate([i_bias.reshape(-1), f_bias.reshape(-1), dt_bias.reshape(-1)])
    alog = jnp.concatenate([jnp.zeros((G_DT,), F32), a_log.reshape(-1)])
    padc = lambda u: jnp.pad(u, (0, 128 - G_USED)).reshape(1, 128)
    return padc(used), used.reshape(G_USED, 1), padc(alog), alog.reshape(G_USED, 1)


def kernel(x, c, ctx, c_ctx, mod_w, mod_b, norm1, w_in, w_out, ml_i_bias, ml_f_bias, ml_norm, ssd_conv_w, ssd_conv_b, ssd_dt_bias, ssd_A_log, ssd_D, ssd_norm, mla_q_norm, mla_w_qb, mla_kv_norm, mla_w_kvb, mla_gq, mla_gk, na_gq, na_gk, na_rpb, norm2, ffn_w1, ffn_w3, ffn_w2, moe_router, moe_w1, moe_w3, moe_w2):
    xs = jnp.concatenate([ctx, x], axis=1).reshape(N_ROWS, D)
    c_all = jnp.concatenate([c, c_ctx[None, :], jnp.zeros((8 - NB - 1, D), F32)], axis=0)
    mod_all = modulation(c_all, mod_w, mod_b)
    tabs = rope_tables()
    out = None
    for l in range(DEPTH):
        modtab = mod_all[l]
        h = rms_modulate(xs, norm1[l], modtab, 0)
        p = matmul(h, [layout_w_in(w_in[l])], name="w_in")
        gates_t = p[:, C_GATE:C_GATE + G_USED].reshape(N_ROWS // CHUNK, CHUNK, G_USED).transpose(0, 2, 1)
        biasc, biasr, alogc, alogr = _gate_vectors(ml_i_bias[l], ml_f_bias[l], ssd_dt_bias[l], ssd_A_log[l])

        hf, hb = mlstm_scan(p, gates_t, biasc, biasr)
        ml = mlstm_finish(hf, hb, p, ml_norm[l])

        xbc = ssd_conv(p, ssd_conv_w[l], ssd_conv_b[l])
        yf, yb = ssd_scan(xbc, p, gates_t, biasc, biasr, alogc, alogr)
        ss = ssd_finish(yf, yb, xbc, p, ssd_D[l], ssd_norm[l])

        wq = jnp.pad(mla_w_qb[l].reshape(MLA_Q_RANK, MLA_HEADS, MLA_QK),
                     ((0, 512 - MLA_Q_RANK), (0, 0), (0, MLA_HP - MLA_QK))).reshape(512, MLA_HEADS * MLA_HP)
        wkv = jnp.pad(mla_w_kvb[l], ((0, 256 - MLA_KV_RANK), (0, 0)))
        pad1 = lambda u, w: jnp.pad(u, (0, w - u.shape[0])).reshape(1, w)
        q, k, v = mla_prep(p, wq, wkv, pad1(mla_q_norm[l], 512), pad1(mla_kv_norm[l], 256),
                           pad1(mla_gq[l], MLA_HP), pad1(mla_gk[l], MLA_HP), tabs)
        la = mla_attention(q, k, v)

        na = na_attention(p, na_gq[l], na_gk[l], na_bias_table(na_rpb[l]))

        mix = jnp.concatenate([ml, ss, la, na], axis=1)
        xs = matmul(mix, [w_out[l]], mode="resid", res=xs, modtab=modtab, gate_col=2, name="w_out")

        if l % 2 == 0:
            h2 = rms_modulate(xs, norm2[l], modtab, 3)
            hid = matmul(h2, [ffn_w1[l // 2], ffn_w3[l // 2]], mode="swiglu", out_dtype=BF16, name="ffn_up")
            xs = matmul(hid, [ffn_w2[l // 2]], mode="resid", res=xs, modtab=modtab, gate_col=5,
                        tk=1408, name="ffn_down")
        else:
            h2, logits = rms_modulate(xs, norm2[l], modtab, 3, router=_pad_cols(moe_router[l // 2], 128))
            lat = logits.reshape(NB, S_ALL, 128)[:, T_CTX:, :N_EXPERTS].reshape(N_TOK, N_EXPERTS)
            slot_row, blk_e, slots, gates = moe_routing(lat)
            xb = moe_gather(h2, slot_row)
            hid = moe_up(xb, moe_w1[l // 2], moe_w3[l // 2], blk_e)
            yb_ = moe_down(hid, moe_w2[l // 2], blk_e)
            out = moe_combine(yb_, xs, slots, gates, modtab)
    return out.reshape(NB, T_LAT, D)
```

```python
from functools import partial

import numpy as np
import jax
import jax.numpy as jnp
from jax import lax
from jax.experimental import pallas as pl
from jax.experimental.pallas import tpu as pltpu

F32 = jnp.float32
BF16 = jnp.bfloat16
HI = lax.Precision.HIGHEST

D = 2048
NB = 4
T_LAT = 2048
T_CTX = 256
S_ALL = T_LAT + T_CTX
N_ROWS = NB * S_ALL
DEPTH = 2
GRID_W = 64
EPS = 1e-6

ROWBLK = 256
BLK_PER_B = S_ALL // ROWBLK
CHUNK = 64
CH_PER_B = S_ALL // CHUNK
CH_CTX = T_CTX // CHUNK

ML_HEADS, ML_HD = 4, 128
SSD_HEADS, SSD_HD, SSD_GROUPS, SSD_STATE, SSD_CONV = 8, 64, 2, 128, 5
SSD_W = SSD_HEADS * SSD_HD
MLA_HEADS, MLA_NOPE, MLA_ROPE, MLA_V = 4, 128, 64, 128
MLA_QK = MLA_NOPE + MLA_ROPE
MLA_Q_RANK, MLA_KV_RANK = 448, 160
MLA_HP = 256
ROPE_THETA = 10000.0
NA_HEADS, NA_HD, NA_KH, NA_KW = 4, 128, 8, 16
N_EXPERTS, TOP_K = 8, 2

C_Q, C_K, C_V, C_O = 0, 512, 1024, 1536
C_Z = 2048
C_XBC = 2560
C_CQ, C_CKV, C_KR = 3584, 4096, 4352
C_NAQ, C_NAK, C_NAV = 4480, 4992, 5504
C_GATE = 6016
D_INP = 6144
G_I, G_F, G_DT, G_USED = 0, 8, 16, 32

VMEM_LIMIT = 56 * 1024 * 1024


def _cparams(sem):
    return pltpu.CompilerParams(dimension_semantics=sem, vmem_limit_bytes=VMEM_LIMIT)


def _mod_index(blk256):
    return jnp.where(blk256 % BLK_PER_B == 0, NB, blk256 // BLK_PER_B)


def _dot_nt(a, b):
    return lax.dot_general(a, b, (((1,), (1,)), ((), ())), preferred_element_type=F32)


def _dot_tn(a, b):
    return lax.dot_general(a, b, (((0,), (0,)), ((), ())), preferred_element_type=F32)


def _silu(x):
    return x * jax.nn.sigmoid(x)


def _softplus(x):
    return jnp.maximum(x, 0.0) + jnp.log(1.0 + jnp.exp(-jnp.abs(x)))


def _mod_kernel(c_ref, w_ref, b_ref, o_ref):
    o_ref[0] = jnp.dot(_silu(c_ref[...]), w_ref[0], preferred_element_type=F32) + b_ref[0]


def modulation(c_all, mod_w, mod_b):
    tn = 1024
    n_out = mod_w.shape[-1]
    return pl.pallas_call(
        _mod_kernel,
        grid=(DEPTH, n_out // tn),
        in_specs=[pl.BlockSpec((8, D), lambda l, j: (0, 0)),
                  pl.BlockSpec((1, D, tn), lambda l, j: (l, 0, j)),
                  pl.BlockSpec((1, 1, tn), lambda l, j: (l, 0, j))],
        out_specs=pl.BlockSpec((1, 8, tn), lambda l, j: (l, 0, j)),
        out_shape=jax.ShapeDtypeStruct((DEPTH, 8, n_out), F32),
        compiler_params=_cparams(("arbitrary", "arbitrary")),
        name="modulation",
    )(c_all, mod_w, mod_b.reshape(DEPTH, 1, n_out))


def _rmsmod_kernel(x_ref, g_ref, sh_ref, sc_ref, *rest, with_router):
    midx = _mod_index(pl.program_id(0))
    x = x_ref[...]
    y = x * lax.rsqrt(jnp.mean(x * x, axis=-1, keepdims=True) + EPS) * g_ref[...]
    h = y * (1.0 + sc_ref[pl.ds(midx, 1), :]) + sh_ref[pl.ds(midx, 1), :]
    if with_router:
        r_ref, o_ref, lg_ref = rest
        o_ref[...] = h
        lg_ref[...] = jnp.dot(h, r_ref[...], precision=HI, preferred_element_type=F32)
    else:
        (o_ref,) = rest
        o_ref[...] = h.astype(BF16)


def rms_modulate(x, g, modtab, which_shift, router=None):
    with_router = router is not None
    in_specs = [pl.BlockSpec((ROWBLK, D), lambda i: (i, 0)),
                pl.BlockSpec((1, D), lambda i: (0, 0)),
                pl.BlockSpec((8, D), lambda i: (0, which_shift)),
                pl.BlockSpec((8, D), lambda i: (0, which_shift + 1))]
    args = [x, g.reshape(1, D), modtab, modtab]
    if with_router:
        in_specs.append(pl.BlockSpec((D, 128), lambda i: (0, 0)))
        args.append(router)
        out_shape = [jax.ShapeDtypeStruct((N_ROWS, D), F32), jax.ShapeDtypeStruct((N_ROWS, 128), F32)]
        out_specs = [pl.BlockSpec((ROWBLK, D), lambda i: (i, 0)), pl.BlockSpec((ROWBLK, 128), lambda i: (i, 0))]
    else:
        out_shape = jax.ShapeDtypeStruct((N_ROWS, D), BF16)
        out_specs = pl.BlockSpec((ROWBLK, D), lambda i: (i, 0))
    return pl.pallas_call(
        partial(_rmsmod_kernel, with_router=with_router),
        grid=(N_ROWS // ROWBLK,),
        in_specs=in_specs, out_specs=out_specs, out_shape=out_shape,
        compiler_params=_cparams(("arbitrary",)),
        name="rms_modulate_router" if with_router else "rms_modulate",
    )(*args)


def _mm_kernel(*refs, mode, grouped, tm):
    nw = 2 if mode == "swiglu" else 1
    if grouped:
        be_ref, nu_ref = refs[:2]
        refs = refs[2:]
    a_ref = refs[0]
    w_refs = refs[1:1 + nw]
    pos = 1 + nw
    if mode == "resid":
        res_ref, gate_ref = refs[pos:pos + 2]
        pos += 2
    o_ref = refs[pos]
    ws_refs = refs[pos + 1:]
    i = pl.program_id(1)
    if grouped:
        fresh = jnp.logical_or(i == 0, be_ref[i] != be_ref[jnp.maximum(i - 1, 0)])
    else:
        fresh = i == 0

    @pl.when(fresh)
    def _():
        for w_ref, ws_ref in zip(w_refs, ws_refs):
            ws_ref[...] = (w_ref[0] if grouped else w_ref[...]).astype(BF16)

    def compute():
        a = a_ref[...]
        accs = [jnp.dot(a, ws_ref[...], preferred_element_type=F32) for ws_ref in ws_refs]
        if mode == "plain":
            o_ref[...] = accs[0].astype(o_ref.dtype)
        elif mode == "swiglu":
            o_ref[...] = (_silu(accs[0]) * accs[1]).astype(o_ref.dtype)
        else:
            for s in range(tm // ROWBLK):
                midx = _mod_index(i * (tm // ROWBLK) + s)
                rows = slice(s * ROWBLK, (s + 1) * ROWBLK)
                o_ref[rows, :] = res_ref[rows, :] + gate_ref[pl.ds(midx, 1), :] * accs[0][rows, :]

    if grouped:
        pl.when(i < nu_ref[0])(compute)

        @pl.when(i >= nu_ref[0])
        def _():
            o_ref[...] = jnp.zeros_like(o_ref)
    else:
        compute()


def matmul(a, ws, mode="plain", out_dtype=F32, res=None, modtab=None, gate_col=None,
           tm=768, tn=512, groups=None, w_buffers=2, name="matmul"):
    m, kdim = a.shape
    n = ws[0].shape[-1]
    grouped = groups is not None
    wmode = {} if w_buffers == 2 else dict(pipeline_mode=pl.Buffered(w_buffers))
    if grouped:
        row = lambda i, nu: jnp.minimum(i, nu[0] - 1)
        a_spec = pl.BlockSpec((tm, kdim), lambda j, i, be, nu: (row(i, nu), 0))
        w_spec = pl.BlockSpec((1, kdim, tn), lambda j, i, be, nu: (be[i], 0, j), **wmode)
        o_spec = pl.BlockSpec((tm, tn), lambda j, i, be, nu: (i, j))
    else:
        a_spec = pl.BlockSpec((tm, kdim), lambda j, i: (i, 0))
        w_spec = pl.BlockSpec((kdim, tn), lambda j, i: (0, j), **wmode)
        o_spec = pl.BlockSpec((tm, tn), lambda j, i: (i, j))
    in_specs = [a_spec] + [w_spec for _ in ws]
    args = [a, *ws]
    if mode == "resid":
        in_specs += [pl.BlockSpec((tm, tn), lambda j, i: (i, j)),
                     pl.BlockSpec((8, tn), lambda j, i: (0, gate_col * (D // tn) + j))]
        args += [res, modtab]
    grid_spec = pltpu.PrefetchScalarGridSpec(
        num_scalar_prefetch=2 if grouped else 0,
        grid=(n // tn, m // tm),
        in_specs=in_specs, out_specs=o_spec,
        scratch_shapes=[pltpu.VMEM((kdim, tn), BF16) for _ in ws])
    return pl.pallas_call(
        partial(_mm_kernel, mode=mode, grouped=grouped, tm=tm),
        grid_spec=grid_spec,
        out_shape=jax.ShapeDtypeStruct((m, n), out_dtype),
        compiler_params=_cparams(("arbitrary", "arbitrary")),
        name=name,
    )(*(list(groups) if grouped else []), *args)


def _fwd_chunk(i):
    return i


def _bwd_chunk(i):
    return jnp.where(i < CH_CTX, CH_CTX - 1 - i, CH_PER_B + CH_CTX - 1 - i)


def _tri_masks():
    r = lax.broadcasted_iota(jnp.int32, (CHUNK, CHUNK), 0)
    c = lax.broadcasted_iota(jnp.int32, (CHUNK, CHUNK), 1)
    return c <= r, c >= r


def _mlstm_kernel(qf_ref, kf_ref, vf_ref, gcf_ref, grf_ref,
                  qb_ref, kb_ref, vb_ref, gcb_ref, grb_ref,
                  biasc_ref, biasr_ref, hf_ref, hb_ref, c_ref, n_ref, m_ref):
    @pl.when(pl.program_id(1) == 0)
    def _():
        c_ref[...] = jnp.zeros_like(c_ref)
        n_ref[...] = jnp.zeros_like(n_ref)
        m_ref[...] = jnp.zeros_like(m_ref)

    low, upp = _tri_masks()
    lowf, uppf = low.astype(F32), upp.astype(F32)
    streams = ((0, qf_ref, kf_ref, vf_ref, gcf_ref, grf_ref, hf_ref, low, lowf, uppf, CHUNK - 1),
               (1, qb_ref, kb_ref, vb_ref, gcb_ref, grb_ref, hb_ref, upp, uppf, lowf, 0))
    for d, q_ref, k_ref, v_ref, gc_ref, gr_ref, h_ref, mask, tri_c, tri_r, last in streams:
        ac = gc_ref[...] + biasc_ref[...]
        ar = gr_ref[0] + biasr_ref[...]
        lfc = -_softplus(-ac)
        lfr = -_softplus(-ar)
        bc_all = jnp.dot(tri_c, lfc, precision=HI, preferred_element_type=F32)
        br_all = jnp.dot(lfr, tri_r, precision=HI, preferred_element_type=F32)
        for h in range(ML_HEADS):
            s_idx = d * ML_HEADS + h
            ji, jf = G_I + s_idx, G_F + s_idx
            cols = slice(h * ML_HD, (h + 1) * ML_HD)
            q = q_ref[:, cols] * (ML_HD ** -0.5)
            k = k_ref[:, cols]
            v = v_ref[:, cols]
            bc = bc_all[:, jf:jf + 1]
            br = br_all[jf:jf + 1, :]
            icc = ac[:, ji:ji + 1]
            icr = ar[ji:ji + 1, :]
            cmat = c_ref[s_idx]
            nvec = n_ref[s_idx]
            m_prev = m_ref[s_idx][:, 0:1]
            dmat = jnp.where(mask, bc - br + icr, -1e30)
            inter = bc + m_prev
            mt = jnp.maximum(inter, jnp.max(dmat, axis=1, keepdims=True))
            w_intra = jnp.exp(dmat - mt)
            w_state = jnp.exp(inter - mt)
            s = _dot_nt(q, k) * w_intra
            num = jnp.dot(s, v, preferred_element_type=F32) + w_state * _dot_nt(q, cmat)
            den = jnp.sum(s, axis=1, keepdims=True) + w_state * jnp.sum(q * nvec, axis=1, keepdims=True)
            h_ref[:, cols] = num / jnp.maximum(jnp.abs(den), jnp.exp(-mt))
            bl = bc[last:last + 1, :]
            g = bl - bc + icc
            m_new = jnp.maximum(bl + m_prev, jnp.max(g, axis=0, keepdims=True))
            wg = jnp.exp(g - m_new)
            wc = jnp.exp(bl + m_prev - m_new)
            c_ref[s_idx] = wc * cmat + _dot_tn(wg * v, k)
            n_ref[s_idx] = wc * nvec + jnp.sum(wg * k, axis=0, keepdims=True)
            m_ref[s_idx] = jnp.broadcast_to(m_new, (1, 128))


def mlstm_scan(p, gates_t, biasc, biasr):
    def rows(chunk_fn):
        return lambda b, i: b * CH_PER_B + chunk_fn(i)

    def stream_specs(chunk_fn):
        r = rows(chunk_fn)
        return [pl.BlockSpec((CHUNK, 512), lambda b, i: (r(b, i), C_Q // 512)),
                pl.BlockSpec((CHUNK, 512), lambda b, i: (r(b, i), C_K // 512)),
                pl.BlockSpec((CHUNK, 512), lambda b, i: (r(b, i), C_V // 512)),
                pl.BlockSpec((CHUNK, 128), lambda b, i: (r(b, i), C_GATE // 128)),
                pl.BlockSpec((1, G_USED, CHUNK), lambda b, i: (r(b, i), 0, 0))]

    rf, rb = rows(_fwd_chunk), rows(_bwd_chunk)
    n_streams = 2 * ML_HEADS
    return pl.pallas_call(
        _mlstm_kernel,
        grid=(NB, CH_PER_B),
        in_specs=stream_specs(_fwd_chunk) + stream_specs(_bwd_chunk) + [
            pl.BlockSpec((1, 128), lambda b, i: (0, 0)),
            pl.BlockSpec((G_USED, 1), lambda b, i: (0, 0))],
        out_specs=[pl.BlockSpec((CHUNK, 512), lambda b, i: (rf(b, i), 0)),
                   pl.BlockSpec((CHUNK, 512), lambda b, i: (rb(b, i), 0))],
        out_shape=[jax.ShapeDtypeStruct((N_ROWS, 512), F32)] * 2,
        scratch_shapes=[pltpu.VMEM((n_streams, ML_HD, ML_HD), F32),
                        pltpu.VMEM((n_streams, 1, ML_HD), F32),
                        pltpu.VMEM((n_streams, 1, 128), F32)],
        compiler_params=_cparams(("arbitrary", "arbitrary")),
        name="mlstm_scan",
    )(p, p, p, p, gates_t, p, p, p, p, gates_t, biasc, biasr)


def _mlstm_finish_kernel(hf_ref, hb_ref, o_ref, g_ref, out_ref):
    for h in range(ML_HEADS):
        cols = slice(h * ML_HD, (h + 1) * ML_HD)
        x = hf_ref[:, cols] + hb_ref[:, cols]
        y = x * lax.rsqrt(jnp.mean(x * x, axis=-1, keepdims=True) + EPS) * g_ref[:, cols]
        out_ref[:, cols] = (y * jax.nn.sigmoid(o_ref[:, cols])).astype(BF16)


def mlstm_finish(hf, hb, p, norm_g):
    blk = lambda c: pl.BlockSpec((ROWBLK, 512), lambda i: (i, c))
    return pl.pallas_call(
        _mlstm_finish_kernel,
        grid=(N_ROWS // ROWBLK,),
        in_specs=[blk(0), blk(0), blk(C_O // 512), pl.BlockSpec((1, 512), lambda i: (0, 0))],
        out_specs=blk(0),
        out_shape=jax.ShapeDtypeStruct((N_ROWS, 512), BF16),
        compiler_params=_cparams(("arbitrary",)),
        name="mlstm_finish",
    )(hf, hb, p, norm_g.reshape(1, 512))


_PAD = 8


def _ssd_conv_kernel(u_ref, w_ref, b_ref, o_ref, pad_ref):
    zeros = jnp.zeros((_PAD, u_ref.shape[1]), F32)
    segs = ((0, T_CTX, _PAD), (T_CTX, T_LAT, 2 * _PAD + T_CTX))
    pad_ref[0:_PAD] = zeros
    pad_ref[_PAD + T_CTX:2 * _PAD + T_CTX] = zeros
    pad_ref[2 * _PAD + S_ALL:3 * _PAD + S_ALL] = zeros
    for src, n, dst in segs:
        pad_ref[dst:dst + n] = u_ref[src:src + n]
    for src, n, dst in segs:
        acc = jnp.broadcast_to(b_ref[...], (n, u_ref.shape[1]))
        for j in range(SSD_CONV):
            lo = dst + j - SSD_CONV // 2
            acc = acc + w_ref[j:j + 1, :] * pad_ref[lo:lo + n]
        o_ref[src:src + n] = _silu(acc)


def ssd_conv(p, conv_w, conv_b):
    tc = 256
    return pl.pallas_call(
        _ssd_conv_kernel,
        grid=(NB, 1024 // tc),
        in_specs=[pl.BlockSpec((S_ALL, tc), lambda b, c: (b, C_XBC // tc + c)),
                  pl.BlockSpec((SSD_CONV, tc), lambda b, c: (0, c)),
                  pl.BlockSpec((1, tc), lambda b, c: (0, c))],
        out_specs=pl.BlockSpec((S_ALL, tc), lambda b, c: (b, c)),
        out_shape=jax.ShapeDtypeStruct((N_ROWS, 1024), F32),
        scratch_shapes=[pltpu.VMEM((S_ALL + 3 * _PAD, tc), F32)],
        compiler_params=_cparams(("arbitrary", "arbitrary")),
        name="ssd_conv",
    )(p, conv_w, conv_b.reshape(1, 1024))


def _ssd_kernel(xf_ref, bf_ref, cf_ref, gcf_ref, grf_ref,
                xb_ref, bb_ref, cb_ref, gcb_ref, grb_ref,
                biasc_ref, biasr_ref, alogc_ref, alogr_ref, yf_ref, yb_ref, st_ref):
    @pl.when(pl.program_id(1) == 0)
    def _():
        st_ref[...] = jnp.zeros_like(st_ref)

    low, upp = _tri_masks()
    lowf, uppf = low.astype(F32), upp.astype(F32)
    a_c = -jnp.exp(alogc_ref[...])
    a_r = -jnp.exp(alogr_ref[...])
    streams = ((0, xf_ref, bf_ref, cf_ref, gcf_ref, grf_ref, yf_ref, low, lowf, uppf, CHUNK - 1),
               (1, xb_ref, bb_ref, cb_ref, gcb_ref, grb_ref, yb_ref, upp, uppf, lowf, 0))
    for d, x_ref, b_ref, c_ref, gc_ref, gr_ref, y_ref, mask, tri_c, tri_r, last in streams:
        dtc_all = _softplus(gc_ref[...] + biasc_ref[...])
        dtr_all = _softplus(gr_ref[0] + biasr_ref[...])
        cumc_all = jnp.dot(tri_c, dtc_all * a_c, precision=HI, preferred_element_type=F32)
        cumr_all = jnp.dot(dtr_all * a_r, tri_r, precision=HI, preferred_element_type=F32)
        for g in range(SSD_GROUPS):
            gcols = slice(g * SSD_STATE, (g + 1) * SSD_STATE)
            bm = b_ref[:, gcols]
            cm = c_ref[:, gcols]
            cb = _dot_nt(cm, bm)
            for hh in range(SSD_HEADS // SSD_GROUPS):
                h = g * (SSD_HEADS // SSD_GROUPS) + hh
                j = G_DT + d * SSD_HEADS + h
                s_idx = d * SSD_HEADS + h
                cols = slice(h * SSD_HD, (h + 1) * SSD_HD)
                x = x_ref[:, cols]
                cc = cumc_all[:, j:j + 1]
                cr = cumr_all[j:j + 1, :]
                dtc = dtc_all[:, j:j + 1]
                dtr = dtr_all[j:j + 1, :]
                st = st_ref[s_idx]
                decay = jnp.exp(jnp.where(mask, cc - cr, -1e30))
                w = cb * decay * dtr
                y_ref[:, cols] = (jnp.dot(w, x, preferred_element_type=F32)
                                  + _dot_nt(cm, st) * jnp.exp(cc))
                cl = cc[last:last + 1, :]
                tail = jnp.exp(cl - cc) * dtc
                st_ref[s_idx] = st * jnp.exp(cl) + _dot_tn(tail * x, bm)


def ssd_scan(xbc, p, gates_t, biasc, biasr, alogc, alogr):
    def rows(chunk_fn):
        return lambda b, i: b * CH_PER_B + chunk_fn(i)

    def stream_specs(chunk_fn):
        r = rows(chunk_fn)
        return [pl.BlockSpec((CHUNK, 512), lambda b, i: (r(b, i), 0)),
                pl.BlockSpec((CHUNK, 256), lambda b, i: (r(b, i), 2)),
                pl.BlockSpec((CHUNK, 256), lambda b, i: (r(b, i), 3)),
                pl.BlockSpec((CHUNK, 128), lambda b, i: (r(b, i), C_GATE // 128)),
                pl.BlockSpec((1, G_USED, CHUNK), lambda b, i: (r(b, i), 0, 0))]

    rf, rb = rows(_fwd_chunk), rows(_bwd_chunk)
    const = lambda shape: pl.BlockSpec(shape, lambda b, i: (0, 0))
    return pl.pallas_call(
        _ssd_kernel,
        grid=(NB, CH_PER_B),
        in_specs=stream_specs(_fwd_chunk) + stream_specs(_bwd_chunk) + [
            const((1, 128)), const((G_USED, 1)), const((1, 128)), const((G_USED, 1))],
        out_specs=[pl.BlockSpec((CHUNK, 512), lambda b, i: (rf(b, i), 0)),
                   pl.BlockSpec((CHUNK, 512), lambda b, i: (rb(b, i), 0))],
        out_shape=[jax.ShapeDtypeStruct((N_ROWS, 512), F32)] * 2,
        scratch_shapes=[pltpu.VMEM((2 * SSD_HEADS, SSD_HD, SSD_STATE), F32)],
        compiler_params=_cparams(("arbitrary", "arbitrary")),
        name="ssd_scan",
    )(xbc, xbc, xbc, p, gates_t, xbc, xbc, xbc, p, gates_t, biasc, biasr, alogc, alogr)


def _ssd_finish_kernel(yf_ref, yb_ref, xs_ref, z_ref, dsk_ref, g_ref, out_ref):
    y = yf_ref[...] + yb_ref[...] + dsk_ref[...] * xs_ref[...]
    y = y * _silu(z_ref[...])
    out_ref[...] = (y * lax.rsqrt(jnp.mean(y * y, axis=-1, keepdims=True) + EPS) * g_ref[...]).astype(BF16)


def ssd_finish(yf, yb, xbc, p, d_skip, norm_g):
    blk = lambda c: pl.BlockSpec((ROWBLK, 512), lambda i: (i, c))
    vec = pl.BlockSpec((1, 512), lambda i: (0, 0))
    return pl.pallas_call(
        _ssd_finish_kernel,
        grid=(N_ROWS // ROWBLK,),
        in_specs=[blk(0), blk(0), blk(0), blk(C_Z // 512), vec, vec],
        out_specs=blk(0),
        out_shape=jax.ShapeDtypeStruct((N_ROWS, 512), BF16),
        compiler_params=_cparams(("arbitrary",)),
        name="ssd_finish",
    )(yf, yb, xbc, p, jnp.repeat(d_skip, SSD_HD).reshape(1, 512), norm_g.reshape(1, 512))


def _rope_tile(r, cs_ref, s1_ref, s2_ref):
    return (r * cs_ref[...] + pltpu.roll(r, 128 - MLA_ROPE // 2, 1) * s1_ref[...]
            + pltpu.roll(r, MLA_ROPE // 2, 1) * s2_ref[...])


def _mla_prep_kernel(cq_ref, ckv_ref, kr_ref, wq_ref, wkv_ref, qn_ref, kvn_ref, gq_ref, gk_ref,
                     cs_ref, s1_ref, s2_ref, q_out, k_out, v_out):
    cq = cq_ref[...]
    cqn = cq * lax.rsqrt(jnp.sum(cq * cq, axis=-1, keepdims=True) / MLA_Q_RANK + EPS) * qn_ref[...]
    q_raw = jnp.dot(cqn, wq_ref[...], preferred_element_type=F32)
    ckv = ckv_ref[...]
    ckvn = ckv * lax.rsqrt(jnp.sum(ckv * ckv, axis=-1, keepdims=True) / MLA_KV_RANK + EPS) * kvn_ref[...]
    kv_raw = jnp.dot(ckvn, wkv_ref[...], preferred_element_type=F32)
    kr = kr_ref[...]
    kr_ss = jnp.sum(kr * kr, axis=-1, keepdims=True)
    for h in range(MLA_HEADS):
        base = h * MLA_HP
        qa = q_raw[:, base:base + 128]
        qb = q_raw[:, base + 128:base + 256]
        q_scale = lax.rsqrt((jnp.sum(qa * qa, axis=-1, keepdims=True)
                             + jnp.sum(qb * qb, axis=-1, keepdims=True)) / MLA_QK + EPS)
        q_out[:, base:base + 128] = (qa * q_scale * gq_ref[:, 0:128]).astype(BF16)
        q_out[:, base + 128:base + 256] = _rope_tile(qb * q_scale * gq_ref[:, 128:256],
                                                     cs_ref, s1_ref, s2_ref).astype(BF16)
        kn = kv_raw[:, base:base + 128]
        k_scale = lax.rsqrt((jnp.sum(kn * kn, axis=-1, keepdims=True) + kr_ss) / MLA_QK + EPS)
        k_out[:, base:base + 128] = (kn * k_scale * gk_ref[:, 0:128]).astype(BF16)
        k_out[:, base + 128:base + 256] = _rope_tile(kr * k_scale * gk_ref[:, 128:256],
                                                     cs_ref, s1_ref, s2_ref).astype(BF16)
        v_out[:, h * MLA_V:(h + 1) * MLA_V] = kv_raw[:, base + 128:base + 256].astype(BF16)


def mla_prep(p, wq, wkv, qn, kvn, gq, gk, rope_tabs):
    const = lambda shape: pl.BlockSpec(shape, lambda i: (0, 0))
    tab = pl.BlockSpec((ROWBLK, 128), lambda i: (i % BLK_PER_B, 0))
    row = lambda w: pl.BlockSpec((ROWBLK, w), lambda i: (i, 0))
    return pl.pallas_call(
        _mla_prep_kernel,
        grid=(N_ROWS // ROWBLK,),
        in_specs=[pl.BlockSpec((ROWBLK, 512), lambda i: (i, C_CQ // 512)),
                  pl.BlockSpec((ROWBLK, 256), lambda i: (i, C_CKV // 256)),
                  pl.BlockSpec((ROWBLK, 128), lambda i: (i, C_KR // 128)),
                  const((512, MLA_HEADS * MLA_HP)), const((256, MLA_HEADS * MLA_HP)),
                  const((1, 512)), const((1, 256)), const((1, MLA_HP)), const((1, MLA_HP)),
                  tab, tab, tab],
        out_specs=[row(MLA_HEADS * MLA_HP), row(MLA_HEADS * MLA_HP), row(MLA_HEADS * MLA_V)],
        out_shape=[jax.ShapeDtypeStruct((N_ROWS, MLA_HEADS * MLA_HP), BF16),
                   jax.ShapeDtypeStruct((N_ROWS, MLA_HEADS * MLA_HP), BF16),
                   jax.ShapeDtypeStruct((N_ROWS, MLA_HEADS * MLA_V), BF16)],
        compiler_params=_cparams(("arbitrary",)),
        name="mla_prep",
    )(p, p, p, wq, wkv, qn, kvn, gq, gk, *rope_tabs)


def _softmax_pv(s_list, v_list):
    m = s_list[0].max(axis=-1, keepdims=True)
    for s in s_list[1:]:
        m = jnp.maximum(m, s.max(axis=-1, keepdims=True))
    ps = [jnp.exp(s - m) for s in s_list]
    l = sum(p.sum(axis=-1, keepdims=True) for p in ps)
    o = sum(jnp.dot(p.astype(BF16), v, preferred_element_type=F32) for p, v in zip(ps, v_list))
    return o / l


def _mla_attn_kernel(q_ref, k_ref, v_ref, o_ref):
    scale = MLA_QK ** -0.5
    q = q_ref[...]

    @pl.when(pl.program_id(2) == 0)
    def _():
        s = _dot_nt(q, k_ref[0:T_CTX, :]) * scale
        o_ref[...] = _softmax_pv([s], [v_ref[0:T_CTX, :]]).astype(BF16)

    @pl.when(pl.program_id(2) > 0)
    def _():
        s = _dot_nt(q, k_ref[...]) * scale
        o_ref[...] = _softmax_pv([s], [v_ref[...]]).astype(BF16)


def mla_attention(q, k, v):
    return pl.pallas_call(
        _mla_attn_kernel,
        grid=(NB, MLA_HEADS, BLK_PER_B),
        in_specs=[pl.BlockSpec((ROWBLK, MLA_HP), lambda b, h, i: (b * BLK_PER_B + i, h)),
                  pl.BlockSpec((S_ALL, MLA_HP), lambda b, h, i: (b, h)),
                  pl.BlockSpec((S_ALL, MLA_V), lambda b, h, i: (b, h))],
        out_specs=pl.BlockSpec((ROWBLK, MLA_V), lambda b, h, i: (b * BLK_PER_B + i, h)),
        out_shape=jax.ShapeDtypeStruct((N_ROWS, MLA_HEADS * MLA_V), BF16),
        compiler_params=_cparams(("arbitrary", "arbitrary", "arbitrary")),
        name="mla_attention",
    )(q, k, v)


N_GRID_ROWS = T_LAT // GRID_W
NA_QROWS = ROWBLK // GRID_W
NA_GROUPS = N_GRID_ROWS // NA_QROWS
NA_WIN_ROWS = NA_KH + NA_QROWS - 1
NA_WIN = NA_WIN_ROWS * GRID_W


def _na_win_row(g):
    return jnp.clip(g * NA_QROWS - NA_KH // 2, 0, N_GRID_ROWS - NA_WIN_ROWS)


def _na_kernel(q_ref, k_ref, v_ref, gq_ref, gk_ref, bias_ref, o_ref, kn_ref, vb_ref):
    st = pl.program_id(2)
    scale = NA_HD ** -0.5

    @pl.when(st == 0)
    def _():
        k = k_ref[...]
        kn_ref[...] = (k * lax.rsqrt(jnp.mean(k * k, axis=-1, keepdims=True) + EPS) * gk_ref[...]).astype(BF16)
        vb_ref[...] = v_ref[...].astype(BF16)

    q = q_ref[...]
    qn = (q * lax.rsqrt(jnp.mean(q * q, axis=-1, keepdims=True) + EPS) * gq_ref[...]).astype(BF16)
    s_cx = _dot_nt(qn, kn_ref[0:T_CTX, :]) * scale
    v_cx = vb_ref[0:T_CTX, :]

    @pl.when(st == 0)
    def _():
        o_ref[...] = _softmax_pv([s_cx], [v_cx]).astype(BF16)

    @pl.when(st > 0)
    def _():
        start = pl.multiple_of(T_CTX + _na_win_row(st - 1) * GRID_W, GRID_W)
        s_nb = _dot_nt(qn, kn_ref[pl.ds(start, NA_WIN), :]) * scale + bias_ref[0, 0]
        o_ref[...] = _softmax_pv([s_nb, s_cx], [vb_ref[pl.ds(start, NA_WIN), :], v_cx]).astype(BF16)


def _na_class(g):
    return jnp.where(g == 0, 0, jnp.where(g == NA_GROUPS - 1, 2, 1))


def na_attention(p, gq, gk, bias_tab):
    return pl.pallas_call(
        _na_kernel,
        grid=(NB, NA_HEADS, BLK_PER_B),
        in_specs=[pl.BlockSpec((ROWBLK, NA_HD), lambda b, h, st: (b * BLK_PER_B + st, C_NAQ // NA_HD + h)),
                  pl.BlockSpec((S_ALL, NA_HD), lambda b, h, st: (b, C_NAK // NA_HD + h)),
                  pl.BlockSpec((S_ALL, NA_HD), lambda b, h, st: (b, C_NAV // NA_HD + h)),
                  pl.BlockSpec((1, NA_HD), lambda b, h, st: (0, 0)),
                  pl.BlockSpec((1, NA_HD), lambda b, h, st: (0, 0)),
                  pl.BlockSpec((1, 1, ROWBLK, NA_WIN),
                               lambda b, h, st: (h, _na_class(jnp.maximum(st - 1, 0)), 0, 0))],
        out_specs=pl.BlockSpec((ROWBLK, NA_HD), lambda b, h, st: (b * BLK_PER_B + st, h)),
        out_shape=jax.ShapeDtypeStruct((N_ROWS, NA_HEADS * NA_HD), BF16),
        scratch_shapes=[pltpu.VMEM((S_ALL, NA_HD), BF16), pltpu.VMEM((S_ALL, NA_HD), BF16)],
        compiler_params=_cparams(("arbitrary", "arbitrary", "arbitrary")),
        name="na_attention",
    )(p, p, p, gq.reshape(1, NA_HD), gk.reshape(1, NA_HD), bias_tab)


def na_bias_table(rpb):
    g_rep = np.array([0, 1, NA_GROUPS - 1])
    r = g_rep[:, None] * NA_QROWS + np.arange(NA_QROWS)[None, :]
    r0 = np.clip(r - NA_KH // 2, 0, N_GRID_ROWS - NA_KH)
    w0 = np.clip(g_rep * NA_QROWS - NA_KH // 2, 0, N_GRID_ROWS - NA_WIN_ROWS)
    kr = w0[:, None] + np.arange(NA_WIN_ROWS)[None, :]
    valid_r = (kr[:, None, :] >= r0[:, :, None]) & (kr[:, None, :] < r0[:, :, None] + NA_KH)
    dr = np.clip(kr[:, None, :] - r[:, :, None] + NA_KH - 1, 0, 2 * NA_KH - 2)
    c = np.arange(GRID_W)[:, None]
    kc = np.arange(GRID_W)[None, :]
    c0 = np.clip(c - NA_KW // 2, 0, GRID_W - NA_KW)
    valid_c = (kc >= c0) & (kc < c0 + NA_KW)
    dc = np.clip(kc - c + NA_KW - 1, 0, 2 * NA_KW - 2)
    t = rpb[:, dr]
    t = t[..., dc]
    valid = valid_r[:, :, :, None, None] & valid_c[None, None, None]
    t = jnp.where(valid[None], t, -1e30)
    return t.transpose(0, 1, 2, 4, 3, 5).reshape(NA_HEADS, 3, ROWBLK, NA_WIN).astype(F32)


N_TOK = NB * T_LAT
MOE_TM = 512
N_BLK = N_TOK * TOP_K // MOE_TM + N_EXPERTS
N_SLOT = N_BLK * MOE_TM
GATHER_ROWS = 256
COMBINE_TOK = 128


def _row_copies(n_rows, copy_fn):
    def start_all():
        def body(r, carry):
            copy_fn(r).start()
            return carry
        lax.fori_loop(0, n_rows, body, 0, unroll=8)

    def wait_all():
        def body(r, carry):
            copy_fn(r).wait()
            return carry
        lax.fori_loop(0, n_rows, body, 0, unroll=8)

    return start_all, wait_all


def _gather_kernel(idx_ref, nu_ref, h_hbm, o_ref, buf_ref, sem):
    i = pl.program_id(0)
    n_steps = nu_ref[0] * (MOE_TM // GATHER_ROWS)

    def copies(step):
        slot = step % 2
        return _row_copies(GATHER_ROWS, lambda r: pltpu.make_async_copy(
            h_hbm.at[pl.ds(idx_ref[step * GATHER_ROWS + r], 1)], buf_ref.at[slot, pl.ds(r, 1)], sem.at[slot]))

    @pl.when(i == 0)
    def _():
        copies(i)[0]()

    @pl.when(i + 1 < n_steps)
    def _():
        copies(i + 1)[0]()

    @pl.when(i < n_steps)
    def _():
        copies(i)[1]()
        o_ref[...] = buf_ref[i % 2].astype(BF16)

    @pl.when(i >= n_steps)
    def _():
        o_ref[...] = jnp.zeros_like(o_ref)


def moe_gather(h2, slot_row, n_used):
    return pl.pallas_call(
        _gather_kernel,
        grid_spec=pltpu.PrefetchScalarGridSpec(
            num_scalar_prefetch=2,
            grid=(N_SLOT // GATHER_ROWS,),
            in_specs=[pl.BlockSpec(memory_space=pl.ANY)],
            out_specs=pl.BlockSpec((GATHER_ROWS, D), lambda i, idx, nu: (i, 0)),
            scratch_shapes=[pltpu.VMEM((2, GATHER_ROWS, D), F32), pltpu.SemaphoreType.DMA((2,))]),
        out_shape=jax.ShapeDtypeStruct((N_SLOT, D), BF16),
        compiler_params=_cparams(("arbitrary",)),
        name="moe_gather",
    )(slot_row, n_used, h2)


def _combine_kernel(slot_ref, y_hbm, x_ref, gate_ref, g2_ref, o_ref, buf_ref, sem):
    i = pl.program_id(0)

    def copies(step):
        slot = step % 2
        base = step * COMBINE_TOK * TOP_K
        return _row_copies(COMBINE_TOK * TOP_K, lambda r: pltpu.make_async_copy(
            y_hbm.at[pl.ds(slot_ref[base + r], 1)], buf_ref.at[slot, pl.ds(r, 1)], sem.at[slot]))

    @pl.when(i == 0)
    def _():
        copies(i)[0]()

    @pl.when(i + 1 < N_TOK // COMBINE_TOK)
    def _():
        copies(i + 1)[0]()

    copies(i)[1]()
    b = i // (T_LAT // COMBINE_TOK)
    rows = buf_ref[i % 2]
    f = gate_ref[:, 0:1] * rows[0:COMBINE_TOK] + gate_ref[:, 1:2] * rows[COMBINE_TOK:2 * COMBINE_TOK]
    o_ref[...] = x_ref[...] + g2_ref[pl.ds(b, 1), :] * f


def moe_combine(yb, x, slots_km, gates, modtab):
    per_b = T_LAT // COMBINE_TOK
    x_map = lambda i, s: ((i // per_b) * (S_ALL // COMBINE_TOK) + T_CTX // COMBINE_TOK + i % per_b, 0)
    return pl.pallas_call(
        _combine_kernel,
        grid_spec=pltpu.PrefetchScalarGridSpec(
            num_scalar_prefetch=1,
            grid=(N_TOK // COMBINE_TOK,),
            in_specs=[pl.BlockSpec(memory_space=pl.ANY),
                      pl.BlockSpec((COMBINE_TOK, D), x_map),
                      pl.BlockSpec((COMBINE_TOK, TOP_K), lambda i, s: (i, 0)),
                      pl.BlockSpec((8, D), lambda i, s: (0, 5))],
            out_specs=pl.BlockSpec((COMBINE_TOK, D), lambda i, s: (i, 0)),
            scratch_shapes=[pltpu.VMEM((2, COMBINE_TOK * TOP_K, D), F32), pltpu.SemaphoreType.DMA((2,))]),
        out_shape=jax.ShapeDtypeStruct((N_TOK, D), F32),
        compiler_params=_cparams(("arbitrary",)),
        name="moe_combine",
    )(slots_km, yb, x, gates, modtab)


def moe_routing(logits):
    top_v, top_e = lax.top_k(logits, TOP_K)
    gates = jax.nn.softmax(top_v, axis=-1)
    flat_e = top_e.reshape(-1)
    onehot = (flat_e[:, None] == jnp.arange(N_EXPERTS)[None, :]).astype(jnp.int32)
    rank = jnp.take_along_axis(jnp.cumsum(onehot, axis=0) - onehot, flat_e[:, None], axis=1)[:, 0]
    counts = onehot.sum(axis=0)
    padded = (counts + MOE_TM - 1) // MOE_TM * MOE_TM
    pend = jnp.cumsum(padded)
    slot = ((pend - padded)[flat_e] + rank).astype(jnp.int32)
    tok = jnp.arange(N_TOK, dtype=jnp.int32)
    tok_row = (tok // T_LAT) * S_ALL + T_CTX + tok % T_LAT
    slot_row = jnp.zeros((N_SLOT,), jnp.int32).at[slot].set(jnp.repeat(tok_row, TOP_K))
    n_used = (pend[-1] // MOE_TM).astype(jnp.int32)
    blk = jnp.minimum(jnp.arange(N_BLK), n_used - 1)
    blk_e = jnp.minimum(jnp.searchsorted(pend, blk * MOE_TM, side='right'), N_EXPERTS - 1).astype(jnp.int32)
    slots_km = slot.reshape(N_TOK // COMBINE_TOK, COMBINE_TOK, TOP_K).transpose(0, 2, 1).reshape(-1)
    return slot_row, blk_e, n_used.reshape(1), slots_km, gates


def _pad_cols(w, width):
    return jnp.pad(w, ((0, 0), (0, width - w.shape[1])))


def layout_w_in(w):
    offs = np.cumsum((512, 512, 512, 512, 8, 8, 512, 1024, 16, 448, 160, 64, 512, 512, 512))[:-1].tolist()
    q, k, v, o, ig, fg, z, xbc, dt, cq, ckv, kr, naq, nak, nav = jnp.split(w, offs, axis=1)
    gate = _pad_cols(jnp.concatenate([ig, fg, dt], axis=1), 128)
    return jnp.concatenate([q, k, v, o, z, xbc, _pad_cols(cq, 512), _pad_cols(ckv, 256), _pad_cols(kr, 128),
                            naq, nak, nav, gate], axis=1)


def rope_tables():
    t = np.arange(T_LAT)
    n_freq = MLA_ROPE // 4
    freqs = ROPE_THETA ** (-jnp.arange(n_freq, dtype=F32) / n_freq)
    row = jnp.asarray(t // GRID_W, F32)
    col = jnp.asarray(t % GRID_W, F32)
    ang = jnp.concatenate([row[:, None] * freqs, col[:, None] * freqs], axis=-1)
    cos, sin = jnp.cos(ang), jnp.sin(ang)
    half = MLA_ROPE // 2
    zc = jnp.zeros((T_LAT, 128 - MLA_ROPE), F32)
    zh = jnp.zeros((T_LAT, half), F32)
    cs = jnp.concatenate([cos, cos, zc], axis=1)
    s1 = jnp.concatenate([-sin, zh, zc], axis=1)
    s2 = jnp.concatenate([zh, sin, zc], axis=1)
    ident = jnp.concatenate([jnp.ones((T_CTX, MLA_ROPE), F32), jnp.zeros((T_CTX, 128 - MLA_ROPE), F32)], axis=1)
    zeros = jnp.zeros((T_CTX, 128), F32)
    return (jnp.concatenate([ident, cs], axis=0), jnp.concatenate([zeros, s1], axis=0),
            jnp.concatenate([zeros, s2], axis=0))


def _gate_vectors(i_bias, f_bias, dt_bias, a_log):
    used = jnp.concatenate([i_bias.reshape(-1), f_bias.reshape(-1), dt_bias.reshape(-1)])
    alog = jnp.concatenate([jnp.zeros((G_DT,), F32), a_log.reshape(-1)])
    padc = lambda u: jnp.pad(u, (0, 128 - G_USED)).reshape(1, 128)
    return padc(used), used.reshape(G_USED, 1), padc(alog), alog.reshape(G_USED, 1)


def kernel(x, c, ctx, c_ctx, mod_w, mod_b, norm1, w_in, w_out, ml_i_bias, ml_f_bias, ml_norm, ssd_conv_w, ssd_conv_b, ssd_dt_bias, ssd_A_log, ssd_D, ssd_norm, mla_q_norm, mla_w_qb, mla_kv_norm, mla_w_kvb, mla_gq, mla_gk, na_gq, na_gk, na_rpb, norm2, ffn_w1, ffn_w3, ffn_w2, moe_router, moe_w1, moe_w3, moe_w2):
    xs = jnp.concatenate([ctx, x], axis=1).reshape(N_ROWS, D)
    c_all = jnp.concatenate([c, c_ctx[None, :], jnp.zeros((8 - NB - 1, D), F32)], axis=0)
    mod_all = modulation(c_all, mod_w, mod_b)
    tabs = rope_tables()
    out = None
    for l in range(DEPTH):
        modtab = mod_all[l]
        h = rms_modulate(xs, norm1[l], modtab, 0)
        p = matmul(h, [layout_w_in(w_in[l])], tn=1024, name="w_in")
        gates_t = p[:, C_GATE:C_GATE + G_USED].reshape(N_ROWS // CHUNK, CHUNK, G_USED).transpose(0, 2, 1)
        biasc, biasr, alogc, alogr = _gate_vectors(ml_i_bias[l], ml_f_bias[l], ssd_dt_bias[l], ssd_A_log[l])

        hf, hb = mlstm_scan(p, gates_t, biasc, biasr)
        ml = mlstm_finish(hf, hb, p, ml_norm[l])

        xbc = ssd_conv(p, ssd_conv_w[l], ssd_conv_b[l])
        yf, yb = ssd_scan(xbc, p, gates_t, biasc, biasr, alogc, alogr)
        ss = ssd_finish(yf, yb, xbc, p, ssd_D[l], ssd_norm[l])

        wq = jnp.pad(mla_w_qb[l].reshape(MLA_Q_RANK, MLA_HEADS, MLA_QK),
                     ((0, 512 - MLA_Q_RANK), (0, 0), (0, MLA_HP - MLA_QK))).reshape(512, MLA_HEADS * MLA_HP)
        wkv = jnp.pad(mla_w_kvb[l], ((0, 256 - MLA_KV_RANK), (0, 0)))
        pad1 = lambda u, w: jnp.pad(u, (0, w - u.shape[0])).reshape(1, w)
        q, k, v = mla_prep(p, wq, wkv, pad1(mla_q_norm[l], 512), pad1(mla_kv_norm[l], 256),
                           pad1(mla_gq[l], MLA_HP), pad1(mla_gk[l], MLA_HP), tabs)
        la = mla_attention(q, k, v)

        na = na_attention(p, na_gq[l], na_gk[l], na_bias_table(na_rpb[l]))

        mix = jnp.concatenate([ml, ss, la, na], axis=1)
        xs = matmul(mix, [w_out[l]], mode="resid", res=xs, modtab=modtab, gate_col=2, name="w_out")

        if l % 2 == 0:
            h2 = rms_modulate(xs, norm2[l], modtab, 3)
            hid = matmul(h2, [ffn_w1[l // 2], ffn_w3[l // 2]], mode="swiglu", out_dtype=BF16, name="ffn_up")
            xs = matmul(hid, [ffn_w2[l // 2]], mode="resid", res=xs, modtab=modtab, gate_col=5,
                        w_buffers=1, name="ffn_down")
        else:
            h2, logits = rms_modulate(xs, norm2[l], modtab, 3, router=_pad_cols(moe_router[l // 2], 128))
            lat = logits.reshape(NB, S_ALL, 128)[:, T_CTX:, :N_EXPERTS].reshape(N_TOK, N_EXPERTS)
            slot_row, blk_e, n_used, slots_km, gates = moe_routing(lat)
            groups = (blk_e, n_used)
            xb = moe_gather(h2, slot_row, n_used)
            hid = matmul(xb, [moe_w1[l // 2], moe_w3[l // 2]], mode="swiglu", out_dtype=BF16,
                         tm=MOE_TM, groups=groups, name="moe_up")
            yb_ = matmul(hid, [moe_w2[l // 2]], tm=MOE_TM, tn=256, groups=groups, name="moe_down")
            out = moe_combine(yb_, xs, slots_km, gates, modtab)
    return out.reshape(NB, T_LAT, D)
```

```python
from functools import partial

import numpy as np
import jax
import jax.numpy as jnp
from jax import lax
from jax.experimental import pallas as pl
from jax.experimental.pallas import tpu as pltpu

F32 = jnp.float32
BF16 = jnp.bfloat16
HI = lax.Precision.HIGHEST

D = 2048
NB = 4
T_LAT = 2048
T_CTX = 256
S_ALL = T_LAT + T_CTX
N_ROWS = NB * S_ALL
DEPTH = 2
GRID_W = 64
EPS = 1e-6

ROWBLK = 256
BLK_PER_B = S_ALL // ROWBLK
CHUNK = 128
CH_PER_B = S_ALL // CHUNK
CH_CTX = T_CTX // CHUNK

ML_HEADS, ML_HD = 4, 128
SSD_HEADS, SSD_HD, SSD_GROUPS, SSD_STATE, SSD_CONV = 8, 64, 2, 128, 5
SSD_W = SSD_HEADS * SSD_HD
MLA_HEADS, MLA_NOPE, MLA_ROPE, MLA_V = 4, 128, 64, 128
MLA_QK = MLA_NOPE + MLA_ROPE
MLA_Q_RANK, MLA_KV_RANK = 448, 160
MLA_HP = 256
ROPE_THETA = 10000.0
NA_HEADS, NA_HD, NA_KH, NA_KW = 4, 128, 8, 16
N_EXPERTS, TOP_K = 8, 2

C_Q, C_K, C_V, C_O = 0, 512, 1024, 1536
C_Z = 2048
C_XBC = 2560
C_CQ, C_CKV, C_KR = 3584, 4096, 4352
C_NAQ, C_NAK, C_NAV = 4480, 4992, 5504
C_GATE = 6016
D_INP = 6144
G_I, G_F, G_DT, G_USED = 0, 8, 16, 32

VMEM_LIMIT = 56 * 1024 * 1024


def _cparams(sem):
    return pltpu.CompilerParams(dimension_semantics=sem, vmem_limit_bytes=VMEM_LIMIT)


def _mod_index(blk256):
    return jnp.where(blk256 % BLK_PER_B == 0, NB, blk256 // BLK_PER_B)


def _dot_nt(a, b):
    return lax.dot_general(a, b, (((1,), (1,)), ((), ())), preferred_element_type=F32)


def _dot_tn(a, b):
    return lax.dot_general(a, b, (((0,), (0,)), ((), ())), preferred_element_type=F32)


def _silu(x):
    return x * jax.nn.sigmoid(x)


def _softplus(x):
    return jnp.maximum(x, 0.0) + jnp.log(1.0 + jnp.exp(-jnp.abs(x)))


def _mod_kernel(c_ref, w_ref, b_ref, o_ref):
    o_ref[0] = jnp.dot(_silu(c_ref[...]), w_ref[0], preferred_element_type=F32) + b_ref[0]


def modulation(c_all, mod_w, mod_b):
    tn = 1024
    n_out = mod_w.shape[-1]
    return pl.pallas_call(
        _mod_kernel,
        grid=(DEPTH, n_out // tn),
        in_specs=[pl.BlockSpec((8, D), lambda l, j: (0, 0)),
                  pl.BlockSpec((1, D, tn), lambda l, j: (l, 0, j)),
                  pl.BlockSpec((1, 1, tn), lambda l, j: (l, 0, j))],
        out_specs=pl.BlockSpec((1, 8, tn), lambda l, j: (l, 0, j)),
        out_shape=jax.ShapeDtypeStruct((DEPTH, 8, n_out), F32),
        compiler_params=_cparams(("arbitrary", "arbitrary")),
        name="modulation",
    )(c_all, mod_w, mod_b.reshape(DEPTH, 1, n_out))


def _rmsmod_kernel(x_ref, g_ref, sh_ref, sc_ref, *rest, with_router):
    midx = _mod_index(pl.program_id(0))
    x = x_ref[...]
    y = x * lax.rsqrt(jnp.mean(x * x, axis=-1, keepdims=True) + EPS) * g_ref[...]
    h = y * (1.0 + sc_ref[pl.ds(midx, 1), :]) + sh_ref[pl.ds(midx, 1), :]
    if with_router:
        r_ref, o_ref, lg_ref = rest
        o_ref[...] = h
        lg_ref[...] = jnp.dot(h, r_ref[...], precision=HI, preferred_element_type=F32)
    else:
        (o_ref,) = rest
        o_ref[...] = h.astype(BF16)


def rms_modulate(x, g, modtab, which_shift, router=None):
    with_router = router is not None
    in_specs = [pl.BlockSpec((ROWBLK, D), lambda i: (i, 0)),
                pl.BlockSpec((1, D), lambda i: (0, 0)),
                pl.BlockSpec((8, D), lambda i: (0, which_shift)),
                pl.BlockSpec((8, D), lambda i: (0, which_shift + 1))]
    args = [x, g.reshape(1, D), modtab, modtab]
    if with_router:
        in_specs.append(pl.BlockSpec((D, 128), lambda i: (0, 0)))
        args.append(router)
        out_shape = [jax.ShapeDtypeStruct((N_ROWS, D), F32), jax.ShapeDtypeStruct((N_ROWS, 128), F32)]
        out_specs = [pl.BlockSpec((ROWBLK, D), lambda i: (i, 0)), pl.BlockSpec((ROWBLK, 128), lambda i: (i, 0))]
    else:
        out_shape = jax.ShapeDtypeStruct((N_ROWS, D), BF16)
        out_specs = pl.BlockSpec((ROWBLK, D), lambda i: (i, 0))
    return pl.pallas_call(
        partial(_rmsmod_kernel, with_router=with_router),
        grid=(N_ROWS // ROWBLK,),
        in_specs=in_specs, out_specs=out_specs, out_shape=out_shape,
        compiler_params=_cparams(("arbitrary",)),
        name="rms_modulate_router" if with_router else "rms_modulate",
    )(*args)


def _mm_epilogue(mode, accs, o_ref, i=None, tm=None, res_ref=None, gate_ref=None):
    if mode == "plain":
        o_ref[...] = accs[0].astype(o_ref.dtype)
    elif mode == "swiglu":
        o_ref[...] = (_silu(accs[0]) * accs[1]).astype(o_ref.dtype)
    else:
        for s in range(tm // ROWBLK):
            midx = _mod_index(i * (tm // ROWBLK) + s)
            rows = slice(s * ROWBLK, (s + 1) * ROWBLK)
            o_ref[rows, :] = res_ref[rows, :] + gate_ref[pl.ds(midx, 1), :] * accs[0][rows, :]


def _mm_kernel(*refs, mode, tm):
    nw = 2 if mode == "swiglu" else 1
    a_ref = refs[0]
    w_refs = refs[1:1 + nw]
    pos = 1 + nw
    res_ref = gate_ref = None
    if mode == "resid":
        res_ref, gate_ref = refs[pos:pos + 2]
        pos += 2
    o_ref = refs[pos]
    ws_refs = refs[pos + 1:]
    i = pl.program_id(1)

    @pl.when(i == 0)
    def _():
        for w_ref, ws_ref in zip(w_refs, ws_refs):
            ws_ref[...] = w_ref[...].astype(BF16)

    a = a_ref[...]
    accs = [jnp.dot(a, ws_ref[...], preferred_element_type=F32) for ws_ref in ws_refs]
    _mm_epilogue(mode, accs, o_ref, i, tm, res_ref, gate_ref)


def matmul(a, ws, mode="plain", out_dtype=F32, res=None, modtab=None, gate_col=None,
           tm=768, tn=512, w_buffers=2, name="matmul"):
    m, kdim = a.shape
    n = ws[0].shape[-1]
    wmode = {} if w_buffers == 2 else dict(pipeline_mode=pl.Buffered(w_buffers))
    in_specs = [pl.BlockSpec((tm, kdim), lambda j, i: (i, 0))]
    in_specs += [pl.BlockSpec((kdim, tn), lambda j, i: (0, j), **wmode) for _ in ws]
    args = [a, *ws]
    if mode == "resid":
        in_specs += [pl.BlockSpec((tm, tn), lambda j, i: (i, j)),
                     pl.BlockSpec((8, tn), lambda j, i: (0, gate_col * (D // tn) + j))]
        args += [res, modtab]
    return pl.pallas_call(
        partial(_mm_kernel, mode=mode, tm=tm),
        grid=(n // tn, m // tm),
        in_specs=in_specs,
        out_specs=pl.BlockSpec((tm, tn), lambda j, i: (i, j)),
        out_shape=jax.ShapeDtypeStruct((m, n), out_dtype),
        scratch_shapes=[pltpu.VMEM((kdim, tn), BF16) for _ in ws],
        compiler_params=_cparams(("arbitrary", "arbitrary")),
        name=name,
    )(*args)


def _gmm_kernel(be_ref, fresh_ref, run_ref, nexte_ref, lastrun_ref, meta_ref, a_ref, *rest, mode, nw, tn):
    w_hbm = rest[:nw]
    o_ref = rest[nw]
    stage_ref, ws_ref, sem = rest[nw + 1:]
    j = pl.program_id(0)
    i = pl.program_id(1)
    n_used, n_runs = meta_ref[0], meta_ref[1]

    def w_copy(widx, e, jj, slot):
        col = pl.multiple_of(jj * tn, tn)
        return pltpu.make_async_copy(w_hbm[widx].at[e, :, pl.ds(col, tn)], stage_ref.at[slot, widx],
                                     sem.at[slot, widx])

    @pl.when(fresh_ref[i] == 1)
    def _():
        g = j * n_runs + run_ref[i]
        slot = g % 2

        @pl.when(g == 0)
        def _():
            for widx in range(nw):
                w_copy(widx, be_ref[i], j, slot).start()

        for widx in range(nw):
            w_copy(widx, be_ref[i], j, slot).wait()
        is_last = lastrun_ref[i] == 1

        @pl.when(jnp.logical_not(jnp.logical_and(is_last, j == pl.num_programs(0) - 1)))
        def _():
            jn = jnp.where(is_last, j + 1, j)
            for widx in range(nw):
                w_copy(widx, nexte_ref[i], jn, 1 - slot).start()

        for widx in range(nw):
            ws_ref[widx] = stage_ref[slot, widx].astype(BF16)

    @pl.when(i < n_used)
    def _():
        a = a_ref[...]
        accs = [jnp.dot(a, ws_ref[widx], preferred_element_type=F32) for widx in range(nw)]
        _mm_epilogue(mode, accs, o_ref)

    @pl.when(i >= n_used)
    def _():
        o_ref[...] = jnp.zeros_like(o_ref)


def grouped_matmul(a, ws, sched, mode="plain", out_dtype=F32, tm=512, tn=512, name="grouped_matmul"):
    m, kdim = a.shape
    n = ws[0].shape[-1]
    nw = len(ws)
    row = lambda j, i, be, fr, ru, ne, lr, meta: jnp.minimum(i, meta[0] - 1)
    grid_spec = pltpu.PrefetchScalarGridSpec(
        num_scalar_prefetch=len(sched),
        grid=(n // tn, m // tm),
        in_specs=[pl.BlockSpec((tm, kdim), lambda *s: (row(*s), 0))] + [pl.BlockSpec(memory_space=pl.ANY)] * nw,
        out_specs=pl.BlockSpec((tm, tn), lambda j, i, *_: (i, j)),
        scratch_shapes=[pltpu.VMEM((2, nw, kdim, tn), F32), pltpu.VMEM((nw, kdim, tn), BF16),
                        pltpu.SemaphoreType.DMA((2, nw))])
    return pl.pallas_call(
        partial(_gmm_kernel, mode=mode, nw=nw, tn=tn),
        grid_spec=grid_spec,
        out_shape=jax.ShapeDtypeStruct((m, n), out_dtype),
        compiler_params=_cparams(("arbitrary", "arbitrary")),
        name=name,
    )(*sched, a, *ws)


def _fwd_chunk(i):
    return i


def _bwd_chunk(i):
    return jnp.where(i < CH_CTX, CH_CTX - 1 - i, CH_PER_B + CH_CTX - 1 - i)


def _tri_masks():
    r = lax.broadcasted_iota(jnp.int32, (CHUNK, CHUNK), 0)
    c = lax.broadcasted_iota(jnp.int32, (CHUNK, CHUNK), 1)
    return c <= r, c >= r


def _mlstm_kernel(qf_ref, kf_ref, vf_ref, gcf_ref, grf_ref,
                  qb_ref, kb_ref, vb_ref, gcb_ref, grb_ref,
                  biasc_ref, biasr_ref, hf_ref, hb_ref, c_ref, n_ref, m_ref):
    @pl.when(pl.program_id(1) == 0)
    def _():
        c_ref[...] = jnp.zeros_like(c_ref)
        n_ref[...] = jnp.zeros_like(n_ref)
        m_ref[...] = jnp.zeros_like(m_ref)

    low, upp = _tri_masks()
    lowf, uppf = low.astype(F32), upp.astype(F32)
    streams = ((0, qf_ref, kf_ref, vf_ref, gcf_ref, grf_ref, hf_ref, low, lowf, uppf, CHUNK - 1),
               (1, qb_ref, kb_ref, vb_ref, gcb_ref, grb_ref, hb_ref, upp, uppf, lowf, 0))
    for d, q_ref, k_ref, v_ref, gc_ref, gr_ref, h_ref, mask, tri_c, tri_r, last in streams:
        ac = gc_ref[...] + biasc_ref[...]
        ar = gr_ref[0] + biasr_ref[...]
        lfc = -_softplus(-ac)
        lfr = -_softplus(-ar)
        bc_all = jnp.dot(tri_c, lfc, precision=HI, preferred_element_type=F32)
        br_all = jnp.dot(lfr, tri_r, precision=HI, preferred_element_type=F32)
        for h in range(ML_HEADS):
            s_idx = d * ML_HEADS + h
            ji, jf = G_I + s_idx, G_F + s_idx
            cols = slice(h * ML_HD, (h + 1) * ML_HD)
            q = q_ref[:, cols] * (ML_HD ** -0.5)
            k = k_ref[:, cols]
            v = v_ref[:, cols]
            bc = bc_all[:, jf:jf + 1]
            br = br_all[jf:jf + 1, :]
            icc = ac[:, ji:ji + 1]
            icr = ar[ji:ji + 1, :]
            cmat = c_ref[s_idx]
            nvec = n_ref[s_idx]
            m_prev = m_ref[s_idx][:, 0:1]
            dmat = jnp.where(mask, bc - br + icr, -1e30)
            inter = bc + m_prev
            mt = jnp.maximum(inter, jnp.max(dmat, axis=1, keepdims=True))
            w_intra = jnp.exp(dmat - mt)
            w_state = jnp.exp(inter - mt)
            s = _dot_nt(q, k) * w_intra
            num = jnp.dot(s, v, preferred_element_type=F32) + w_state * _dot_nt(q, cmat)
            den = jnp.sum(s, axis=1, keepdims=True) + w_state * jnp.sum(q * nvec, axis=1, keepdims=True)
            h_ref[:, cols] = num / jnp.maximum(jnp.abs(den), jnp.exp(-mt))
            bl = bc[last:last + 1, :]
            g = bl - bc + icc
            m_new = jnp.maximum(bl + m_prev, jnp.max(g, axis=0, keepdims=True))
            wg = jnp.exp(g - m_new)
            wc = jnp.exp(bl + m_prev - m_new)
            c_ref[s_idx] = wc * cmat + _dot_tn(wg * v, k)
            n_ref[s_idx] = wc * nvec + jnp.sum(wg * k, axis=0, keepdims=True)
            m_ref[s_idx] = jnp.broadcast_to(m_new, (1, 128))


def mlstm_scan(p, gates_t, biasc, biasr):
    def rows(chunk_fn):
        return lambda b, i: b * CH_PER_B + chunk_fn(i)

    def stream_specs(chunk_fn):
        r = rows(chunk_fn)
        return [pl.BlockSpec((CHUNK, 512), lambda b, i: (r(b, i), C_Q // 512)),
                pl.BlockSpec((CHUNK, 512), lambda b, i: (r(b, i), C_K // 512)),
                pl.BlockSpec((CHUNK, 512), lambda b, i: (r(b, i), C_V // 512)),
                pl.BlockSpec((CHUNK, 128), lambda b, i: (r(b, i), C_GATE // 128)),
                pl.BlockSpec((1, G_USED, CHUNK), lambda b, i: (r(b, i), 0, 0))]

    rf, rb = rows(_fwd_chunk), rows(_bwd_chunk)
    n_streams = 2 * ML_HEADS
    return pl.pallas_call(
        _mlstm_kernel,
        grid=(NB, CH_PER_B),
        in_specs=stream_specs(_fwd_chunk) + stream_specs(_bwd_chunk) + [
            pl.BlockSpec((1, 128), lambda b, i: (0, 0)),
            pl.BlockSpec((G_USED, 1), lambda b, i: (0, 0))],
        out_specs=[pl.BlockSpec((CHUNK, 512), lambda b, i: (rf(b, i), 0)),
                   pl.BlockSpec((CHUNK, 512), lambda b, i: (rb(b, i), 0))],
        out_shape=[jax.ShapeDtypeStruct((N_ROWS, 512), F32)] * 2,
        scratch_shapes=[pltpu.VMEM((n_streams, ML_HD, ML_HD), F32),
                        pltpu.VMEM((n_streams, 1, ML_HD), F32),
                        pltpu.VMEM((n_streams, 1, 128), F32)],
        compiler_params=_cparams(("arbitrary", "arbitrary")),
        name="mlstm_scan",
    )(p, p, p, p, gates_t, p, p, p, p, gates_t, biasc, biasr)


def _mlstm_finish_kernel(hf_ref, hb_ref, o_ref, g_ref, out_ref):
    for h in range(ML_HEADS):
        cols = slice(h * ML_HD, (h + 1) * ML_HD)
        x = hf_ref[:, cols] + hb_ref[:, cols]
        y = x * lax.rsqrt(jnp.mean(x * x, axis=-1, keepdims=True) + EPS) * g_ref[:, cols]
        out_ref[:, cols] = (y * jax.nn.sigmoid(o_ref[:, cols])).astype(BF16)


def mlstm_finish(hf, hb, p, norm_g):
    blk = lambda c: pl.BlockSpec((ROWBLK, 512), lambda i: (i, c))
    return pl.pallas_call(
        _mlstm_finish_kernel,
        grid=(N_ROWS // ROWBLK,),
        in_specs=[blk(0), blk(0), blk(C_O // 512), pl.BlockSpec((1, 512), lambda i: (0, 0))],
        out_specs=blk(0),
        out_shape=jax.ShapeDtypeStruct((N_ROWS, 512), BF16),
        compiler_params=_cparams(("arbitrary",)),
        name="mlstm_finish",
    )(hf, hb, p, norm_g.reshape(1, 512))


_PAD = 8


def _ssd_conv_kernel(u_ref, w_ref, b_ref, o_ref, pad_ref):
    zeros = jnp.zeros((_PAD, u_ref.shape[1]), F32)
    segs = ((0, T_CTX, _PAD), (T_CTX, T_LAT, 2 * _PAD + T_CTX))
    pad_ref[0:_PAD] = zeros
    pad_ref[_PAD + T_CTX:2 * _PAD + T_CTX] = zeros
    pad_ref[2 * _PAD + S_ALL:3 * _PAD + S_ALL] = zeros
    for src, n, dst in segs:
        pad_ref[dst:dst + n] = u_ref[src:src + n]
    for src, n, dst in segs:
        acc = jnp.broadcast_to(b_ref[...], (n, u_ref.shape[1]))
        for j in range(SSD_CONV):
            lo = dst + j - SSD_CONV // 2
            acc = acc + w_ref[j:j + 1, :] * pad_ref[lo:lo + n]
        o_ref[src:src + n] = _silu(acc)


def ssd_conv(p, conv_w, conv_b):
    tc = 256
    return pl.pallas_call(
        _ssd_conv_kernel,
        grid=(NB, 1024 // tc),
        in_specs=[pl.BlockSpec((S_ALL, tc), lambda b, c: (b, C_XBC // tc + c)),
                  pl.BlockSpec((SSD_CONV, tc), lambda b, c: (0, c)),
                  pl.BlockSpec((1, tc), lambda b, c: (0, c))],
        out_specs=pl.BlockSpec((S_ALL, tc), lambda b, c: (b, c)),
        out_shape=jax.ShapeDtypeStruct((N_ROWS, 1024), F32),
        scratch_shapes=[pltpu.VMEM((S_ALL + 3 * _PAD, tc), F32)],
        compiler_params=_cparams(("arbitrary", "arbitrary")),
        name="ssd_conv",
    )(p, conv_w, conv_b.reshape(1, 1024))


def _ssd_kernel(xf_ref, bf_ref, cf_ref, gcf_ref, grf_ref,
                xb_ref, bb_ref, cb_ref, gcb_ref, grb_ref,
                biasc_ref, biasr_ref, alogc_ref, alogr_ref, yf_ref, yb_ref, st_ref):
    @pl.when(pl.program_id(1) == 0)
    def _():
        st_ref[...] = jnp.zeros_like(st_ref)

    low, upp = _tri_masks()
    lowf, uppf = low.astype(F32), upp.astype(F32)
    a_c = -jnp.exp(alogc_ref[...])
    a_r = -jnp.exp(alogr_ref[...])
    streams = ((0, xf_ref, bf_ref, cf_ref, gcf_ref, grf_ref, yf_ref, low, lowf, uppf, CHUNK - 1),
               (1, xb_ref, bb_ref, cb_ref, gcb_ref, grb_ref, yb_ref, upp, uppf, lowf, 0))
    for d, x_ref, b_ref, c_ref, gc_ref, gr_ref, y_ref, mask, tri_c, tri_r, last in streams:
        dtc_all = _softplus(gc_ref[...] + biasc_ref[...])
        dtr_all = _softplus(gr_ref[0] + biasr_ref[...])
        cumc_all = jnp.dot(tri_c, dtc_all * a_c, precision=HI, preferred_element_type=F32)
        cumr_all = jnp.dot(dtr_all * a_r, tri_r, precision=HI, preferred_element_type=F32)
        for g in range(SSD_GROUPS):
            gcols = slice(g * SSD_STATE, (g + 1) * SSD_STATE)
            bm = b_ref[:, gcols]
            cm = c_ref[:, gcols]
            cb = _dot_nt(cm, bm)
            for hh in range(SSD_HEADS // SSD_GROUPS):
                h = g * (SSD_HEADS // SSD_GROUPS) + hh
                j = G_DT + d * SSD_HEADS + h
                s_idx = d * SSD_HEADS + h
                cols = slice(h * SSD_HD, (h + 1) * SSD_HD)
                x = x_ref[:, cols]
                cc = cumc_all[:, j:j + 1]
                cr = cumr_all[j:j + 1, :]
                dtc = dtc_all[:, j:j + 1]
                dtr = dtr_all[j:j + 1, :]
                st = st_ref[s_idx]
                decay = jnp.exp(jnp.where(mask, cc - cr, -1e30))
                w = cb * decay * dtr
                y_ref[:, cols] = (jnp.dot(w, x, preferred_element_type=F32)
                                  + _dot_nt(cm, st) * jnp.exp(cc))
                cl = cc[last:last + 1, :]
                tail = jnp.exp(cl - cc) * dtc
                st_ref[s_idx] = st * jnp.exp(cl) + _dot_tn(tail * x, bm)


def ssd_scan(xbc, p, gates_t, biasc, biasr, alogc, alogr):
    def rows(chunk_fn):
        return lambda b, i: b * CH_PER_B + chunk_fn(i)

    def stream_specs(chunk_fn):
        r = rows(chunk_fn)
        return [pl.BlockSpec((CHUNK, 512), lambda b, i: (r(b, i), 0)),
                pl.BlockSpec((CHUNK, 256), lambda b, i: (r(b, i), 2)),
                pl.BlockSpec((CHUNK, 256), lambda b, i: (r(b, i), 3)),
                pl.BlockSpec((CHUNK, 128), lambda b, i: (r(b, i), C_GATE // 128)),
                pl.BlockSpec((1, G_USED, CHUNK), lambda b, i: (r(b, i), 0, 0))]

    rf, rb = rows(_fwd_chunk), rows(_bwd_chunk)
    const = lambda shape: pl.BlockSpec(shape, lambda b, i: (0, 0))
    return pl.pallas_call(
        _ssd_kernel,
        grid=(NB, CH_PER_B),
        in_specs=stream_specs(_fwd_chunk) + stream_specs(_bwd_chunk) + [
            const((1, 128)), const((G_USED, 1)), const((1, 128)), const((G_USED, 1))],
        out_specs=[pl.BlockSpec((CHUNK, 512), lambda b, i: (rf(b, i), 0)),
                   pl.BlockSpec((CHUNK, 512), lambda b, i: (rb(b, i), 0))],
        out_shape=[jax.ShapeDtypeStruct((N_ROWS, 512), F32)] * 2,
        scratch_shapes=[pltpu.VMEM((2 * SSD_HEADS, SSD_HD, SSD_STATE), F32)],
        compiler_params=_cparams(("arbitrary", "arbitrary")),
        name="ssd_scan",
    )(xbc, xbc, xbc, p, gates_t, xbc, xbc, xbc, p, gates_t, biasc, biasr, alogc, alogr)


def _ssd_finish_kernel(yf_ref, yb_ref, xs_ref, z_ref, dsk_ref, g_ref, out_ref):
    y = yf_ref[...] + yb_ref[...] + dsk_ref[...] * xs_ref[...]
    y = y * _silu(z_ref[...])
    out_ref[...] = (y * lax.rsqrt(jnp.mean(y * y, axis=-1, keepdims=True) + EPS) * g_ref[...]).astype(BF16)


def ssd_finish(yf, yb, xbc, p, d_skip, norm_g):
    blk = lambda c: pl.BlockSpec((ROWBLK, 512), lambda i: (i, c))
    vec = pl.BlockSpec((1, 512), lambda i: (0, 0))
    return pl.pallas_call(
        _ssd_finish_kernel,
        grid=(N_ROWS // ROWBLK,),
        in_specs=[blk(0), blk(0), blk(0), blk(C_Z // 512), vec, vec],
        out_specs=blk(0),
        out_shape=jax.ShapeDtypeStruct((N_ROWS, 512), BF16),
        compiler_params=_cparams(("arbitrary",)),
        name="ssd_finish",
    )(yf, yb, xbc, p, jnp.repeat(d_skip, SSD_HD).reshape(1, 512), norm_g.reshape(1, 512))


def _rope_tile(r, cs_ref, s1_ref, s2_ref):
    return (r * cs_ref[...] + pltpu.roll(r, 128 - MLA_ROPE // 2, 1) * s1_ref[...]
            + pltpu.roll(r, MLA_ROPE // 2, 1) * s2_ref[...])


def _mla_prep_kernel(cq_ref, ckv_ref, kr_ref, wq_ref, wkv_ref, qn_ref, kvn_ref, gq_ref, gk_ref,
                     cs_ref, s1_ref, s2_ref, q_out, k_out, v_out):
    cq = cq_ref[...]
    cqn = cq * lax.rsqrt(jnp.sum(cq * cq, axis=-1, keepdims=True) / MLA_Q_RANK + EPS) * qn_ref[...]
    q_raw = jnp.dot(cqn, wq_ref[...], preferred_element_type=F32)
    ckv = ckv_ref[...]
    ckvn = ckv * lax.rsqrt(jnp.sum(ckv * ckv, axis=-1, keepdims=True) / MLA_KV_RANK + EPS) * kvn_ref[...]
    kv_raw = jnp.dot(ckvn, wkv_ref[...], preferred_element_type=F32)
    kr = kr_ref[...]
    kr_ss = jnp.sum(kr * kr, axis=-1, keepdims=True)
    for h in range(MLA_HEADS):
        base = h * MLA_HP
        qa = q_raw[:, base:base + 128]
        qb = q_raw[:, base + 128:base + 256]
        q_scale = lax.rsqrt((jnp.sum(qa * qa, axis=-1, keepdims=True)
                             + jnp.sum(qb * qb, axis=-1, keepdims=True)) / MLA_QK + EPS)
        q_out[:, base:base + 128] = (qa * q_scale * gq_ref[:, 0:128]).astype(BF16)
        q_out[:, base + 128:base + 256] = _rope_tile(qb * q_scale * gq_ref[:, 128:256],
                                                     cs_ref, s1_ref, s2_ref).astype(BF16)
        kn = kv_raw[:, base:base + 128]
        k_scale = lax.rsqrt((jnp.sum(kn * kn, axis=-1, keepdims=True) + kr_ss) / MLA_QK + EPS)
        k_out[:, base:base + 128] = (kn * k_scale * gk_ref[:, 0:128]).astype(BF16)
        k_out[:, base + 128:base + 256] = _rope_tile(kr * k_scale * gk_ref[:, 128:256],
                                                     cs_ref, s1_ref, s2_ref).astype(BF16)
        v_out[:, h * MLA_V:(h + 1) * MLA_V] = kv_raw[:, base + 128:base + 256].astype(BF16)


def mla_prep(p, wq, wkv, qn, kvn, gq, gk, rope_tabs):
    const = lambda shape: pl.BlockSpec(shape, lambda i: (0, 0))
    tab = pl.BlockSpec((ROWBLK, 128), lambda i: (i % BLK_PER_B, 0))
    row = lambda w: pl.BlockSpec((ROWBLK, w), lambda i: (i, 0))
    return pl.pallas_call(
        _mla_prep_kernel,
        grid=(N_ROWS // ROWBLK,),
        in_specs=[pl.BlockSpec((ROWBLK, 512), lambda i: (i, C_CQ // 512)),
                  pl.BlockSpec((ROWBLK, 256), lambda i: (i, C_CKV // 256)),
                  pl.BlockSpec((ROWBLK, 128), lambda i: (i, C_KR // 128)),
                  const((512, MLA_HEADS * MLA_HP)), const((256, MLA_HEADS * MLA_HP)),
                  const((1, 512)), const((1, 256)), const((1, MLA_HP)), const((1, MLA_HP)),
                  tab, tab, tab],
        out_specs=[row(MLA_HEADS * MLA_HP), row(MLA_HEADS * MLA_HP), row(MLA_HEADS * MLA_V)],
        out_shape=[jax.ShapeDtypeStruct((N_ROWS, MLA_HEADS * MLA_HP), BF16),
                   jax.ShapeDtypeStruct((N_ROWS, MLA_HEADS * MLA_HP), BF16),
                   jax.ShapeDtypeStruct((N_ROWS, MLA_HEADS * MLA_V), BF16)],
        compiler_params=_cparams(("arbitrary",)),
        name="mla_prep",
    )(p, p, p, wq, wkv, qn, kvn, gq, gk, *rope_tabs)


def _softmax_pv(s_list, v_list):
    m = s_list[0].max(axis=-1, keepdims=True)
    for s in s_list[1:]:
        m = jnp.maximum(m, s.max(axis=-1, keepdims=True))
    ps = [jnp.exp(s - m) for s in s_list]
    l = sum(p.sum(axis=-1, keepdims=True) for p in ps)
    o = sum(jnp.dot(p.astype(BF16), v, preferred_element_type=F32) for p, v in zip(ps, v_list))
    return o / l


def _mla_attn_kernel(q_ref, k_ref, v_ref, o_ref):
    scale = MLA_QK ** -0.5
    q = q_ref[...]

    @pl.when(pl.program_id(2) == 0)
    def _():
        s = _dot_nt(q, k_ref[0:T_CTX, :]) * scale
        o_ref[...] = _softmax_pv([s], [v_ref[0:T_CTX, :]]).astype(BF16)

    @pl.when(pl.program_id(2) > 0)
    def _():
        s = _dot_nt(q, k_ref[...]) * scale
        o_ref[...] = _softmax_pv([s], [v_ref[...]]).astype(BF16)


def mla_attention(q, k, v):
    return pl.pallas_call(
        _mla_attn_kernel,
        grid=(NB, MLA_HEADS, BLK_PER_B),
        in_specs=[pl.BlockSpec((ROWBLK, MLA_HP), lambda b, h, i: (b * BLK_PER_B + i, h)),
                  pl.BlockSpec((S_ALL, MLA_HP), lambda b, h, i: (b, h)),
                  pl.BlockSpec((S_ALL, MLA_V), lambda b, h, i: (b, h))],
        out_specs=pl.BlockSpec((ROWBLK, MLA_V), lambda b, h, i: (b * BLK_PER_B + i, h)),
        out_shape=jax.ShapeDtypeStruct((N_ROWS, MLA_HEADS * MLA_V), BF16),
        compiler_params=_cparams(("arbitrary", "arbitrary", "arbitrary")),
        name="mla_attention",
    )(q, k, v)


N_GRID_ROWS = T_LAT // GRID_W
NA_QROWS = ROWBLK // GRID_W
NA_GROUPS = N_GRID_ROWS // NA_QROWS
NA_WIN_ROWS = NA_KH + NA_QROWS - 1
NA_WIN = NA_WIN_ROWS * GRID_W


def _na_win_row(g):
    return jnp.clip(g * NA_QROWS - NA_KH // 2, 0, N_GRID_ROWS - NA_WIN_ROWS)


def _na_kernel(q_ref, k_ref, v_ref, gq_ref, gk_ref, bias_ref, o_ref, kn_ref, vb_ref):
    st = pl.program_id(2)
    scale = NA_HD ** -0.5

    @pl.when(st == 0)
    def _():
        k = k_ref[...]
        kn_ref[...] = (k * lax.rsqrt(jnp.mean(k * k, axis=-1, keepdims=True) + EPS) * gk_ref[...]).astype(BF16)
        vb_ref[...] = v_ref[...].astype(BF16)

    q = q_ref[...]
    qn = (q * lax.rsqrt(jnp.mean(q * q, axis=-1, keepdims=True) + EPS) * gq_ref[...]).astype(BF16)
    s_cx = _dot_nt(qn, kn_ref[0:T_CTX, :]) * scale
    v_cx = vb_ref[0:T_CTX, :]

    @pl.when(st == 0)
    def _():
        o_ref[...] = _softmax_pv([s_cx], [v_cx]).astype(BF16)

    @pl.when(st > 0)
    def _():
        start = pl.multiple_of(T_CTX + _na_win_row(st - 1) * GRID_W, GRID_W)
        s_nb = _dot_nt(qn, kn_ref[pl.ds(start, NA_WIN), :]) * scale + bias_ref[0, 0]
        o_ref[...] = _softmax_pv([s_nb, s_cx], [vb_ref[pl.ds(start, NA_WIN), :], v_cx]).astype(BF16)


def _na_class(g):
    return jnp.where(g == 0, 0, jnp.where(g == NA_GROUPS - 1, 2, 1))


def na_attention(p, gq, gk, bias_tab):
    return pl.pallas_call(
        _na_kernel,
        grid=(NB, NA_HEADS, BLK_PER_B),
        in_specs=[pl.BlockSpec((ROWBLK, NA_HD), lambda b, h, st: (b * BLK_PER_B + st, C_NAQ // NA_HD + h)),
                  pl.BlockSpec((S_ALL, NA_HD), lambda b, h, st: (b, C_NAK // NA_HD + h)),
                  pl.BlockSpec((S_ALL, NA_HD), lambda b, h, st: (b, C_NAV // NA_HD + h)),
                  pl.BlockSpec((1, NA_HD), lambda b, h, st: (0, 0)),
                  pl.BlockSpec((1, NA_HD), lambda b, h, st: (0, 0)),
                  pl.BlockSpec((1, 1, ROWBLK, NA_WIN),
                               lambda b, h, st: (h, _na_class(jnp.maximum(st - 1, 0)), 0, 0))],
        out_specs=pl.BlockSpec((ROWBLK, NA_HD), lambda b, h, st: (b * BLK_PER_B + st, h)),
        out_shape=jax.ShapeDtypeStruct((N_ROWS, NA_HEADS * NA_HD), BF16),
        scratch_shapes=[pltpu.VMEM((S_ALL, NA_HD), BF16), pltpu.VMEM((S_ALL, NA_HD), BF16)],
        compiler_params=_cparams(("arbitrary", "arbitrary", "arbitrary")),
        name="na_attention",
    )(p, p, p, gq.reshape(1, NA_HD), gk.reshape(1, NA_HD), bias_tab)


def na_bias_table(rpb):
    g_rep = np.array([0, 1, NA_GROUPS - 1])
    r = g_rep[:, None] * NA_QROWS + np.arange(NA_QROWS)[None, :]
    r0 = np.clip(r - NA_KH // 2, 0, N_GRID_ROWS - NA_KH)
    w0 = np.clip(g_rep * NA_QROWS - NA_KH // 2, 0, N_GRID_ROWS - NA_WIN_ROWS)
    kr = w0[:, None] + np.arange(NA_WIN_ROWS)[None, :]
    valid_r = (kr[:, None, :] >= r0[:, :, None]) & (kr[:, None, :] < r0[:, :, None] + NA_KH)
    dr = np.clip(kr[:, None, :] - r[:, :, None] + NA_KH - 1, 0, 2 * NA_KH - 2)
    c = np.arange(GRID_W)[:, None]
    kc = np.arange(GRID_W)[None, :]
    c0 = np.clip(c - NA_KW // 2, 0, GRID_W - NA_KW)
    valid_c = (kc >= c0) & (kc < c0 + NA_KW)
    dc = kc - c + NA_KW - 1
    n_dc = 2 * NA_KW - 1
    onehot = (np.arange(n_dc)[:, None, None] == dc[None]) & valid_c[None]
    rows = rpb[:, dr]
    t = jnp.einsum('hsajd,dck->hsacjk', rows, jnp.asarray(onehot, F32), precision=HI)
    valid = valid_r[:, :, None, :, None] & valid_c[None, None, :, None, :]
    t = jnp.where(valid[None], t, -1e30)
    return t.reshape(NA_HEADS, 3, ROWBLK, NA_WIN)


N_TOK = NB * T_LAT
MOE_TM = 512
N_BLK = N_TOK * TOP_K // MOE_TM + N_EXPERTS
N_SLOT = N_BLK * MOE_TM
GATHER_ROWS = 256
COMBINE_TOK = 128


def _row_copies(n_rows, copy_fn):
    def start_all():
        def body(r, carry):
            copy_fn(r).start()
            return carry
        lax.fori_loop(0, n_rows, body, 0, unroll=8)

    def wait_all():
        def body(r, carry):
            copy_fn(r).wait()
            return carry
        lax.fori_loop(0, n_rows, body, 0, unroll=8)

    return start_all, wait_all


def _gather_kernel(idx_ref, nu_ref, h_hbm, o_ref, buf_ref, sem):
    i = pl.program_id(0)
    n_steps = nu_ref[0] * (MOE_TM // GATHER_ROWS)

    def copies(step):
        slot = step % 2
        return _row_copies(GATHER_ROWS, lambda r: pltpu.make_async_copy(
            h_hbm.at[pl.ds(idx_ref[step * GATHER_ROWS + r], 1)], buf_ref.at[slot, pl.ds(r, 1)], sem.at[slot]))

    @pl.when(i == 0)
    def _():
        copies(i)[0]()

    @pl.when(i + 1 < n_steps)
    def _():
        copies(i + 1)[0]()

    @pl.when(i < n_steps)
    def _():
        copies(i)[1]()
        o_ref[...] = buf_ref[i % 2].astype(BF16)

    @pl.when(i >= n_steps)
    def _():
        o_ref[...] = jnp.zeros_like(o_ref)


def moe_gather(h2, slot_row, n_used):
    return pl.pallas_call(
        _gather_kernel,
        grid_spec=pltpu.PrefetchScalarGridSpec(
            num_scalar_prefetch=2,
            grid=(N_SLOT // GATHER_ROWS,),
            in_specs=[pl.BlockSpec(memory_space=pl.ANY)],
            out_specs=pl.BlockSpec((GATHER_ROWS, D), lambda i, idx, nu: (i, 0)),
            scratch_shapes=[pltpu.VMEM((2, GATHER_ROWS, D), F32), pltpu.SemaphoreType.DMA((2,))]),
        out_shape=jax.ShapeDtypeStruct((N_SLOT, D), BF16),
        compiler_params=_cparams(("arbitrary",)),
        name="moe_gather",
    )(slot_row, n_used, h2)


def _combine_kernel(slot_ref, y_hbm, x_ref, gate_ref, g2_ref, o_ref, buf_ref, sem):
    i = pl.program_id(0)

    def copies(step):
        slot = step % 2
        base = step * COMBINE_TOK * TOP_K
        return _row_copies(COMBINE_TOK * TOP_K, lambda r: pltpu.make_async_copy(
            y_hbm.at[pl.ds(slot_ref[base + r], 1)], buf_ref.at[slot, pl.ds(r, 1)], sem.at[slot]))

    @pl.when(i == 0)
    def _():
        copies(i)[0]()

    @pl.when(i + 1 < N_TOK // COMBINE_TOK)
    def _():
        copies(i + 1)[0]()

    copies(i)[1]()
    b = i // (T_LAT // COMBINE_TOK)
    rows = buf_ref[i % 2]
    f = gate_ref[:, 0:1] * rows[0:COMBINE_TOK] + gate_ref[:, 1:2] * rows[COMBINE_TOK:2 * COMBINE_TOK]
    o_ref[...] = x_ref[...] + g2_ref[pl.ds(b, 1), :] * f


def moe_combine(yb, x, slots_km, gates, modtab):
    per_b = T_LAT // COMBINE_TOK
    x_map = lambda i, s: ((i // per_b) * (S_ALL // COMBINE_TOK) + T_CTX // COMBINE_TOK + i % per_b, 0)
    return pl.pallas_call(
        _combine_kernel,
        grid_spec=pltpu.PrefetchScalarGridSpec(
            num_scalar_prefetch=1,
            grid=(N_TOK // COMBINE_TOK,),
            in_specs=[pl.BlockSpec(memory_space=pl.ANY),
                      pl.BlockSpec((COMBINE_TOK, D), x_map),
                      pl.BlockSpec((COMBINE_TOK, TOP_K), lambda i, s: (i, 0)),
                      pl.BlockSpec((8, D), lambda i, s: (0, 5))],
            out_specs=pl.BlockSpec((COMBINE_TOK, D), lambda i, s: (i, 0)),
            scratch_shapes=[pltpu.VMEM((2, COMBINE_TOK * TOP_K, D), F32), pltpu.SemaphoreType.DMA((2,))]),
        out_shape=jax.ShapeDtypeStruct((N_TOK, D), F32),
        compiler_params=_cparams(("arbitrary",)),
        name="moe_combine",
    )(slots_km, yb, x, gates, modtab)


def moe_routing(logits):
    top_v, top_e = lax.top_k(logits, TOP_K)
    gates = jax.nn.softmax(top_v, axis=-1)
    flat_e = top_e.reshape(-1)
    onehot = (flat_e[:, None] == jnp.arange(N_EXPERTS)[None, :]).astype(jnp.int32)
    rank = jnp.take_along_axis(jnp.cumsum(onehot, axis=0) - onehot, flat_e[:, None], axis=1)[:, 0]
    counts = onehot.sum(axis=0)
    padded = (counts + MOE_TM - 1) // MOE_TM * MOE_TM
    pend = jnp.cumsum(padded)
    slot = ((pend - padded)[flat_e] + rank).astype(jnp.int32)
    tok = jnp.arange(N_TOK, dtype=jnp.int32)
    tok_row = (tok // T_LAT) * S_ALL + T_CTX + tok % T_LAT
    slot_row = jnp.zeros((N_SLOT,), jnp.int32).at[slot].set(jnp.repeat(tok_row, TOP_K))
    n_used = (pend[-1] // MOE_TM).astype(jnp.int32)
    blk = jnp.arange(N_BLK)
    used = blk < n_used
    blk_e = jnp.sum(jnp.minimum(blk, n_used - 1)[:, None] * MOE_TM >= pend[None, :], axis=1).astype(jnp.int32)
    fresh = (used & ((blk == 0) | (blk_e != jnp.roll(blk_e, 1)))).astype(jnp.int32)
    run_idx = jnp.cumsum(fresh) - 1
    n_runs = fresh.sum()
    has = counts > 0
    ids = jnp.arange(N_EXPERTS)
    first_e = jnp.min(jnp.where(has, ids, N_EXPERTS))
    later = jnp.where(has[None, :] & (ids[None, :] > ids[:, None]), ids[None, :], N_EXPERTS).min(axis=1)
    next_of = jnp.where(later < N_EXPERTS, later, first_e)
    sched = (blk_e, fresh, run_idx.astype(jnp.int32), next_of[blk_e].astype(jnp.int32),
             (run_idx == n_runs - 1).astype(jnp.int32), jnp.stack([n_used, n_runs]).astype(jnp.int32))
    slots_km = slot.reshape(N_TOK // COMBINE_TOK, COMBINE_TOK, TOP_K).transpose(0, 2, 1).reshape(-1)
    return slot_row, sched, n_used.reshape(1), slots_km, gates


def _pad_cols(w, width):
    return jnp.pad(w, ((0, 0), (0, width - w.shape[1])))


def layout_w_in(w):
    offs = np.cumsum((512, 512, 512, 512, 8, 8, 512, 1024, 16, 448, 160, 64, 512, 512, 512))[:-1].tolist()
    q, k, v, o, ig, fg, z, xbc, dt, cq, ckv, kr, naq, nak, nav = jnp.split(w, offs, axis=1)
    gate = _pad_cols(jnp.concatenate([ig, fg, dt], axis=1), 128)
    return jnp.concatenate([q, k, v, o, z, xbc, _pad_cols(cq, 512), _pad_cols(ckv, 256), _pad_cols(kr, 128),
                            naq, nak, nav, gate], axis=1)


def rope_tables():
    t = np.arange(T_LAT)
    n_freq = MLA_ROPE // 4
    freqs = ROPE_THETA ** (-jnp.arange(n_freq, dtype=F32) / n_freq)
    row = jnp.asarray(t // GRID_W, F32)
    col = jnp.asarray(t % GRID_W, F32)
    ang = jnp.concatenate([row[:, None] * freqs, col[:, None] * freqs], axis=-1)
    cos, sin = jnp.cos(ang), jnp.sin(ang)
    half = MLA_ROPE // 2
    zc = jnp.zeros((T_LAT, 128 - MLA_ROPE), F32)
    zh = jnp.zeros((T_LAT, half), F32)
    cs = jnp.concatenate([cos, cos, zc], axis=1)
    s1 = jnp.concatenate([-sin, zh, zc], axis=1)
    s2 = jnp.concatenate([zh, sin, zc], axis=1)
    ident = jnp.concatenate([jnp.ones((T_CTX, MLA_ROPE), F32), jnp.zeros((T_CTX, 128 - MLA_ROPE), F32)], axis=1)
    zeros = jnp.zeros((T_CTX, 128), F32)
    return (jnp.concatenate([ident, cs], axis=0), jnp.concatenate([zeros, s1], axis=0),
            jnp.concatenate([zeros, s2], axis=0))


def _gate_vectors(i_bias, f_bias, dt_bias, a_log):
    used = jnp.concatenate([i_bias.reshape(-1), f_bias.reshape(-1), dt_bias.reshape(-1)])
    alog = jnp.concatenate([jnp.zeros((G_DT,), F32), a_log.reshape(-1)])
    padc = lambda u: jnp.pad(u, (0, 128 - G_USED)).reshape(1, 128)
    return padc(used), used.reshape(G_USED, 1), padc(alog), alog.reshape(G_USED, 1)


def kernel(x, c, ctx, c_ctx, mod_w, mod_b, norm1, w_in, w_out, ml_i_bias, ml_f_bias, ml_norm, ssd_conv_w, ssd_conv_b, ssd_dt_bias, ssd_A_log, ssd_D, ssd_norm, mla_q_norm, mla_w_qb, mla_kv_norm, mla_w_kvb, mla_gq, mla_gk, na_gq, na_gk, na_rpb, norm2, ffn_w1, ffn_w3, ffn_w2, moe_router, moe_w1, moe_w3, moe_w2):
    xs = jnp.concatenate([ctx, x], axis=1).reshape(N_ROWS, D)
    c_all = jnp.concatenate([c, c_ctx[None, :], jnp.zeros((8 - NB - 1, D), F32)], axis=0)
    mod_all = modulation(c_all, mod_w, mod_b)
    tabs = rope_tables()
    out = None
    for l in range(DEPTH):
        modtab = mod_all[l]
        h = rms_modulate(xs, norm1[l], modtab, 0)
        p = matmul(h, [layout_w_in(w_in[l])], tn=1024, name="w_in")
        gates_t = p[:, C_GATE:C_GATE + G_USED].reshape(N_ROWS // CHUNK, CHUNK, G_USED).transpose(0, 2, 1)
        biasc, biasr, alogc, alogr = _gate_vectors(ml_i_bias[l], ml_f_bias[l], ssd_dt_bias[l], ssd_A_log[l])

        hf, hb = mlstm_scan(p, gates_t, biasc, biasr)
        ml = mlstm_finish(hf, hb, p, ml_norm[l])

        xbc = ssd_conv(p, ssd_conv_w[l], ssd_conv_b[l])
        yf, yb = ssd_scan(xbc, p, gates_t, biasc, biasr, alogc, alogr)
        ss = ssd_finish(yf, yb, xbc, p, ssd_D[l], ssd_norm[l])

        wq = jnp.pad(mla_w_qb[l].reshape(MLA_Q_RANK, MLA_HEADS, MLA_QK),
                     ((0, 512 - MLA_Q_RANK), (0, 0), (0, MLA_HP - MLA_QK))).reshape(512, MLA_HEADS * MLA_HP)
        wkv = jnp.pad(mla_w_kvb[l], ((0, 256 - MLA_KV_RANK), (0, 0)))
        pad1 = lambda u, w: jnp.pad(u, (0, w - u.shape[0])).reshape(1, w)
        q, k, v = mla_prep(p, wq, wkv, pad1(mla_q_norm[l], 512), pad1(mla_kv_norm[l], 256),
                           pad1(mla_gq[l], MLA_HP), pad1(mla_gk[l], MLA_HP), tabs)
        la = mla_attention(q, k, v)

        na = na_attention(p, na_gq[l], na_gk[l], na_bias_table(na_rpb[l]))

        mix = jnp.concatenate([ml, ss, la, na], axis=1)
        xs = matmul(mix, [w_out[l]], mode="resid", res=xs, modtab=modtab, gate_col=2, name="w_out")

        if l % 2 == 0:
            h2 = rms_modulate(xs, norm2[l], modtab, 3)
            hid = matmul(h2, [ffn_w1[l // 2], ffn_w3[l // 2]], mode="swiglu", out_dtype=BF16, name="ffn_up")
            xs = matmul(hid, [ffn_w2[l // 2]], mode="resid", res=xs, modtab=modtab, gate_col=5,
                        w_buffers=1, name="ffn_down")
        else:
            h2, logits = rms_modulate(xs, norm2[l], modtab, 3, router=_pad_cols(moe_router[l // 2], 128))
            lat = logits.reshape(NB, S_ALL, 128)[:, T_CTX:, :N_EXPERTS].reshape(N_TOK, N_EXPERTS)
            slot_row, sched, n_used, slots_km, gates = moe_routing(lat)
            xb = moe_gather(h2, slot_row, n_used)
            hid = grouped_matmul(xb, [moe_w1[l // 2], moe_w3[l // 2]], sched, mode="swiglu", out_dtype=BF16,
                                 tm=MOE_TM, name="moe_up")
            yb_ = grouped_matmul(hid, [moe_w2[l // 2]], sched, tm=MOE_TM, tn=256, name="moe_down")
            out = moe_combine(yb_, xs, slots_km, gates, modtab)
    return out.reshape(NB, T_LAT, D)
```

```python
from functools import partial

import numpy as np
import jax
import jax.numpy as jnp
from jax import lax
from jax.experimental import pallas as pl
from jax.experimental.pallas import tpu as pltpu

F32 = jnp.float32
BF16 = jnp.bfloat16
HI = lax.Precision.HIGHEST

D = 2048
NB = 4
T_LAT = 2048
T_CTX = 256
S_ALL = T_LAT + T_CTX
N_ROWS = NB * S_ALL
DEPTH = 2
GRID_W = 64
EPS = 1e-6

ROWBLK = 256
BLK_PER_B = S_ALL // ROWBLK
CHUNK = 128
CH_PER_B = S_ALL // CHUNK
CH_CTX = T_CTX // CHUNK

ML_HEADS, ML_HD = 4, 128
SSD_HEADS, SSD_HD, SSD_GROUPS, SSD_STATE, SSD_CONV = 8, 64, 2, 128, 5
SSD_W = SSD_HEADS * SSD_HD
MLA_HEADS, MLA_NOPE, MLA_ROPE, MLA_V = 4, 128, 64, 128
MLA_QK = MLA_NOPE + MLA_ROPE
MLA_Q_RANK, MLA_KV_RANK = 448, 160
MLA_HP = 256
ROPE_THETA = 10000.0
NA_HEADS, NA_HD, NA_KH, NA_KW = 4, 128, 8, 16
N_EXPERTS, TOP_K = 8, 2

C_Q, C_K, C_V, C_O = 0, 512, 1024, 1536
C_Z = 2048
C_XBC = 2560
C_CQ, C_CKV, C_KR = 3584, 4096, 4352
C_NAQ, C_NAK, C_NAV = 4480, 4992, 5504
C_GATE = 6016
D_INP = 6144
G_I, G_F, G_DT, G_USED = 0, 8, 16, 32

VMEM_LIMIT = 56 * 1024 * 1024


def _cparams(sem):
    return pltpu.CompilerParams(dimension_semantics=sem, vmem_limit_bytes=VMEM_LIMIT)


def _mod_index(blk256):
    return jnp.where(blk256 % BLK_PER_B == 0, NB, blk256 // BLK_PER_B)


def _dot_nt(a, b):
    return lax.dot_general(a, b, (((1,), (1,)), ((), ())), preferred_element_type=F32)


def _dot_tn(a, b):
    return lax.dot_general(a, b, (((0,), (0,)), ((), ())), preferred_element_type=F32)


def _silu(x):
    return x * jax.nn.sigmoid(x)


def _softplus(x):
    return jnp.maximum(x, 0.0) + jnp.log(1.0 + jnp.exp(-jnp.abs(x)))


def _mod_kernel(c_ref, w_ref, b_ref, o_ref):
    o_ref[0] = jnp.dot(_silu(c_ref[...]), w_ref[0], preferred_element_type=F32) + b_ref[0]


def modulation(c_all, mod_w, mod_b):
    tn = 1024
    n_out = mod_w.shape[-1]
    return pl.pallas_call(
        _mod_kernel,
        grid=(DEPTH, n_out // tn),
        in_specs=[pl.BlockSpec((8, D), lambda l, j: (0, 0)),
                  pl.BlockSpec((1, D, tn), lambda l, j: (l, 0, j)),
                  pl.BlockSpec((1, 1, tn), lambda l, j: (l, 0, j))],
        out_specs=pl.BlockSpec((1, 8, tn), lambda l, j: (l, 0, j)),
        out_shape=jax.ShapeDtypeStruct((DEPTH, 8, n_out), F32),
        compiler_params=_cparams(("arbitrary", "arbitrary")),
        name="modulation",
    )(c_all, mod_w, mod_b.reshape(DEPTH, 1, n_out))


def _rmsmod_kernel(x_ref, g_ref, sh_ref, sc_ref, *rest, with_router):
    midx = _mod_index(pl.program_id(0))
    x = x_ref[...]
    y = x * lax.rsqrt(jnp.mean(x * x, axis=-1, keepdims=True) + EPS) * g_ref[...]
    h = y * (1.0 + sc_ref[pl.ds(midx, 1), :]) + sh_ref[pl.ds(midx, 1), :]
    if with_router:
        r_ref, o_ref, lg_ref = rest
        o_ref[...] = h
        lg_ref[...] = jnp.dot(h, r_ref[...], precision=HI, preferred_element_type=F32)
    else:
        (o_ref,) = rest
        o_ref[...] = h.astype(BF16)


def rms_modulate(x, g, modtab, which_shift, router=None):
    with_router = router is not None
    in_specs = [pl.BlockSpec((ROWBLK, D), lambda i: (i, 0)),
                pl.BlockSpec((1, D), lambda i: (0, 0)),
                pl.BlockSpec((8, D), lambda i: (0, which_shift)),
                pl.BlockSpec((8, D), lambda i: (0, which_shift + 1))]
    args = [x, g.reshape(1, D), modtab, modtab]
    if with_router:
        in_specs.append(pl.BlockSpec((D, 128), lambda i: (0, 0)))
        args.append(router)
        out_shape = [jax.ShapeDtypeStruct((N_ROWS, D), F32), jax.ShapeDtypeStruct((N_ROWS, 128), F32)]
        out_specs = [pl.BlockSpec((ROWBLK, D), lambda i: (i, 0)), pl.BlockSpec((ROWBLK, 128), lambda i: (i, 0))]
    else:
        out_shape = jax.ShapeDtypeStruct((N_ROWS, D), BF16)
        out_specs = pl.BlockSpec((ROWBLK, D), lambda i: (i, 0))
    return pl.pallas_call(
        partial(_rmsmod_kernel, with_router=with_router),
        grid=(N_ROWS // ROWBLK,),
        in_specs=in_specs, out_specs=out_specs, out_shape=out_shape,
        compiler_params=_cparams(("arbitrary",)),
        name="rms_modulate_router" if with_router else "rms_modulate",
    )(*args)


MM_COLS = 512


def _mm_compute(mode, a, w_tiles, o_ref, i=None, tm=None, res_ref=None, gate_ref=None):
    tn = o_ref.shape[1]
    pieces = a if isinstance(a, (list, tuple)) else [a]
    koffs = np.cumsum([0] + [p.shape[1] for p in pieces]).tolist()
    for c0 in range(0, tn, MM_COLS):
        cols = slice(c0, min(c0 + MM_COLS, tn))
        accs = []
        for w in w_tiles:
            acc = None
            for g, p in enumerate(pieces):
                d = jnp.dot(p, w[koffs[g]:koffs[g + 1], cols], preferred_element_type=F32)
                acc = d if acc is None else acc + d
            accs.append(acc)
        if mode == "plain":
            o_ref[:, cols] = accs[0].astype(o_ref.dtype)
        elif mode == "swiglu":
            o_ref[:, cols] = (_silu(accs[0]) * accs[1]).astype(o_ref.dtype)
        else:
            for s in range(tm // ROWBLK):
                midx = _mod_index(i * (tm // ROWBLK) + s)
                rows = slice(s * ROWBLK, (s + 1) * ROWBLK)
                o_ref[rows, cols] = res_ref[rows, cols] + gate_ref[pl.ds(midx, 1), cols] * accs[0][rows, :]


def _mm_kernel(*refs, mode, tm, na):
    nw = 2 if mode == "swiglu" else 1
    a_refs = refs[:na]
    w_refs = refs[na:na + nw]
    pos = na + nw
    res_ref = gate_ref = None
    if mode == "resid":
        res_ref, gate_ref = refs[pos:pos + 2]
        pos += 2
    o_ref = refs[pos]
    ws_refs = refs[pos + 1:]
    i = pl.program_id(1)

    @pl.when(i == 0)
    def _():
        for w_ref, ws_ref in zip(w_refs, ws_refs):
            ws_ref[...] = w_ref[...].astype(BF16)

    _mm_compute(mode, [a_ref[...] for a_ref in a_refs], ws_refs, o_ref, i, tm, res_ref, gate_ref)


def matmul(a, ws, mode="plain", out_dtype=F32, res=None, modtab=None, gate_col=None,
           tm=768, tn=512, w_buffers=2, name="matmul"):
    pieces = list(a) if isinstance(a, (list, tuple)) else [a]
    m = pieces[0].shape[0]
    kdim = sum(p.shape[1] for p in pieces)
    n = ws[0].shape[-1]
    wmode = {} if w_buffers == 2 else dict(pipeline_mode=pl.Buffered(w_buffers))
    in_specs = [pl.BlockSpec((tm, p.shape[1]), lambda j, i: (i, 0)) for p in pieces]
    in_specs += [pl.BlockSpec((kdim, tn), lambda j, i: (0, j), **wmode) for _ in ws]
    args = [*pieces, *ws]
    if mode == "resid":
        in_specs += [pl.BlockSpec((tm, tn), lambda j, i: (i, j)),
                     pl.BlockSpec((8, tn), lambda j, i: (0, gate_col * (D // tn) + j))]
        args += [res, modtab]
    return pl.pallas_call(
        partial(_mm_kernel, mode=mode, tm=tm, na=len(pieces)),
        grid=(n // tn, m // tm),
        in_specs=in_specs,
        out_specs=pl.BlockSpec((tm, tn), lambda j, i: (i, j)),
        out_shape=jax.ShapeDtypeStruct((m, n), out_dtype),
        scratch_shapes=[pltpu.VMEM((kdim, tn), BF16) for _ in ws],
        compiler_params=_cparams(("arbitrary", "arbitrary")),
        name=name,
    )(*args)


def _gmm_kernel(be_ref, fresh_ref, run_ref, nexte_ref, lastrun_ref, meta_ref, a_ref, *rest, mode, nw, tn):
    w_hbm = rest[:nw]
    o_ref = rest[nw]
    stage_ref, ws_ref, sem = rest[nw + 1:]
    j = pl.program_id(0)
    i = pl.program_id(1)
    n_used, n_runs = meta_ref[0], meta_ref[1]

    def w_copy(widx, e, jj):
        col = pl.multiple_of(jj * tn, tn)
        return pltpu.make_async_copy(w_hbm[widx].at[e, :, pl.ds(col, tn)], stage_ref.at[widx], sem.at[widx])

    @pl.when(fresh_ref[i] == 1)
    def _():
        @pl.when(j * n_runs + run_ref[i] == 0)
        def _():
            for widx in range(nw):
                w_copy(widx, be_ref[i], j).start()

        for widx in range(nw):
            w_copy(widx, be_ref[i], j).wait()
            ws_ref[widx] = stage_ref[widx].astype(BF16)
        is_last = lastrun_ref[i] == 1

        @pl.when(jnp.logical_not(jnp.logical_and(is_last, j == pl.num_programs(0) - 1)))
        def _():
            jn = jnp.where(is_last, j + 1, j)
            for widx in range(nw):
                w_copy(widx, nexte_ref[i], jn).start()

    @pl.when(i < n_used)
    def _():
        _mm_compute(mode, a_ref[...], [ws_ref.at[widx] for widx in range(nw)], o_ref)

    @pl.when(i >= n_used)
    def _():
        o_ref[...] = jnp.zeros_like(o_ref)


def grouped_matmul(a, ws, sched, mode="plain", out_dtype=F32, tm=512, tn=512, name="grouped_matmul"):
    m, kdim = a.shape
    n = ws[0].shape[-1]
    nw = len(ws)
    row = lambda j, i, be, fr, ru, ne, lr, meta: jnp.minimum(i, meta[0] - 1)
    grid_spec = pltpu.PrefetchScalarGridSpec(
        num_scalar_prefetch=len(sched),
        grid=(n // tn, m // tm),
        in_specs=[pl.BlockSpec((tm, kdim), lambda *s: (row(*s), 0))] + [pl.BlockSpec(memory_space=pl.ANY)] * nw,
        out_specs=pl.BlockSpec((tm, tn), lambda j, i, *_: (i, j)),
        scratch_shapes=[pltpu.VMEM((nw, kdim, tn), F32), pltpu.VMEM((nw, kdim, tn), BF16),
                        pltpu.SemaphoreType.DMA((nw,))])
    return pl.pallas_call(
        partial(_gmm_kernel, mode=mode, nw=nw, tn=tn),
        grid_spec=grid_spec,
        out_shape=jax.ShapeDtypeStruct((m, n), out_dtype),
        compiler_params=_cparams(("arbitrary", "arbitrary")),
        name=name,
    )(*sched, a, *ws)


def _fwd_chunk(i):
    return i


def _bwd_chunk(i):
    return jnp.where(i < CH_CTX, CH_CTX - 1 - i, CH_PER_B + CH_CTX - 1 - i)


def _tri_masks():
    r = lax.broadcasted_iota(jnp.int32, (CHUNK, CHUNK), 0)
    c = lax.broadcasted_iota(jnp.int32, (CHUNK, CHUNK), 1)
    return c <= r, c >= r


def _mlstm_kernel(qf_ref, kf_ref, vf_ref, gcf_ref, grf_ref,
                  qb_ref, kb_ref, vb_ref, gcb_ref, grb_ref,
                  biasc_ref, biasr_ref, hf_ref, hb_ref, c_ref, n_ref, m_ref):
    @pl.when(pl.program_id(1) == 0)
    def _():
        c_ref[...] = jnp.zeros_like(c_ref)
        n_ref[...] = jnp.zeros_like(n_ref)
        m_ref[...] = jnp.zeros_like(m_ref)

    low, upp = _tri_masks()
    lowf, uppf = low.astype(F32), upp.astype(F32)
    streams = ((0, qf_ref, kf_ref, vf_ref, gcf_ref, grf_ref, hf_ref, low, lowf, uppf, CHUNK - 1),
               (1, qb_ref, kb_ref, vb_ref, gcb_ref, grb_ref, hb_ref, upp, uppf, lowf, 0))
    for d, q_ref, k_ref, v_ref, gc_ref, gr_ref, h_ref, mask, tri_c, tri_r, last in streams:
        ac = gc_ref[...] + biasc_ref[...]
        ar = gr_ref[0] + biasr_ref[...]
        lfc = -_softplus(-ac)
        lfr = -_softplus(-ar)
        bc_all = jnp.dot(tri_c, lfc, precision=HI, preferred_element_type=F32)
        br_all = jnp.dot(lfr, tri_r, precision=HI, preferred_element_type=F32)
        for h in range(ML_HEADS):
            s_idx = d * ML_HEADS + h
            ji, jf = G_I + s_idx, G_F + s_idx
            cols = slice(h * ML_HD, (h + 1) * ML_HD)
            q = q_ref[:, cols] * (ML_HD ** -0.5)
            k = k_ref[:, cols]
            v = v_ref[:, cols]
            bc = bc_all[:, jf:jf + 1]
            br = br_all[jf:jf + 1, :]
            icc = ac[:, ji:ji + 1]
            icr = ar[ji:ji + 1, :]
            cmat = c_ref[s_idx]
            nvec = n_ref[s_idx]
            m_prev = m_ref[s_idx][:, 0:1]
            dmat = jnp.where(mask, bc - br + icr, -1e30)
            inter = bc + m_prev
            mt = jnp.maximum(inter, jnp.max(dmat, axis=1, keepdims=True))
            w_intra = jnp.exp(dmat - mt)
            w_state = jnp.exp(inter - mt)
            s = _dot_nt(q, k) * w_intra
            num = jnp.dot(s, v, preferred_element_type=F32) + w_state * _dot_nt(q, cmat)
            den = jnp.sum(s, axis=1, keepdims=True) + w_state * jnp.sum(q * nvec, axis=1, keepdims=True)
            h_ref[:, cols] = num / jnp.maximum(jnp.abs(den), jnp.exp(-mt))
            bl = bc[last:last + 1, :]
            g = bl - bc + icc
            m_new = jnp.maximum(bl + m_prev, jnp.max(g, axis=0, keepdims=True))
            wg = jnp.exp(g - m_new)
            wc = jnp.exp(bl + m_prev - m_new)
            c_ref[s_idx] = wc * cmat + _dot_tn(wg * v, k)
            n_ref[s_idx] = wc * nvec + jnp.sum(wg * k, axis=0, keepdims=True)
            m_ref[s_idx] = jnp.broadcast_to(m_new, (1, 128))


def mlstm_scan(p, gates_t, biasc, biasr):
    def rows(chunk_fn):
        return lambda b, i: b * CH_PER_B + chunk_fn(i)

    def stream_specs(chunk_fn):
        r = rows(chunk_fn)
        return [pl.BlockSpec((CHUNK, 512), lambda b, i: (r(b, i), C_Q // 512)),
                pl.BlockSpec((CHUNK, 512), lambda b, i: (r(b, i), C_K // 512)),
                pl.BlockSpec((CHUNK, 512), lambda b, i: (r(b, i), C_V // 512)),
                pl.BlockSpec((CHUNK, 128), lambda b, i: (r(b, i), C_GATE // 128)),
                pl.BlockSpec((1, G_USED, CHUNK), lambda b, i: (r(b, i), 0, 0))]

    rf, rb = rows(_fwd_chunk), rows(_bwd_chunk)
    n_streams = 2 * ML_HEADS
    return pl.pallas_call(
        _mlstm_kernel,
        grid=(NB, CH_PER_B),
        in_specs=stream_specs(_fwd_chunk) + stream_specs(_bwd_chunk) + [
            pl.BlockSpec((1, 128), lambda b, i: (0, 0)),
            pl.BlockSpec((G_USED, 1), lambda b, i: (0, 0))],
        out_specs=[pl.BlockSpec((CHUNK, 512), lambda b, i: (rf(b, i), 0)),
                   pl.BlockSpec((CHUNK, 512), lambda b, i: (rb(b, i), 0))],
        out_shape=[jax.ShapeDtypeStruct((N_ROWS, 512), F32)] * 2,
        scratch_shapes=[pltpu.VMEM((n_streams, ML_HD, ML_HD), F32),
                        pltpu.VMEM((n_streams, 1, ML_HD), F32),
                        pltpu.VMEM((n_streams, 1, 128), F32)],
        compiler_params=_cparams(("arbitrary", "arbitrary")),
        name="mlstm_scan",
    )(p, p, p, p, gates_t, p, p, p, p, gates_t, biasc, biasr)


def _mlstm_finish_kernel(hf_ref, hb_ref, o_ref, g_ref, out_ref):
    for h in range(ML_HEADS):
        cols = slice(h * ML_HD, (h + 1) * ML_HD)
        x = hf_ref[:, cols] + hb_ref[:, cols]
        y = x * lax.rsqrt(jnp.mean(x * x, axis=-1, keepdims=True) + EPS) * g_ref[:, cols]
        out_ref[:, cols] = (y * jax.nn.sigmoid(o_ref[:, cols])).astype(BF16)


def mlstm_finish(hf, hb, p, norm_g):
    blk = lambda c: pl.BlockSpec((ROWBLK, 512), lambda i: (i, c))
    return pl.pallas_call(
        _mlstm_finish_kernel,
        grid=(N_ROWS // ROWBLK,),
        in_specs=[blk(0), blk(0), blk(C_O // 512), pl.BlockSpec((1, 512), lambda i: (0, 0))],
        out_specs=blk(0),
        out_shape=jax.ShapeDtypeStruct((N_ROWS, 512), BF16),
        compiler_params=_cparams(("arbitrary",)),
        name="mlstm_finish",
    )(hf, hb, p, norm_g.reshape(1, 512))


_PAD = 8


def _ssd_conv_kernel(u_ref, w_ref, b_ref, o_ref, pad_ref):
    zeros = jnp.zeros((_PAD, u_ref.shape[1]), F32)
    segs = ((0, T_CTX, _PAD), (T_CTX, T_LAT, 2 * _PAD + T_CTX))
    pad_ref[0:_PAD] = zeros
    pad_ref[_PAD + T_CTX:2 * _PAD + T_CTX] = zeros
    pad_ref[2 * _PAD + S_ALL:3 * _PAD + S_ALL] = zeros
    for src, n, dst in segs:
        pad_ref[dst:dst + n] = u_ref[src:src + n]
    for src, n, dst in segs:
        acc = jnp.broadcast_to(b_ref[...], (n, u_ref.shape[1]))
        for j in range(SSD_CONV):
            lo = dst + j - SSD_CONV // 2
            acc = acc + w_ref[j:j + 1, :] * pad_ref[lo:lo + n]
        o_ref[src:src + n] = _silu(acc)


def ssd_conv(p, conv_w, conv_b):
    tc = 256
    return pl.pallas_call(
        _ssd_conv_kernel,
        grid=(NB, 1024 // tc),
        in_specs=[pl.BlockSpec((S_ALL, tc), lambda b, c: (b, C_XBC // tc + c)),
                  pl.BlockSpec((SSD_CONV, tc), lambda b, c: (0, c)),
                  pl.BlockSpec((1, tc), lambda b, c: (0, c))],
        out_specs=pl.BlockSpec((S_ALL, tc), lambda b, c: (b, c)),
        out_shape=jax.ShapeDtypeStruct((N_ROWS, 1024), F32),
        scratch_shapes=[pltpu.VMEM((S_ALL + 3 * _PAD, tc), F32)],
        compiler_params=_cparams(("arbitrary", "arbitrary")),
        name="ssd_conv",
    )(p, conv_w, conv_b.reshape(1, 1024))


def _ssd_kernel(xf_ref, bf_ref, cf_ref, gcf_ref, grf_ref,
                xb_ref, bb_ref, cb_ref, gcb_ref, grb_ref,
                biasc_ref, biasr_ref, alogc_ref, alogr_ref, yf_ref, yb_ref, st_ref):
    @pl.when(pl.program_id(1) == 0)
    def _():
        st_ref[...] = jnp.zeros_like(st_ref)

    low, upp = _tri_masks()
    lowf, uppf = low.astype(F32), upp.astype(F32)
    a_c = -jnp.exp(alogc_ref[...])
    a_r = -jnp.exp(alogr_ref[...])
    streams = ((0, xf_ref, bf_ref, cf_ref, gcf_ref, grf_ref, yf_ref, low, lowf, uppf, CHUNK - 1),
               (1, xb_ref, bb_ref, cb_ref, gcb_ref, grb_ref, yb_ref, upp, uppf, lowf, 0))
    for d, x_ref, b_ref, c_ref, gc_ref, gr_ref, y_ref, mask, tri_c, tri_r, last in streams:
        dtc_all = _softplus(gc_ref[...] + biasc_ref[...])
        dtr_all = _softplus(gr_ref[0] + biasr_ref[...])
        cumc_all = jnp.dot(tri_c, dtc_all * a_c, precision=HI, preferred_element_type=F32)
        cumr_all = jnp.dot(dtr_all * a_r, tri_r, precision=HI, preferred_element_type=F32)
        for g in range(SSD_GROUPS):
            gcols = slice(g * SSD_STATE, (g + 1) * SSD_STATE)
            bm = b_ref[:, gcols]
            cm = c_ref[:, gcols]
            cb = _dot_nt(cm, bm)
            for hh in range(SSD_HEADS // SSD_GROUPS):
                h = g * (SSD_HEADS // SSD_GROUPS) + hh
                j = G_DT + d * SSD_HEADS + h
                s_idx = d * SSD_HEADS + h
                cols = slice(h * SSD_HD, (h + 1) * SSD_HD)
                x = x_ref[:, cols]
                cc = cumc_all[:, j:j + 1]
                cr = cumr_all[j:j + 1, :]
                dtc = dtc_all[:, j:j + 1]
                dtr = dtr_all[j:j + 1, :]
                st = st_ref[s_idx]
                decay = jnp.exp(jnp.where(mask, cc - cr, -1e30))
                w = cb * decay * dtr
                y_ref[:, cols] = (jnp.dot(w, x, preferred_element_type=F32)
                                  + _dot_nt(cm, st) * jnp.exp(cc))
                cl = cc[last:last + 1, :]
                tail = jnp.exp(cl - cc) * dtc
                st_ref[s_idx] = st * jnp.exp(cl) + _dot_tn(tail * x, bm)


def ssd_scan(xbc, p, gates_t, biasc, biasr, alogc, alogr):
    def rows(chunk_fn):
        return lambda b, i: b * CH_PER_B + chunk_fn(i)

    def stream_specs(chunk_fn):
        r = rows(chunk_fn)
        return [pl.BlockSpec((CHUNK, 512), lambda b, i: (r(b, i), 0)),
                pl.BlockSpec((CHUNK, 256), lambda b, i: (r(b, i), 2)),
                pl.BlockSpec((CHUNK, 256), lambda b, i: (r(b, i), 3)),
                pl.BlockSpec((CHUNK, 128), lambda b, i: (r(b, i), C_GATE // 128)),
                pl.BlockSpec((1, G_USED, CHUNK), lambda b, i: (r(b, i), 0, 0))]

    rf, rb = rows(_fwd_chunk), rows(_bwd_chunk)
    const = lambda shape: pl.BlockSpec(shape, lambda b, i: (0, 0))
    return pl.pallas_call(
        _ssd_kernel,
        grid=(NB, CH_PER_B),
        in_specs=stream_specs(_fwd_chunk) + stream_specs(_bwd_chunk) + [
            const((1, 128)), const((G_USED, 1)), const((1, 128)), const((G_USED, 1))],
        out_specs=[pl.BlockSpec((CHUNK, 512), lambda b, i: (rf(b, i), 0)),
                   pl.BlockSpec((CHUNK, 512), lambda b, i: (rb(b, i), 0))],
        out_shape=[jax.ShapeDtypeStruct((N_ROWS, 512), F32)] * 2,
        scratch_shapes=[pltpu.VMEM((2 * SSD_HEADS, SSD_HD, SSD_STATE), F32)],
        compiler_params=_cparams(("arbitrary", "arbitrary")),
        name="ssd_scan",
    )(xbc, xbc, xbc, p, gates_t, xbc, xbc, xbc, p, gates_t, biasc, biasr, alogc, alogr)


def _ssd_finish_kernel(yf_ref, yb_ref, xs_ref, z_ref, dsk_ref, g_ref, out_ref):
    y = yf_ref[...] + yb_ref[...] + dsk_ref[...] * xs_ref[...]
    y = y * _silu(z_ref[...])
    out_ref[...] = (y * lax.rsqrt(jnp.mean(y * y, axis=-1, keepdims=True) + EPS) * g_ref[...]).astype(BF16)


def ssd_finish(yf, yb, xbc, p, d_skip, norm_g):
    blk = lambda c: pl.BlockSpec((ROWBLK, 512), lambda i: (i, c))
    vec = pl.BlockSpec((1, 512), lambda i: (0, 0))
    return pl.pallas_call(
        _ssd_finish_kernel,
        grid=(N_ROWS // ROWBLK,),
        in_specs=[blk(0), blk(0), blk(0), blk(C_Z // 512), vec, vec],
        out_specs=blk(0),
        out_shape=jax.ShapeDtypeStruct((N_ROWS, 512), BF16),
        compiler_params=_cparams(("arbitrary",)),
        name="ssd_finish",
    )(yf, yb, xbc, p, jnp.repeat(d_skip, SSD_HD).reshape(1, 512), norm_g.reshape(1, 512))


def _rope_tile(r, cs_ref, s1_ref, s2_ref):
    return (r * cs_ref[...] + pltpu.roll(r, 128 - MLA_ROPE // 2, 1) * s1_ref[...]
            + pltpu.roll(r, MLA_ROPE // 2, 1) * s2_ref[...])


def _mla_prep_kernel(cq_ref, ckv_ref, kr_ref, wq_ref, wkv_ref, qn_ref, kvn_ref, gq_ref, gk_ref,
                     cs_ref, s1_ref, s2_ref, q_out, k_out, v_out):
    cq = cq_ref[...]
    cqn = cq * lax.rsqrt(jnp.sum(cq * cq, axis=-1, keepdims=True) / MLA_Q_RANK + EPS) * qn_ref[...]
    q_raw = jnp.dot(cqn, wq_ref[...], preferred_element_type=F32)
    ckv = ckv_ref[...]
    ckvn = ckv * lax.rsqrt(jnp.sum(ckv * ckv, axis=-1, keepdims=True) / MLA_KV_RANK + EPS) * kvn_ref[...]
    kv_raw = jnp.dot(ckvn, wkv_ref[...], preferred_element_type=F32)
    kr = kr_ref[...]
    kr_ss = jnp.sum(kr * kr, axis=-1, keepdims=True)
    for h in range(MLA_HEADS):
        base = h * MLA_HP
        qa = q_raw[:, base:base + 128]
        qb = q_raw[:, base + 128:base + 256]
        q_scale = lax.rsqrt((jnp.sum(qa * qa, axis=-1, keepdims=True)
                             + jnp.sum(qb * qb, axis=-1, keepdims=True)) / MLA_QK + EPS)
        q_out[:, base:base + 128] = (qa * q_scale * gq_ref[:, 0:128]).astype(BF16)
        q_out[:, base + 128:base + 256] = _rope_tile(qb * q_scale * gq_ref[:, 128:256],
                                                     cs_ref, s1_ref, s2_ref).astype(BF16)
        kn = kv_raw[:, base:base + 128]
        k_scale = lax.rsqrt((jnp.sum(kn * kn, axis=-1, keepdims=True) + kr_ss) / MLA_QK + EPS)
        k_out[:, base:base + 128] = (kn * k_scale * gk_ref[:, 0:128]).astype(BF16)
        k_out[:, base + 128:base + 256] = _rope_tile(kr * k_scale * gk_ref[:, 128:256],
                                                     cs_ref, s1_ref, s2_ref).astype(BF16)
        v_out[:, h * MLA_V:(h + 1) * MLA_V] = kv_raw[:, base + 128:base + 256].astype(BF16)


def mla_prep(p, wq, wkv, qn, kvn, gq, gk, rope_tabs):
    const = lambda shape: pl.BlockSpec(shape, lambda i: (0, 0))
    tab = pl.BlockSpec((ROWBLK, 128), lambda i: (i % BLK_PER_B, 0))
    row = lambda w: pl.BlockSpec((ROWBLK, w), lambda i: (i, 0))
    return pl.pallas_call(
        _mla_prep_kernel,
        grid=(N_ROWS // ROWBLK,),
        in_specs=[pl.BlockSpec((ROWBLK, 512), lambda i: (i, C_CQ // 512)),
                  pl.BlockSpec((ROWBLK, 256), lambda i: (i, C_CKV // 256)),
                  pl.BlockSpec((ROWBLK, 128), lambda i: (i, C_KR // 128)),
                  const((512, MLA_HEADS * MLA_HP)), const((256, MLA_HEADS * MLA_HP)),
                  const((1, 512)), const((1, 256)), const((1, MLA_HP)), const((1, MLA_HP)),
                  tab, tab, tab],
        out_specs=[row(MLA_HEADS * MLA_HP), row(MLA_HEADS * MLA_HP), row(MLA_HEADS * MLA_V)],
        out_shape=[jax.ShapeDtypeStruct((N_ROWS, MLA_HEADS * MLA_HP), BF16),
                   jax.ShapeDtypeStruct((N_ROWS, MLA_HEADS * MLA_HP), BF16),
                   jax.ShapeDtypeStruct((N_ROWS, MLA_HEADS * MLA_V), BF16)],
        compiler_params=_cparams(("arbitrary",)),
        name="mla_prep",
    )(p, p, p, wq, wkv, qn, kvn, gq, gk, *rope_tabs)


def _softmax_pv(s_list, v_list):
    m = s_list[0].max(axis=-1, keepdims=True)
    for s in s_list[1:]:
        m = jnp.maximum(m, s.max(axis=-1, keepdims=True))
    ps = [jnp.exp(s - m) for s in s_list]
    l = sum(p.sum(axis=-1, keepdims=True) for p in ps)
    o = sum(jnp.dot(p.astype(BF16), v, preferred_element_type=F32) for p, v in zip(ps, v_list))
    return o / l


def _mla_attn_kernel(q_ref, k_ref, v_ref, o_ref):
    scale = MLA_QK ** -0.5
    q = q_ref[...]

    @pl.when(pl.program_id(2) == 0)
    def _():
        s = _dot_nt(q, k_ref[0:T_CTX, :]) * scale
        o_ref[...] = _softmax_pv([s], [v_ref[0:T_CTX, :]]).astype(BF16)

    @pl.when(pl.program_id(2) > 0)
    def _():
        s = _dot_nt(q, k_ref[...]) * scale
        o_ref[...] = _softmax_pv([s], [v_ref[...]]).astype(BF16)


def mla_attention(q, k, v):
    return pl.pallas_call(
        _mla_attn_kernel,
        grid=(NB, MLA_HEADS, BLK_PER_B),
        in_specs=[pl.BlockSpec((ROWBLK, MLA_HP), lambda b, h, i: (b * BLK_PER_B + i, h)),
                  pl.BlockSpec((S_ALL, MLA_HP), lambda b, h, i: (b, h)),
                  pl.BlockSpec((S_ALL, MLA_V), lambda b, h, i: (b, h))],
        out_specs=pl.BlockSpec((ROWBLK, MLA_V), lambda b, h, i: (b * BLK_PER_B + i, h)),
        out_shape=jax.ShapeDtypeStruct((N_ROWS, MLA_HEADS * MLA_V), BF16),
        compiler_params=_cparams(("arbitrary", "arbitrary", "arbitrary")),
        name="mla_attention",
    )(q, k, v)


N_GRID_ROWS = T_LAT // GRID_W
NA_QROWS = ROWBLK // GRID_W
NA_GROUPS = N_GRID_ROWS // NA_QROWS
NA_WIN_ROWS = NA_KH + NA_QROWS - 1
NA_WIN = NA_WIN_ROWS * GRID_W


def _na_win_row(g):
    return jnp.clip(g * NA_QROWS - NA_KH // 2, 0, N_GRID_ROWS - NA_WIN_ROWS)


def _na_kernel(q_ref, k_ref, v_ref, gq_ref, gk_ref, bias_ref, o_ref, kn_ref, vb_ref):
    st = pl.program_id(2)
    scale = NA_HD ** -0.5

    @pl.when(st == 0)
    def _():
        k = k_ref[...]
        kn_ref[...] = (k * lax.rsqrt(jnp.mean(k * k, axis=-1, keepdims=True) + EPS) * gk_ref[...]).astype(BF16)
        vb_ref[...] = v_ref[...].astype(BF16)

    q = q_ref[...]
    qn = (q * lax.rsqrt(jnp.mean(q * q, axis=-1, keepdims=True) + EPS) * gq_ref[...]).astype(BF16)
    s_cx = _dot_nt(qn, kn_ref[0:T_CTX, :]) * scale
    v_cx = vb_ref[0:T_CTX, :]

    @pl.when(st == 0)
    def _():
        o_ref[...] = _softmax_pv([s_cx], [v_cx]).astype(BF16)

    @pl.when(st > 0)
    def _():
        start = pl.multiple_of(T_CTX + _na_win_row(st - 1) * GRID_W, GRID_W)
        s_nb = _dot_nt(qn, kn_ref[pl.ds(start, NA_WIN), :]) * scale + bias_ref[0, 0]
        o_ref[...] = _softmax_pv([s_nb, s_cx], [vb_ref[pl.ds(start, NA_WIN), :], v_cx]).astype(BF16)


def _na_class(g):
    return jnp.where(g == 0, 0, jnp.where(g == NA_GROUPS - 1, 2, 1))


def na_attention(p, gq, gk, bias_tab):
    return pl.pallas_call(
        _na_kernel,
        grid=(NB, NA_HEADS, BLK_PER_B),
        in_specs=[pl.BlockSpec((ROWBLK, NA_HD), lambda b, h, st: (b * BLK_PER_B + st, C_NAQ // NA_HD + h)),
                  pl.BlockSpec((S_ALL, NA_HD), lambda b, h, st: (b, C_NAK // NA_HD + h)),
                  pl.BlockSpec((S_ALL, NA_HD), lambda b, h, st: (b, C_NAV // NA_HD + h)),
                  pl.BlockSpec((1, NA_HD), lambda b, h, st: (0, 0)),
                  pl.BlockSpec((1, NA_HD), lambda b, h, st: (0, 0)),
                  pl.BlockSpec((1, 1, ROWBLK, NA_WIN),
                               lambda b, h, st: (h, _na_class(jnp.maximum(st - 1, 0)), 0, 0))],
        out_specs=pl.BlockSpec((ROWBLK, NA_HD), lambda b, h, st: (b * BLK_PER_B + st, h)),
        out_shape=jax.ShapeDtypeStruct((N_ROWS, NA_HEADS * NA_HD), BF16),
        scratch_shapes=[pltpu.VMEM((S_ALL, NA_HD), BF16), pltpu.VMEM((S_ALL, NA_HD), BF16)],
        compiler_params=_cparams(("arbitrary", "arbitrary", "arbitrary")),
        name="na_attention",
    )(p, p, p, gq.reshape(1, NA_HD), gk.reshape(1, NA_HD), bias_tab)


def na_bias_table(rpb):
    g_rep = np.array([0, 1, NA_GROUPS - 1])
    r = g_rep[:, None] * NA_QROWS + np.arange(NA_QROWS)[None, :]
    r0 = np.clip(r - NA_KH // 2, 0, N_GRID_ROWS - NA_KH)
    w0 = np.clip(g_rep * NA_QROWS - NA_KH // 2, 0, N_GRID_ROWS - NA_WIN_ROWS)
    kr = w0[:, None] + np.arange(NA_WIN_ROWS)[None, :]
    valid_r = (kr[:, None, :] >= r0[:, :, None]) & (kr[:, None, :] < r0[:, :, None] + NA_KH)
    dr = np.clip(kr[:, None, :] - r[:, :, None] + NA_KH - 1, 0, 2 * NA_KH - 2)
    c = np.arange(GRID_W)[:, None]
    kc = np.arange(GRID_W)[None, :]
    c0 = np.clip(c - NA_KW // 2, 0, GRID_W - NA_KW)
    valid_c = (kc >= c0) & (kc < c0 + NA_KW)
    dc = kc - c + NA_KW - 1
    n_dc = 2 * NA_KW - 1
    onehot = (np.arange(n_dc)[:, None, None] == dc[None]) & valid_c[None]
    rows = rpb[:, dr]
    t = jnp.einsum('hsajd,dck->hsacjk', rows, jnp.asarray(onehot, F32), precision=HI)
    valid = valid_r[:, :, None, :, None] & valid_c[None, None, :, None, :]
    t = jnp.where(valid[None], t, -1e30)
    return t.reshape(NA_HEADS, 3, ROWBLK, NA_WIN)


N_TOK = NB * T_LAT
MOE_TM = 512
N_BLK = N_TOK * TOP_K // MOE_TM + N_EXPERTS
N_SLOT = N_BLK * MOE_TM
GATHER_ROWS = 256
COMBINE_TOK = 128


def _row_copies(n_rows, copy_fn):
    def start_all():
        def body(r, carry):
            copy_fn(r).start()
            return carry
        lax.fori_loop(0, n_rows, body, 0, unroll=8)

    def wait_all():
        def body(r, carry):
            copy_fn(r).wait()
            return carry
        lax.fori_loop(0, n_rows, body, 0, unroll=8)

    return start_all, wait_all


def _gather_kernel(idx_ref, nu_ref, h_hbm, o_ref, buf_ref, sem):
    i = pl.program_id(0)
    n_steps = nu_ref[0] * (MOE_TM // GATHER_ROWS)

    def copies(step):
        slot = step % 2
        return _row_copies(GATHER_ROWS, lambda r: pltpu.make_async_copy(
            h_hbm.at[pl.ds(idx_ref[step * GATHER_ROWS + r], 1)], buf_ref.at[slot, pl.ds(r, 1)], sem.at[slot]))

    @pl.when(i == 0)
    def _():
        copies(i)[0]()

    @pl.when(i + 1 < n_steps)
    def _():
        copies(i + 1)[0]()

    @pl.when(i < n_steps)
    def _():
        copies(i)[1]()
        o_ref[...] = buf_ref[i % 2].astype(BF16)

    @pl.when(i >= n_steps)
    def _():
        o_ref[...] = jnp.zeros_like(o_ref)


def moe_gather(h2, slot_row, n_used):
    return pl.pallas_call(
        _gather_kernel,
        grid_spec=pltpu.PrefetchScalarGridSpec(
            num_scalar_prefetch=2,
            grid=(N_SLOT // GATHER_ROWS,),
            in_specs=[pl.BlockSpec(memory_space=pl.ANY)],
            out_specs=pl.BlockSpec((GATHER_ROWS, D), lambda i, idx, nu: (i, 0)),
            scratch_shapes=[pltpu.VMEM((2, GATHER_ROWS, D), F32), pltpu.SemaphoreType.DMA((2,))]),
        out_shape=jax.ShapeDtypeStruct((N_SLOT, D), BF16),
        compiler_params=_cparams(("arbitrary",)),
        name="moe_gather",
    )(slot_row, n_used, h2)


def _combine_kernel(slot_ref, y_hbm, x_ref, gate_ref, g2_ref, o_ref, buf_ref, sem):
    i = pl.program_id(0)

    def copies(step):
        slot = step % 2
        base = step * COMBINE_TOK * TOP_K
        return _row_copies(COMBINE_TOK * TOP_K, lambda r: pltpu.make_async_copy(
            y_hbm.at[pl.ds(slot_ref[base + r], 1)], buf_ref.at[slot, pl.ds(r, 1)], sem.at[slot]))

    @pl.when(i == 0)
    def _():
        copies(i)[0]()

    @pl.when(i + 1 < N_TOK // COMBINE_TOK)
    def _():
        copies(i + 1)[0]()

    copies(i)[1]()
    b = i // (T_LAT // COMBINE_TOK)
    rows = buf_ref[i % 2]
    f = gate_ref[:, 0:1] * rows[0:COMBINE_TOK] + gate_ref[:, 1:2] * rows[COMBINE_TOK:2 * COMBINE_TOK]
    o_ref[...] = x_ref[...] + g2_ref[pl.ds(b, 1), :] * f


def moe_combine(yb, x, slots_km, gates, modtab):
    per_b = T_LAT // COMBINE_TOK
    x_map = lambda i, s: ((i // per_b) * (S_ALL // COMBINE_TOK) + T_CTX // COMBINE_TOK + i % per_b, 0)
    return pl.pallas_call(
        _combine_kernel,
        grid_spec=pltpu.PrefetchScalarGridSpec(
            num_scalar_prefetch=1,
            grid=(N_TOK // COMBINE_TOK,),
            in_specs=[pl.BlockSpec(memory_space=pl.ANY),
                      pl.BlockSpec((COMBINE_TOK, D), x_map),
                      pl.BlockSpec((COMBINE_TOK, TOP_K), lambda i, s: (i, 0)),
                      pl.BlockSpec((8, D), lambda i, s: (0, 5))],
            out_specs=pl.BlockSpec((COMBINE_TOK, D), lambda i, s: (i, 0)),
            scratch_shapes=[pltpu.VMEM((2, COMBINE_TOK * TOP_K, D), F32), pltpu.SemaphoreType.DMA((2,))]),
        out_shape=jax.ShapeDtypeStruct((N_TOK, D), F32),
        compiler_params=_cparams(("arbitrary",)),
        name="moe_combine",
    )(slots_km, yb, x, gates, modtab)


def moe_routing(logits):
    top_v, top_e = lax.top_k(logits, TOP_K)
    gates = jax.nn.softmax(top_v, axis=-1)
    flat_e = top_e.reshape(-1)
    onehot = (flat_e[:, None] == jnp.arange(N_EXPERTS)[None, :]).astype(jnp.int32)
    rank = jnp.take_along_axis(jnp.cumsum(onehot, axis=0) - onehot, flat_e[:, None], axis=1)[:, 0]
    counts = onehot.sum(axis=0)
    padded = (counts + MOE_TM - 1) // MOE_TM * MOE_TM
    pend = jnp.cumsum(padded)
    slot = ((pend - padded)[flat_e] + rank).astype(jnp.int32)
    tok = jnp.arange(N_TOK, dtype=jnp.int32)
    tok_row = (tok // T_LAT) * S_ALL + T_CTX + tok % T_LAT
    slot_row = jnp.zeros((N_SLOT,), jnp.int32).at[slot].set(jnp.repeat(tok_row, TOP_K))
    n_used = (pend[-1] // MOE_TM).astype(jnp.int32)
    blk = jnp.arange(N_BLK)
    used = blk < n_used
    blk_e = jnp.sum(jnp.minimum(blk, n_used - 1)[:, None] * MOE_TM >= pend[None, :], axis=1).astype(jnp.int32)
    fresh = (used & ((blk == 0) | (blk_e != jnp.roll(blk_e, 1)))).astype(jnp.int32)
    run_idx = jnp.cumsum(fresh) - 1
    n_runs = fresh.sum()
    has = counts > 0
    ids = jnp.arange(N_EXPERTS)
    first_e = jnp.min(jnp.where(has, ids, N_EXPERTS))
    later = jnp.where(has[None, :] & (ids[None, :] > ids[:, None]), ids[None, :], N_EXPERTS).min(axis=1)
    next_of = jnp.where(later < N_EXPERTS, later, first_e)
    sched = (blk_e, fresh, run_idx.astype(jnp.int32), next_of[blk_e].astype(jnp.int32),
             (run_idx == n_runs - 1).astype(jnp.int32), jnp.stack([n_used, n_runs]).astype(jnp.int32))
    slots_km = slot.reshape(N_TOK // COMBINE_TOK, COMBINE_TOK, TOP_K).transpose(0, 2, 1).reshape(-1)
    return slot_row, sched, n_used.reshape(1), slots_km, gates


def _pad_cols(w, width):
    return jnp.pad(w, ((0, 0), (0, width - w.shape[1])))


def layout_w_in(w):
    offs = np.cumsum((512, 512, 512, 512, 8, 8, 512, 1024, 16, 448, 160, 64, 512, 512, 512))[:-1].tolist()
    q, k, v, o, ig, fg, z, xbc, dt, cq, ckv, kr, naq, nak, nav = jnp.split(w, offs, axis=1)
    gate = _pad_cols(jnp.concatenate([ig, fg, dt], axis=1), 128)
    return jnp.concatenate([q, k, v, o, z, xbc, _pad_cols(cq, 512), _pad_cols(ckv, 256), _pad_cols(kr, 128),
                            naq, nak, nav, gate], axis=1)


def rope_tables():
    t = np.arange(T_LAT)
    n_freq = MLA_ROPE // 4
    freqs = ROPE_THETA ** (-jnp.arange(n_freq, dtype=F32) / n_freq)
    row = jnp.asarray(t // GRID_W, F32)
    col = jnp.asarray(t % GRID_W, F32)
    ang = jnp.concatenate([row[:, None] * freqs, col[:, None] * freqs], axis=-1)
    cos, sin = jnp.cos(ang), jnp.sin(ang)
    half = MLA_ROPE // 2
    zc = jnp.zeros((T_LAT, 128 - MLA_ROPE), F32)
    zh = jnp.zeros((T_LAT, half), F32)
    cs = jnp.concatenate([cos, cos, zc], axis=1)
    s1 = jnp.concatenate([-sin, zh, zc], axis=1)
    s2 = jnp.concatenate([zh, sin, zc], axis=1)
    ident = jnp.concatenate([jnp.ones((T_CTX, MLA_ROPE), F32), jnp.zeros((T_CTX, 128 - MLA_ROPE), F32)], axis=1)
    zeros = jnp.zeros((T_CTX, 128), F32)
    return (jnp.concatenate([ident, cs], axis=0), jnp.concatenate([zeros, s1], axis=0),
            jnp.concatenate([zeros, s2], axis=0))


def _gate_vectors(i_bias, f_bias, dt_bias, a_log):
    used = jnp.concatenate([i_bias.reshape(-1), f_bias.reshape(-1), dt_bias.reshape(-1)])
    alog = jnp.concatenate([jnp.zeros((G_DT,), F32), a_log.reshape(-1)])
    padc = lambda u: jnp.pad(u, (0, 128 - G_USED)).reshape(1, 128)
    return padc(used), used.reshape(G_USED, 1), padc(alog), alog.reshape(G_USED, 1)


def kernel(x, c, ctx, c_ctx, mod_w, mod_b, norm1, w_in, w_out, ml_i_bias, ml_f_bias, ml_norm, ssd_conv_w, ssd_conv_b, ssd_dt_bias, ssd_A_log, ssd_D, ssd_norm, mla_q_norm, mla_w_qb, mla_kv_norm, mla_w_kvb, mla_gq, mla_gk, na_gq, na_gk, na_rpb, norm2, ffn_w1, ffn_w3, ffn_w2, moe_router, moe_w1, moe_w3, moe_w2):
    xs = jnp.concatenate([ctx, x], axis=1).reshape(N_ROWS, D)
    c_all = jnp.concatenate([c, c_ctx[None, :], jnp.zeros((8 - NB - 1, D), F32)], axis=0)
    mod_all = modulation(c_all, mod_w, mod_b)
    tabs = rope_tables()
    out = None
    for l in range(DEPTH):
        modtab = mod_all[l]
        h = rms_modulate(xs, norm1[l], modtab, 0)
        p = matmul(h, [layout_w_in(w_in[l])], tm=1536, tn=1024, name="w_in")
        gates_t = p[:, C_GATE:C_GATE + G_USED].reshape(N_ROWS // CHUNK, CHUNK, G_USED).transpose(0, 2, 1)
        biasc, biasr, alogc, alogr = _gate_vectors(ml_i_bias[l], ml_f_bias[l], ssd_dt_bias[l], ssd_A_log[l])

        hf, hb = mlstm_scan(p, gates_t, biasc, biasr)
        ml = mlstm_finish(hf, hb, p, ml_norm[l])

        xbc = ssd_conv(p, ssd_conv_w[l], ssd_conv_b[l])
        yf, yb = ssd_scan(xbc, p, gates_t, biasc, biasr, alogc, alogr)
        ss = ssd_finish(yf, yb, xbc, p, ssd_D[l], ssd_norm[l])

        wq = jnp.pad(mla_w_qb[l].reshape(MLA_Q_RANK, MLA_HEADS, MLA_QK),
                     ((0, 512 - MLA_Q_RANK), (0, 0), (0, MLA_HP - MLA_QK))).reshape(512, MLA_HEADS * MLA_HP)
        wkv = jnp.pad(mla_w_kvb[l], ((0, 256 - MLA_KV_RANK), (0, 0)))
        pad1 = lambda u, w: jnp.pad(u, (0, w - u.shape[0])).reshape(1, w)
        q, k, v = mla_prep(p, wq, wkv, pad1(mla_q_norm[l], 512), pad1(mla_kv_norm[l], 256),
                           pad1(mla_gq[l], MLA_HP), pad1(mla_gk[l], MLA_HP), tabs)
        la = mla_attention(q, k, v)

        na = na_attention(p, na_gq[l], na_gk[l], na_bias_table(na_rpb[l]))

        xs = matmul([ml, ss, la, na], [w_out[l]], mode="resid", res=xs, modtab=modtab, gate_col=2,
                    tm=1536, name="w_out")

        if l % 2 == 0:
            h2 = rms_modulate(xs, norm2[l], modtab, 3)
            hid = matmul(h2, [ffn_w1[l // 2], ffn_w3[l // 2]], mode="swiglu", out_dtype=BF16, tm=1536,
                         name="ffn_up")
            xs = matmul(hid, [ffn_w2[l // 2]], mode="resid", res=xs, modtab=modtab, gate_col=5,
                        w_buffers=1, name="ffn_down")
        else:
            h2, logits = rms_modulate(xs, norm2[l], modtab, 3, router=_pad_cols(moe_router[l // 2], 128))
            lat = logits.reshape(NB, S_ALL, 128)[:, T_CTX:, :N_EXPERTS].reshape(N_TOK, N_EXPERTS)
            slot_row, sched, n_used, slots_km, gates = moe_routing(lat)
            xb = moe_gather(h2, slot_row, n_used)
            hid = grouped_matmul(xb, [moe_w1[l // 2], moe_w3[l // 2]], sched, mode="swiglu", out_dtype=BF16,
                                 tm=MOE_TM, tn=1024, name="moe_up")
            yb_ = grouped_matmul(hid, [moe_w2[l // 2]], sched, tm=MOE_TM, tn=512, name="moe_down")
            out = moe_combine(yb_, xs, slots_km, gates, modtab)
    return out.reshape(NB, T_LAT, D)
```

```python
from functools import partial

import numpy as np
import jax
import jax.numpy as jnp
from jax import lax
from jax.experimental import pallas as pl
from jax.experimental.pallas import tpu as pltpu

F32 = jnp.float32
BF16 = jnp.bfloat16
HI = lax.Precision.HIGHEST

D = 2048
NB = 4
T_LAT = 2048
T_CTX = 256
S_ALL = T_LAT + T_CTX
N_ROWS = NB * S_ALL
DEPTH = 2
GRID_W = 64
EPS = 1e-6

ROWBLK = 256
BLK_PER_B = S_ALL // ROWBLK
CHUNK = 128
CH_PER_B = S_ALL // CHUNK
CH_CTX = T_CTX // CHUNK

ML_HEADS, ML_HD = 4, 128
SSD_HEADS, SSD_HD, SSD_GROUPS, SSD_STATE, SSD_CONV = 8, 64, 2, 128, 5
SSD_W = SSD_HEADS * SSD_HD
MLA_HEADS, MLA_NOPE, MLA_ROPE, MLA_V = 4, 128, 64, 128
MLA_QK = MLA_NOPE + MLA_ROPE
MLA_Q_RANK, MLA_KV_RANK = 448, 160
MLA_HP = 256
ROPE_THETA = 10000.0
NA_HEADS, NA_HD, NA_KH, NA_KW = 4, 128, 8, 16
N_EXPERTS, TOP_K = 8, 2

C_Q, C_K, C_V, C_O = 0, 512, 1024, 1536
C_Z = 2048
C_XBC = 2560
C_CQ, C_CKV, C_KR = 3584, 4096, 4352
C_NAQ, C_NAK, C_NAV = 4480, 4992, 5504
C_GATE = 6016
D_INP = 6144
G_I, G_F, G_DT, G_USED = 0, 8, 16, 32

VMEM_LIMIT = 56 * 1024 * 1024


def _cparams(sem):
    return pltpu.CompilerParams(dimension_semantics=sem, vmem_limit_bytes=VMEM_LIMIT)


def _mod_index(blk256):
    return jnp.where(blk256 % BLK_PER_B == 0, NB, blk256 // BLK_PER_B)


def _dot_nt(a, b):
    return lax.dot_general(a, b, (((1,), (1,)), ((), ())), preferred_element_type=F32)


def _dot_tn(a, b):
    return lax.dot_general(a, b, (((0,), (0,)), ((), ())), preferred_element_type=F32)


def _silu(x):
    return x * jax.nn.sigmoid(x)


def _softplus(x):
    return jnp.maximum(x, 0.0) + jnp.log(1.0 + jnp.exp(-jnp.abs(x)))


def _mod_kernel(c_ref, w_ref, b_ref, o_ref):
    o_ref[0] = jnp.dot(_silu(c_ref[...]), w_ref[0], preferred_element_type=F32) + b_ref[0]


def modulation(c_all, mod_w, mod_b):
    tn = 1024
    n_out = mod_w.shape[-1]
    return pl.pallas_call(
        _mod_kernel,
        grid=(DEPTH, n_out // tn),
        in_specs=[pl.BlockSpec((8, D), lambda l, j: (0, 0)),
                  pl.BlockSpec((1, D, tn), lambda l, j: (l, 0, j)),
                  pl.BlockSpec((1, 1, tn), lambda l, j: (l, 0, j))],
        out_specs=pl.BlockSpec((1, 8, tn), lambda l, j: (l, 0, j)),
        out_shape=jax.ShapeDtypeStruct((DEPTH, 8, n_out), F32),
        compiler_params=_cparams(("arbitrary", "arbitrary")),
        name="modulation",
    )(c_all, mod_w, mod_b.reshape(DEPTH, 1, n_out))


def _rmsmod_kernel(x_ref, g_ref, sh_ref, sc_ref, *rest, with_router):
    midx = _mod_index(pl.program_id(0))
    x = x_ref[...]
    y = x * lax.rsqrt(jnp.mean(x * x, axis=-1, keepdims=True) + EPS) * g_ref[...]
    h = y * (1.0 + sc_ref[pl.ds(midx, 1), :]) + sh_ref[pl.ds(midx, 1), :]
    if with_router:
        r_ref, o_ref, lg_ref = rest
        o_ref[...] = h
        lg_ref[...] = jnp.dot(h, r_ref[...], precision=HI, preferred_element_type=F32)
    else:
        (o_ref,) = rest
        o_ref[...] = h.astype(BF16)


def rms_modulate(x, g, modtab, which_shift, router=None):
    with_router = router is not None
    in_specs = [pl.BlockSpec((ROWBLK, D), lambda i: (i, 0)),
                pl.BlockSpec((1, D), lambda i: (0, 0)),
                pl.BlockSpec((8, D), lambda i: (0, which_shift)),
                pl.BlockSpec((8, D), lambda i: (0, which_shift + 1))]
    args = [x, g.reshape(1, D), modtab, modtab]
    if with_router:
        in_specs.append(pl.BlockSpec((D, 128), lambda i: (0, 0)))
        args.append(router)
        out_shape = [jax.ShapeDtypeStruct((N_ROWS, D), F32), jax.ShapeDtypeStruct((N_ROWS, 128), F32)]
        out_specs = [pl.BlockSpec((ROWBLK, D), lambda i: (i, 0)), pl.BlockSpec((ROWBLK, 128), lambda i: (i, 0))]
    else:
        out_shape = jax.ShapeDtypeStruct((N_ROWS, D), BF16)
        out_specs = pl.BlockSpec((ROWBLK, D), lambda i: (i, 0))
    return pl.pallas_call(
        partial(_rmsmod_kernel, with_router=with_router),
        grid=(N_ROWS // ROWBLK,),
        in_specs=in_specs, out_specs=out_specs, out_shape=out_shape,
        compiler_params=_cparams(("arbitrary",)),
        name="rms_modulate_router" if with_router else "rms_modulate",
    )(*args)


MM_COLS = 512


def _mm_compute(mode, a, w_tiles, o_ref, i=None, tm=None, res_ref=None, gate_ref=None):
    tn = o_ref.shape[1]
    pieces = a if isinstance(a, (list, tuple)) else [a]
    koffs = np.cumsum([0] + [p.shape[1] for p in pieces]).tolist()
    for c0 in range(0, tn, MM_COLS):
        cols = slice(c0, min(c0 + MM_COLS, tn))
        accs = []
        for w in w_tiles:
            acc = None
            for g, p in enumerate(pieces):
                d = jnp.dot(p, w[koffs[g]:koffs[g + 1], cols], preferred_element_type=F32)
                acc = d if acc is None else acc + d
            accs.append(acc)
        if mode == "plain":
            o_ref[:, cols] = accs[0].astype(o_ref.dtype)
        elif mode == "swiglu":
            o_ref[:, cols] = (_silu(accs[0]) * accs[1]).astype(o_ref.dtype)
        else:
            for s in range(tm // ROWBLK):
                midx = _mod_index(i * (tm // ROWBLK) + s)
                rows = slice(s * ROWBLK, (s + 1) * ROWBLK)
                o_ref[rows, cols] = res_ref[rows, cols] + gate_ref[pl.ds(midx, 1), cols] * accs[0][rows, :]


def _mm_kernel(*refs, mode, tm, na, stacked):
    nw = 2 if mode == "swiglu" else 1
    a_refs = refs[:na]
    w_refs = refs[na:na + nw]
    pos = na + nw
    res_ref = gate_ref = None
    if mode == "resid":
        res_ref, gate_ref = refs[pos:pos + 2]
        pos += 2
    o_ref = refs[pos]
    ws_refs = refs[pos + 1:]
    i = pl.program_id(1)

    @pl.when(i == 0)
    def _():
        for w_ref, ws_ref in zip(w_refs, ws_refs):
            ws_ref[...] = (w_ref[0] if stacked else w_ref[...]).astype(BF16)

    _mm_compute(mode, [a_ref[...] for a_ref in a_refs], ws_refs, o_ref, i, tm, res_ref, gate_ref)


def matmul(a, ws, mode="plain", out_dtype=F32, res=None, modtab=None, gate_col=None,
           tm=768, tn=512, w_buffers=2, layer=None, name="matmul"):
    pieces = list(a) if isinstance(a, (list, tuple)) else [a]
    m = pieces[0].shape[0]
    kdim = sum(p.shape[1] for p in pieces)
    n = ws[0].shape[-1]
    wmode = {} if w_buffers == 2 else dict(pipeline_mode=pl.Buffered(w_buffers))
    in_specs = [pl.BlockSpec((tm, p.shape[1]), lambda j, i: (i, 0)) for p in pieces]
    if layer is None:
        in_specs += [pl.BlockSpec((kdim, tn), lambda j, i: (0, j), **wmode) for _ in ws]
    else:
        in_specs += [pl.BlockSpec((1, kdim, tn), lambda j, i: (layer, 0, j), **wmode) for _ in ws]
    args = [*pieces, *ws]
    if mode == "resid":
        in_specs += [pl.BlockSpec((tm, tn), lambda j, i: (i, j)),
                     pl.BlockSpec((8, tn), lambda j, i: (0, gate_col * (D // tn) + j))]
        args += [res, modtab]
    return pl.pallas_call(
        partial(_mm_kernel, mode=mode, tm=tm, na=len(pieces), stacked=layer is not None),
        grid=(n // tn, m // tm),
        in_specs=in_specs,
        out_specs=pl.BlockSpec((tm, tn), lambda j, i: (i, j)),
        out_shape=jax.ShapeDtypeStruct((m, n), out_dtype),
        scratch_shapes=[pltpu.VMEM((kdim, tn), BF16) for _ in ws],
        compiler_params=_cparams(("arbitrary", "arbitrary")),
        name=name,
    )(*args)


def _gmm_kernel(be_ref, fresh_ref, run_ref, nexte_ref, lastrun_ref, meta_ref, a_ref, *rest, mode, nw, tn):
    w_hbm = rest[:nw]
    o_ref = rest[nw]
    stage_ref, ws_ref, sem = rest[nw + 1:]
    j = pl.program_id(0)
    i = pl.program_id(1)
    n_used, n_runs = meta_ref[0], meta_ref[1]

    def w_copy(widx, e, jj):
        col = pl.multiple_of(jj * tn, tn)
        return pltpu.make_async_copy(w_hbm[widx].at[e, :, pl.ds(col, tn)], stage_ref.at[widx], sem.at[widx])

    @pl.when(fresh_ref[i] == 1)
    def _():
        @pl.when(j * n_runs + run_ref[i] == 0)
        def _():
            for widx in range(nw):
                w_copy(widx, be_ref[i], j).start()

        for widx in range(nw):
            w_copy(widx, be_ref[i], j).wait()
            ws_ref[widx] = stage_ref[widx].astype(BF16)
        is_last = lastrun_ref[i] == 1

        @pl.when(jnp.logical_not(jnp.logical_and(is_last, j == pl.num_programs(0) - 1)))
        def _():
            jn = jnp.where(is_last, j + 1, j)
            for widx in range(nw):
                w_copy(widx, nexte_ref[i], jn).start()

    @pl.when(i < n_used)
    def _():
        _mm_compute(mode, a_ref[...], [ws_ref.at[widx] for widx in range(nw)], o_ref)

    @pl.when(i >= n_used)
    def _():
        o_ref[...] = jnp.zeros_like(o_ref)


def grouped_matmul(a, ws, sched, mode="plain", out_dtype=F32, tm=512, tn=512, name="grouped_matmul"):
    m, kdim = a.shape
    n = ws[0].shape[-1]
    nw = len(ws)
    row = lambda j, i, be, fr, ru, ne, lr, meta: jnp.minimum(i, meta[0] - 1)
    grid_spec = pltpu.PrefetchScalarGridSpec(
        num_scalar_prefetch=len(sched),
        grid=(n // tn, m // tm),
        in_specs=[pl.BlockSpec((tm, kdim), lambda *s: (row(*s), 0))] + [pl.BlockSpec(memory_space=pl.ANY)] * nw,
        out_specs=pl.BlockSpec((tm, tn), lambda j, i, *_: (i, j)),
        scratch_shapes=[pltpu.VMEM((nw, kdim, tn), F32), pltpu.VMEM((nw, kdim, tn), BF16),
                        pltpu.SemaphoreType.DMA((nw,))])
    return pl.pallas_call(
        partial(_gmm_kernel, mode=mode, nw=nw, tn=tn),
        grid_spec=grid_spec,
        out_shape=jax.ShapeDtypeStruct((m, n), out_dtype),
        compiler_params=_cparams(("arbitrary", "arbitrary")),
        name=name,
    )(*sched, a, *ws)


def _fwd_chunk(i):
    return i


def _bwd_chunk(i):
    return jnp.where(i < CH_CTX, CH_CTX - 1 - i, CH_PER_B + CH_CTX - 1 - i)


def _tri_masks():
    r = lax.broadcasted_iota(jnp.int32, (CHUNK, CHUNK), 0)
    c = lax.broadcasted_iota(jnp.int32, (CHUNK, CHUNK), 1)
    return c <= r, c >= r


def _mlstm_kernel(qf_ref, kf_ref, vf_ref, gcf_ref, grf_ref,
                  qb_ref, kb_ref, vb_ref, gcb_ref, grb_ref,
                  biasc_ref, biasr_ref, hf_ref, hb_ref, c_ref, n_ref, m_ref):
    @pl.when(pl.program_id(1) == 0)
    def _():
        c_ref[...] = jnp.zeros_like(c_ref)
        n_ref[...] = jnp.zeros_like(n_ref)
        m_ref[...] = jnp.zeros_like(m_ref)

    low, upp = _tri_masks()
    lowf, uppf = low.astype(F32), upp.astype(F32)
    streams = ((0, qf_ref, kf_ref, vf_ref, gcf_ref, grf_ref, hf_ref, low, lowf, uppf, CHUNK - 1),
               (1, qb_ref, kb_ref, vb_ref, gcb_ref, grb_ref, hb_ref, upp, uppf, lowf, 0))
    for d, q_ref, k_ref, v_ref, gc_ref, gr_ref, h_ref, mask, tri_c, tri_r, last in streams:
        ac = gc_ref[...] + biasc_ref[...]
        ar = gr_ref[0] + biasr_ref[...]
        lfc = -_softplus(-ac)
        lfr = -_softplus(-ar)
        bc_all = jnp.dot(tri_c, lfc, precision=HI, preferred_element_type=F32)
        br_all = jnp.dot(lfr, tri_r, precision=HI, preferred_element_type=F32)
        for h in range(ML_HEADS):
            s_idx = d * ML_HEADS + h
            ji, jf = G_I + s_idx, G_F + s_idx
            cols = slice(h * ML_HD, (h + 1) * ML_HD)
            q = q_ref[:, cols] * (ML_HD ** -0.5)
            k = k_ref[:, cols]
            v = v_ref[:, cols]
            bc = bc_all[:, jf:jf + 1]
            br = br_all[jf:jf + 1, :]
            icc = ac[:, ji:ji + 1]
            icr = ar[ji:ji + 1, :]
            cmat = c_ref[s_idx]
            nvec = n_ref[s_idx]
            m_prev = m_ref[s_idx][:, 0:1]
            dmat = jnp.where(mask, bc - br + icr, -1e30)
            inter = bc + m_prev
            mt = jnp.maximum(inter, jnp.max(dmat, axis=1, keepdims=True))
            w_intra = jnp.exp(dmat - mt)
            w_state = jnp.exp(inter - mt)
            s = _dot_nt(q, k) * w_intra
            num = jnp.dot(s, v, preferred_element_type=F32) + w_state * _dot_nt(q, cmat)
            den = jnp.sum(s, axis=1, keepdims=True) + w_state * jnp.sum(q * nvec, axis=1, keepdims=True)
            h_ref[:, cols] = num / jnp.maximum(jnp.abs(den), jnp.exp(-mt))
            bl = bc[last:last + 1, :]
            g = bl - bc + icc
            m_new = jnp.maximum(bl + m_prev, jnp.max(g, axis=0, keepdims=True))
            wg = jnp.exp(g - m_new)
            wc = jnp.exp(bl + m_prev - m_new)
            c_ref[s_idx] = wc * cmat + _dot_tn(wg * v, k)
            n_ref[s_idx] = wc * nvec + jnp.sum(wg * k, axis=0, keepdims=True)
            m_ref[s_idx] = jnp.broadcast_to(m_new, (1, 128))


def mlstm_scan(p, gates_t, biasc, biasr):
    def rows(chunk_fn):
        return lambda b, i: b * CH_PER_B + chunk_fn(i)

    def stream_specs(chunk_fn):
        r = rows(chunk_fn)
        return [pl.BlockSpec((CHUNK, 512), lambda b, i: (r(b, i), C_Q // 512)),
                pl.BlockSpec((CHUNK, 512), lambda b, i: (r(b, i), C_K // 512)),
                pl.BlockSpec((CHUNK, 512), lambda b, i: (r(b, i), C_V // 512)),
                pl.BlockSpec((CHUNK, 128), lambda b, i: (r(b, i), C_GATE // 128)),
                pl.BlockSpec((1, G_USED, CHUNK), lambda b, i: (r(b, i), 0, 0))]

    rf, rb = rows(_fwd_chunk), rows(_bwd_chunk)
    n_streams = 2 * ML_HEADS
    return pl.pallas_call(
        _mlstm_kernel,
        grid=(NB, CH_PER_B),
        in_specs=stream_specs(_fwd_chunk) + stream_specs(_bwd_chunk) + [
            pl.BlockSpec((1, 128), lambda b, i: (0, 0)),
            pl.BlockSpec((G_USED, 1), lambda b, i: (0, 0))],
        out_specs=[pl.BlockSpec((CHUNK, 512), lambda b, i: (rf(b, i), 0)),
                   pl.BlockSpec((CHUNK, 512), lambda b, i: (rb(b, i), 0))],
        out_shape=[jax.ShapeDtypeStruct((N_ROWS, 512), F32)] * 2,
        scratch_shapes=[pltpu.VMEM((n_streams, ML_HD, ML_HD), F32),
                        pltpu.VMEM((n_streams, 1, ML_HD), F32),
                        pltpu.VMEM((n_streams, 1, 128), F32)],
        compiler_params=_cparams(("arbitrary", "arbitrary")),
        name="mlstm_scan",
    )(p, p, p, p, gates_t, p, p, p, p, gates_t, biasc, biasr)


def _mlstm_finish_kernel(hf_ref, hb_ref, o_ref, g_ref, out_ref):
    for h in range(ML_HEADS):
        cols = slice(h * ML_HD, (h + 1) * ML_HD)
        x = hf_ref[:, cols] + hb_ref[:, cols]
        y = x * lax.rsqrt(jnp.mean(x * x, axis=-1, keepdims=True) + EPS) * g_ref[:, cols]
        out_ref[:, cols] = (y * jax.nn.sigmoid(o_ref[:, cols])).astype(BF16)


def mlstm_finish(hf, hb, p, norm_g):
    blk = lambda c: pl.BlockSpec((ROWBLK, 512), lambda i: (i, c))
    return pl.pallas_call(
        _mlstm_finish_kernel,
        grid=(N_ROWS // ROWBLK,),
        in_specs=[blk(0), blk(0), blk(C_O // 512), pl.BlockSpec((1, 512), lambda i: (0, 0))],
        out_specs=blk(0),
        out_shape=jax.ShapeDtypeStruct((N_ROWS, 512), BF16),
        compiler_params=_cparams(("arbitrary",)),
        name="mlstm_finish",
    )(hf, hb, p, norm_g.reshape(1, 512))


_PAD = 8


def _ssd_conv_kernel(u_ref, w_ref, b_ref, o_ref, pad_ref):
    zeros = jnp.zeros((_PAD, u_ref.shape[1]), F32)
    segs = ((0, T_CTX, _PAD), (T_CTX, T_LAT, 2 * _PAD + T_CTX))
    pad_ref[0:_PAD] = zeros
    pad_ref[_PAD + T_CTX:2 * _PAD + T_CTX] = zeros
    pad_ref[2 * _PAD + S_ALL:3 * _PAD + S_ALL] = zeros
    for src, n, dst in segs:
        pad_ref[dst:dst + n] = u_ref[src:src + n]
    for src, n, dst in segs:
        acc = jnp.broadcast_to(b_ref[...], (n, u_ref.shape[1]))
        for j in range(SSD_CONV):
            lo = dst + j - SSD_CONV // 2
            acc = acc + w_ref[j:j + 1, :] * pad_ref[lo:lo + n]
        o_ref[src:src + n] = _silu(acc)


def ssd_conv(p, conv_w, conv_b):
    tc = 256
    return pl.pallas_call(
        _ssd_conv_kernel,
        grid=(NB, 1024 // tc),
        in_specs=[pl.BlockSpec((S_ALL, tc), lambda b, c: (b, C_XBC // tc + c)),
                  pl.BlockSpec((SSD_CONV, tc), lambda b, c: (0, c)),
                  pl.BlockSpec((1, tc), lambda b, c: (0, c))],
        out_specs=pl.BlockSpec((S_ALL, tc), lambda b, c: (b, c)),
        out_shape=jax.ShapeDtypeStruct((N_ROWS, 1024), F32),
        scratch_shapes=[pltpu.VMEM((S_ALL + 3 * _PAD, tc), F32)],
        compiler_params=_cparams(("arbitrary", "arbitrary")),
        name="ssd_conv",
    )(p, conv_w, conv_b.reshape(1, 1024))


def _ssd_kernel(xf_ref, bf_ref, cf_ref, gcf_ref, grf_ref,
                xb_ref, bb_ref, cb_ref, gcb_ref, grb_ref,
                biasc_ref, biasr_ref, alogc_ref, alogr_ref, yf_ref, yb_ref, st_ref):
    @pl.when(pl.program_id(1) == 0)
    def _():
        st_ref[...] = jnp.zeros_like(st_ref)

    low, upp = _tri_masks()
    lowf, uppf = low.astype(F32), upp.astype(F32)
    a_c = -jnp.exp(alogc_ref[...])
    a_r = -jnp.exp(alogr_ref[...])
    streams = ((0, xf_ref, bf_ref, cf_ref, gcf_ref, grf_ref, yf_ref, low, lowf, uppf, CHUNK - 1),
               (1, xb_ref, bb_ref, cb_ref, gcb_ref, grb_ref, yb_ref, upp, uppf, lowf, 0))
    for d, x_ref, b_ref, c_ref, gc_ref, gr_ref, y_ref, mask, tri_c, tri_r, last in streams:
        dtc_all = _softplus(gc_ref[...] + biasc_ref[...])
        dtr_all = _softplus(gr_ref[0] + biasr_ref[...])
        cumc_all = jnp.dot(tri_c, dtc_all * a_c, precision=HI, preferred_element_type=F32)
        cumr_all = jnp.dot(dtr_all * a_r, tri_r, precision=HI, preferred_element_type=F32)
        for g in range(SSD_GROUPS):
            gcols = slice(g * SSD_STATE, (g + 1) * SSD_STATE)
            bm = b_ref[:, gcols]
            cm = c_ref[:, gcols]
            cb = _dot_nt(cm, bm)
            for hh in range(SSD_HEADS // SSD_GROUPS):
                h = g * (SSD_HEADS // SSD_GROUPS) + hh
                j = G_DT + d * SSD_HEADS + h
                s_idx = d * SSD_HEADS + h
                cols = slice(h * SSD_HD, (h + 1) * SSD_HD)
                x = x_ref[:, cols]
                cc = cumc_all[:, j:j + 1]
                cr = cumr_all[j:j + 1, :]
                dtc = dtc_all[:, j:j + 1]
                dtr = dtr_all[j:j + 1, :]
                st = st_ref[s_idx]
                decay = jnp.exp(jnp.where(mask, cc - cr, -1e30))
                w = cb * decay * dtr
                y_ref[:, cols] = (jnp.dot(w, x, preferred_element_type=F32)
                                  + _dot_nt(cm, st) * jnp.exp(cc))
                cl = cc[last:last + 1, :]
                tail = jnp.exp(cl - cc) * dtc
                st_ref[s_idx] = st * jnp.exp(cl) + _dot_tn(tail * x, bm)


def ssd_scan(xbc, p, gates_t, biasc, biasr, alogc, alogr):
    def rows(chunk_fn):
        return lambda b, i: b * CH_PER_B + chunk_fn(i)

    def stream_specs(chunk_fn):
        r = rows(chunk_fn)
        return [pl.BlockSpec((CHUNK, 512), lambda b, i: (r(b, i), 0)),
                pl.BlockSpec((CHUNK, 256), lambda b, i: (r(b, i), 2)),
                pl.BlockSpec((CHUNK, 256), lambda b, i: (r(b, i), 3)),
                pl.BlockSpec((CHUNK, 128), lambda b, i: (r(b, i), C_GATE // 128)),
                pl.BlockSpec((1, G_USED, CHUNK), lambda b, i: (r(b, i), 0, 0))]

    rf, rb = rows(_fwd_chunk), rows(_bwd_chunk)
    const = lambda shape: pl.BlockSpec(shape, lambda b, i: (0, 0))
    return pl.pallas_call(
        _ssd_kernel,
        grid=(NB, CH_PER_B),
        in_specs=stream_specs(_fwd_chunk) + stream_specs(_bwd_chunk) + [
            const((1, 128)), const((G_USED, 1)), const((1, 128)), const((G_USED, 1))],
        out_specs=[pl.BlockSpec((CHUNK, 512), lambda b, i: (rf(b, i), 0)),
                   pl.BlockSpec((CHUNK, 512), lambda b, i: (rb(b, i), 0))],
        out_shape=[jax.ShapeDtypeStruct((N_ROWS, 512), F32)] * 2,
        scratch_shapes=[pltpu.VMEM((2 * SSD_HEADS, SSD_HD, SSD_STATE), F32)],
        compiler_params=_cparams(("arbitrary", "arbitrary")),
        name="ssd_scan",
    )(xbc, xbc, xbc, p, gates_t, xbc, xbc, xbc, p, gates_t, biasc, biasr, alogc, alogr)


def _ssd_finish_kernel(yf_ref, yb_ref, xs_ref, z_ref, dsk_ref, g_ref, out_ref):
    y = yf_ref[...] + yb_ref[...] + dsk_ref[...] * xs_ref[...]
    y = y * _silu(z_ref[...])
    out_ref[...] = (y * lax.rsqrt(jnp.mean(y * y, axis=-1, keepdims=True) + EPS) * g_ref[...]).astype(BF16)


def ssd_finish(yf, yb, xbc, p, d_skip, norm_g):
    blk = lambda c: pl.BlockSpec((ROWBLK, 512), lambda i: (i, c))
    vec = pl.BlockSpec((1, 512), lambda i: (0, 0))
    return pl.pallas_call(
        _ssd_finish_kernel,
        grid=(N_ROWS // ROWBLK,),
        in_specs=[blk(0), blk(0), blk(0), blk(C_Z // 512), vec, vec],
        out_specs=blk(0),
        out_shape=jax.ShapeDtypeStruct((N_ROWS, 512), BF16),
        compiler_params=_cparams(("arbitrary",)),
        name="ssd_finish",
    )(yf, yb, xbc, p, jnp.repeat(d_skip, SSD_HD).reshape(1, 512), norm_g.reshape(1, 512))


def _rope_tile(r, cs_ref, s1_ref, s2_ref):
    return (r * cs_ref[...] + pltpu.roll(r, 128 - MLA_ROPE // 2, 1) * s1_ref[...]
            + pltpu.roll(r, MLA_ROPE // 2, 1) * s2_ref[...])


def _mla_prep_kernel(cq_ref, ckv_ref, kr_ref, wq_ref, wkv_ref, qn_ref, kvn_ref, gq_ref, gk_ref,
                     cs_ref, s1_ref, s2_ref, q_out, k_out, v_out):
    cq = cq_ref[...]
    cqn = cq * lax.rsqrt(jnp.sum(cq * cq, axis=-1, keepdims=True) / MLA_Q_RANK + EPS) * qn_ref[...]
    q_raw = jnp.dot(cqn, wq_ref[...], preferred_element_type=F32)
    ckv = ckv_ref[...]
    ckvn = ckv * lax.rsqrt(jnp.sum(ckv * ckv, axis=-1, keepdims=True) / MLA_KV_RANK + EPS) * kvn_ref[...]
    kv_raw = jnp.dot(ckvn, wkv_ref[...], preferred_element_type=F32)
    kr = kr_ref[...]
    kr_ss = jnp.sum(kr * kr, axis=-1, keepdims=True)
    for h in range(MLA_HEADS):
        base = h * MLA_HP
        qa = q_raw[:, base:base + 128]
        qb = q_raw[:, base + 128:base + 256]
        q_scale = lax.rsqrt((jnp.sum(qa * qa, axis=-1, keepdims=True)
                             + jnp.sum(qb * qb, axis=-1, keepdims=True)) / MLA_QK + EPS)
        q_out[:, base:base + 128] = (qa * q_scale * gq_ref[:, 0:128]).astype(BF16)
        q_out[:, base + 128:base + 256] = _rope_tile(qb * q_scale * gq_ref[:, 128:256],
                                                     cs_ref, s1_ref, s2_ref).astype(BF16)
        kn = kv_raw[:, base:base + 128]
        k_scale = lax.rsqrt((jnp.sum(kn * kn, axis=-1, keepdims=True) + kr_ss) / MLA_QK + EPS)
        k_out[:, base:base + 128] = (kn * k_scale * gk_ref[:, 0:128]).astype(BF16)
        k_out[:, base + 128:base + 256] = _rope_tile(kr * k_scale * gk_ref[:, 128:256],
                                                     cs_ref, s1_ref, s2_ref).astype(BF16)
        v_out[:, h * MLA_V:(h + 1) * MLA_V] = kv_raw[:, base + 128:base + 256].astype(BF16)


def mla_prep(p, wq, wkv, qn, kvn, gq, gk, rope_tabs):
    const = lambda shape: pl.BlockSpec(shape, lambda i: (0, 0))
    tab = pl.BlockSpec((ROWBLK, 128), lambda i: (i % BLK_PER_B, 0))
    row = lambda w: pl.BlockSpec((ROWBLK, w), lambda i: (i, 0))
    return pl.pallas_call(
        _mla_prep_kernel,
        grid=(N_ROWS // ROWBLK,),
        in_specs=[pl.BlockSpec((ROWBLK, 512), lambda i: (i, C_CQ // 512)),
                  pl.BlockSpec((ROWBLK, 256), lambda i: (i, C_CKV // 256)),
                  pl.BlockSpec((ROWBLK, 128), lambda i: (i, C_KR // 128)),
                  const((512, MLA_HEADS * MLA_HP)), const((256, MLA_HEADS * MLA_HP)),
                  const((1, 512)), const((1, 256)), const((1, MLA_HP)), const((1, MLA_HP)),
                  tab, tab, tab],
        out_specs=[row(MLA_HEADS * MLA_HP), row(MLA_HEADS * MLA_HP), row(MLA_HEADS * MLA_V)],
        out_shape=[jax.ShapeDtypeStruct((N_ROWS, MLA_HEADS * MLA_HP), BF16),
                   jax.ShapeDtypeStruct((N_ROWS, MLA_HEADS * MLA_HP), BF16),
                   jax.ShapeDtypeStruct((N_ROWS, MLA_HEADS * MLA_V), BF16)],
        compiler_params=_cparams(("arbitrary",)),
        name="mla_prep",
    )(p, p, p, wq, wkv, qn, kvn, gq, gk, *rope_tabs)


def _softmax_pv(s_list, v_list):
    m = s_list[0].max(axis=-1, keepdims=True)
    for s in s_list[1:]:
        m = jnp.maximum(m, s.max(axis=-1, keepdims=True))
    ps = [jnp.exp(s - m) for s in s_list]
    l = sum(p.sum(axis=-1, keepdims=True) for p in ps)
    o = sum(jnp.dot(p.astype(BF16), v, preferred_element_type=F32) for p, v in zip(ps, v_list))
    return o / l


def _mla_attn_kernel(q_ref, k_ref, v_ref, o_ref):
    scale = MLA_QK ** -0.5
    q = q_ref[...]

    @pl.when(pl.program_id(2) == 0)
    def _():
        s = _dot_nt(q, k_ref[0:T_CTX, :]) * scale
        o_ref[...] = _softmax_pv([s], [v_ref[0:T_CTX, :]]).astype(BF16)

    @pl.when(pl.program_id(2) > 0)
    def _():
        s = _dot_nt(q, k_ref[...]) * scale
        o_ref[...] = _softmax_pv([s], [v_ref[...]]).astype(BF16)


def mla_attention(q, k, v):
    return pl.pallas_call(
        _mla_attn_kernel,
        grid=(NB, MLA_HEADS, BLK_PER_B),
        in_specs=[pl.BlockSpec((ROWBLK, MLA_HP), lambda b, h, i: (b * BLK_PER_B + i, h)),
                  pl.BlockSpec((S_ALL, MLA_HP), lambda b, h, i: (b, h)),
                  pl.BlockSpec((S_ALL, MLA_V), lambda b, h, i: (b, h))],
        out_specs=pl.BlockSpec((ROWBLK, MLA_V), lambda b, h, i: (b * BLK_PER_B + i, h)),
        out_shape=jax.ShapeDtypeStruct((N_ROWS, MLA_HEADS * MLA_V), BF16),
        compiler_params=_cparams(("arbitrary", "arbitrary", "arbitrary")),
        name="mla_attention",
    )(q, k, v)


N_GRID_ROWS = T_LAT // GRID_W
NA_QROWS = ROWBLK // GRID_W
NA_GROUPS = N_GRID_ROWS // NA_QROWS
NA_WIN_ROWS = NA_KH + NA_QROWS - 1
NA_WIN = NA_WIN_ROWS * GRID_W


def _na_win_row(g):
    return jnp.clip(g * NA_QROWS - NA_KH // 2, 0, N_GRID_ROWS - NA_WIN_ROWS)


def _na_kernel(q_ref, k_ref, v_ref, gq_ref, gk_ref, bias_ref, o_ref, kn_ref, vb_ref):
    st = pl.program_id(2)
    scale = NA_HD ** -0.5

    @pl.when(st == 0)
    def _():
        k = k_ref[...]
        kn_ref[...] = (k * lax.rsqrt(jnp.mean(k * k, axis=-1, keepdims=True) + EPS) * gk_ref[...]).astype(BF16)
        vb_ref[...] = v_ref[...].astype(BF16)

    q = q_ref[...]
    qn = (q * lax.rsqrt(jnp.mean(q * q, axis=-1, keepdims=True) + EPS) * gq_ref[...]).astype(BF16)
    s_cx = _dot_nt(qn, kn_ref[0:T_CTX, :]) * scale
    v_cx = vb_ref[0:T_CTX, :]

    @pl.when(st == 0)
    def _():
        o_ref[...] = _softmax_pv([s_cx], [v_cx]).astype(BF16)

    @pl.when(st > 0)
    def _():
        start = pl.multiple_of(T_CTX + _na_win_row(st - 1) * GRID_W, GRID_W)
        s_nb = _dot_nt(qn, kn_ref[pl.ds(start, NA_WIN), :]) * scale + bias_ref[0, 0]
        o_ref[...] = _softmax_pv([s_nb, s_cx], [vb_ref[pl.ds(start, NA_WIN), :], v_cx]).astype(BF16)


def _na_class(g):
    return jnp.where(g == 0, 0, jnp.where(g == NA_GROUPS - 1, 2, 1))


def na_attention(p, gq, gk, bias_tab):
    return pl.pallas_call(
        _na_kernel,
        grid=(NB, NA_HEADS, BLK_PER_B),
        in_specs=[pl.BlockSpec((ROWBLK, NA_HD), lambda b, h, st: (b * BLK_PER_B + st, C_NAQ // NA_HD + h)),
                  pl.BlockSpec((S_ALL, NA_HD), lambda b, h, st: (b, C_NAK // NA_HD + h)),
                  pl.BlockSpec((S_ALL, NA_HD), lambda b, h, st: (b, C_NAV // NA_HD + h)),
                  pl.BlockSpec((1, NA_HD), lambda b, h, st: (0, 0)),
                  pl.BlockSpec((1, NA_HD), lambda b, h, st: (0, 0)),
                  pl.BlockSpec((1, 1, ROWBLK, NA_WIN),
                               lambda b, h, st: (h, _na_class(jnp.maximum(st - 1, 0)), 0, 0))],
        out_specs=pl.BlockSpec((ROWBLK, NA_HD), lambda b, h, st: (b * BLK_PER_B + st, h)),
        out_shape=jax.ShapeDtypeStruct((N_ROWS, NA_HEADS * NA_HD), BF16),
        scratch_shapes=[pltpu.VMEM((S_ALL, NA_HD), BF16), pltpu.VMEM((S_ALL, NA_HD), BF16)],
        compiler_params=_cparams(("arbitrary", "arbitrary", "arbitrary")),
        name="na_attention",
    )(p, p, p, gq.reshape(1, NA_HD), gk.reshape(1, NA_HD), bias_tab)


def _na_window_tables():
    g_rep = np.array([0, 1, NA_GROUPS - 1])
    r = g_rep[:, None] * NA_QROWS + np.arange(NA_QROWS)[None, :]
    r0 = np.clip(r - NA_KH // 2, 0, N_GRID_ROWS - NA_KH)
    w0 = np.clip(g_rep * NA_QROWS - NA_KH // 2, 0, N_GRID_ROWS - NA_WIN_ROWS)
    kr = w0[:, None] + np.arange(NA_WIN_ROWS)[None, :]
    valid_r = (kr[:, None, :] >= r0[:, :, None]) & (kr[:, None, :] < r0[:, :, None] + NA_KH)
    dr = kr[:, None, :] - r[:, :, None] + NA_KH - 1
    return dr, valid_r


def _na_bias_kernel(toe_ref, o_ref):
    dr, valid_r = _na_window_tables()
    masked = jnp.full((GRID_W, GRID_W), -1e30, F32)
    for cls in range(dr.shape[0]):
        @pl.when(pl.program_id(1) == cls)
        def _(cls=cls):
            for a in range(NA_QROWS):
                for jw in range(NA_WIN_ROWS):
                    blk = toe_ref[0, int(dr[cls, a, jw])] if valid_r[cls, a, jw] else masked
                    o_ref[0, 0, a * GRID_W:(a + 1) * GRID_W, jw * GRID_W:(jw + 1) * GRID_W] = blk


def na_bias_table(rpb):
    c = np.arange(GRID_W)[:, None]
    kc = np.arange(GRID_W)[None, :]
    c0 = np.clip(c - NA_KW // 2, 0, GRID_W - NA_KW)
    valid_c = (kc >= c0) & (kc < c0 + NA_KW)
    dc = kc - c + NA_KW - 1
    onehot = (np.arange(2 * NA_KW - 1)[:, None, None] == dc[None]) & valid_c[None]
    toe = jnp.einsum('hrd,dck->hrck', rpb, jnp.asarray(onehot, F32), precision=HI)
    toe = jnp.where(valid_c[None, None], toe, -1e30)
    n_rows = 2 * NA_KH - 1
    return pl.pallas_call(
        _na_bias_kernel,
        grid=(NA_HEADS, 3),
        in_specs=[pl.BlockSpec((1, n_rows, GRID_W, GRID_W), lambda h, s: (h, 0, 0, 0))],
        out_specs=pl.BlockSpec((1, 1, ROWBLK, NA_WIN), lambda h, s: (h, s, 0, 0)),
        out_shape=jax.ShapeDtypeStruct((NA_HEADS, 3, ROWBLK, NA_WIN), F32),
        compiler_params=_cparams(("arbitrary", "arbitrary")),
        name="na_bias",
    )(toe)


N_TOK = NB * T_LAT
MOE_TM = 512
N_BLK = N_TOK * TOP_K // MOE_TM + N_EXPERTS
N_SLOT = N_BLK * MOE_TM
GATHER_ROWS = 256
COMBINE_TOK = 128


def _row_copies(n_rows, copy_fn):
    def start_all():
        def body(r8, carry):
            for u in range(8):
                copy_fn(r8 * 8 + u).start(priority=u % 2)
            return carry
        lax.fori_loop(0, n_rows // 8, body, 0)

    def wait_all():
        def body(r, carry):
            copy_fn(r).wait()
            return carry
        lax.fori_loop(0, n_rows, body, 0, unroll=8)

    return start_all, wait_all


def _gather_kernel(idx_ref, nu_ref, h_hbm, o_ref, buf_ref, sem):
    i = pl.program_id(0)
    n_steps = nu_ref[0] * (MOE_TM // GATHER_ROWS)

    def copies(step):
        slot = step % 2
        return _row_copies(GATHER_ROWS, lambda r: pltpu.make_async_copy(
            h_hbm.at[pl.ds(idx_ref[step * GATHER_ROWS + r], 1)], buf_ref.at[slot, pl.ds(r, 1)], sem.at[slot]))

    @pl.when(i == 0)
    def _():
        copies(i)[0]()

    @pl.when(i + 1 < n_steps)
    def _():
        copies(i + 1)[0]()

    @pl.when(i < n_steps)
    def _():
        copies(i)[1]()
        o_ref[...] = buf_ref[i % 2].astype(BF16)

    @pl.when(i >= n_steps)
    def _():
        o_ref[...] = jnp.zeros_like(o_ref)


def moe_gather(h2, slot_row, n_used):
    return pl.pallas_call(
        _gather_kernel,
        grid_spec=pltpu.PrefetchScalarGridSpec(
            num_scalar_prefetch=2,
            grid=(N_SLOT // GATHER_ROWS,),
            in_specs=[pl.BlockSpec(memory_space=pl.ANY)],
            out_specs=pl.BlockSpec((GATHER_ROWS, D), lambda i, idx, nu: (i, 0)),
            scratch_shapes=[pltpu.VMEM((2, GATHER_ROWS, D), F32), pltpu.SemaphoreType.DMA((2,))]),
        out_shape=jax.ShapeDtypeStruct((N_SLOT, D), BF16),
        compiler_params=_cparams(("arbitrary",)),
        name="moe_gather",
    )(slot_row, n_used, h2)


def _combine_kernel(slot_ref, y_hbm, x_ref, gate_ref, g2_ref, o_ref, buf_ref, sem):
    i = pl.program_id(0)

    def copies(step):
        slot = step % 2
        base = step * COMBINE_TOK * TOP_K
        return _row_copies(COMBINE_TOK * TOP_K, lambda r: pltpu.make_async_copy(
            y_hbm.at[pl.ds(slot_ref[base + r], 1)], buf_ref.at[slot, pl.ds(r, 1)], sem.at[slot]))

    @pl.when(i == 0)
    def _():
        copies(i)[0]()

    @pl.when(i + 1 < N_TOK // COMBINE_TOK)
    def _():
        copies(i + 1)[0]()

    copies(i)[1]()
    b = i // (T_LAT // COMBINE_TOK)
    rows = buf_ref[i % 2]
    f = gate_ref[:, 0:1] * rows[0:COMBINE_TOK] + gate_ref[:, 1:2] * rows[COMBINE_TOK:2 * COMBINE_TOK]
    o_ref[...] = x_ref[...] + g2_ref[pl.ds(b, 1), :] * f


def moe_combine(yb, x, slots_km, gates, modtab):
    per_b = T_LAT // COMBINE_TOK
    x_map = lambda i, s: ((i // per_b) * (S_ALL // COMBINE_TOK) + T_CTX // COMBINE_TOK + i % per_b, 0)
    return pl.pallas_call(
        _combine_kernel,
        grid_spec=pltpu.PrefetchScalarGridSpec(
            num_scalar_prefetch=1,
            grid=(N_TOK // COMBINE_TOK,),
            in_specs=[pl.BlockSpec(memory_space=pl.ANY),
                      pl.BlockSpec((COMBINE_TOK, D), x_map),
                      pl.BlockSpec((COMBINE_TOK, TOP_K), lambda i, s: (i, 0)),
                      pl.BlockSpec((8, D), lambda i, s: (0, 5))],
            out_specs=pl.BlockSpec((COMBINE_TOK, D), lambda i, s: (i, 0)),
            scratch_shapes=[pltpu.VMEM((2, COMBINE_TOK * TOP_K, D), F32), pltpu.SemaphoreType.DMA((2,))]),
        out_shape=jax.ShapeDtypeStruct((N_TOK, D), F32),
        compiler_params=_cparams(("arbitrary",)),
        name="moe_combine",
    )(slots_km, yb, x, gates, modtab)


def moe_routing(logits):
    top_v, top_e = lax.top_k(logits, TOP_K)
    gates = jax.nn.softmax(top_v, axis=-1)
    flat_e = top_e.reshape(-1)
    onehot = (flat_e[:, None] == jnp.arange(N_EXPERTS)[None, :]).astype(jnp.int32)
    rank = jnp.take_along_axis(jnp.cumsum(onehot, axis=0) - onehot, flat_e[:, None], axis=1)[:, 0]
    counts = onehot.sum(axis=0)
    padded = (counts + MOE_TM - 1) // MOE_TM * MOE_TM
    pend = jnp.cumsum(padded)
    slot = ((pend - padded)[flat_e] + rank).astype(jnp.int32)
    tok = jnp.arange(N_TOK, dtype=jnp.int32)
    tok_row = (tok // T_LAT) * S_ALL + T_CTX + tok % T_LAT
    slot_row = jnp.zeros((N_SLOT,), jnp.int32).at[slot].set(jnp.repeat(tok_row, TOP_K))
    n_used = (pend[-1] // MOE_TM).astype(jnp.int32)
    blk = jnp.arange(N_BLK)
    used = blk < n_used
    blk_e = jnp.sum(jnp.minimum(blk, n_used - 1)[:, None] * MOE_TM >= pend[None, :], axis=1).astype(jnp.int32)
    fresh = (used & ((blk == 0) | (blk_e != jnp.roll(blk_e, 1)))).astype(jnp.int32)
    run_idx = jnp.cumsum(fresh) - 1
    n_runs = fresh.sum()
    has = counts > 0
    ids = jnp.arange(N_EXPERTS)
    first_e = jnp.min(jnp.where(has, ids, N_EXPERTS))
    later = jnp.where(has[None, :] & (ids[None, :] > ids[:, None]), ids[None, :], N_EXPERTS).min(axis=1)
    next_of = jnp.where(later < N_EXPERTS, later, first_e)
    sched = (blk_e, fresh, run_idx.astype(jnp.int32), next_of[blk_e].astype(jnp.int32),
             (run_idx == n_runs - 1).astype(jnp.int32), jnp.stack([n_used, n_runs]).astype(jnp.int32))
    slots_km = slot.reshape(N_TOK // COMBINE_TOK, COMBINE_TOK, TOP_K).transpose(0, 2, 1).reshape(-1)
    return slot_row, sched, n_used.reshape(1), slots_km, gates


def _pad_cols(w, width):
    return jnp.pad(w, [(0, 0)] * (w.ndim - 1) + [(0, width - w.shape[-1])])


def layout_w_in(w):
    offs = np.cumsum((512, 512, 512, 512, 8, 8, 512, 1024, 16, 448, 160, 64, 512, 512, 512))[:-1].tolist()
    q, k, v, o, ig, fg, z, xbc, dt, cq, ckv, kr, naq, nak, nav = jnp.split(w, offs, axis=-1)
    gate = _pad_cols(jnp.concatenate([ig, fg, dt], axis=-1), 128)
    return jnp.concatenate([q, k, v, o, z, xbc, _pad_cols(cq, 512), _pad_cols(ckv, 256), _pad_cols(kr, 128),
                            naq, nak, nav, gate], axis=-1)


def rope_tables():
    t = np.arange(T_LAT)
    n_freq = MLA_ROPE // 4
    freqs = ROPE_THETA ** (-jnp.arange(n_freq, dtype=F32) / n_freq)
    row = jnp.asarray(t // GRID_W, F32)
    col = jnp.asarray(t % GRID_W, F32)
    ang = jnp.concatenate([row[:, None] * freqs, col[:, None] * freqs], axis=-1)
    cos, sin = jnp.cos(ang), jnp.sin(ang)
    half = MLA_ROPE // 2
    zc = jnp.zeros((T_LAT, 128 - MLA_ROPE), F32)
    zh = jnp.zeros((T_LAT, half), F32)
    cs = jnp.concatenate([cos, cos, zc], axis=1)
    s1 = jnp.concatenate([-sin, zh, zc], axis=1)
    s2 = jnp.concatenate([zh, sin, zc], axis=1)
    ident = jnp.concatenate([jnp.ones((T_CTX, MLA_ROPE), F32), jnp.zeros((T_CTX, 128 - MLA_ROPE), F32)], axis=1)
    zeros = jnp.zeros((T_CTX, 128), F32)
    return (jnp.concatenate([ident, cs], axis=0), jnp.concatenate([zeros, s1], axis=0),
            jnp.concatenate([zeros, s2], axis=0))


def _gate_vectors(i_bias, f_bias, dt_bias, a_log):
    used = jnp.concatenate([i_bias.reshape(-1), f_bias.reshape(-1), dt_bias.reshape(-1)])
    alog = jnp.concatenate([jnp.zeros((G_DT,), F32), a_log.reshape(-1)])
    padc = lambda u: jnp.pad(u, (0, 128 - G_USED)).reshape(1, 128)
    return padc(used), used.reshape(G_USED, 1), padc(alog), alog.reshape(G_USED, 1)


def kernel(x, c, ctx, c_ctx, mod_w, mod_b, norm1, w_in, w_out, ml_i_bias, ml_f_bias, ml_norm, ssd_conv_w, ssd_conv_b, ssd_dt_bias, ssd_A_log, ssd_D, ssd_norm, mla_q_norm, mla_w_qb, mla_kv_norm, mla_w_kvb, mla_gq, mla_gk, na_gq, na_gk, na_rpb, norm2, ffn_w1, ffn_w3, ffn_w2, moe_router, moe_w1, moe_w3, moe_w2):
    xs = jnp.concatenate([ctx, x], axis=1).reshape(N_ROWS, D)
    c_all = jnp.concatenate([c, c_ctx[None, :], jnp.zeros((8 - NB - 1, D), F32)], axis=0)
    mod_all = modulation(c_all, mod_w, mod_b)
    tabs = rope_tables()
    w_in_p = layout_w_in(w_in)
    out = None
    for l in range(DEPTH):
        modtab = mod_all[l]
        h = rms_modulate(xs, norm1[l], modtab, 0)
        p = matmul(h, [w_in_p], tm=1536, tn=1024, layer=l, name="w_in")
        gates_t = p[:, C_GATE:C_GATE + G_USED].reshape(N_ROWS // CHUNK, CHUNK, G_USED).transpose(0, 2, 1)
        biasc, biasr, alogc, alogr = _gate_vectors(ml_i_bias[l], ml_f_bias[l], ssd_dt_bias[l], ssd_A_log[l])

        hf, hb = mlstm_scan(p, gates_t, biasc, biasr)
        ml = mlstm_finish(hf, hb, p, ml_norm[l])

        xbc = ssd_conv(p, ssd_conv_w[l], ssd_conv_b[l])
        yf, yb = ssd_scan(xbc, p, gates_t, biasc, biasr, alogc, alogr)
        ss = ssd_finish(yf, yb, xbc, p, ssd_D[l], ssd_norm[l])

        wq = jnp.pad(mla_w_qb[l].reshape(MLA_Q_RANK, MLA_HEADS, MLA_QK),
                     ((0, 512 - MLA_Q_RANK), (0, 0), (0, MLA_HP - MLA_QK))).reshape(512, MLA_HEADS * MLA_HP)
        wkv = jnp.pad(mla_w_kvb[l], ((0, 256 - MLA_KV_RANK), (0, 0)))
        pad1 = lambda u, w: jnp.pad(u, (0, w - u.shape[0])).reshape(1, w)
        q, k, v = mla_prep(p, wq, wkv, pad1(mla_q_norm[l], 512), pad1(mla_kv_norm[l], 256),
                           pad1(mla_gq[l], MLA_HP), pad1(mla_gk[l], MLA_HP), tabs)
        la = mla_attention(q, k, v)

        na = na_attention(p, na_gq[l], na_gk[l], na_bias_table(na_rpb[l]))

        xs = matmul([ml, ss, la, na], [w_out], mode="resid", res=xs, modtab=modtab, gate_col=2,
                    tm=1536, layer=l, name="w_out")

        if l % 2 == 0:
            h2 = rms_modulate(xs, norm2[l], modtab, 3)
            hid = matmul(h2, [ffn_w1[l // 2], ffn_w3[l // 2]], mode="swiglu", out_dtype=BF16, tm=1536,
                         name="ffn_up")
            xs = matmul(hid, [ffn_w2[l // 2]], mode="resid", res=xs, modtab=modtab, gate_col=5,
                        w_buffers=1, name="ffn_down")
        else:
            h2, logits = rms_modulate(xs, norm2[l], modtab, 3, router=_pad_cols(moe_router[l // 2], 128))
            lat = logits.reshape(NB, S_ALL, 128)[:, T_CTX:, :N_EXPERTS].reshape(N_TOK, N_EXPERTS)
            slot_row, sched, n_used, slots_km, gates = moe_routing(lat)
            xb = moe_gather(h2, slot_row, n_used)
            hid = grouped_matmul(xb, [moe_w1[l // 2], moe_w3[l // 2]], sched, mode="swiglu", out_dtype=BF16,
                                 tm=MOE_TM, tn=1024, name="moe_up")
            yb_ = grouped_matmul(hid, [moe_w2[l // 2]], sched, tm=MOE_TM, tn=512, name="moe_down")
            out = moe_combine(yb_, xs, slots_km, gates, modtab)
    return out.reshape(NB, T_LAT, D)
```

```python
from functools import partial

import numpy as np
import jax
import jax.numpy as jnp
from jax import lax
from jax.experimental import pallas as pl
from jax.experimental.pallas import tpu as pltpu

F32 = jnp.float32
BF16 = jnp.bfloat16
HI = lax.Precision.HIGHEST

D = 2048
NB = 4
T_LAT = 2048
T_CTX = 256
S_ALL = T_LAT + T_CTX
N_ROWS = NB * S_ALL
DEPTH = 2
GRID_W = 64
EPS = 1e-6
LOG2E = 1.4426950408889634

ROWBLK = 256
BLK_PER_B = S_ALL // ROWBLK
CHUNK = 128
CH_PER_B = S_ALL // CHUNK
CH_CTX = T_CTX // CHUNK

ML_HEADS, ML_HD = 4, 128
SSD_HEADS, SSD_HD, SSD_GROUPS, SSD_STATE, SSD_CONV = 8, 64, 2, 128, 5
SSD_W = SSD_HEADS * SSD_HD
MLA_HEADS, MLA_NOPE, MLA_ROPE, MLA_V = 4, 128, 64, 128
MLA_QK = MLA_NOPE + MLA_ROPE
MLA_Q_RANK, MLA_KV_RANK = 448, 160
MLA_HP = 256
ROPE_THETA = 10000.0
NA_HEADS, NA_HD, NA_KH, NA_KW = 4, 128, 8, 16
N_EXPERTS, TOP_K = 8, 2

C_Q, C_K, C_V, C_O = 0, 512, 1024, 1536
C_Z = 2048
C_XBC = 2560
C_CQ, C_CKV, C_KR = 3584, 4096, 4352
C_NAQ, C_NAK, C_NAV = 4480, 4992, 5504
C_GATE = 6016
D_INP = 6144
G_I, G_F, G_DT, G_USED = 0, 8, 16, 32

VMEM_LIMIT = 56 * 1024 * 1024


def _cparams(sem):
    return pltpu.CompilerParams(dimension_semantics=sem, vmem_limit_bytes=VMEM_LIMIT)


def _mod_index(blk256):
    return jnp.where(blk256 % BLK_PER_B == 0, NB, blk256 // BLK_PER_B)


def _dot_nt(a, b):
    return lax.dot_general(a, b, (((1,), (1,)), ((), ())), preferred_element_type=F32)


def _dot_tn(a, b):
    return lax.dot_general(a, b, (((0,), (0,)), ((), ())), preferred_element_type=F32)


def _silu(x):
    return x * jax.nn.sigmoid(x)


def _softplus(x):
    return jnp.maximum(x, 0.0) + jnp.log(1.0 + jnp.exp(-jnp.abs(x)))


def _mod_kernel(c_ref, w_ref, b_ref, o_ref):
    o_ref[0] = jnp.dot(_silu(c_ref[...]), w_ref[0], preferred_element_type=F32) + b_ref[0]


def modulation(c_all, mod_w, mod_b):
    tn = 1024
    n_out = mod_w.shape[-1]
    return pl.pallas_call(
        _mod_kernel,
        grid=(DEPTH, n_out // tn),
        in_specs=[pl.BlockSpec((8, D), lambda l, j: (0, 0)),
                  pl.BlockSpec((1, D, tn), lambda l, j: (l, 0, j)),
                  pl.BlockSpec((1, 1, tn), lambda l, j: (l, 0, j))],
        out_specs=pl.BlockSpec((1, 8, tn), lambda l, j: (l, 0, j)),
        out_shape=jax.ShapeDtypeStruct((DEPTH, 8, n_out), F32),
        compiler_params=_cparams(("arbitrary", "arbitrary")),
        name="modulation",
    )(c_all, mod_w, mod_b.reshape(DEPTH, 1, n_out))


def _rmsmod_kernel(x_ref, g_ref, sh_ref, sc_ref, *rest, with_router):
    midx = _mod_index(pl.program_id(0))
    x = x_ref[...]
    y = x * lax.rsqrt(jnp.mean(x * x, axis=-1, keepdims=True) + EPS) * g_ref[...]
    h = y * (1.0 + sc_ref[pl.ds(midx, 1), :]) + sh_ref[pl.ds(midx, 1), :]
    if with_router:
        r_ref, o_ref, lg_ref = rest
        o_ref[...] = h
        lg_ref[...] = jnp.dot(h, r_ref[...], precision=HI, preferred_element_type=F32)
    else:
        (o_ref,) = rest
        o_ref[...] = h.astype(BF16)


def rms_modulate(x, g, modtab, which_shift, router=None):
    with_router = router is not None
    in_specs = [pl.BlockSpec((ROWBLK, D), lambda i: (i, 0)),
                pl.BlockSpec((1, D), lambda i: (0, 0)),
                pl.BlockSpec((8, D), lambda i: (0, which_shift)),
                pl.BlockSpec((8, D), lambda i: (0, which_shift + 1))]
    args = [x, g.reshape(1, D), modtab, modtab]
    if with_router:
        in_specs.append(pl.BlockSpec((D, 128), lambda i: (0, 0)))
        args.append(router)
        out_shape = [jax.ShapeDtypeStruct((N_ROWS, D), F32), jax.ShapeDtypeStruct((N_ROWS, 128), F32)]
        out_specs = [pl.BlockSpec((ROWBLK, D), lambda i: (i, 0)), pl.BlockSpec((ROWBLK, 128), lambda i: (i, 0))]
    else:
        out_shape = jax.ShapeDtypeStruct((N_ROWS, D), BF16)
        out_specs = pl.BlockSpec((ROWBLK, D), lambda i: (i, 0))
    return pl.pallas_call(
        partial(_rmsmod_kernel, with_router=with_router),
        grid=(N_ROWS // ROWBLK,),
        in_specs=in_specs, out_specs=out_specs, out_shape=out_shape,
        compiler_params=_cparams(("arbitrary",)),
        name="rms_modulate_router" if with_router else "rms_modulate",
    )(*args)


MM_COLS = 512


def _mm_compute(mode, a, w_tiles, o_ref, i=None, tm=None, res_ref=None, gate_ref=None):
    tn = o_ref.shape[1]
    pieces = a if isinstance(a, (list, tuple)) else [a]
    koffs = np.cumsum([0] + [p.shape[1] for p in pieces]).tolist()
    for c0 in range(0, tn, MM_COLS):
        cols = slice(c0, min(c0 + MM_COLS, tn))
        accs = []
        for w in w_tiles:
            acc = None
            for g, p in enumerate(pieces):
                d = jnp.dot(p, w[koffs[g]:koffs[g + 1], cols], preferred_element_type=F32)
                acc = d if acc is None else acc + d
            accs.append(acc)
        if mode == "plain":
            o_ref[:, cols] = accs[0].astype(o_ref.dtype)
        elif mode == "swiglu":
            o_ref[:, cols] = (_silu(accs[0]) * accs[1]).astype(o_ref.dtype)
        else:
            for s in range(tm // ROWBLK):
                midx = _mod_index(i * (tm // ROWBLK) + s)
                rows = slice(s * ROWBLK, (s + 1) * ROWBLK)
                o_ref[rows, cols] = res_ref[rows, cols] + gate_ref[pl.ds(midx, 1), cols] * accs[0][rows, :]


def _mm_kernel(*refs, mode, tm, na, stacked):
    nw = 2 if mode == "swiglu" else 1
    a_refs = refs[:na]
    w_refs = refs[na:na + nw]
    pos = na + nw
    res_ref = gate_ref = None
    if mode == "resid":
        res_ref, gate_ref = refs[pos:pos + 2]
        pos += 2
    o_ref = refs[pos]
    ws_refs = refs[pos + 1:]
    i = pl.program_id(1)

    @pl.when(i == 0)
    def _():
        for w_ref, ws_ref in zip(w_refs, ws_refs):
            ws_ref[...] = (w_ref[0] if stacked else w_ref[...]).astype(BF16)

    _mm_compute(mode, [a_ref[...] for a_ref in a_refs], ws_refs, o_ref, i, tm, res_ref, gate_ref)


def matmul(a, ws, mode="plain", out_dtype=F32, res=None, modtab=None, gate_col=None,
           tm=768, tn=512, w_buffers=2, layer=None, name="matmul"):
    pieces = list(a) if isinstance(a, (list, tuple)) else [a]
    m = pieces[0].shape[0]
    kdim = sum(p.shape[1] for p in pieces)
    n = ws[0].shape[-1]
    wmode = {} if w_buffers == 2 else dict(pipeline_mode=pl.Buffered(w_buffers))
    in_specs = [pl.BlockSpec((tm, p.shape[1]), lambda j, i: (i, 0)) for p in pieces]
    if layer is None:
        in_specs += [pl.BlockSpec((kdim, tn), lambda j, i: (0, j), **wmode) for _ in ws]
    else:
        in_specs += [pl.BlockSpec((1, kdim, tn), lambda j, i: (layer, 0, j), **wmode) for _ in ws]
    args = [*pieces, *ws]
    if mode == "resid":
        in_specs += [pl.BlockSpec((tm, tn), lambda j, i: (i, j)),
                     pl.BlockSpec((8, tn), lambda j, i: (0, gate_col * (D // tn) + j))]
        args += [res, modtab]
    return pl.pallas_call(
        partial(_mm_kernel, mode=mode, tm=tm, na=len(pieces), stacked=layer is not None),
        grid=(n // tn, m // tm),
        in_specs=in_specs,
        out_specs=pl.BlockSpec((tm, tn), lambda j, i: (i, j)),
        out_shape=jax.ShapeDtypeStruct((m, n), out_dtype),
        scratch_shapes=[pltpu.VMEM((kdim, tn), BF16) for _ in ws],
        compiler_params=_cparams(("arbitrary", "arbitrary")),
        name=name,
    )(*args)


def _gmm_kernel(be_ref, fresh_ref, run_ref, nexte_ref, lastrun_ref, meta_ref, a_ref, *rest, mode, nw, tn):
    w_hbm = rest[:nw]
    o_ref = rest[nw]
    stage_ref, ws_ref, sem = rest[nw + 1:]
    j = pl.program_id(0)
    i = pl.program_id(1)
    n_used, n_runs = meta_ref[0], meta_ref[1]

    def w_copy(widx, e, jj):
        col = pl.multiple_of(jj * tn, tn)
        return pltpu.make_async_copy(w_hbm[widx].at[e, :, pl.ds(col, tn)], stage_ref.at[widx], sem.at[widx])

    @pl.when(fresh_ref[i] == 1)
    def _():
        @pl.when(j * n_runs + run_ref[i] == 0)
        def _():
            for widx in range(nw):
                w_copy(widx, be_ref[i], j).start()

        for widx in range(nw):
            w_copy(widx, be_ref[i], j).wait()
            ws_ref[widx] = stage_ref[widx].astype(BF16)
        is_last = lastrun_ref[i] == 1

        @pl.when(jnp.logical_not(jnp.logical_and(is_last, j == pl.num_programs(0) - 1)))
        def _():
            jn = jnp.where(is_last, j + 1, j)
            for widx in range(nw):
                w_copy(widx, nexte_ref[i], jn).start()

    @pl.when(i < n_used)
    def _():
        _mm_compute(mode, a_ref[...], [ws_ref.at[widx] for widx in range(nw)], o_ref)

    @pl.when(i >= n_used)
    def _():
        o_ref[...] = jnp.zeros_like(o_ref)


def grouped_matmul(a, ws, sched, mode="plain", out_dtype=F32, tm=512, tn=512, name="grouped_matmul"):
    m, kdim = a.shape
    n = ws[0].shape[-1]
    nw = len(ws)
    row = lambda j, i, be, fr, ru, ne, lr, meta: jnp.minimum(i, meta[0] - 1)
    grid_spec = pltpu.PrefetchScalarGridSpec(
        num_scalar_prefetch=len(sched),
        grid=(n // tn, m // tm),
        in_specs=[pl.BlockSpec((tm, kdim), lambda *s: (row(*s), 0))] + [pl.BlockSpec(memory_space=pl.ANY)] * nw,
        out_specs=pl.BlockSpec((tm, tn), lambda j, i, *_: (i, j)),
        scratch_shapes=[pltpu.VMEM((nw, kdim, tn), F32), pltpu.VMEM((nw, kdim, tn), BF16),
                        pltpu.SemaphoreType.DMA((nw,))])
    return pl.pallas_call(
        partial(_gmm_kernel, mode=mode, nw=nw, tn=tn),
        grid_spec=grid_spec,
        out_shape=jax.ShapeDtypeStruct((m, n), out_dtype),
        compiler_params=_cparams(("arbitrary", "arbitrary")),
        name=name,
    )(*sched, a, *ws)


def _fwd_chunk(i):
    return i


def _bwd_chunk(i):
    return jnp.where(i < CH_CTX, CH_CTX - 1 - i, CH_PER_B + CH_CTX - 1 - i)


def _tri_masks():
    r = lax.broadcasted_iota(jnp.int32, (CHUNK, CHUNK), 0)
    c = lax.broadcasted_iota(jnp.int32, (CHUNK, CHUNK), 1)
    return c <= r, c >= r


def _mlstm_kernel(qf_ref, kf_ref, vf_ref, gcf_ref, grf_ref,
                  qb_ref, kb_ref, vb_ref, gcb_ref, grb_ref,
                  biasc_ref, biasr_ref, hf_ref, hb_ref, c_ref, n_ref, m_ref):
    @pl.when(pl.program_id(1) == 0)
    def _():
        c_ref[...] = jnp.zeros_like(c_ref)
        n_ref[...] = jnp.zeros_like(n_ref)
        m_ref[...] = jnp.zeros_like(m_ref)

    low, upp = _tri_masks()
    lowf, uppf = low.astype(F32), upp.astype(F32)
    streams = ((0, qf_ref, kf_ref, vf_ref, gcf_ref, grf_ref, hf_ref, upp, lowf, uppf, CHUNK - 1),
               (1, qb_ref, kb_ref, vb_ref, gcb_ref, grb_ref, hb_ref, low, uppf, lowf, 0))
    for d, q_ref, k_ref, v_ref, gc_ref, gr_ref, h_ref, mask_t, tri_c, tri_r, last in streams:
        ac = gc_ref[...] + biasc_ref[...]
        ar = gr_ref[0] + biasr_ref[...]
        lfc = -_softplus(-ac)
        lfr = -_softplus(-ar)
        bc_all = jnp.dot(tri_c, lfc, precision=HI, preferred_element_type=F32)
        br_all = jnp.dot(lfr, tri_r, precision=HI, preferred_element_type=F32)
        for h in range(ML_HEADS):
            s_idx = d * ML_HEADS + h
            ji, jf = G_I + s_idx, G_F + s_idx
            cols = slice(h * ML_HD, (h + 1) * ML_HD)
            q = q_ref[:, cols] * (ML_HD ** -0.5)
            k = k_ref[:, cols]
            vt = v_ref[:, cols].T
            b_row = br_all[jf:jf + 1, :]
            i_row = ar[ji:ji + 1, :]
            key_col = ac[:, ji:ji + 1] - bc_all[:, jf:jf + 1]
            cmat = c_ref[s_idx]
            nvec = n_ref[s_idx]
            m_prev = m_ref[s_idx][:, 0:1]
            dmat = jnp.where(mask_t, b_row + key_col, -1e30)
            inter = b_row + m_prev
            mt = jnp.maximum(inter, jnp.max(dmat, axis=0, keepdims=True))
            w_intra = jnp.exp(dmat - mt)
            w_state = jnp.exp(inter - mt)
            st = _dot_nt(k, q) * w_intra
            num = jnp.dot(vt, st, preferred_element_type=F32) + w_state * _dot_nt(cmat, q)
            den = jnp.sum(st, axis=0, keepdims=True) + w_state * _dot_nt(nvec, q)
            h_ref[:, cols] = (num / jnp.maximum(jnp.abs(den), jnp.exp(-mt))).T
            bl = b_row[:, last:last + 1]
            g = bl - b_row + i_row
            m_new = jnp.maximum(bl + m_prev, jnp.max(g, axis=1, keepdims=True))
            wg = jnp.exp(g - m_new)
            wc = jnp.exp(bl + m_prev - m_new)
            c_ref[s_idx] = wc * cmat + jnp.dot(vt * wg, k, preferred_element_type=F32)
            n_ref[s_idx] = wc * nvec + jnp.dot(wg, k, preferred_element_type=F32)
            m_ref[s_idx] = jnp.broadcast_to(m_new, (1, 128))


def mlstm_scan(p, gates_t, biasc, biasr):
    def rows(chunk_fn):
        return lambda b, i: b * CH_PER_B + chunk_fn(i)

    def stream_specs(chunk_fn):
        r = rows(chunk_fn)
        return [pl.BlockSpec((CHUNK, 512), lambda b, i: (r(b, i), C_Q // 512)),
                pl.BlockSpec((CHUNK, 512), lambda b, i: (r(b, i), C_K // 512)),
                pl.BlockSpec((CHUNK, 512), lambda b, i: (r(b, i), C_V // 512)),
                pl.BlockSpec((CHUNK, 128), lambda b, i: (r(b, i), C_GATE // 128)),
                pl.BlockSpec((1, G_USED, CHUNK), lambda b, i: (r(b, i), 0, 0))]

    rf, rb = rows(_fwd_chunk), rows(_bwd_chunk)
    n_streams = 2 * ML_HEADS
    return pl.pallas_call(
        _mlstm_kernel,
        grid=(NB, CH_PER_B),
        in_specs=stream_specs(_fwd_chunk) + stream_specs(_bwd_chunk) + [
            pl.BlockSpec((1, 128), lambda b, i: (0, 0)),
            pl.BlockSpec((G_USED, 1), lambda b, i: (0, 0))],
        out_specs=[pl.BlockSpec((CHUNK, 512), lambda b, i: (rf(b, i), 0)),
                   pl.BlockSpec((CHUNK, 512), lambda b, i: (rb(b, i), 0))],
        out_shape=[jax.ShapeDtypeStruct((N_ROWS, 512), F32)] * 2,
        scratch_shapes=[pltpu.VMEM((n_streams, ML_HD, ML_HD), F32),
                        pltpu.VMEM((n_streams, 1, ML_HD), F32),
                        pltpu.VMEM((n_streams, 1, 128), F32)],
        compiler_params=_cparams(("arbitrary", "arbitrary")),
        name="mlstm_scan",
    )(p, p, p, p, gates_t, p, p, p, p, gates_t, biasc, biasr)


def _mlstm_finish_kernel(hf_ref, hb_ref, o_ref, g_ref, out_ref):
    for h in range(ML_HEADS):
        cols = slice(h * ML_HD, (h + 1) * ML_HD)
        x = hf_ref[:, cols] + hb_ref[:, cols]
        y = x * lax.rsqrt(jnp.mean(x * x, axis=-1, keepdims=True) + EPS) * g_ref[:, cols]
        out_ref[:, cols] = (y * jax.nn.sigmoid(o_ref[:, cols])).astype(BF16)


def mlstm_finish(hf, hb, p, norm_g):
    blk = lambda c: pl.BlockSpec((ROWBLK, 512), lambda i: (i, c))
    return pl.pallas_call(
        _mlstm_finish_kernel,
        grid=(N_ROWS // ROWBLK,),
        in_specs=[blk(0), blk(0), blk(C_O // 512), pl.BlockSpec((1, 512), lambda i: (0, 0))],
        out_specs=blk(0),
        out_shape=jax.ShapeDtypeStruct((N_ROWS, 512), BF16),
        compiler_params=_cparams(("arbitrary",)),
        name="mlstm_finish",
    )(hf, hb, p, norm_g.reshape(1, 512))


_PAD = 8


def _ssd_conv_kernel(u_ref, w_ref, b_ref, o_ref, pad_ref):
    zeros = jnp.zeros((_PAD, u_ref.shape[1]), F32)
    segs = ((0, T_CTX, _PAD), (T_CTX, T_LAT, 2 * _PAD + T_CTX))
    pad_ref[0:_PAD] = zeros
    pad_ref[_PAD + T_CTX:2 * _PAD + T_CTX] = zeros
    pad_ref[2 * _PAD + S_ALL:3 * _PAD + S_ALL] = zeros
    for src, n, dst in segs:
        pad_ref[dst:dst + n] = u_ref[src:src + n]
    for src, n, dst in segs:
        acc = jnp.broadcast_to(b_ref[...], (n, u_ref.shape[1]))
        for j in range(SSD_CONV):
            lo = dst + j - SSD_CONV // 2
            acc = acc + w_ref[j:j + 1, :] * pad_ref[lo:lo + n]
        o_ref[src:src + n] = _silu(acc)


def ssd_conv(p, conv_w, conv_b):
    tc = 256
    return pl.pallas_call(
        _ssd_conv_kernel,
        grid=(NB, 1024 // tc),
        in_specs=[pl.BlockSpec((S_ALL, tc), lambda b, c: (b, C_XBC // tc + c)),
                  pl.BlockSpec((SSD_CONV, tc), lambda b, c: (0, c)),
                  pl.BlockSpec((1, tc), lambda b, c: (0, c))],
        out_specs=pl.BlockSpec((S_ALL, tc), lambda b, c: (b, c)),
        out_shape=jax.ShapeDtypeStruct((N_ROWS, 1024), F32),
        scratch_shapes=[pltpu.VMEM((S_ALL + 3 * _PAD, tc), F32)],
        compiler_params=_cparams(("arbitrary", "arbitrary")),
        name="ssd_conv",
    )(p, conv_w, conv_b.reshape(1, 1024))


def _ssd_kernel(xf_ref, bf_ref, cf_ref, gcf_ref, grf_ref,
                xb_ref, bb_ref, cb_ref, gcb_ref, grb_ref,
                biasc_ref, biasr_ref, alogc_ref, alogr_ref, yf_ref, yb_ref, st_ref):
    @pl.when(pl.program_id(1) == 0)
    def _():
        st_ref[...] = jnp.zeros_like(st_ref)

    low, upp = _tri_masks()
    lowf, uppf = low.astype(F32), upp.astype(F32)
    a_c = -jnp.exp(alogc_ref[...])
    a_r = -jnp.exp(alogr_ref[...])
    streams = ((0, xf_ref, bf_ref, cf_ref, gcf_ref, grf_ref, yf_ref, upp, lowf, uppf, CHUNK - 1),
               (1, xb_ref, bb_ref, cb_ref, gcb_ref, grb_ref, yb_ref, low, uppf, lowf, 0))
    for d, x_ref, b_ref, c_ref, gc_ref, gr_ref, y_ref, mask_t, tri_c, tri_r, last in streams:
        dtc_all = _softplus(gc_ref[...] + biasc_ref[...])
        dtr_all = _softplus(gr_ref[0] + biasr_ref[...])
        negc_all = -jnp.dot(tri_c, dtc_all * a_c, precision=HI, preferred_element_type=F32)
        cumr_all = jnp.dot(dtr_all * a_r, tri_r, precision=HI, preferred_element_type=F32)
        xt = x_ref[...].T
        yts = []
        for g in range(SSD_GROUPS):
            gcols = slice(g * SSD_STATE, (g + 1) * SSD_STATE)
            bm = b_ref[:, gcols]
            cm = c_ref[:, gcols]
            bc = _dot_nt(bm, cm)
            for hh in range(SSD_HEADS // SSD_GROUPS):
                h = g * (SSD_HEADS // SSD_GROUPS) + hh
                j = G_DT + d * SSD_HEADS + h
                s_idx = d * SSD_HEADS + h
                cum = cumr_all[j:j + 1, :]
                xd = xt[h * SSD_HD:(h + 1) * SSD_HD, :] * dtr_all[j:j + 1, :]
                st = st_ref[s_idx]
                w = bc * jnp.exp(jnp.where(mask_t, cum + negc_all[:, j:j + 1], -1e30))
                yts.append(jnp.dot(xd, w, preferred_element_type=F32) + _dot_nt(st, cm) * jnp.exp(cum))
                cl = cum[:, last:last + 1]
                st_ref[s_idx] = st * jnp.exp(cl) + jnp.dot(xd * jnp.exp(cl - cum), bm, preferred_element_type=F32)
        y_ref[...] = jnp.concatenate(yts, axis=0).T


def ssd_scan(xbc, p, gates_t, biasc, biasr, alogc, alogr):
    def rows(chunk_fn):
        return lambda b, i: b * CH_PER_B + chunk_fn(i)

    def stream_specs(chunk_fn):
        r = rows(chunk_fn)
        return [pl.BlockSpec((CHUNK, 512), lambda b, i: (r(b, i), 0)),
                pl.BlockSpec((CHUNK, 256), lambda b, i: (r(b, i), 2)),
                pl.BlockSpec((CHUNK, 256), lambda b, i: (r(b, i), 3)),
                pl.BlockSpec((CHUNK, 128), lambda b, i: (r(b, i), C_GATE // 128)),
                pl.BlockSpec((1, G_USED, CHUNK), lambda b, i: (r(b, i), 0, 0))]

    rf, rb = rows(_fwd_chunk), rows(_bwd_chunk)
    const = lambda shape: pl.BlockSpec(shape, lambda b, i: (0, 0))
    return pl.pallas_call(
        _ssd_kernel,
        grid=(NB, CH_PER_B),
        in_specs=stream_specs(_fwd_chunk) + stream_specs(_bwd_chunk) + [
            const((1, 128)), const((G_USED, 1)), const((1, 128)), const((G_USED, 1))],
        out_specs=[pl.BlockSpec((CHUNK, 512), lambda b, i: (rf(b, i), 0)),
                   pl.BlockSpec((CHUNK, 512), lambda b, i: (rb(b, i), 0))],
        out_shape=[jax.ShapeDtypeStruct((N_ROWS, 512), F32)] * 2,
        scratch_shapes=[pltpu.VMEM((2 * SSD_HEADS, SSD_HD, SSD_STATE), F32)],
        compiler_params=_cparams(("arbitrary", "arbitrary")),
        name="ssd_scan",
    )(xbc, xbc, xbc, p, gates_t, xbc, xbc, xbc, p, gates_t, biasc, biasr, alogc, alogr)


def _ssd_finish_kernel(yf_ref, yb_ref, xs_ref, z_ref, dsk_ref, g_ref, out_ref):
    y = yf_ref[...] + yb_ref[...] + dsk_ref[...] * xs_ref[...]
    y = y * _silu(z_ref[...])
    out_ref[...] = (y * lax.rsqrt(jnp.mean(y * y, axis=-1, keepdims=True) + EPS) * g_ref[...]).astype(BF16)


def ssd_finish(yf, yb, xbc, p, d_skip, norm_g):
    blk = lambda c: pl.BlockSpec((ROWBLK, 512), lambda i: (i, c))
    vec = pl.BlockSpec((1, 512), lambda i: (0, 0))
    return pl.pallas_call(
        _ssd_finish_kernel,
        grid=(N_ROWS // ROWBLK,),
        in_specs=[blk(0), blk(0), blk(0), blk(C_Z // 512), vec, vec],
        out_specs=blk(0),
        out_shape=jax.ShapeDtypeStruct((N_ROWS, 512), BF16),
        compiler_params=_cparams(("arbitrary",)),
        name="ssd_finish",
    )(yf, yb, xbc, p, jnp.repeat(d_skip, SSD_HD).reshape(1, 512), norm_g.reshape(1, 512))


def _rope_tile(r, cs_ref, s1_ref, s2_ref):
    return (r * cs_ref[...] + pltpu.roll(r, 128 - MLA_ROPE // 2, 1) * s1_ref[...]
            + pltpu.roll(r, MLA_ROPE // 2, 1) * s2_ref[...])


def _mla_prep_kernel(cq_ref, ckv_ref, kr_ref, wq_ref, wkv_ref, qn_ref, kvn_ref, gq_ref, gk_ref,
                     cs_ref, s1_ref, s2_ref, q_out, k_out, v_out):
    cq = cq_ref[...]
    cqn = cq * lax.rsqrt(jnp.sum(cq * cq, axis=-1, keepdims=True) / MLA_Q_RANK + EPS) * qn_ref[...]
    q_raw = jnp.dot(cqn, wq_ref[...], preferred_element_type=F32)
    ckv = ckv_ref[...]
    ckvn = ckv * lax.rsqrt(jnp.sum(ckv * ckv, axis=-1, keepdims=True) / MLA_KV_RANK + EPS) * kvn_ref[...]
    kv_raw = jnp.dot(ckvn, wkv_ref[...], preferred_element_type=F32)
    kr = kr_ref[...]
    kr_ss = jnp.sum(kr * kr, axis=-1, keepdims=True)
    for h in range(MLA_HEADS):
        base = h * MLA_HP
        qa = q_raw[:, base:base + 128]
        qb = q_raw[:, base + 128:base + 256]
        q_scale = lax.rsqrt((jnp.sum(qa * qa, axis=-1, keepdims=True)
                             + jnp.sum(qb * qb, axis=-1, keepdims=True)) / MLA_QK + EPS) * (MLA_QK ** -0.5 * LOG2E)
        q_out[:, base:base + 128] = (qa * q_scale * gq_ref[:, 0:128]).astype(BF16)
        q_out[:, base + 128:base + 256] = _rope_tile(qb * q_scale * gq_ref[:, 128:256],
                                                     cs_ref, s1_ref, s2_ref).astype(BF16)
        kn = kv_raw[:, base:base + 128]
        k_scale = lax.rsqrt((jnp.sum(kn * kn, axis=-1, keepdims=True) + kr_ss) / MLA_QK + EPS)
        k_out[:, base:base + 128] = (kn * k_scale * gk_ref[:, 0:128]).astype(BF16)
        k_out[:, base + 128:base + 256] = _rope_tile(kr * k_scale * gk_ref[:, 128:256],
                                                     cs_ref, s1_ref, s2_ref).astype(BF16)
        v_out[:, h * MLA_V:(h + 1) * MLA_V] = kv_raw[:, base + 128:base + 256].astype(BF16)


def mla_prep(p, wq, wkv, qn, kvn, gq, gk, rope_tabs):
    const = lambda shape: pl.BlockSpec(shape, lambda i: (0, 0))
    tab = pl.BlockSpec((ROWBLK, 128), lambda i: (i % BLK_PER_B, 0))
    row = lambda w: pl.BlockSpec((ROWBLK, w), lambda i: (i, 0))
    return pl.pallas_call(
        _mla_prep_kernel,
        grid=(N_ROWS // ROWBLK,),
        in_specs=[pl.BlockSpec((ROWBLK, 512), lambda i: (i, C_CQ // 512)),
                  pl.BlockSpec((ROWBLK, 256), lambda i: (i, C_CKV // 256)),
                  pl.BlockSpec((ROWBLK, 128), lambda i: (i, C_KR // 128)),
                  const((512, MLA_HEADS * MLA_HP)), const((256, MLA_HEADS * MLA_HP)),
                  const((1, 512)), const((1, 256)), const((1, MLA_HP)), const((1, MLA_HP)),
                  tab, tab, tab],
        out_specs=[row(MLA_HEADS * MLA_HP), row(MLA_HEADS * MLA_HP), row(MLA_HEADS * MLA_V)],
        out_shape=[jax.ShapeDtypeStruct((N_ROWS, MLA_HEADS * MLA_HP), BF16),
                   jax.ShapeDtypeStruct((N_ROWS, MLA_HEADS * MLA_HP), BF16),
                   jax.ShapeDtypeStruct((N_ROWS, MLA_HEADS * MLA_V), BF16)],
        compiler_params=_cparams(("arbitrary",)),
        name="mla_prep",
    )(p, p, p, wq, wkv, qn, kvn, gq, gk, *rope_tabs)


def _softmax_pv(s_list, v_list):
    m = s_list[0].max(axis=-1, keepdims=True)
    for s in s_list[1:]:
        m = jnp.maximum(m, s.max(axis=-1, keepdims=True))
    ps = [jnp.exp2(s - m) for s in s_list]
    l = sum(p.sum(axis=-1, keepdims=True) for p in ps)
    o = sum(jnp.dot(p.astype(BF16), v, preferred_element_type=F32) for p, v in zip(ps, v_list))
    return o / l


def _mla_attn_kernel(q_ref, k_ref, v_ref, o_ref):
    q = q_ref[...]

    @pl.when(pl.program_id(2) == 0)
    def _():
        o_ref[...] = _softmax_pv([_dot_nt(q, k_ref[0:T_CTX, :])], [v_ref[0:T_CTX, :]]).astype(BF16)

    @pl.when(pl.program_id(2) > 0)
    def _():
        o_ref[...] = _softmax_pv([_dot_nt(q, k_ref[...])], [v_ref[...]]).astype(BF16)


def mla_attention(q, k, v):
    return pl.pallas_call(
        _mla_attn_kernel,
        grid=(NB, MLA_HEADS, BLK_PER_B),
        in_specs=[pl.BlockSpec((ROWBLK, MLA_HP), lambda b, h, i: (b * BLK_PER_B + i, h)),
                  pl.BlockSpec((S_ALL, MLA_HP), lambda b, h, i: (b, h)),
                  pl.BlockSpec((S_ALL, MLA_V), lambda b, h, i: (b, h))],
        out_specs=pl.BlockSpec((ROWBLK, MLA_V), lambda b, h, i: (b * BLK_PER_B + i, h)),
        out_shape=jax.ShapeDtypeStruct((N_ROWS, MLA_HEADS * MLA_V), BF16),
        compiler_params=_cparams(("arbitrary", "arbitrary", "arbitrary")),
        name="mla_attention",
    )(q, k, v)


N_GRID_ROWS = T_LAT // GRID_W
NA_QROWS = ROWBLK // GRID_W
NA_GROUPS = N_GRID_ROWS // NA_QROWS
NA_WIN_ROWS = NA_KH + NA_QROWS - 1
NA_WIN = NA_WIN_ROWS * GRID_W


def _na_win_row(g):
    return jnp.clip(g * NA_QROWS - NA_KH // 2, 0, N_GRID_ROWS - NA_WIN_ROWS)


def _na_kernel(q_ref, k_ref, v_ref, gq_ref, gk_ref, bias_ref, o_ref, kn_ref, vb_ref):
    st = pl.program_id(2)

    @pl.when(st == 0)
    def _():
        k = k_ref[...]
        kn_ref[...] = (k * lax.rsqrt(jnp.mean(k * k, axis=-1, keepdims=True) + EPS) * gk_ref[...]).astype(BF16)
        vb_ref[...] = v_ref[...].astype(BF16)

    q = q_ref[...]
    q_scale = lax.rsqrt(jnp.mean(q * q, axis=-1, keepdims=True) + EPS) * (NA_HD ** -0.5 * LOG2E)
    qn = (q * q_scale * gq_ref[...]).astype(BF16)
    s_cx = _dot_nt(qn, kn_ref[0:T_CTX, :])
    v_cx = vb_ref[0:T_CTX, :]

    @pl.when(st == 0)
    def _():
        o_ref[...] = _softmax_pv([s_cx], [v_cx]).astype(BF16)

    @pl.when(st > 0)
    def _():
        start = pl.multiple_of(T_CTX + _na_win_row(st - 1) * GRID_W, GRID_W)
        s_nb = _dot_nt(qn, kn_ref[pl.ds(start, NA_WIN), :]) + bias_ref[0, 0]
        o_ref[...] = _softmax_pv([s_nb, s_cx], [vb_ref[pl.ds(start, NA_WIN), :], v_cx]).astype(BF16)


def _na_class(g):
    return jnp.where(g == 0, 0, jnp.where(g == NA_GROUPS - 1, 2, 1))


def na_attention(p, gq, gk, bias_tab):
    return pl.pallas_call(
        _na_kernel,
        grid=(NB, NA_HEADS, BLK_PER_B),
        in_specs=[pl.BlockSpec((ROWBLK, NA_HD), lambda b, h, st: (b * BLK_PER_B + st, C_NAQ // NA_HD + h)),
                  pl.BlockSpec((S_ALL, NA_HD), lambda b, h, st: (b, C_NAK // NA_HD + h)),
                  pl.BlockSpec((S_ALL, NA_HD), lambda b, h, st: (b, C_NAV // NA_HD + h)),
                  pl.BlockSpec((1, NA_HD), lambda b, h, st: (0, 0)),
                  pl.BlockSpec((1, NA_HD), lambda b, h, st: (0, 0)),
                  pl.BlockSpec((1, 1, ROWBLK, NA_WIN),
                               lambda b, h, st: (h, _na_class(jnp.maximum(st - 1, 0)), 0, 0))],
        out_specs=pl.BlockSpec((ROWBLK, NA_HD), lambda b, h, st: (b * BLK_PER_B + st, h)),
        out_shape=jax.ShapeDtypeStruct((N_ROWS, NA_HEADS * NA_HD), BF16),
        scratch_shapes=[pltpu.VMEM((S_ALL, NA_HD), BF16), pltpu.VMEM((S_ALL, NA_HD), BF16)],
        compiler_params=_cparams(("arbitrary", "arbitrary", "arbitrary")),
        name="na_attention",
    )(p, p, p, gq.reshape(1, NA_HD), gk.reshape(1, NA_HD), bias_tab)


def _na_window_tables():
    g_rep = np.array([0, 1, NA_GROUPS - 1])
    r = g_rep[:, None] * NA_QROWS + np.arange(NA_QROWS)[None, :]
    r0 = np.clip(r - NA_KH // 2, 0, N_GRID_ROWS - NA_KH)
    w0 = np.clip(g_rep * NA_QROWS - NA_KH // 2, 0, N_GRID_ROWS - NA_WIN_ROWS)
    kr = w0[:, None] + np.arange(NA_WIN_ROWS)[None, :]
    valid_r = (kr[:, None, :] >= r0[:, :, None]) & (kr[:, None, :] < r0[:, :, None] + NA_KH)
    dr = kr[:, None, :] - r[:, :, None] + NA_KH - 1
    return dr, valid_r


def _na_bias_kernel(toe_ref, o_ref):
    dr, valid_r = _na_window_tables()
    masked = jnp.full((GRID_W, GRID_W), -1e30, F32)
    for cls in range(dr.shape[0]):
        @pl.when(pl.program_id(1) == cls)
        def _(cls=cls):
            for a in range(NA_QROWS):
                for jw in range(NA_WIN_ROWS):
                    blk = toe_ref[0, int(dr[cls, a, jw])] if valid_r[cls, a, jw] else masked
                    o_ref[0, 0, a * GRID_W:(a + 1) * GRID_W, jw * GRID_W:(jw + 1) * GRID_W] = blk


def na_bias_table(rpb):
    c = np.arange(GRID_W)[:, None]
    kc = np.arange(GRID_W)[None, :]
    c0 = np.clip(c - NA_KW // 2, 0, GRID_W - NA_KW)
    valid_c = (kc >= c0) & (kc < c0 + NA_KW)
    dc = kc - c + NA_KW - 1
    onehot = (np.arange(2 * NA_KW - 1)[:, None, None] == dc[None]) & valid_c[None]
    toe = jnp.einsum('hrd,dck->hrck', rpb, jnp.asarray(onehot, F32), precision=HI)
    toe = jnp.where(valid_c[None, None], toe * LOG2E, -1e30)
    n_rows = 2 * NA_KH - 1
    return pl.pallas_call(
        _na_bias_kernel,
        grid=(NA_HEADS, 3),
        in_specs=[pl.BlockSpec((1, n_rows, GRID_W, GRID_W), lambda h, s: (h, 0, 0, 0))],
        out_specs=pl.BlockSpec((1, 1, ROWBLK, NA_WIN), lambda h, s: (h, s, 0, 0)),
        out_shape=jax.ShapeDtypeStruct((NA_HEADS, 3, ROWBLK, NA_WIN), F32),
        compiler_params=_cparams(("arbitrary", "arbitrary")),
        name="na_bias",
    )(toe)


N_TOK = NB * T_LAT
MOE_TM = 512
N_BLK = N_TOK * TOP_K // MOE_TM + N_EXPERTS
N_SLOT = N_BLK * MOE_TM
GATHER_ROWS = 256
COMBINE_TOK = 128


def _row_copies(n_rows, copy_fn):
    def start_all():
        def body(r8, carry):
            for u in range(8):
                copy_fn(r8 * 8 + u).start(priority=u % 2)
            return carry
        lax.fori_loop(0, n_rows // 8, body, 0)

    def wait_all():
        def body(r, carry):
            copy_fn(r).wait()
            return carry
        lax.fori_loop(0, n_rows, body, 0, unroll=8)

    return start_all, wait_all


def _gather_kernel(idx_ref, nu_ref, h_hbm, o_ref, buf_ref, sem):
    i = pl.program_id(0)
    n_steps = nu_ref[0] * (MOE_TM // GATHER_ROWS)

    def copies(step):
        slot = step % 2
        return _row_copies(GATHER_ROWS, lambda r: pltpu.make_async_copy(
            h_hbm.at[pl.ds(idx_ref[step * GATHER_ROWS + r], 1)], buf_ref.at[slot, pl.ds(r, 1)], sem.at[slot]))

    @pl.when(i == 0)
    def _():
        copies(i)[0]()

    @pl.when(i + 1 < n_steps)
    def _():
        copies(i + 1)[0]()

    @pl.when(i < n_steps)
    def _():
        copies(i)[1]()
        o_ref[...] = buf_ref[i % 2].astype(BF16)

    @pl.when(i >= n_steps)
    def _():
        o_ref[...] = jnp.zeros_like(o_ref)


def moe_gather(h2, slot_row, n_used):
    return pl.pallas_call(
        _gather_kernel,
        grid_spec=pltpu.PrefetchScalarGridSpec(
            num_scalar_prefetch=2,
            grid=(N_SLOT // GATHER_ROWS,),
            in_specs=[pl.BlockSpec(memory_space=pl.ANY)],
            out_specs=pl.BlockSpec((GATHER_ROWS, D), lambda i, idx, nu: (i, 0)),
            scratch_shapes=[pltpu.VMEM((2, GATHER_ROWS, D), F32), pltpu.SemaphoreType.DMA((2,))]),
        out_shape=jax.ShapeDtypeStruct((N_SLOT, D), BF16),
        compiler_params=_cparams(("arbitrary",)),
        name="moe_gather",
    )(slot_row, n_used, h2)


def _combine_kernel(slot_ref, y_hbm, x_ref, gate_ref, g2_ref, o_ref, buf_ref, sem):
    i = pl.program_id(0)

    def copies(step):
        slot = step % 2
        base = step * COMBINE_TOK * TOP_K
        return _row_copies(COMBINE_TOK * TOP_K, lambda r: pltpu.make_async_copy(
            y_hbm.at[pl.ds(slot_ref[base + r], 1)], buf_ref.at[slot, pl.ds(r, 1)], sem.at[slot]))

    @pl.when(i == 0)
    def _():
        copies(i)[0]()

    @pl.when(i + 1 < N_TOK // COMBINE_TOK)
    def _():
        copies(i + 1)[0]()

    copies(i)[1]()
    b = i // (T_LAT // COMBINE_TOK)
    rows = buf_ref[i % 2]
    f = gate_ref[:, 0:1] * rows[0:COMBINE_TOK] + gate_ref[:, 1:2] * rows[COMBINE_TOK:2 * COMBINE_TOK]
    o_ref[...] = x_ref[...] + g2_ref[pl.ds(b, 1), :] * f


def moe_combine(yb, x, slots_km, gates, modtab):
    per_b = T_LAT // COMBINE_TOK
    x_map = lambda i, s: ((i // per_b) * (S_ALL // COMBINE_TOK) + T_CTX // COMBINE_TOK + i % per_b, 0)
    return pl.pallas_call(
        _combine_kernel,
        grid_spec=pltpu.PrefetchScalarGridSpec(
            num_scalar_prefetch=1,
            grid=(N_TOK // COMBINE_TOK,),
            in_specs=[pl.BlockSpec(memory_space=pl.ANY),
                      pl.BlockSpec((COMBINE_TOK, D), x_map),
                      pl.BlockSpec((COMBINE_TOK, TOP_K), lambda i, s: (i, 0)),
                      pl.BlockSpec((8, D), lambda i, s: (0, 5))],
            out_specs=pl.BlockSpec((COMBINE_TOK, D), lambda i, s: (i, 0)),
            scratch_shapes=[pltpu.VMEM((2, COMBINE_TOK * TOP_K, D), F32), pltpu.SemaphoreType.DMA((2,))]),
        out_shape=jax.ShapeDtypeStruct((N_TOK, D), F32),
        compiler_params=_cparams(("arbitrary",)),
        name="moe_combine",
    )(slots_km, yb, x, gates, modtab)


def moe_routing(logits):
    top_v, top_e = lax.top_k(logits, TOP_K)
    gates = jax.nn.softmax(top_v, axis=-1)
    flat_e = top_e.reshape(-1)
    onehot = (flat_e[:, None] == jnp.arange(N_EXPERTS)[None, :]).astype(jnp.int32)
    rank = jnp.take_along_axis(jnp.cumsum(onehot, axis=0) - onehot, flat_e[:, None], axis=1)[:, 0]
    counts = onehot.sum(axis=0)
    padded = (counts + MOE_TM - 1) // MOE_TM * MOE_TM
    pend = jnp.cumsum(padded)
    slot = ((pend - padded)[flat_e] + rank).astype(jnp.int32)
    tok = jnp.arange(N_TOK, dtype=jnp.int32)
    tok_row = (tok // T_LAT) * S_ALL + T_CTX + tok % T_LAT
    slot_row = jnp.zeros((N_SLOT,), jnp.int32).at[slot].set(jnp.repeat(tok_row, TOP_K))
    n_used = (pend[-1] // MOE_TM).astype(jnp.int32)
    blk = jnp.arange(N_BLK)
    used = blk < n_used
    blk_e = jnp.sum(jnp.minimum(blk, n_used - 1)[:, None] * MOE_TM >= pend[None, :], axis=1).astype(jnp.int32)
    fresh = (used & ((blk == 0) | (blk_e != jnp.roll(blk_e, 1)))).astype(jnp.int32)
    run_idx = jnp.cumsum(fresh) - 1
    n_runs = fresh.sum()
    has = counts > 0
    ids = jnp.arange(N_EXPERTS)
    first_e = jnp.min(jnp.where(has, ids, N_EXPERTS))
    later = jnp.where(has[None, :] & (ids[None, :] > ids[:, None]), ids[None, :], N_EXPERTS).min(axis=1)
    next_of = jnp.where(later < N_EXPERTS, later, first_e)
    sched = (blk_e, fresh, run_idx.astype(jnp.int32), next_of[blk_e].astype(jnp.int32),
             (run_idx == n_runs - 1).astype(jnp.int32), jnp.stack([n_used, n_runs]).astype(jnp.int32))
    slots_km = slot.reshape(N_TOK // COMBINE_TOK, COMBINE_TOK, TOP_K).transpose(0, 2, 1).reshape(-1)
    return slot_row, sched, n_used.reshape(1), slots_km, gates


def _pad_cols(w, width):
    return jnp.pad(w, [(0, 0)] * (w.ndim - 1) + [(0, width - w.shape[-1])])


def layout_w_in(w):
    offs = np.cumsum((512, 512, 512, 512, 8, 8, 512, 1024, 16, 448, 160, 64, 512, 512, 512))[:-1].tolist()
    q, k, v, o, ig, fg, z, xbc, dt, cq, ckv, kr, naq, nak, nav = jnp.split(w, offs, axis=-1)
    gate = _pad_cols(jnp.concatenate([ig, fg, dt], axis=-1), 128)
    return jnp.concatenate([q, k, v, o, z, xbc, _pad_cols(cq, 512), _pad_cols(ckv, 256), _pad_cols(kr, 128),
                            naq, nak, nav, gate], axis=-1)


def rope_tables():
    t = np.arange(T_LAT)
    n_freq = MLA_ROPE // 4
    freqs = ROPE_THETA ** (-jnp.arange(n_freq, dtype=F32) / n_freq)
    row = jnp.asarray(t // GRID_W, F32)
    col = jnp.asarray(t % GRID_W, F32)
    ang = jnp.concatenate([row[:, None] * freqs, col[:, None] * freqs], axis=-1)
    cos, sin = jnp.cos(ang), jnp.sin(ang)
    half = MLA_ROPE // 2
    zc = jnp.zeros((T_LAT, 128 - MLA_ROPE), F32)
    zh = jnp.zeros((T_LAT, half), F32)
    cs = jnp.concatenate([cos, cos, zc], axis=1)
    s1 = jnp.concatenate([-sin, zh, zc], axis=1)
    s2 = jnp.concatenate([zh, sin, zc], axis=1)
    ident = jnp.concatenate([jnp.ones((T_CTX, MLA_ROPE), F32), jnp.zeros((T_CTX, 128 - MLA_ROPE), F32)], axis=1)
    zeros = jnp.zeros((T_CTX, 128), F32)
    return (jnp.concatenate([ident, cs], axis=0), jnp.concatenate([zeros, s1], axis=0),
            jnp.concatenate([zeros, s2], axis=0))


def _gate_vectors(i_bias, f_bias, dt_bias, a_log):
    used = jnp.concatenate([i_bias.reshape(-1), f_bias.reshape(-1), dt_bias.reshape(-1)])
    alog = jnp.concatenate([jnp.zeros((G_DT,), F32), a_log.reshape(-1)])
    padc = lambda u: jnp.pad(u, (0, 128 - G_USED)).reshape(1, 128)
    return padc(used), used.reshape(G_USED, 1), padc(alog), alog.reshape(G_USED, 1)


def kernel(x, c, ctx, c_ctx, mod_w, mod_b, norm1, w_in, w_out, ml_i_bias, ml_f_bias, ml_norm, ssd_conv_w, ssd_conv_b, ssd_dt_bias, ssd_A_log, ssd_D, ssd_norm, mla_q_norm, mla_w_qb, mla_kv_norm, mla_w_kvb, mla_gq, mla_gk, na_gq, na_gk, na_rpb, norm2, ffn_w1, ffn_w3, ffn_w2, moe_router, moe_w1, moe_w3, moe_w2):
    xs = jnp.concatenate([ctx, x], axis=1).reshape(N_ROWS, D)
    c_all = jnp.concatenate([c, c_ctx[None, :], jnp.zeros((8 - NB - 1, D), F32)], axis=0)
    mod_all = modulation(c_all, mod_w, mod_b)
    tabs = rope_tables()
    w_in_p = layout_w_in(w_in)
    out = None
    for l in range(DEPTH):
        modtab = mod_all[l]
        h = rms_modulate(xs, norm1[l], modtab, 0)
        p = matmul(h, [w_in_p], tm=1536, tn=1024, layer=l, name="w_in")
        gates_t = p[:, C_GATE:C_GATE + G_USED].reshape(N_ROWS // CHUNK, CHUNK, G_USED).transpose(0, 2, 1)
        biasc, biasr, alogc, alogr = _gate_vectors(ml_i_bias[l], ml_f_bias[l], ssd_dt_bias[l], ssd_A_log[l])

        hf, hb = mlstm_scan(p, gates_t, biasc, biasr)
        ml = mlstm_finish(hf, hb, p, ml_norm[l])

        xbc = ssd_conv(p, ssd_conv_w[l], ssd_conv_b[l])
        yf, yb = ssd_scan(xbc, p, gates_t, biasc, biasr, alogc, alogr)
        ss = ssd_finish(yf, yb, xbc, p, ssd_D[l], ssd_norm[l])

        wq = jnp.pad(mla_w_qb[l].reshape(MLA_Q_RANK, MLA_HEADS, MLA_QK),
                     ((0, 512 - MLA_Q_RANK), (0, 0), (0, MLA_HP - MLA_QK))).reshape(512, MLA_HEADS * MLA_HP)
        wkv = jnp.pad(mla_w_kvb[l], ((0, 256 - MLA_KV_RANK), (0, 0)))
        pad1 = lambda u, w: jnp.pad(u, (0, w - u.shape[0])).reshape(1, w)
        q, k, v = mla_prep(p, wq, wkv, pad1(mla_q_norm[l], 512), pad1(mla_kv_norm[l], 256),
                           pad1(mla_gq[l], MLA_HP), pad1(mla_gk[l], MLA_HP), tabs)
        la = mla_attention(q, k, v)

        na = na_attention(p, na_gq[l], na_gk[l], na_bias_table(na_rpb[l]))

        xs = matmul([ml, ss, la, na], [w_out], mode="resid", res=xs, modtab=modtab, gate_col=2,
                    tm=1536, layer=l, name="w_out")

        if l % 2 == 0:
            h2 = rms_modulate(xs, norm2[l], modtab, 3)
            hid = matmul(h2, [ffn_w1[l // 2], ffn_w3[l // 2]], mode="swiglu", out_dtype=BF16, tm=1536,
                         name="ffn_up")
            xs = matmul(hid, [ffn_w2[l // 2]], mode="resid", res=xs, modtab=modtab, gate_col=5,
                        w_buffers=1, name="ffn_down")
        else:
            h2, logits = rms_modulate(xs, norm2[l], modtab, 3, router=_pad_cols(moe_router[l // 2], 128))
            lat = logits.reshape(NB, S_ALL, 128)[:, T_CTX:, :N_EXPERTS].reshape(N_TOK, N_EXPERTS)
            slot_row, sched, n_used, slots_km, gates = moe_routing(lat)
            xb = moe_gather(h2, slot_row, n_used)
            hid = grouped_matmul(xb, [moe_w1[l // 2], moe_w3[l // 2]], sched, mode="swiglu", out_dtype=BF16,
                                 tm=MOE_TM, tn=1024, name="moe_up")
            yb_ = grouped_matmul(hid, [moe_w2[l // 2]], sched, tm=MOE_TM, tn=512, name="moe_down")
            out = moe_combine(yb_, xs, slots_km, gates, modtab)
    return out.reshape(NB, T_LAT, D)
```

```python
from functools import partial

import numpy as np
import jax
import jax.numpy as jnp
from jax import lax
from jax.experimental import pallas as pl
from jax.experimental.pallas import tpu as pltpu

F32 = jnp.float32
BF16 = jnp.bfloat16
HI = lax.Precision.HIGHEST

D = 2048
NB = 4
T_LAT = 2048
T_CTX = 256
S_ALL = T_LAT + T_CTX
N_ROWS = NB * S_ALL
DEPTH = 2
GRID_W = 64
EPS = 1e-6
LOG2E = 1.4426950408889634

ROWBLK = 256
BLK_PER_B = S_ALL // ROWBLK
CHUNK = 128
CH_PER_B = S_ALL // CHUNK
CH_CTX = T_CTX // CHUNK

ML_HEADS, ML_HD = 4, 128
SSD_HEADS, SSD_HD, SSD_GROUPS, SSD_STATE, SSD_CONV = 8, 64, 2, 128, 5
SSD_W = SSD_HEADS * SSD_HD
MLA_HEADS, MLA_NOPE, MLA_ROPE, MLA_V = 4, 128, 64, 128
MLA_QK = MLA_NOPE + MLA_ROPE
MLA_Q_RANK, MLA_KV_RANK = 448, 160
MLA_HP = 256
ROPE_THETA = 10000.0
NA_HEADS, NA_HD, NA_KH, NA_KW = 4, 128, 8, 16
N_EXPERTS, TOP_K = 8, 2

C_Q, C_K, C_V, C_O = 0, 512, 1024, 1536
C_Z = 2048
C_XBC = 2560
C_CQ, C_CKV, C_KR = 3584, 4096, 4352
C_NAQ, C_NAK, C_NAV = 4480, 4992, 5504
C_GATE = 6016
D_INP = 6144
G_I, G_F, G_DT, G_USED = 0, 8, 16, 32

VMEM_LIMIT = 56 * 1024 * 1024


def _cparams(sem):
    return pltpu.CompilerParams(dimension_semantics=sem, vmem_limit_bytes=VMEM_LIMIT)


def _mod_index(blk256):
    return jnp.where(blk256 % BLK_PER_B == 0, NB, blk256 // BLK_PER_B)


def _dot_nt(a, b):
    return lax.dot_general(a, b, (((1,), (1,)), ((), ())), preferred_element_type=F32)


def _dot_tn(a, b):
    return lax.dot_general(a, b, (((0,), (0,)), ((), ())), preferred_element_type=F32)


def _silu(x):
    return x * jax.nn.sigmoid(x)


def _softplus(x):
    return jnp.maximum(x, 0.0) + jnp.log(1.0 + jnp.exp(-jnp.abs(x)))


def _mod_kernel(c_ref, w_ref, b_ref, o_ref):
    o_ref[0] = jnp.dot(_silu(c_ref[...]), w_ref[0], preferred_element_type=F32) + b_ref[0]


def modulation(c_all, mod_w, mod_b):
    tn = 1024
    n_out = mod_w.shape[-1]
    return pl.pallas_call(
        _mod_kernel,
        grid=(DEPTH, n_out // tn),
        in_specs=[pl.BlockSpec((8, D), lambda l, j: (0, 0)),
                  pl.BlockSpec((1, D, tn), lambda l, j: (l, 0, j)),
                  pl.BlockSpec((1, 1, tn), lambda l, j: (l, 0, j))],
        out_specs=pl.BlockSpec((1, 8, tn), lambda l, j: (l, 0, j)),
        out_shape=jax.ShapeDtypeStruct((DEPTH, 8, n_out), F32),
        compiler_params=_cparams(("arbitrary", "arbitrary")),
        name="modulation",
    )(c_all, mod_w, mod_b.reshape(DEPTH, 1, n_out))


def _rmsmod_kernel(x_ref, g_ref, sh_ref, sc_ref, *rest, with_router):
    midx = _mod_index(pl.program_id(0))
    x = x_ref[...]
    y = x * lax.rsqrt(jnp.mean(x * x, axis=-1, keepdims=True) + EPS) * g_ref[...]
    h = y * (1.0 + sc_ref[pl.ds(midx, 1), :]) + sh_ref[pl.ds(midx, 1), :]
    if with_router:
        r_ref, o_ref, lg_ref = rest
        o_ref[...] = h
        lg_ref[...] = jnp.dot(h, r_ref[...], precision=HI, preferred_element_type=F32)
    else:
        (o_ref,) = rest
        o_ref[...] = h.astype(BF16)


def rms_modulate(x, g, modtab, which_shift, router=None):
    with_router = router is not None
    in_specs = [pl.BlockSpec((ROWBLK, D), lambda i: (i, 0)),
                pl.BlockSpec((1, D), lambda i: (0, 0)),
                pl.BlockSpec((8, D), lambda i: (0, which_shift)),
                pl.BlockSpec((8, D), lambda i: (0, which_shift + 1))]
    args = [x, g.reshape(1, D), modtab, modtab]
    if with_router:
        in_specs.append(pl.BlockSpec((D, 128), lambda i: (0, 0)))
        args.append(router)
        out_shape = [jax.ShapeDtypeStruct((N_ROWS, D), F32), jax.ShapeDtypeStruct((N_ROWS, 128), F32)]
        out_specs = [pl.BlockSpec((ROWBLK, D), lambda i: (i, 0)), pl.BlockSpec((ROWBLK, 128), lambda i: (i, 0))]
    else:
        out_shape = jax.ShapeDtypeStruct((N_ROWS, D), BF16)
        out_specs = pl.BlockSpec((ROWBLK, D), lambda i: (i, 0))
    return pl.pallas_call(
        partial(_rmsmod_kernel, with_router=with_router),
        grid=(N_ROWS // ROWBLK,),
        in_specs=in_specs, out_specs=out_specs, out_shape=out_shape,
        compiler_params=_cparams(("arbitrary",)),
        name="rms_modulate_router" if with_router else "rms_modulate",
    )(*args)


MM_COLS = 512


def _mm_compute(mode, a, w_tiles, o_ref, i=None, tm=None, res_ref=None, gate_ref=None, rows=slice(None)):
    tn = o_ref.shape[1]
    pieces = a if isinstance(a, (list, tuple)) else [a]
    koffs = np.cumsum([0] + [p.shape[1] for p in pieces]).tolist()
    for c0 in range(0, tn, MM_COLS):
        cols = slice(c0, min(c0 + MM_COLS, tn))
        accs = []
        for w in w_tiles:
            acc = None
            for g, p in enumerate(pieces):
                d = jnp.dot(p, w[koffs[g]:koffs[g + 1], cols], preferred_element_type=F32)
                acc = d if acc is None else acc + d
            accs.append(acc)
        if mode == "plain":
            o_ref[rows, cols] = accs[0].astype(o_ref.dtype)
        elif mode == "swiglu":
            o_ref[rows, cols] = (_silu(accs[0]) * accs[1]).astype(o_ref.dtype)
        else:
            for s in range(tm // ROWBLK):
                midx = _mod_index(i * (tm // ROWBLK) + s)
                rows = slice(s * ROWBLK, (s + 1) * ROWBLK)
                o_ref[rows, cols] = res_ref[rows, cols] + gate_ref[pl.ds(midx, 1), cols] * accs[0][rows, :]


def _mm_kernel(*refs, mode, tm, na, stacked):
    nw = 2 if mode == "swiglu" else 1
    a_refs = refs[:na]
    w_refs = refs[na:na + nw]
    pos = na + nw
    res_ref = gate_ref = None
    if mode == "resid":
        res_ref, gate_ref = refs[pos:pos + 2]
        pos += 2
    o_ref = refs[pos]
    ws_refs = refs[pos + 1:]
    i = pl.program_id(1)

    @pl.when(i == 0)
    def _():
        for w_ref, ws_ref in zip(w_refs, ws_refs):
            ws_ref[...] = (w_ref[0] if stacked else w_ref[...]).astype(BF16)

    _mm_compute(mode, [a_ref[...] for a_ref in a_refs], ws_refs, o_ref, i, tm, res_ref, gate_ref)


def matmul(a, ws, mode="plain", out_dtype=F32, res=None, modtab=None, gate_col=None,
           tm=768, tn=512, w_buffers=2, layer=None, name="matmul"):
    pieces = list(a) if isinstance(a, (list, tuple)) else [a]
    m = pieces[0].shape[0]
    kdim = sum(p.shape[1] for p in pieces)
    n = ws[0].shape[-1]
    wmode = {} if w_buffers == 2 else dict(pipeline_mode=pl.Buffered(w_buffers))
    in_specs = [pl.BlockSpec((tm, p.shape[1]), lambda j, i: (i, 0)) for p in pieces]
    if layer is None:
        in_specs += [pl.BlockSpec((kdim, tn), lambda j, i: (0, j), **wmode) for _ in ws]
    else:
        in_specs += [pl.BlockSpec((1, kdim, tn), lambda j, i: (layer, 0, j), **wmode) for _ in ws]
    args = [*pieces, *ws]
    if mode == "resid":
        in_specs += [pl.BlockSpec((tm, tn), lambda j, i: (i, j)),
                     pl.BlockSpec((8, tn), lambda j, i: (0, gate_col * (D // tn) + j))]
        args += [res, modtab]
    return pl.pallas_call(
        partial(_mm_kernel, mode=mode, tm=tm, na=len(pieces), stacked=layer is not None),
        grid=(n // tn, m // tm),
        in_specs=in_specs,
        out_specs=pl.BlockSpec((tm, tn), lambda j, i: (i, j)),
        out_shape=jax.ShapeDtypeStruct((m, n), out_dtype),
        scratch_shapes=[pltpu.VMEM((kdim, tn), BF16) for _ in ws],
        compiler_params=_cparams(("arbitrary", "arbitrary")),
        name=name,
    )(*args)


def _gmm_kernel(be_ref, fresh_ref, run_ref, nexte_ref, lastrun_ref, fill_ref, meta_ref, a_ref, *rest,
                mode, nw, tn):
    w_hbm = rest[:nw]
    o_ref = rest[nw]
    stage_ref, ws_ref, sem = rest[nw + 1:]
    j = pl.program_id(0)
    i = pl.program_id(1)
    n_runs = meta_ref[1]
    half = a_ref.shape[0] // 2

    def w_copy(widx, e, jj):
        col = pl.multiple_of(jj * tn, tn)
        return pltpu.make_async_copy(w_hbm[widx].at[e, :, pl.ds(col, tn)], stage_ref.at[widx], sem.at[widx])

    @pl.when(fresh_ref[i] == 1)
    def _():
        @pl.when(j * n_runs + run_ref[i] == 0)
        def _():
            for widx in range(nw):
                w_copy(widx, be_ref[i], j).start()

        for widx in range(nw):
            w_copy(widx, be_ref[i], j).wait()
            ws_ref[widx] = stage_ref[widx].astype(BF16)
        is_last = lastrun_ref[i] == 1

        @pl.when(jnp.logical_not(jnp.logical_and(is_last, j == pl.num_programs(0) - 1)))
        def _():
            jn = jnp.where(is_last, j + 1, j)
            for widx in range(nw):
                w_copy(widx, nexte_ref[i], jn).start()

    w_tiles = [ws_ref.at[widx] for widx in range(nw)]
    fill = fill_ref[i]

    @pl.when(fill > half)
    def _():
        _mm_compute(mode, a_ref[...], w_tiles, o_ref)

    @pl.when(jnp.logical_and(fill > 0, fill <= half))
    def _():
        _mm_compute(mode, a_ref[0:half, :], w_tiles, o_ref, rows=slice(0, half))
        o_ref[half:, :] = jnp.zeros((half, o_ref.shape[1]), o_ref.dtype)

    @pl.when(fill == 0)
    def _():
        o_ref[...] = jnp.zeros_like(o_ref)


def grouped_matmul(a, ws, sched, mode="plain", out_dtype=F32, tm=512, tn=512, name="grouped_matmul"):
    m, kdim = a.shape
    n = ws[0].shape[-1]
    nw = len(ws)
    row = lambda j, i, be, fr, ru, ne, lr, fill, meta: jnp.minimum(i, meta[0] - 1)
    grid_spec = pltpu.PrefetchScalarGridSpec(
        num_scalar_prefetch=len(sched),
        grid=(n // tn, m // tm),
        in_specs=[pl.BlockSpec((tm, kdim), lambda *s: (row(*s), 0))] + [pl.BlockSpec(memory_space=pl.ANY)] * nw,
        out_specs=pl.BlockSpec((tm, tn), lambda j, i, *_: (i, j)),
        scratch_shapes=[pltpu.VMEM((nw, kdim, tn), F32), pltpu.VMEM((nw, kdim, tn), BF16),
                        pltpu.SemaphoreType.DMA((nw,))])
    return pl.pallas_call(
        partial(_gmm_kernel, mode=mode, nw=nw, tn=tn),
        grid_spec=grid_spec,
        out_shape=jax.ShapeDtypeStruct((m, n), out_dtype),
        compiler_params=_cparams(("arbitrary", "arbitrary")),
        name=name,
    )(*sched, a, *ws)


def _fwd_chunk(i):
    return i


def _bwd_chunk(i):
    return jnp.where(i < CH_CTX, CH_CTX - 1 - i, CH_PER_B + CH_CTX - 1 - i)


def _tri_masks():
    r = lax.broadcasted_iota(jnp.int32, (CHUNK, CHUNK), 0)
    c = lax.broadcasted_iota(jnp.int32, (CHUNK, CHUNK), 1)
    return c <= r, c >= r


def _mlstm_kernel(qf_ref, kf_ref, vf_ref, gcf_ref, grf_ref,
                  qb_ref, kb_ref, vb_ref, gcb_ref, grb_ref,
                  biasc_ref, biasr_ref, hf_ref, hb_ref, c_ref, n_ref, m_ref):
    @pl.when(pl.program_id(1) == 0)
    def _():
        c_ref[...] = jnp.zeros_like(c_ref)
        n_ref[...] = jnp.zeros_like(n_ref)
        m_ref[...] = jnp.zeros_like(m_ref)

    low, upp = _tri_masks()
    lowf, uppf = low.astype(F32), upp.astype(F32)
    streams = ((0, qf_ref, kf_ref, vf_ref, gcf_ref, grf_ref, hf_ref, upp, lowf, uppf, CHUNK - 1),
               (1, qb_ref, kb_ref, vb_ref, gcb_ref, grb_ref, hb_ref, low, uppf, lowf, 0))
    for d, q_ref, k_ref, v_ref, gc_ref, gr_ref, h_ref, mask_t, tri_c, tri_r, last in streams:
        ac = gc_ref[...] + biasc_ref[...]
        ar = gr_ref[0] + biasr_ref[...]
        lfc = -_softplus(-ac)
        lfr = -_softplus(-ar)
        bc_all = jnp.dot(tri_c, lfc, precision=HI, preferred_element_type=F32)
        br_all = jnp.dot(lfr, tri_r, precision=HI, preferred_element_type=F32)
        for h in range(ML_HEADS):
            s_idx = d * ML_HEADS + h
            ji, jf = G_I + s_idx, G_F + s_idx
            cols = slice(h * ML_HD, (h + 1) * ML_HD)
            q = q_ref[:, cols] * (ML_HD ** -0.5)
            k = k_ref[:, cols]
            vt = v_ref[:, cols].T
            b_row = br_all[jf:jf + 1, :]
            i_row = ar[ji:ji + 1, :]
            key_col = ac[:, ji:ji + 1] - bc_all[:, jf:jf + 1]
            cmat = c_ref[s_idx]
            nvec = n_ref[s_idx]
            m_prev = m_ref[s_idx][:, 0:1]
            dmat = jnp.where(mask_t, b_row + key_col, -1e30)
            inter = b_row + m_prev
            mt = jnp.maximum(inter, jnp.max(dmat, axis=0, keepdims=True))
            w_intra = jnp.exp(dmat - mt)
            w_state = jnp.exp(inter - mt)
            st = _dot_nt(k, q) * w_intra
            num = jnp.dot(vt, st, preferred_element_type=F32) + w_state * _dot_nt(cmat, q)
            den = jnp.sum(st, axis=0, keepdims=True) + w_state * _dot_nt(nvec, q)
            h_ref[:, cols] = (num / jnp.maximum(jnp.abs(den), jnp.exp(-mt))).T
            bl = b_row[:, last:last + 1]
            g = bl - b_row + i_row
            m_new = jnp.maximum(bl + m_prev, jnp.max(g, axis=1, keepdims=True))
            wg = jnp.exp(g - m_new)
            wc = jnp.exp(bl + m_prev - m_new)
            c_ref[s_idx] = wc * cmat + jnp.dot(vt * wg, k, preferred_element_type=F32)
            n_ref[s_idx] = wc * nvec + jnp.dot(wg, k, preferred_element_type=F32)
            m_ref[s_idx] = jnp.broadcast_to(m_new, (1, 128))


def mlstm_scan(p, gates_t, biasc, biasr):
    def rows(chunk_fn):
        return lambda b, i: b * CH_PER_B + chunk_fn(i)

    def stream_specs(chunk_fn):
        r = rows(chunk_fn)
        return [pl.BlockSpec((CHUNK, 512), lambda b, i: (r(b, i), C_Q // 512)),
                pl.BlockSpec((CHUNK, 512), lambda b, i: (r(b, i), C_K // 512)),
                pl.BlockSpec((CHUNK, 512), lambda b, i: (r(b, i), C_V // 512)),
                pl.BlockSpec((CHUNK, 128), lambda b, i: (r(b, i), C_GATE // 128)),
                pl.BlockSpec((1, G_USED, CHUNK), lambda b, i: (r(b, i), 0, 0))]

    rf, rb = rows(_fwd_chunk), rows(_bwd_chunk)
    n_streams = 2 * ML_HEADS
    return pl.pallas_call(
        _mlstm_kernel,
        grid=(NB, CH_PER_B),
        in_specs=stream_specs(_fwd_chunk) + stream_specs(_bwd_chunk) + [
            pl.BlockSpec((1, 128), lambda b, i: (0, 0)),
            pl.BlockSpec((G_USED, 1), lambda b, i: (0, 0))],
        out_specs=[pl.BlockSpec((CHUNK, 512), lambda b, i: (rf(b, i), 0)),
                   pl.BlockSpec((CHUNK, 512), lambda b, i: (rb(b, i), 0))],
        out_shape=[jax.ShapeDtypeStruct((N_ROWS, 512), F32)] * 2,
        scratch_shapes=[pltpu.VMEM((n_streams, ML_HD, ML_HD), F32),
                        pltpu.VMEM((n_streams, 1, ML_HD), F32),
                        pltpu.VMEM((n_streams, 1, 128), F32)],
        compiler_params=_cparams(("arbitrary", "arbitrary")),
        name="mlstm_scan",
    )(p, p, p, p, gates_t, p, p, p, p, gates_t, biasc, biasr)


def _mlstm_finish_kernel(hf_ref, hb_ref, o_ref, g_ref, out_ref):
    for h in range(ML_HEADS):
        cols = slice(h * ML_HD, (h + 1) * ML_HD)
        x = hf_ref[:, cols] + hb_ref[:, cols]
        y = x * lax.rsqrt(jnp.mean(x * x, axis=-1, keepdims=True) + EPS) * g_ref[:, cols]
        out_ref[:, cols] = (y * jax.nn.sigmoid(o_ref[:, cols])).astype(BF16)


def mlstm_finish(hf, hb, p, norm_g):
    blk = lambda c: pl.BlockSpec((ROWBLK, 512), lambda i: (i, c))
    return pl.pallas_call(
        _mlstm_finish_kernel,
        grid=(N_ROWS // ROWBLK,),
        in_specs=[blk(0), blk(0), blk(C_O // 512), pl.BlockSpec((1, 512), lambda i: (0, 0))],
        out_specs=blk(0),
        out_shape=jax.ShapeDtypeStruct((N_ROWS, 512), BF16),
        compiler_params=_cparams(("arbitrary",)),
        name="mlstm_finish",
    )(hf, hb, p, norm_g.reshape(1, 512))


_PAD = 8


def _ssd_conv_kernel(u_ref, w_ref, b_ref, o_ref, pad_ref):
    zeros = jnp.zeros((_PAD, u_ref.shape[1]), F32)
    segs = ((0, T_CTX, _PAD), (T_CTX, T_LAT, 2 * _PAD + T_CTX))
    pad_ref[0:_PAD] = zeros
    pad_ref[_PAD + T_CTX:2 * _PAD + T_CTX] = zeros
    pad_ref[2 * _PAD + S_ALL:3 * _PAD + S_ALL] = zeros
    for src, n, dst in segs:
        pad_ref[dst:dst + n] = u_ref[src:src + n]
    for src, n, dst in segs:
        acc = jnp.broadcast_to(b_ref[...], (n, u_ref.shape[1]))
        for j in range(SSD_CONV):
            lo = dst + j - SSD_CONV // 2
            acc = acc + w_ref[j:j + 1, :] * pad_ref[lo:lo + n]
        o_ref[src:src + n] = _silu(acc)


def ssd_conv(p, conv_w, conv_b):
    tc = 256
    return pl.pallas_call(
        _ssd_conv_kernel,
        grid=(NB, 1024 // tc),
        in_specs=[pl.BlockSpec((S_ALL, tc), lambda b, c: (b, C_XBC // tc + c)),
                  pl.BlockSpec((SSD_CONV, tc), lambda b, c: (0, c)),
                  pl.BlockSpec((1, tc), lambda b, c: (0, c))],
        out_specs=pl.BlockSpec((S_ALL, tc), lambda b, c: (b, c)),
        out_shape=jax.ShapeDtypeStruct((N_ROWS, 1024), F32),
        scratch_shapes=[pltpu.VMEM((S_ALL + 3 * _PAD, tc), F32)],
        compiler_params=_cparams(("arbitrary", "arbitrary")),
        name="ssd_conv",
    )(p, conv_w, conv_b.reshape(1, 1024))


def _ssd_kernel(xf_ref, bf_ref, cf_ref, gcf_ref, grf_ref,
                xb_ref, bb_ref, cb_ref, gcb_ref, grb_ref,
                biasc_ref, biasr_ref, alogc_ref, alogr_ref, yf_ref, yb_ref, st_ref):
    @pl.when(pl.program_id(1) == 0)
    def _():
        st_ref[...] = jnp.zeros_like(st_ref)

    low, upp = _tri_masks()
    lowf, uppf = low.astype(F32), upp.astype(F32)
    a_c = -jnp.exp(alogc_ref[...])
    a_r = -jnp.exp(alogr_ref[...])
    streams = ((0, xf_ref, bf_ref, cf_ref, gcf_ref, grf_ref, yf_ref, upp, lowf, uppf, CHUNK - 1),
               (1, xb_ref, bb_ref, cb_ref, gcb_ref, grb_ref, yb_ref, low, uppf, lowf, 0))
    for d, x_ref, b_ref, c_ref, gc_ref, gr_ref, y_ref, mask_t, tri_c, tri_r, last in streams:
        dtc_all = _softplus(gc_ref[...] + biasc_ref[...])
        dtr_all = _softplus(gr_ref[0] + biasr_ref[...])
        negc_all = -jnp.dot(tri_c, dtc_all * a_c, precision=HI, preferred_element_type=F32)
        cumr_all = jnp.dot(dtr_all * a_r, tri_r, precision=HI, preferred_element_type=F32)
        xt = x_ref[...].T
        yts = []
        for g in range(SSD_GROUPS):
            gcols = slice(g * SSD_STATE, (g + 1) * SSD_STATE)
            bm = b_ref[:, gcols]
            cm = c_ref[:, gcols]
            bc = _dot_nt(bm, cm)
            for hh in range(SSD_HEADS // SSD_GROUPS):
                h = g * (SSD_HEADS // SSD_GROUPS) + hh
                j = G_DT + d * SSD_HEADS + h
                s_idx = d * SSD_HEADS + h
                cum = cumr_all[j:j + 1, :]
                xd = xt[h * SSD_HD:(h + 1) * SSD_HD, :] * dtr_all[j:j + 1, :]
                st = st_ref[s_idx]
                w = bc * jnp.exp(jnp.where(mask_t, cum + negc_all[:, j:j + 1], -1e30))
                yts.append(jnp.dot(xd, w, preferred_element_type=F32) + _dot_nt(st, cm) * jnp.exp(cum))
                cl = cum[:, last:last + 1]
                st_ref[s_idx] = st * jnp.exp(cl) + jnp.dot(xd * jnp.exp(cl - cum), bm, preferred_element_type=F32)
        y_ref[...] = jnp.concatenate(yts, axis=0).T


def ssd_scan(xbc, p, gates_t, biasc, biasr, alogc, alogr):
    def rows(chunk_fn):
        return lambda b, i: b * CH_PER_B + chunk_fn(i)

    def stream_specs(chunk_fn):
        r = rows(chunk_fn)
        return [pl.BlockSpec((CHUNK, 512), lambda b, i: (r(b, i), 0)),
                pl.BlockSpec((CHUNK, 256), lambda b, i: (r(b, i), 2)),
                pl.BlockSpec((CHUNK, 256), lambda b, i: (r(b, i), 3)),
                pl.BlockSpec((CHUNK, 128), lambda b, i: (r(b, i), C_GATE // 128)),
                pl.BlockSpec((1, G_USED, CHUNK), lambda b, i: (r(b, i), 0, 0))]

    rf, rb = rows(_fwd_chunk), rows(_bwd_chunk)
    const = lambda shape: pl.BlockSpec(shape, lambda b, i: (0, 0))
    return pl.pallas_call(
        _ssd_kernel,
        grid=(NB, CH_PER_B),
        in_specs=stream_specs(_fwd_chunk) + stream_specs(_bwd_chunk) + [
            const((1, 128)), const((G_USED, 1)), const((1, 128)), const((G_USED, 1))],
        out_specs=[pl.BlockSpec((CHUNK, 512), lambda b, i: (rf(b, i), 0)),
                   pl.BlockSpec((CHUNK, 512), lambda b, i: (rb(b, i), 0))],
        out_shape=[jax.ShapeDtypeStruct((N_ROWS, 512), F32)] * 2,
        scratch_shapes=[pltpu.VMEM((2 * SSD_HEADS, SSD_HD, SSD_STATE), F32)],
        compiler_params=_cparams(("arbitrary", "arbitrary")),
        name="ssd_scan",
    )(xbc, xbc, xbc, p, gates_t, xbc, xbc, xbc, p, gates_t, biasc, biasr, alogc, alogr)


def _ssd_finish_kernel(yf_ref, yb_ref, xs_ref, z_ref, dsk_ref, g_ref, out_ref):
    y = yf_ref[...] + yb_ref[...] + dsk_ref[...] * xs_ref[...]
    y = y * _silu(z_ref[...])
    out_ref[...] = (y * lax.rsqrt(jnp.mean(y * y, axis=-1, keepdims=True) + EPS) * g_ref[...]).astype(BF16)


def ssd_finish(yf, yb, xbc, p, d_skip, norm_g):
    blk = lambda c: pl.BlockSpec((ROWBLK, 512), lambda i: (i, c))
    vec = pl.BlockSpec((1, 512), lambda i: (0, 0))
    return pl.pallas_call(
        _ssd_finish_kernel,
        grid=(N_ROWS // ROWBLK,),
        in_specs=[blk(0), blk(0), blk(0), blk(C_Z // 512), vec, vec],
        out_specs=blk(0),
        out_shape=jax.ShapeDtypeStruct((N_ROWS, 512), BF16),
        compiler_params=_cparams(("arbitrary",)),
        name="ssd_finish",
    )(yf, yb, xbc, p, jnp.repeat(d_skip, SSD_HD).reshape(1, 512), norm_g.reshape(1, 512))


def _rope_tile(r, cs_ref, s1_ref, s2_ref):
    return (r * cs_ref[...] + pltpu.roll(r, 128 - MLA_ROPE // 2, 1) * s1_ref[...]
            + pltpu.roll(r, MLA_ROPE // 2, 1) * s2_ref[...])


def _mla_prep_kernel(cq_ref, ckv_ref, kr_ref, wq_ref, wkv_ref, qn_ref, kvn_ref, gq_ref, gk_ref,
                     cs_ref, s1_ref, s2_ref, q_out, k_out, v_out):
    cq = cq_ref[...]
    cqn = cq * lax.rsqrt(jnp.sum(cq * cq, axis=-1, keepdims=True) / MLA_Q_RANK + EPS) * qn_ref[...]
    q_raw = jnp.dot(cqn, wq_ref[...], preferred_element_type=F32)
    ckv = ckv_ref[...]
    ckvn = ckv * lax.rsqrt(jnp.sum(ckv * ckv, axis=-1, keepdims=True) / MLA_KV_RANK + EPS) * kvn_ref[...]
    kv_raw = jnp.dot(ckvn, wkv_ref[...], preferred_element_type=F32)
    kr = kr_ref[...]
    kr_ss = jnp.sum(kr * kr, axis=-1, keepdims=True)
    for h in range(MLA_HEADS):
        base = h * MLA_HP
        qa = q_raw[:, base:base + 128]
        qb = q_raw[:, base + 128:base + 256]
        q_scale = lax.rsqrt((jnp.sum(qa * qa, axis=-1, keepdims=True)
                             + jnp.sum(qb * qb, axis=-1, keepdims=True)) / MLA_QK + EPS) * (MLA_QK ** -0.5 * LOG2E)
        q_out[:, base:base + 128] = (qa * q_scale * gq_ref[:, 0:128]).astype(BF16)
        q_out[:, base + 128:base + 256] = _rope_tile(qb * q_scale * gq_ref[:, 128:256],
                                                     cs_ref, s1_ref, s2_ref).astype(BF16)
        kn = kv_raw[:, base:base + 128]
        k_scale = lax.rsqrt((jnp.sum(kn * kn, axis=-1, keepdims=True) + kr_ss) / MLA_QK + EPS)
        k_out[:, base:base + 128] = (kn * k_scale * gk_ref[:, 0:128]).astype(BF16)
        k_out[:, base + 128:base + 256] = _rope_tile(kr * k_scale * gk_ref[:, 128:256],
                                                     cs_ref, s1_ref, s2_ref).astype(BF16)
        v_out[:, h * MLA_V:(h + 1) * MLA_V] = kv_raw[:, base + 128:base + 256].astype(BF16)


def mla_prep(p, wq, wkv, qn, kvn, gq, gk, rope_tabs):
    const = lambda shape: pl.BlockSpec(shape, lambda i: (0, 0))
    tab = pl.BlockSpec((ROWBLK, 128), lambda i: (i % BLK_PER_B, 0))
    row = lambda w: pl.BlockSpec((ROWBLK, w), lambda i: (i, 0))
    return pl.pallas_call(
        _mla_prep_kernel,
        grid=(N_ROWS // ROWBLK,),
        in_specs=[pl.BlockSpec((ROWBLK, 512), lambda i: (i, C_CQ // 512)),
                  pl.BlockSpec((ROWBLK, 256), lambda i: (i, C_CKV // 256)),
                  pl.BlockSpec((ROWBLK, 128), lambda i: (i, C_KR // 128)),
                  const((512, MLA_HEADS * MLA_HP)), const((256, MLA_HEADS * MLA_HP)),
                  const((1, 512)), const((1, 256)), const((1, MLA_HP)), const((1, MLA_HP)),
                  tab, tab, tab],
        out_specs=[row(MLA_HEADS * MLA_HP), row(MLA_HEADS * MLA_HP), row(MLA_HEADS * MLA_V)],
        out_shape=[jax.ShapeDtypeStruct((N_ROWS, MLA_HEADS * MLA_HP), BF16),
                   jax.ShapeDtypeStruct((N_ROWS, MLA_HEADS * MLA_HP), BF16),
                   jax.ShapeDtypeStruct((N_ROWS, MLA_HEADS * MLA_V), BF16)],
        compiler_params=_cparams(("arbitrary",)),
        name="mla_prep",
    )(p, p, p, wq, wkv, qn, kvn, gq, gk, *rope_tabs)


def _softmax_pv(s_list, v_list):
    m = s_list[0].max(axis=-1, keepdims=True)
    for s in s_list[1:]:
        m = jnp.maximum(m, s.max(axis=-1, keepdims=True))
    ps = [jnp.exp2(s - m) for s in s_list]
    l = sum(p.sum(axis=-1, keepdims=True) for p in ps)
    o = sum(jnp.dot(p.astype(BF16), v, preferred_element_type=F32) for p, v in zip(ps, v_list))
    return o / l


MLA_KCHUNK = 512


def _mla_attn_kernel(q_ref, k_ref, v_ref, o_ref):
    q = q_ref[...]

    @pl.when(pl.program_id(2) == 0)
    def _():
        o_ref[...] = _softmax_pv([_dot_nt(q, k_ref[0:T_CTX, :])], [v_ref[0:T_CTX, :]]).astype(BF16)

    @pl.when(pl.program_id(2) > 0)
    def _():
        chunks = [slice(c0, min(c0 + MLA_KCHUNK, S_ALL)) for c0 in range(0, S_ALL, MLA_KCHUNK)]
        o_ref[...] = _softmax_pv([_dot_nt(q, k_ref[c, :]) for c in chunks], [v_ref[c, :] for c in chunks]).astype(BF16)


def mla_attention(q, k, v):
    return pl.pallas_call(
        _mla_attn_kernel,
        grid=(NB, MLA_HEADS, BLK_PER_B),
        in_specs=[pl.BlockSpec((ROWBLK, MLA_HP), lambda b, h, i: (b * BLK_PER_B + i, h)),
                  pl.BlockSpec((S_ALL, MLA_HP), lambda b, h, i: (b, h)),
                  pl.BlockSpec((S_ALL, MLA_V), lambda b, h, i: (b, h))],
        out_specs=pl.BlockSpec((ROWBLK, MLA_V), lambda b, h, i: (b * BLK_PER_B + i, h)),
        out_shape=jax.ShapeDtypeStruct((N_ROWS, MLA_HEADS * MLA_V), BF16),
        compiler_params=_cparams(("arbitrary", "arbitrary", "arbitrary")),
        name="mla_attention",
    )(q, k, v)


N_GRID_ROWS = T_LAT // GRID_W
NA_QROWS = ROWBLK // GRID_W
NA_GROUPS = N_GRID_ROWS // NA_QROWS
NA_WIN_ROWS = NA_KH + NA_QROWS - 1
NA_WIN = NA_WIN_ROWS * GRID_W


def _na_win_row(g):
    return jnp.clip(g * NA_QROWS - NA_KH // 2, 0, N_GRID_ROWS - NA_WIN_ROWS)


def _na_kernel(q_ref, k_ref, v_ref, gq_ref, gk_ref, bias_ref, o_ref, kn_ref, vb_ref):
    st = pl.program_id(2)

    @pl.when(st == 0)
    def _():
        k = k_ref[...]
        kn_ref[...] = (k * lax.rsqrt(jnp.mean(k * k, axis=-1, keepdims=True) + EPS) * gk_ref[...]).astype(BF16)
        vb_ref[...] = v_ref[...].astype(BF16)

    q = q_ref[...]
    q_scale = lax.rsqrt(jnp.mean(q * q, axis=-1, keepdims=True) + EPS) * (NA_HD ** -0.5 * LOG2E)
    qn = (q * q_scale * gq_ref[...]).astype(BF16)
    s_cx = _dot_nt(qn, kn_ref[0:T_CTX, :])
    v_cx = vb_ref[0:T_CTX, :]

    @pl.when(st == 0)
    def _():
        o_ref[...] = _softmax_pv([s_cx], [v_cx]).astype(BF16)

    @pl.when(st > 0)
    def _():
        start = pl.multiple_of(T_CTX + _na_win_row(st - 1) * GRID_W, GRID_W)
        s_nb = _dot_nt(qn, kn_ref[pl.ds(start, NA_WIN), :]) + bias_ref[0, 0]
        o_ref[...] = _softmax_pv([s_nb, s_cx], [vb_ref[pl.ds(start, NA_WIN), :], v_cx]).astype(BF16)


def _na_class(g):
    return jnp.where(g == 0, 0, jnp.where(g == NA_GROUPS - 1, 2, 1))


def na_attention(p, gq, gk, bias_tab):
    return pl.pallas_call(
        _na_kernel,
        grid=(NB, NA_HEADS, BLK_PER_B),
        in_specs=[pl.BlockSpec((ROWBLK, NA_HD), lambda b, h, st: (b * BLK_PER_B + st, C_NAQ // NA_HD + h)),
                  pl.BlockSpec((S_ALL, NA_HD), lambda b, h, st: (b, C_NAK // NA_HD + h)),
                  pl.BlockSpec((S_ALL, NA_HD), lambda b, h, st: (b, C_NAV // NA_HD + h)),
                  pl.BlockSpec((1, NA_HD), lambda b, h, st: (0, 0)),
                  pl.BlockSpec((1, NA_HD), lambda b, h, st: (0, 0)),
                  pl.BlockSpec((1, 1, ROWBLK, NA_WIN),
                               lambda b, h, st: (h, _na_class(jnp.maximum(st - 1, 0)), 0, 0))],
        out_specs=pl.BlockSpec((ROWBLK, NA_HD), lambda b, h, st: (b * BLK_PER_B + st, h)),
        out_shape=jax.ShapeDtypeStruct((N_ROWS, NA_HEADS * NA_HD), BF16),
        scratch_shapes=[pltpu.VMEM((S_ALL, NA_HD), BF16), pltpu.VMEM((S_ALL, NA_HD), BF16)],
        compiler_params=_cparams(("arbitrary", "arbitrary", "arbitrary")),
        name="na_attention",
    )(p, p, p, gq.reshape(1, NA_HD), gk.reshape(1, NA_HD), bias_tab)


def _na_window_tables():
    g_rep = np.array([0, 1, NA_GROUPS - 1])
    r = g_rep[:, None] * NA_QROWS + np.arange(NA_QROWS)[None, :]
    r0 = np.clip(r - NA_KH // 2, 0, N_GRID_ROWS - NA_KH)
    w0 = np.clip(g_rep * NA_QROWS - NA_KH // 2, 0, N_GRID_ROWS - NA_WIN_ROWS)
    kr = w0[:, None] + np.arange(NA_WIN_ROWS)[None, :]
    valid_r = (kr[:, None, :] >= r0[:, :, None]) & (kr[:, None, :] < r0[:, :, None] + NA_KH)
    dr = kr[:, None, :] - r[:, :, None] + NA_KH - 1
    return dr, valid_r


def _na_bias_kernel(toe_ref, o_ref):
    dr, valid_r = _na_window_tables()
    masked = jnp.full((GRID_W, GRID_W), -1e30, F32)
    for cls in range(dr.shape[0]):
        @pl.when(pl.program_id(1) == cls)
        def _(cls=cls):
            for a in range(NA_QROWS):
                for jw in range(NA_WIN_ROWS):
                    blk = toe_ref[0, int(dr[cls, a, jw])] if valid_r[cls, a, jw] else masked
                    o_ref[0, 0, a * GRID_W:(a + 1) * GRID_W, jw * GRID_W:(jw + 1) * GRID_W] = blk


def na_bias_table(rpb):
    c = np.arange(GRID_W)[:, None]
    kc = np.arange(GRID_W)[None, :]
    c0 = np.clip(c - NA_KW // 2, 0, GRID_W - NA_KW)
    valid_c = (kc >= c0) & (kc < c0 + NA_KW)
    dc = kc - c + NA_KW - 1
    onehot = (np.arange(2 * NA_KW - 1)[:, None, None] == dc[None]) & valid_c[None]
    toe = jnp.einsum('hrd,dck->hrck', rpb, jnp.asarray(onehot, F32), precision=HI)
    toe = jnp.where(valid_c[None, None], toe * LOG2E, -1e30)
    n_rows = 2 * NA_KH - 1
    return pl.pallas_call(
        _na_bias_kernel,
        grid=(NA_HEADS, 3),
        in_specs=[pl.BlockSpec((1, n_rows, GRID_W, GRID_W), lambda h, s: (h, 0, 0, 0))],
        out_specs=pl.BlockSpec((1, 1, ROWBLK, NA_WIN), lambda h, s: (h, s, 0, 0)),
        out_shape=jax.ShapeDtypeStruct((NA_HEADS, 3, ROWBLK, NA_WIN), F32),
        compiler_params=_cparams(("arbitrary", "arbitrary")),
        name="na_bias",
    )(toe)


N_TOK = NB * T_LAT
MOE_TM = 512
N_BLK = N_TOK * TOP_K // MOE_TM + N_EXPERTS
N_SLOT = N_BLK * MOE_TM
GATHER_ROWS = 256
COMBINE_TOK = 128


def _row_copies(n_rows, copy_fn):
    def start_all():
        def body(r8, carry):
            for u in range(8):
                copy_fn(r8 * 8 + u).start(priority=u % 2)
            return carry
        lax.fori_loop(0, n_rows // 8, body, 0)

    def wait_all():
        def body(r, carry):
            copy_fn(r).wait()
            return carry
        lax.fori_loop(0, n_rows, body, 0, unroll=8)

    return start_all, wait_all


def _gather_kernel(idx_ref, nu_ref, h_hbm, o_ref, buf_ref, sem):
    i = pl.program_id(0)
    n_steps = nu_ref[0] * (MOE_TM // GATHER_ROWS)

    def copies(step):
        slot = step % 2
        return _row_copies(GATHER_ROWS, lambda r: pltpu.make_async_copy(
            h_hbm.at[pl.ds(idx_ref[step * GATHER_ROWS + r], 1)], buf_ref.at[slot, pl.ds(r, 1)], sem.at[slot]))

    @pl.when(i == 0)
    def _():
        copies(i)[0]()

    @pl.when(i + 1 < n_steps)
    def _():
        copies(i + 1)[0]()

    @pl.when(i < n_steps)
    def _():
        copies(i)[1]()
        o_ref[...] = buf_ref[i % 2].astype(BF16)

    @pl.when(i >= n_steps)
    def _():
        o_ref[...] = jnp.zeros_like(o_ref)


def moe_gather(h2, slot_row, n_used):
    return pl.pallas_call(
        _gather_kernel,
        grid_spec=pltpu.PrefetchScalarGridSpec(
            num_scalar_prefetch=2,
            grid=(N_SLOT // GATHER_ROWS,),
            in_specs=[pl.BlockSpec(memory_space=pl.ANY)],
            out_specs=pl.BlockSpec((GATHER_ROWS, D), lambda i, idx, nu: (i, 0)),
            scratch_shapes=[pltpu.VMEM((2, GATHER_ROWS, D), F32), pltpu.SemaphoreType.DMA((2,))]),
        out_shape=jax.ShapeDtypeStruct((N_SLOT, D), BF16),
        compiler_params=_cparams(("arbitrary",)),
        name="moe_gather",
    )(slot_row, n_used, h2)


def _combine_kernel(slot_ref, y_hbm, x_ref, gate_ref, g2_ref, o_ref, buf_ref, sem):
    i = pl.program_id(0)

    def copies(step):
        slot = step % 2
        base = step * COMBINE_TOK * TOP_K
        return _row_copies(COMBINE_TOK * TOP_K, lambda r: pltpu.make_async_copy(
            y_hbm.at[pl.ds(slot_ref[base + r], 1)], buf_ref.at[slot, pl.ds(r, 1)], sem.at[slot]))

    @pl.when(i == 0)
    def _():
        copies(i)[0]()

    @pl.when(i + 1 < N_TOK // COMBINE_TOK)
    def _():
        copies(i + 1)[0]()

    copies(i)[1]()
    b = i // (T_LAT // COMBINE_TOK)
    rows = buf_ref[i % 2]
    f = gate_ref[:, 0:1] * rows[0:COMBINE_TOK] + gate_ref[:, 1:2] * rows[COMBINE_TOK:2 * COMBINE_TOK]
    o_ref[...] = x_ref[...] + g2_ref[pl.ds(b, 1), :] * f


def moe_combine(yb, x, slots_km, gates, modtab):
    per_b = T_LAT // COMBINE_TOK
    x_map = lambda i, s: ((i // per_b) * (S_ALL // COMBINE_TOK) + T_CTX // COMBINE_TOK + i % per_b, 0)
    return pl.pallas_call(
        _combine_kernel,
        grid_spec=pltpu.PrefetchScalarGridSpec(
            num_scalar_prefetch=1,
            grid=(N_TOK // COMBINE_TOK,),
            in_specs=[pl.BlockSpec(memory_space=pl.ANY),
                      pl.BlockSpec((COMBINE_TOK, D), x_map),
                      pl.BlockSpec((COMBINE_TOK, TOP_K), lambda i, s: (i, 0)),
                      pl.BlockSpec((8, D), lambda i, s: (0, 5))],
            out_specs=pl.BlockSpec((COMBINE_TOK, D), lambda i, s: (i, 0)),
            scratch_shapes=[pltpu.VMEM((2, COMBINE_TOK * TOP_K, D), F32), pltpu.SemaphoreType.DMA((2,))]),
        out_shape=jax.ShapeDtypeStruct((N_TOK, D), F32),
        compiler_params=_cparams(("arbitrary",)),
        name="moe_combine",
    )(slots_km, yb, x, gates, modtab)


def moe_routing(logits):
    top_v, top_e = lax.top_k(logits, TOP_K)
    gates = jax.nn.softmax(top_v, axis=-1)
    flat_e = top_e.reshape(-1)
    onehot = (flat_e[:, None] == jnp.arange(N_EXPERTS)[None, :]).astype(jnp.int32)
    rank = jnp.take_along_axis(jnp.cumsum(onehot, axis=0) - onehot, flat_e[:, None], axis=1)[:, 0]
    counts = onehot.sum(axis=0)
    padded = (counts + MOE_TM - 1) // MOE_TM * MOE_TM
    pend = jnp.cumsum(padded)
    slot = ((pend - padded)[flat_e] + rank).astype(jnp.int32)
    tok = jnp.arange(N_TOK, dtype=jnp.int32)
    tok_row = (tok // T_LAT) * S_ALL + T_CTX + tok % T_LAT
    slot_row = jnp.zeros((N_SLOT,), jnp.int32).at[slot].set(jnp.repeat(tok_row, TOP_K))
    n_used = (pend[-1] // MOE_TM).astype(jnp.int32)
    blk = jnp.arange(N_BLK)
    used = blk < n_used
    blk_e = jnp.sum(jnp.minimum(blk, n_used - 1)[:, None] * MOE_TM >= pend[None, :], axis=1).astype(jnp.int32)
    fresh = (used & ((blk == 0) | (blk_e != jnp.roll(blk_e, 1)))).astype(jnp.int32)
    run_idx = jnp.cumsum(fresh) - 1
    n_runs = fresh.sum()
    has = counts > 0
    ids = jnp.arange(N_EXPERTS)
    first_e = jnp.min(jnp.where(has, ids, N_EXPERTS))
    later = jnp.where(has[None, :] & (ids[None, :] > ids[:, None]), ids[None, :], N_EXPERTS).min(axis=1)
    next_of = jnp.where(later < N_EXPERTS, later, first_e)
    sched = (blk_e, fresh, run_idx.astype(jnp.int32), next_of[blk_e].astype(jnp.int32),
             (run_idx == n_runs - 1).astype(jnp.int32),
             jnp.where(used, jnp.clip(((pend - padded) + counts)[blk_e] - blk * MOE_TM, 0, MOE_TM), 0).astype(jnp.int32),
             jnp.stack([n_used, n_runs]).astype(jnp.int32))
    slots_km = slot.reshape(N_TOK // COMBINE_TOK, COMBINE_TOK, TOP_K).transpose(0, 2, 1).reshape(-1)
    return slot_row, sched, n_used.reshape(1), slots_km, gates


def _pad_cols(w, width):
    return jnp.pad(w, [(0, 0)] * (w.ndim - 1) + [(0, width - w.shape[-1])])


def layout_w_in(w):
    offs = np.cumsum((512, 512, 512, 512, 8, 8, 512, 1024, 16, 448, 160, 64, 512, 512, 512))[:-1].tolist()
    q, k, v, o, ig, fg, z, xbc, dt, cq, ckv, kr, naq, nak, nav = jnp.split(w, offs, axis=-1)
    gate = _pad_cols(jnp.concatenate([ig, fg, dt], axis=-1), 128)
    return jnp.concatenate([q, k, v, o, z, xbc, _pad_cols(cq, 512), _pad_cols(ckv, 256), _pad_cols(kr, 128),
                            naq, nak, nav, gate], axis=-1)


def rope_tables():
    t = np.arange(T_LAT)
    n_freq = MLA_ROPE // 4
    freqs = ROPE_THETA ** (-jnp.arange(n_freq, dtype=F32) / n_freq)
    row = jnp.asarray(t // GRID_W, F32)
    col = jnp.asarray(t % GRID_W, F32)
    ang = jnp.concatenate([row[:, None] * freqs, col[:, None] * freqs], axis=-1)
    cos, sin = jnp.cos(ang), jnp.sin(ang)
    half = MLA_ROPE // 2
    zc = jnp.zeros((T_LAT, 128 - MLA_ROPE), F32)
    zh = jnp.zeros((T_LAT, half), F32)
    cs = jnp.concatenate([cos, cos, zc], axis=1)
    s1 = jnp.concatenate([-sin, zh, zc], axis=1)
    s2 = jnp.concatenate([zh, sin, zc], axis=1)
    ident = jnp.concatenate([jnp.ones((T_CTX, MLA_ROPE), F32), jnp.zeros((T_CTX, 128 - MLA_ROPE), F32)], axis=1)
    zeros = jnp.zeros((T_CTX, 128), F32)
    return (jnp.concatenate([ident, cs], axis=0), jnp.concatenate([zeros, s1], axis=0),
            jnp.concatenate([zeros, s2], axis=0))


def _gate_vectors(i_bias, f_bias, dt_bias, a_log):
    used = jnp.concatenate([i_bias.reshape(-1), f_bias.reshape(-1), dt_bias.reshape(-1)])
    alog = jnp.concatenate([jnp.zeros((G_DT,), F32), a_log.reshape(-1)])
    padc = lambda u: jnp.pad(u, (0, 128 - G_USED)).reshape(1, 128)
    return padc(used), used.reshape(G_USED, 1), padc(alog), alog.reshape(G_USED, 1)


def kernel(x, c, ctx, c_ctx, mod_w, mod_b, norm1, w_in, w_out, ml_i_bias, ml_f_bias, ml_norm, ssd_conv_w, ssd_conv_b, ssd_dt_bias, ssd_A_log, ssd_D, ssd_norm, mla_q_norm, mla_w_qb, mla_kv_norm, mla_w_kvb, mla_gq, mla_gk, na_gq, na_gk, na_rpb, norm2, ffn_w1, ffn_w3, ffn_w2, moe_router, moe_w1, moe_w3, moe_w2):
    xs = jnp.concatenate([ctx, x], axis=1).reshape(N_ROWS, D)
    c_all = jnp.concatenate([c, c_ctx[None, :], jnp.zeros((8 - NB - 1, D), F32)], axis=0)
    mod_all = modulation(c_all, mod_w, mod_b)
    tabs = rope_tables()
    w_in_p = layout_w_in(w_in)
    out = None
    for l in range(DEPTH):
        modtab = mod_all[l]
        h = rms_modulate(xs, norm1[l], modtab, 0)
        p = matmul(h, [w_in_p], tm=1536, tn=1024, layer=l, name="w_in")
        gates_t = p[:, C_GATE:C_GATE + G_USED].reshape(N_ROWS // CHUNK, CHUNK, G_USED).transpose(0, 2, 1)
        biasc, biasr, alogc, alogr = _gate_vectors(ml_i_bias[l], ml_f_bias[l], ssd_dt_bias[l], ssd_A_log[l])

        hf, hb = mlstm_scan(p, gates_t, biasc, biasr)
        ml = mlstm_finish(hf, hb, p, ml_norm[l])

        xbc = ssd_conv(p, ssd_conv_w[l], ssd_conv_b[l])
        yf, yb = ssd_scan(xbc, p, gates_t, biasc, biasr, alogc, alogr)
        ss = ssd_finish(yf, yb, xbc, p, ssd_D[l], ssd_norm[l])

        wq = jnp.pad(mla_w_qb[l].reshape(MLA_Q_RANK, MLA_HEADS, MLA_QK),
                     ((0, 512 - MLA_Q_RANK), (0, 0), (0, MLA_HP - MLA_QK))).reshape(512, MLA_HEADS * MLA_HP)
        wkv = jnp.pad(mla_w_kvb[l], ((0, 256 - MLA_KV_RANK), (0, 0)))
        pad1 = lambda u, w: jnp.pad(u, (0, w - u.shape[0])).reshape(1, w)
        q, k, v = mla_prep(p, wq, wkv, pad1(mla_q_norm[l], 512), pad1(mla_kv_norm[l], 256),
                           pad1(mla_gq[l], MLA_HP), pad1(mla_gk[l], MLA_HP), tabs)
        la = mla_attention(q, k, v)

        na = na_attention(p, na_gq[l], na_gk[l], na_bias_table(na_rpb[l]))

        xs = matmul([ml, ss, la, na], [w_out], mode="resid", res=xs, modtab=modtab, gate_col=2,
                    tm=1536, layer=l, name="w_out")

        if l % 2 == 0:
            h2 = rms_modulate(xs, norm2[l], modtab, 3)
            hid = matmul(h2, [ffn_w1[l // 2], ffn_w3[l // 2]], mode="swiglu", out_dtype=BF16, tm=1536,
                         name="ffn_up")
            xs = matmul(hid, [ffn_w2[l // 2]], mode="resid", res=xs, modtab=modtab, gate_col=5,
                        w_buffers=1, name="ffn_down")
        else:
            h2, logits = rms_modulate(xs, norm2[l], modtab, 3, router=_pad_cols(moe_router[l // 2], 128))
            lat = logits.reshape(NB, S_ALL, 128)[:, T_CTX:, :N_EXPERTS].reshape(N_TOK, N_EXPERTS)
            slot_row, sched, n_used, slots_km, gates = moe_routing(lat)
            xb = moe_gather(h2, slot_row, n_used)
            hid = grouped_matmul(xb, [moe_w1[l // 2], moe_w3[l // 2]], sched, mode="swiglu", out_dtype=BF16,
                                 tm=MOE_TM, tn=1024, name="moe_up")
            yb_ = grouped_matmul(hid, [moe_w2[l // 2]], sched, tm=MOE_TM, tn=512, name="moe_down")
            out = moe_combine(yb_, xs, slots_km, gates, modtab)
    return out.reshape(NB, T_LAT, D)
```

```python
from functools import partial

import numpy as np
import jax
import jax.numpy as jnp
from jax import lax
from jax.experimental import pallas as pl
from jax.experimental.pallas import tpu as pltpu

F32 = jnp.float32
BF16 = jnp.bfloat16
HI = lax.Precision.HIGHEST

D = 2048
NB = 4
T_LAT = 2048
T_CTX = 256
S_ALL = T_LAT + T_CTX
N_ROWS = NB * S_ALL
DEPTH = 2
GRID_W = 64
EPS = 1e-6
LOG2E = 1.4426950408889634

ROWBLK = 256
BLK_PER_B = S_ALL // ROWBLK
CHUNK = 128
CH_PER_B = S_ALL // CHUNK
CH_CTX = T_CTX // CHUNK

ML_HEADS, ML_HD = 4, 128
SSD_HEADS, SSD_HD, SSD_GROUPS, SSD_STATE, SSD_CONV = 8, 64, 2, 128, 5
SSD_W = SSD_HEADS * SSD_HD
MLA_HEADS, MLA_NOPE, MLA_ROPE, MLA_V = 4, 128, 64, 128
MLA_QK = MLA_NOPE + MLA_ROPE
MLA_Q_RANK, MLA_KV_RANK = 448, 160
MLA_HP = 256
ROPE_THETA = 10000.0
NA_HEADS, NA_HD, NA_KH, NA_KW = 4, 128, 8, 16
N_EXPERTS, TOP_K = 8, 2

C_Q, C_K, C_V, C_O = 0, 512, 1024, 1536
C_Z = 2048
C_XBC = 2560
C_CQ, C_CKV, C_KR = 3584, 4096, 4352
C_NAQ, C_NAK, C_NAV = 4480, 4992, 5504
C_GATE = 6016
D_INP = 6144
G_I, G_F, G_DT, G_USED = 0, 8, 16, 32

VMEM_LIMIT = 56 * 1024 * 1024


def _cparams(sem):
    return pltpu.CompilerParams(dimension_semantics=sem, vmem_limit_bytes=VMEM_LIMIT)


def _mod_index(blk256):
    return jnp.where(blk256 % BLK_PER_B == 0, NB, blk256 // BLK_PER_B)


def _dot_nt(a, b):
    return lax.dot_general(a, b, (((1,), (1,)), ((), ())), preferred_element_type=F32)


def _dot_tn(a, b):
    return lax.dot_general(a, b, (((0,), (0,)), ((), ())), preferred_element_type=F32)


def _silu(x):
    return x * jax.nn.sigmoid(x)


def _softplus(x):
    return jnp.maximum(x, 0.0) + jnp.log(1.0 + jnp.exp(-jnp.abs(x)))


def _mod_kernel(c_ref, w_ref, b_ref, o_ref):
    o_ref[0] = jnp.dot(_silu(c_ref[...]), w_ref[0], preferred_element_type=F32) + b_ref[0]


def modulation(c_all, mod_w, mod_b):
    tn = 1024
    n_out = mod_w.shape[-1]
    return pl.pallas_call(
        _mod_kernel,
        grid=(DEPTH, n_out // tn),
        in_specs=[pl.BlockSpec((8, D), lambda l, j: (0, 0)),
                  pl.BlockSpec((1, D, tn), lambda l, j: (l, 0, j)),
                  pl.BlockSpec((1, 1, tn), lambda l, j: (l, 0, j))],
        out_specs=pl.BlockSpec((1, 8, tn), lambda l, j: (l, 0, j)),
        out_shape=jax.ShapeDtypeStruct((DEPTH, 8, n_out), F32),
        compiler_params=_cparams(("arbitrary", "arbitrary")),
        name="modulation",
    )(c_all, mod_w, mod_b.reshape(DEPTH, 1, n_out))


ELT_ROWS = 768


def _rmsmod_kernel(x_ref, g_ref, sh_ref, sc_ref, *rest, with_router):
    for s in range(ELT_ROWS // ROWBLK):
        midx = _mod_index(pl.program_id(0) * (ELT_ROWS // ROWBLK) + s)
        rows = slice(s * ROWBLK, (s + 1) * ROWBLK)
        x = x_ref[rows, :]
        y = x * lax.rsqrt(jnp.mean(x * x, axis=-1, keepdims=True) + EPS) * g_ref[...]
        h = y * (1.0 + sc_ref[pl.ds(midx, 1), :]) + sh_ref[pl.ds(midx, 1), :]
        if with_router:
            r_ref, o_ref, lg_ref = rest
            o_ref[rows, :] = h
            lg_ref[rows, :] = jnp.dot(h, r_ref[...], precision=HI, preferred_element_type=F32)
        else:
            (o_ref,) = rest
            o_ref[rows, :] = h.astype(BF16)


def rms_modulate(x, g, modtab, which_shift, router=None):
    with_router = router is not None
    in_specs = [pl.BlockSpec((ELT_ROWS, D), lambda i: (i, 0)),
                pl.BlockSpec((1, D), lambda i: (0, 0)),
                pl.BlockSpec((8, D), lambda i: (0, which_shift)),
                pl.BlockSpec((8, D), lambda i: (0, which_shift + 1))]
    args = [x, g.reshape(1, D), modtab, modtab]
    if with_router:
        in_specs.append(pl.BlockSpec((D, 128), lambda i: (0, 0)))
        args.append(router)
        out_shape = [jax.ShapeDtypeStruct((N_ROWS, D), F32), jax.ShapeDtypeStruct((N_ROWS, 128), F32)]
        out_specs = [pl.BlockSpec((ELT_ROWS, D), lambda i: (i, 0)), pl.BlockSpec((ELT_ROWS, 128), lambda i: (i, 0))]
    else:
        out_shape = jax.ShapeDtypeStruct((N_ROWS, D), BF16)
        out_specs = pl.BlockSpec((ELT_ROWS, D), lambda i: (i, 0))
    return pl.pallas_call(
        partial(_rmsmod_kernel, with_router=with_router),
        grid=(N_ROWS // ELT_ROWS,),
        in_specs=in_specs, out_specs=out_specs, out_shape=out_shape,
        compiler_params=_cparams(("arbitrary",)),
        name="rms_modulate_router" if with_router else "rms_modulate",
    )(*args)


MM_COLS = 512


def _mm_compute(mode, a, w_tiles, o_ref, i=None, tm=None, res_ref=None, gate_ref=None, rows=slice(None)):
    tn = o_ref.shape[1]
    for c0 in range(0, tn, MM_COLS):
        cols = slice(c0, min(c0 + MM_COLS, tn))
        accs = [jnp.dot(a, w[:, cols], preferred_element_type=F32) for w in w_tiles]
        if mode == "plain":
            o_ref[rows, cols] = accs[0].astype(o_ref.dtype)
        elif mode == "swiglu":
            o_ref[rows, cols] = (_silu(accs[0]) * accs[1]).astype(o_ref.dtype)
        else:
            for s in range(tm // ROWBLK):
                midx = _mod_index(i * (tm // ROWBLK) + s)
                rows = slice(s * ROWBLK, (s + 1) * ROWBLK)
                o_ref[rows, cols] = res_ref[rows, cols] + gate_ref[pl.ds(midx, 1), cols] * accs[0][rows, :]


def _mm_kernel(*refs, mode, tm, na, stacked):
    nw = 2 if mode == "swiglu" else 1
    a_refs = refs[:na]
    w_refs = refs[na:na + nw]
    pos = na + nw
    res_ref = gate_ref = None
    if mode == "resid":
        res_ref, gate_ref = refs[pos:pos + 2]
        pos += 2
    o_ref = refs[pos]
    ws_refs = refs[pos + 1:]
    i = pl.program_id(1)

    @pl.when(i == 0)
    def _():
        for w_ref, ws_ref in zip(w_refs, ws_refs):
            ws_ref[...] = (w_ref[0] if stacked else w_ref[...]).astype(BF16)

    a = a_refs[0][...] if na == 1 else jnp.concatenate([a_ref[...] for a_ref in a_refs], axis=1)
    _mm_compute(mode, a, ws_refs, o_ref, i, tm, res_ref, gate_ref)


def matmul(a, ws, mode="plain", out_dtype=F32, res=None, modtab=None, gate_col=None,
           tm=768, tn=512, w_buffers=2, layer=None, name="matmul"):
    pieces = list(a) if isinstance(a, (list, tuple)) else [a]
    m = pieces[0].shape[0]
    kdim = sum(p.shape[1] for p in pieces)
    n = ws[0].shape[-1]
    wmode = {} if w_buffers == 2 else dict(pipeline_mode=pl.Buffered(w_buffers))
    in_specs = [pl.BlockSpec((tm, p.shape[1]), lambda j, i: (i, 0)) for p in pieces]
    if layer is None:
        in_specs += [pl.BlockSpec((kdim, tn), lambda j, i: (0, j), **wmode) for _ in ws]
    else:
        in_specs += [pl.BlockSpec((1, kdim, tn), lambda j, i: (layer, 0, j), **wmode) for _ in ws]
    args = [*pieces, *ws]
    if mode == "resid":
        in_specs += [pl.BlockSpec((tm, tn), lambda j, i: (i, j)),
                     pl.BlockSpec((8, tn), lambda j, i: (0, gate_col * (D // tn) + j))]
        args += [res, modtab]
    return pl.pallas_call(
        partial(_mm_kernel, mode=mode, tm=tm, na=len(pieces), stacked=layer is not None),
        grid=(n // tn, m // tm),
        in_specs=in_specs,
        out_specs=pl.BlockSpec((tm, tn), lambda j, i: (i, j)),
        out_shape=jax.ShapeDtypeStruct((m, n), out_dtype),
        scratch_shapes=[pltpu.VMEM((kdim, tn), BF16) for _ in ws],
        compiler_params=_cparams(("arbitrary", "arbitrary")),
        name=name,
    )(*args)


def _gmm_kernel(be_ref, fresh_ref, run_ref, nexte_ref, lastrun_ref, fill_ref, meta_ref, a_ref, *rest,
                mode, nw, tn):
    w_hbm = rest[:nw]
    o_ref = rest[nw]
    stage_ref, ws_ref, sem = rest[nw + 1:]
    j = pl.program_id(0)
    i = pl.program_id(1)
    n_runs = meta_ref[1]
    half = a_ref.shape[0] // 2

    def w_copy(widx, e, jj):
        col = pl.multiple_of(jj * tn, tn)
        return pltpu.make_async_copy(w_hbm[widx].at[e, :, pl.ds(col, tn)], stage_ref.at[widx], sem.at[widx])

    @pl.when(fresh_ref[i] == 1)
    def _():
        @pl.when(j * n_runs + run_ref[i] == 0)
        def _():
            for widx in range(nw):
                w_copy(widx, be_ref[i], j).start()

        for widx in range(nw):
            w_copy(widx, be_ref[i], j).wait()
            ws_ref[widx] = stage_ref[widx].astype(BF16)
        is_last = lastrun_ref[i] == 1

        @pl.when(jnp.logical_not(jnp.logical_and(is_last, j == pl.num_programs(0) - 1)))
        def _():
            jn = jnp.where(is_last, j + 1, j)
            for widx in range(nw):
                w_copy(widx, nexte_ref[i], jn).start()

    w_tiles = [ws_ref.at[widx] for widx in range(nw)]
    fill = fill_ref[i]

    @pl.when(fill > half)
    def _():
        _mm_compute(mode, a_ref[...], w_tiles, o_ref)

    @pl.when(jnp.logical_and(fill > 0, fill <= half))
    def _():
        _mm_compute(mode, a_ref[0:half, :], w_tiles, o_ref, rows=slice(0, half))
        o_ref[half:, :] = jnp.zeros((half, o_ref.shape[1]), o_ref.dtype)

    @pl.when(fill == 0)
    def _():
        o_ref[...] = jnp.zeros_like(o_ref)


def grouped_matmul(a, ws, sched, mode="plain", out_dtype=F32, tm=512, tn=512, name="grouped_matmul"):
    m, kdim = a.shape
    n = ws[0].shape[-1]
    nw = len(ws)
    row = lambda j, i, be, fr, ru, ne, lr, fill, meta: jnp.minimum(i, meta[0] - 1)
    grid_spec = pltpu.PrefetchScalarGridSpec(
        num_scalar_prefetch=len(sched),
        grid=(n // tn, m // tm),
        in_specs=[pl.BlockSpec((tm, kdim), lambda *s: (row(*s), 0))] + [pl.BlockSpec(memory_space=pl.ANY)] * nw,
        out_specs=pl.BlockSpec((tm, tn), lambda j, i, *_: (i, j)),
        scratch_shapes=[pltpu.VMEM((nw, kdim, tn), F32), pltpu.VMEM((nw, kdim, tn), BF16),
                        pltpu.SemaphoreType.DMA((nw,))])
    return pl.pallas_call(
        partial(_gmm_kernel, mode=mode, nw=nw, tn=tn),
        grid_spec=grid_spec,
        out_shape=jax.ShapeDtypeStruct((m, n), out_dtype),
        compiler_params=_cparams(("arbitrary", "arbitrary")),
        name=name,
    )(*sched, a, *ws)


def _fwd_chunk(i):
    return i


def _bwd_chunk(i):
    return jnp.where(i < CH_CTX, CH_CTX - 1 - i, CH_PER_B + CH_CTX - 1 - i)


def _tri_masks():
    r = lax.broadcasted_iota(jnp.int32, (CHUNK, CHUNK), 0)
    c = lax.broadcasted_iota(jnp.int32, (CHUNK, CHUNK), 1)
    return c <= r, c >= r


def _mlstm_kernel(qf_ref, kf_ref, vf_ref, gcf_ref, grf_ref,
                  qb_ref, kb_ref, vb_ref, gcb_ref, grb_ref,
                  biasc_ref, biasr_ref, hf_ref, hb_ref, c_ref, n_ref, m_ref):
    @pl.when(pl.program_id(1) == 0)
    def _():
        c_ref[...] = jnp.zeros_like(c_ref)
        n_ref[...] = jnp.zeros_like(n_ref)
        m_ref[...] = jnp.zeros_like(m_ref)

    low, upp = _tri_masks()
    lowf, uppf = low.astype(F32), upp.astype(F32)
    streams = ((0, qf_ref, kf_ref, vf_ref, gcf_ref, grf_ref, hf_ref, upp, lowf, uppf, CHUNK - 1),
               (1, qb_ref, kb_ref, vb_ref, gcb_ref, grb_ref, hb_ref, low, uppf, lowf, 0))
    for d, q_ref, k_ref, v_ref, gc_ref, gr_ref, h_ref, mask_t, tri_c, tri_r, last in streams:
        ac = gc_ref[...] + biasc_ref[...]
        ar = gr_ref[0] + biasr_ref[...]
        lfc = -_softplus(-ac)
        lfr = -_softplus(-ar)
        bc_all = jnp.dot(tri_c, lfc, precision=HI, preferred_element_type=F32)
        br_all = jnp.dot(lfr, tri_r, precision=HI, preferred_element_type=F32)
        for h in range(ML_HEADS):
            s_idx = d * ML_HEADS + h
            ji, jf = G_I + s_idx, G_F + s_idx
            cols = slice(h * ML_HD, (h + 1) * ML_HD)
            q = q_ref[:, cols] * (ML_HD ** -0.5)
            k = k_ref[:, cols]
            vt = v_ref[:, cols].T
            b_row = br_all[jf:jf + 1, :]
            i_row = ar[ji:ji + 1, :]
            key_col = ac[:, ji:ji + 1] - bc_all[:, jf:jf + 1]
            cmat = c_ref[s_idx]
            nvec = n_ref[s_idx]
            m_prev = m_ref[s_idx][:, 0:1]
            dmat = jnp.where(mask_t, b_row + key_col, -1e30)
            inter = b_row + m_prev
            mt = jnp.maximum(inter, jnp.max(dmat, axis=0, keepdims=True))
            w_intra = jnp.exp(dmat - mt)
            w_state = jnp.exp(inter - mt)
            st = _dot_nt(k, q) * w_intra
            num = jnp.dot(vt, st, preferred_element_type=F32) + w_state * _dot_nt(cmat, q)
            den = jnp.sum(st, axis=0, keepdims=True) + w_state * _dot_nt(nvec, q)
            h_ref[:, cols] = (num / jnp.maximum(jnp.abs(den), jnp.exp(-mt))).T
            bl = b_row[:, last:last + 1]
            g = bl - b_row + i_row
            m_new = jnp.maximum(bl + m_prev, jnp.max(g, axis=1, keepdims=True))
            wg = jnp.exp(g - m_new)
            wc = jnp.exp(bl + m_prev - m_new)
            c_ref[s_idx] = wc * cmat + jnp.dot(vt * wg, k, preferred_element_type=F32)
            n_ref[s_idx] = wc * nvec + jnp.dot(wg, k, preferred_element_type=F32)
            m_ref[s_idx] = jnp.broadcast_to(m_new, (1, 128))


def mlstm_scan(p, gates_t, biasc, biasr):
    def rows(chunk_fn):
        return lambda b, i: b * CH_PER_B + chunk_fn(i)

    def stream_specs(chunk_fn):
        r = rows(chunk_fn)
        return [pl.BlockSpec((CHUNK, 512), lambda b, i: (r(b, i), C_Q // 512)),
                pl.BlockSpec((CHUNK, 512), lambda b, i: (r(b, i), C_K // 512)),
                pl.BlockSpec((CHUNK, 512), lambda b, i: (r(b, i), C_V // 512)),
                pl.BlockSpec((CHUNK, 128), lambda b, i: (r(b, i), C_GATE // 128)),
                pl.BlockSpec((1, G_USED, CHUNK), lambda b, i: (r(b, i), 0, 0))]

    rf, rb = rows(_fwd_chunk), rows(_bwd_chunk)
    n_streams = 2 * ML_HEADS
    return pl.pallas_call(
        _mlstm_kernel,
        grid=(NB, CH_PER_B),
        in_specs=stream_specs(_fwd_chunk) + stream_specs(_bwd_chunk) + [
            pl.BlockSpec((1, 128), lambda b, i: (0, 0)),
            pl.BlockSpec((G_USED, 1), lambda b, i: (0, 0))],
        out_specs=[pl.BlockSpec((CHUNK, 512), lambda b, i: (rf(b, i), 0)),
                   pl.BlockSpec((CHUNK, 512), lambda b, i: (rb(b, i), 0))],
        out_shape=[jax.ShapeDtypeStruct((N_ROWS, 512), F32)] * 2,
        scratch_shapes=[pltpu.VMEM((n_streams, ML_HD, ML_HD), F32),
                        pltpu.VMEM((n_streams, 1, ML_HD), F32),
                        pltpu.VMEM((n_streams, 1, 128), F32)],
        compiler_params=_cparams(("arbitrary", "arbitrary")),
        name="mlstm_scan",
    )(p, p, p, p, gates_t, p, p, p, p, gates_t, biasc, biasr)


def _mlstm_finish_kernel(hf_ref, hb_ref, o_ref, g_ref, out_ref):
    for h in range(ML_HEADS):
        cols = slice(h * ML_HD, (h + 1) * ML_HD)
        x = hf_ref[:, cols] + hb_ref[:, cols]
        y = x * lax.rsqrt(jnp.mean(x * x, axis=-1, keepdims=True) + EPS) * g_ref[:, cols]
        out_ref[:, cols] = (y * jax.nn.sigmoid(o_ref[:, cols])).astype(BF16)


def mlstm_finish(hf, hb, p, norm_g):
    blk = lambda c: pl.BlockSpec((ELT_ROWS, 512), lambda i: (i, c))
    return pl.pallas_call(
        _mlstm_finish_kernel,
        grid=(N_ROWS // ELT_ROWS,),
        in_specs=[blk(0), blk(0), blk(C_O // 512), pl.BlockSpec((1, 512), lambda i: (0, 0))],
        out_specs=blk(0),
        out_shape=jax.ShapeDtypeStruct((N_ROWS, 512), BF16),
        compiler_params=_cparams(("arbitrary",)),
        name="mlstm_finish",
    )(hf, hb, p, norm_g.reshape(1, 512))


_PAD = 8


def _ssd_conv_kernel(u_ref, w_ref, b_ref, o_ref, pad_ref):
    zeros = jnp.zeros((_PAD, u_ref.shape[1]), F32)
    segs = ((0, T_CTX, _PAD), (T_CTX, T_LAT, 2 * _PAD + T_CTX))
    pad_ref[0:_PAD] = zeros
    pad_ref[_PAD + T_CTX:2 * _PAD + T_CTX] = zeros
    pad_ref[2 * _PAD + S_ALL:3 * _PAD + S_ALL] = zeros
    for src, n, dst in segs:
        pad_ref[dst:dst + n] = u_ref[src:src + n]
    for src, n, dst in segs:
        acc = jnp.broadcast_to(b_ref[...], (n, u_ref.shape[1]))
        for j in range(SSD_CONV):
            lo = dst + j - SSD_CONV // 2
            acc = acc + w_ref[j:j + 1, :] * pad_ref[lo:lo + n]
        o_ref[src:src + n] = _silu(acc)


def ssd_conv(p, conv_w, conv_b):
    tc = 256
    return pl.pallas_call(
        _ssd_conv_kernel,
        grid=(NB, 1024 // tc),
        in_specs=[pl.BlockSpec((S_ALL, tc), lambda b, c: (b, C_XBC // tc + c)),
                  pl.BlockSpec((SSD_CONV, tc), lambda b, c: (0, c)),
                  pl.BlockSpec((1, tc), lambda b, c: (0, c))],
        out_specs=pl.BlockSpec((S_ALL, tc), lambda b, c: (b, c)),
        out_shape=jax.ShapeDtypeStruct((N_ROWS, 1024), F32),
        scratch_shapes=[pltpu.VMEM((S_ALL + 3 * _PAD, tc), F32)],
        compiler_params=_cparams(("arbitrary", "arbitrary")),
        name="ssd_conv",
    )(p, conv_w, conv_b.reshape(1, 1024))


def _ssd_kernel(xf_ref, bf_ref, cf_ref, gcf_ref, grf_ref,
                xb_ref, bb_ref, cb_ref, gcb_ref, grb_ref,
                biasc_ref, biasr_ref, alogc_ref, alogr_ref, yf_ref, yb_ref, st_ref):
    @pl.when(pl.program_id(1) == 0)
    def _():
        st_ref[...] = jnp.zeros_like(st_ref)

    low, upp = _tri_masks()
    lowf, uppf = low.astype(F32), upp.astype(F32)
    a_c = -jnp.exp(alogc_ref[...])
    a_r = -jnp.exp(alogr_ref[...])
    streams = ((0, xf_ref, bf_ref, cf_ref, gcf_ref, grf_ref, yf_ref, upp, lowf, uppf, CHUNK - 1),
               (1, xb_ref, bb_ref, cb_ref, gcb_ref, grb_ref, yb_ref, low, uppf, lowf, 0))
    for d, x_ref, b_ref, c_ref, gc_ref, gr_ref, y_ref, mask_t, tri_c, tri_r, last in streams:
        dtc_all = _softplus(gc_ref[...] + biasc_ref[...])
        dtr_all = _softplus(gr_ref[0] + biasr_ref[...])
        negc_all = -jnp.dot(tri_c, dtc_all * a_c, precision=HI, preferred_element_type=F32)
        cumr_all = jnp.dot(dtr_all * a_r, tri_r, precision=HI, preferred_element_type=F32)
        xt = x_ref[...].T
        yts = []
        for g in range(SSD_GROUPS):
            gcols = slice(g * SSD_STATE, (g + 1) * SSD_STATE)
            bm = b_ref[:, gcols]
            cm = c_ref[:, gcols]
            bc = _dot_nt(bm, cm)
            for hh in range(SSD_HEADS // SSD_GROUPS):
                h = g * (SSD_HEADS // SSD_GROUPS) + hh
                j = G_DT + d * SSD_HEADS + h
                s_idx = d * SSD_HEADS + h
                cum = cumr_all[j:j + 1, :]
                xd = xt[h * SSD_HD:(h + 1) * SSD_HD, :] * dtr_all[j:j + 1, :]
                st = st_ref[s_idx]
                w = bc * jnp.exp(jnp.where(mask_t, cum + negc_all[:, j:j + 1], -1e30))
                yts.append(jnp.dot(xd, w, preferred_element_type=F32) + _dot_nt(st, cm) * jnp.exp(cum))
                cl = cum[:, last:last + 1]
                st_ref[s_idx] = st * jnp.exp(cl) + jnp.dot(xd * jnp.exp(cl - cum), bm, preferred_element_type=F32)
        y_ref[...] = jnp.concatenate(yts, axis=0).T


def ssd_scan(xbc, p, gates_t, biasc, biasr, alogc, alogr):
    def rows(chunk_fn):
        return lambda b, i: b * CH_PER_B + chunk_fn(i)

    def stream_specs(chunk_fn):
        r = rows(chunk_fn)
        return [pl.BlockSpec((CHUNK, 512), lambda b, i: (r(b, i), 0)),
                pl.BlockSpec((CHUNK, 256), lambda b, i: (r(b, i), 2)),
                pl.BlockSpec((CHUNK, 256), lambda b, i: (r(b, i), 3)),
                pl.BlockSpec((CHUNK, 128), lambda b, i: (r(b, i), C_GATE // 128)),
                pl.BlockSpec((1, G_USED, CHUNK), lambda b, i: (r(b, i), 0, 0))]

    rf, rb = rows(_fwd_chunk), rows(_bwd_chunk)
    const = lambda shape: pl.BlockSpec(shape, lambda b, i: (0, 0))
    return pl.pallas_call(
        _ssd_kernel,
        grid=(NB, CH_PER_B),
        in_specs=stream_specs(_fwd_chunk) + stream_specs(_bwd_chunk) + [
            const((1, 128)), const((G_USED, 1)), const((1, 128)), const((G_USED, 1))],
        out_specs=[pl.BlockSpec((CHUNK, 512), lambda b, i: (rf(b, i), 0)),
                   pl.BlockSpec((CHUNK, 512), lambda b, i: (rb(b, i), 0))],
        out_shape=[jax.ShapeDtypeStruct((N_ROWS, 512), F32)] * 2,
        scratch_shapes=[pltpu.VMEM((2 * SSD_HEADS, SSD_HD, SSD_STATE), F32)],
        compiler_params=_cparams(("arbitrary", "arbitrary")),
        name="ssd_scan",
    )(xbc, xbc, xbc, p, gates_t, xbc, xbc, xbc, p, gates_t, biasc, biasr, alogc, alogr)


def _ssd_finish_kernel(yf_ref, yb_ref, xs_ref, z_ref, dsk_ref, g_ref, out_ref):
    y = yf_ref[...] + yb_ref[...] + dsk_ref[...] * xs_ref[...]
    y = y * _silu(z_ref[...])
    out_ref[...] = (y * lax.rsqrt(jnp.mean(y * y, axis=-1, keepdims=True) + EPS) * g_ref[...]).astype(BF16)


def ssd_finish(yf, yb, xbc, p, d_skip, norm_g):
    blk = lambda c: pl.BlockSpec((ELT_ROWS, 512), lambda i: (i, c))
    vec = pl.BlockSpec((1, 512), lambda i: (0, 0))
    return pl.pallas_call(
        _ssd_finish_kernel,
        grid=(N_ROWS // ELT_ROWS,),
        in_specs=[blk(0), blk(0), blk(0), blk(C_Z // 512), vec, vec],
        out_specs=blk(0),
        out_shape=jax.ShapeDtypeStruct((N_ROWS, 512), BF16),
        compiler_params=_cparams(("arbitrary",)),
        name="ssd_finish",
    )(yf, yb, xbc, p, jnp.repeat(d_skip, SSD_HD).reshape(1, 512), norm_g.reshape(1, 512))


def _rope_tile(r, cs_ref, s1_ref, s2_ref):
    return (r * cs_ref[...] + pltpu.roll(r, 128 - MLA_ROPE // 2, 1) * s1_ref[...]
            + pltpu.roll(r, MLA_ROPE // 2, 1) * s2_ref[...])


def _mla_prep_kernel(cq_ref, ckv_ref, kr_ref, wq_ref, wkv_ref, qn_ref, kvn_ref, gq_ref, gk_ref,
                     cs_ref, s1_ref, s2_ref, q_out, k_out, v_out):
    cq = cq_ref[...]
    cqn = cq * lax.rsqrt(jnp.sum(cq * cq, axis=-1, keepdims=True) / MLA_Q_RANK + EPS) * qn_ref[...]
    q_raw = jnp.dot(cqn, wq_ref[...], preferred_element_type=F32)
    ckv = ckv_ref[...]
    ckvn = ckv * lax.rsqrt(jnp.sum(ckv * ckv, axis=-1, keepdims=True) / MLA_KV_RANK + EPS) * kvn_ref[...]
    kv_raw = jnp.dot(ckvn, wkv_ref[...], preferred_element_type=F32)
    kr = kr_ref[...]
    kr_ss = jnp.sum(kr * kr, axis=-1, keepdims=True)
    for h in range(MLA_HEADS):
        base = h * MLA_HP
        qa = q_raw[:, base:base + 128]
        qb = q_raw[:, base + 128:base + 256]
        q_scale = lax.rsqrt((jnp.sum(qa * qa, axis=-1, keepdims=True)
                             + jnp.sum(qb * qb, axis=-1, keepdims=True)) / MLA_QK + EPS) * (MLA_QK ** -0.5 * LOG2E)
        q_out[:, base:base + 128] = (qa * q_scale * gq_ref[:, 0:128]).astype(BF16)
        q_out[:, base + 128:base + 256] = _rope_tile(qb * q_scale * gq_ref[:, 128:256],
                                                     cs_ref, s1_ref, s2_ref).astype(BF16)
        kn = kv_raw[:, base:base + 128]
        k_scale = lax.rsqrt((jnp.sum(kn * kn, axis=-1, keepdims=True) + kr_ss) / MLA_QK + EPS)
        k_out[:, base:base + 128] = (kn * k_scale * gk_ref[:, 0:128]).astype(BF16)
        k_out[:, base + 128:base + 256] = _rope_tile(kr * k_scale * gk_ref[:, 128:256],
                                                     cs_ref, s1_ref, s2_ref).astype(BF16)
        v_out[:, h * MLA_V:(h + 1) * MLA_V] = kv_raw[:, base + 128:base + 256].astype(BF16)


MLA_PREP_ROWS = 768


def mla_prep(p, wq, wkv, qn, kvn, gq, gk, rope_tabs):
    const = lambda shape: pl.BlockSpec(shape, lambda i: (0, 0))
    tab = pl.BlockSpec((MLA_PREP_ROWS, 128), lambda i: (i % (S_ALL // MLA_PREP_ROWS), 0))
    row = lambda w: pl.BlockSpec((MLA_PREP_ROWS, w), lambda i: (i, 0))
    return pl.pallas_call(
        _mla_prep_kernel,
        grid=(N_ROWS // MLA_PREP_ROWS,),
        in_specs=[pl.BlockSpec((MLA_PREP_ROWS, 512), lambda i: (i, C_CQ // 512)),
                  pl.BlockSpec((MLA_PREP_ROWS, 256), lambda i: (i, C_CKV // 256)),
                  pl.BlockSpec((MLA_PREP_ROWS, 128), lambda i: (i, C_KR // 128)),
                  const((512, MLA_HEADS * MLA_HP)), const((256, MLA_HEADS * MLA_HP)),
                  const((1, 512)), const((1, 256)), const((1, MLA_HP)), const((1, MLA_HP)),
                  tab, tab, tab],
        out_specs=[row(MLA_HEADS * MLA_HP), row(MLA_HEADS * MLA_HP), row(MLA_HEADS * MLA_V)],
        out_shape=[jax.ShapeDtypeStruct((N_ROWS, MLA_HEADS * MLA_HP), BF16),
                   jax.ShapeDtypeStruct((N_ROWS, MLA_HEADS * MLA_HP), BF16),
                   jax.ShapeDtypeStruct((N_ROWS, MLA_HEADS * MLA_V), BF16)],
        compiler_params=_cparams(("arbitrary",)),
        name="mla_prep",
    )(p, p, p, wq, wkv, qn, kvn, gq, gk, *rope_tabs)


def _softmax_pv(s_list, v_list):
    m = s_list[0].max(axis=-1, keepdims=True)
    for s in s_list[1:]:
        m = jnp.maximum(m, s.max(axis=-1, keepdims=True))
    ps = [jnp.exp2(s - m) for s in s_list]
    l = sum(p.sum(axis=-1, keepdims=True) for p in ps)
    o = sum(jnp.dot(p.astype(BF16), v, preferred_element_type=F32) for p, v in zip(ps, v_list))
    return o / l


MLA_KCHUNK = 512


def _mla_attn_kernel(q_ref, k_ref, v_ref, o_ref):
    q = q_ref[...]

    @pl.when(pl.program_id(2) == 0)
    def _():
        o_ref[...] = _softmax_pv([_dot_nt(q, k_ref[0:T_CTX, :])], [v_ref[0:T_CTX, :]]).astype(BF16)

    @pl.when(pl.program_id(2) > 0)
    def _():
        chunks = [slice(c0, min(c0 + MLA_KCHUNK, S_ALL)) for c0 in range(0, S_ALL, MLA_KCHUNK)]
        o_ref[...] = _softmax_pv([_dot_nt(q, k_ref[c, :]) for c in chunks], [v_ref[c, :] for c in chunks]).astype(BF16)


def mla_attention(q, k, v):
    return pl.pallas_call(
        _mla_attn_kernel,
        grid=(NB, MLA_HEADS, BLK_PER_B),
        in_specs=[pl.BlockSpec((ROWBLK, MLA_HP), lambda b, h, i: (b * BLK_PER_B + i, h)),
                  pl.BlockSpec((S_ALL, MLA_HP), lambda b, h, i: (b, h)),
                  pl.BlockSpec((S_ALL, MLA_V), lambda b, h, i: (b, h))],
        out_specs=pl.BlockSpec((ROWBLK, MLA_V), lambda b, h, i: (b * BLK_PER_B + i, h)),
        out_shape=jax.ShapeDtypeStruct((N_ROWS, MLA_HEADS * MLA_V), BF16),
        compiler_params=_cparams(("arbitrary", "arbitrary", "arbitrary")),
        name="mla_attention",
    )(q, k, v)


N_GRID_ROWS = T_LAT // GRID_W
NA_QROWS = ROWBLK // GRID_W
NA_GROUPS = N_GRID_ROWS // NA_QROWS
NA_WIN_ROWS = NA_KH + NA_QROWS - 1
NA_WIN = NA_WIN_ROWS * GRID_W


def _na_win_row(g):
    return jnp.clip(g * NA_QROWS - NA_KH // 2, 0, N_GRID_ROWS - NA_WIN_ROWS)


def _na_kernel(q_ref, k_ref, v_ref, gq_ref, gk_ref, bias_ref, o_ref, kn_ref, vb_ref):
    st = pl.program_id(2)

    @pl.when(st == 0)
    def _():
        k = k_ref[...]
        kn_ref[...] = (k * lax.rsqrt(jnp.mean(k * k, axis=-1, keepdims=True) + EPS) * gk_ref[...]).astype(BF16)
        vb_ref[...] = v_ref[...].astype(BF16)

    q = q_ref[...]
    q_scale = lax.rsqrt(jnp.mean(q * q, axis=-1, keepdims=True) + EPS) * (NA_HD ** -0.5 * LOG2E)
    qn = (q * q_scale * gq_ref[...]).astype(BF16)
    s_cx = _dot_nt(qn, kn_ref[0:T_CTX, :])
    v_cx = vb_ref[0:T_CTX, :]

    @pl.when(st == 0)
    def _():
        o_ref[...] = _softmax_pv([s_cx], [v_cx]).astype(BF16)

    @pl.when(st > 0)
    def _():
        start = pl.multiple_of(T_CTX + _na_win_row(st - 1) * GRID_W, GRID_W)
        s_nb = _dot_nt(qn, kn_ref[pl.ds(start, NA_WIN), :]) + bias_ref[0, 0]
        o_ref[...] = _softmax_pv([s_nb, s_cx], [vb_ref[pl.ds(start, NA_WIN), :], v_cx]).astype(BF16)


def _na_class(g):
    return jnp.where(g == 0, 0, jnp.where(g == NA_GROUPS - 1, 2, 1))


def na_attention(p, gq, gk, bias_tab):
    return pl.pallas_call(
        _na_kernel,
        grid=(NB, NA_HEADS, BLK_PER_B),
        in_specs=[pl.BlockSpec((ROWBLK, NA_HD), lambda b, h, st: (b * BLK_PER_B + st, C_NAQ // NA_HD + h)),
                  pl.BlockSpec((S_ALL, NA_HD), lambda b, h, st: (b, C_NAK // NA_HD + h)),
                  pl.BlockSpec((S_ALL, NA_HD), lambda b, h, st: (b, C_NAV // NA_HD + h)),
                  pl.BlockSpec((1, NA_HD), lambda b, h, st: (0, 0)),
                  pl.BlockSpec((1, NA_HD), lambda b, h, st: (0, 0)),
                  pl.BlockSpec((1, 1, ROWBLK, NA_WIN),
                               lambda b, h, st: (h, _na_class(jnp.maximum(st - 1, 0)), 0, 0))],
        out_specs=pl.BlockSpec((ROWBLK, NA_HD), lambda b, h, st: (b * BLK_PER_B + st, h)),
        out_shape=jax.ShapeDtypeStruct((N_ROWS, NA_HEADS * NA_HD), BF16),
        scratch_shapes=[pltpu.VMEM((S_ALL, NA_HD), BF16), pltpu.VMEM((S_ALL, NA_HD), BF16)],
        compiler_params=_cparams(("arbitrary", "arbitrary", "arbitrary")),
        name="na_attention",
    )(p, p, p, gq.reshape(1, NA_HD), gk.reshape(1, NA_HD), bias_tab)


def _na_window_tables():
    g_rep = np.array([0, 1, NA_GROUPS - 1])
    r = g_rep[:, None] * NA_QROWS + np.arange(NA_QROWS)[None, :]
    r0 = np.clip(r - NA_KH // 2, 0, N_GRID_ROWS - NA_KH)
    w0 = np.clip(g_rep * NA_QROWS - NA_KH // 2, 0, N_GRID_ROWS - NA_WIN_ROWS)
    kr = w0[:, None] + np.arange(NA_WIN_ROWS)[None, :]
    valid_r = (kr[:, None, :] >= r0[:, :, None]) & (kr[:, None, :] < r0[:, :, None] + NA_KH)
    dr = kr[:, None, :] - r[:, :, None] + NA_KH - 1
    return dr, valid_r


def _na_bias_kernel(toe_ref, o_ref):
    dr, valid_r = _na_window_tables()
    masked = jnp.full((GRID_W, GRID_W), -1e30, F32)
    for cls in range(dr.shape[0]):
        @pl.when(pl.program_id(1) == cls)
        def _(cls=cls):
            for a in range(NA_QROWS):
                for jw in range(NA_WIN_ROWS):
                    blk = toe_ref[0, int(dr[cls, a, jw])] if valid_r[cls, a, jw] else masked
                    o_ref[0, 0, a * GRID_W:(a + 1) * GRID_W, jw * GRID_W:(jw + 1) * GRID_W] = blk


def na_bias_table(rpb):
    c = np.arange(GRID_W)[:, None]
    kc = np.arange(GRID_W)[None, :]
    c0 = np.clip(c - NA_KW // 2, 0, GRID_W - NA_KW)
    valid_c = (kc >= c0) & (kc < c0 + NA_KW)
    dc = kc - c + NA_KW - 1
    onehot = (np.arange(2 * NA_KW - 1)[:, None, None] == dc[None]) & valid_c[None]
    toe = jnp.einsum('hrd,dck->hrck', rpb, jnp.asarray(onehot, F32), precision=HI)
    toe = jnp.where(valid_c[None, None], toe * LOG2E, -1e30)
    n_rows = 2 * NA_KH - 1
    return pl.pallas_call(
        _na_bias_kernel,
        grid=(NA_HEADS, 3),
        in_specs=[pl.BlockSpec((1, n_rows, GRID_W, GRID_W), lambda h, s: (h, 0, 0, 0))],
        out_specs=pl.BlockSpec((1, 1, ROWBLK, NA_WIN), lambda h, s: (h, s, 0, 0)),
        out_shape=jax.ShapeDtypeStruct((NA_HEADS, 3, ROWBLK, NA_WIN), F32),
        compiler_params=_cparams(("arbitrary", "arbitrary")),
        name="na_bias",
    )(toe)


N_TOK = NB * T_LAT
MOE_TM = 512
N_BLK = N_TOK * TOP_K // MOE_TM + N_EXPERTS
N_SLOT = N_BLK * MOE_TM
GATHER_ROWS = 256
COMBINE_TOK = 128


def _row_copies(n_rows, copy_fn):
    def start_all():
        def body(r8, carry):
            for u in range(8):
                copy_fn(r8 * 8 + u).start(priority=u % 2)
            return carry
        lax.fori_loop(0, n_rows // 8, body, 0)

    def wait_all():
        def body(r, carry):
            copy_fn(r).wait()
            return carry
        lax.fori_loop(0, n_rows, body, 0, unroll=8)

    return start_all, wait_all


def _gather_kernel(idx_ref, nu_ref, h_hbm, o_ref, buf_ref, sem):
    i = pl.program_id(0)
    n_steps = nu_ref[0] * (MOE_TM // GATHER_ROWS)

    def copies(step):
        slot = step % 2
        return _row_copies(GATHER_ROWS, lambda r: pltpu.make_async_copy(
            h_hbm.at[pl.ds(idx_ref[step * GATHER_ROWS + r], 1)], buf_ref.at[slot, pl.ds(r, 1)], sem.at[slot]))

    @pl.when(i == 0)
    def _():
        copies(i)[0]()

    @pl.when(i + 1 < n_steps)
    def _():
        copies(i + 1)[0]()

    @pl.when(i < n_steps)
    def _():
        copies(i)[1]()
        o_ref[...] = buf_ref[i % 2].astype(BF16)

    @pl.when(i >= n_steps)
    def _():
        o_ref[...] = jnp.zeros_like(o_ref)


def moe_gather(h2, slot_row, n_used):
    return pl.pallas_call(
        _gather_kernel,
        grid_spec=pltpu.PrefetchScalarGridSpec(
            num_scalar_prefetch=2,
            grid=(N_SLOT // GATHER_ROWS,),
            in_specs=[pl.BlockSpec(memory_space=pl.ANY)],
            out_specs=pl.BlockSpec((GATHER_ROWS, D), lambda i, idx, nu: (i, 0)),
            scratch_shapes=[pltpu.VMEM((2, GATHER_ROWS, D), F32), pltpu.SemaphoreType.DMA((2,))]),
        out_shape=jax.ShapeDtypeStruct((N_SLOT, D), BF16),
        compiler_params=_cparams(("arbitrary",)),
        name="moe_gather",
    )(slot_row, n_used, h2)


def _combine_kernel(slot_ref, y_hbm, x_ref, gate_ref, g2_ref, o_ref, buf_ref, sem):
    i = pl.program_id(0)

    def copies(step):
        slot = step % 2
        base = step * COMBINE_TOK * TOP_K
        return _row_copies(COMBINE_TOK * TOP_K, lambda r: pltpu.make_async_copy(
            y_hbm.at[pl.ds(slot_ref[base + r], 1)], buf_ref.at[slot, pl.ds(r, 1)], sem.at[slot]))

    @pl.when(i == 0)
    def _():
        copies(i)[0]()

    @pl.when(i + 1 < N_TOK // COMBINE_TOK)
    def _():
        copies(i + 1)[0]()

    copies(i)[1]()
    b = i // (T_LAT // COMBINE_TOK)
    rows = buf_ref[i % 2]
    f = gate_ref[:, 0:1] * rows[0:COMBINE_TOK] + gate_ref[:, 1:2] * rows[COMBINE_TOK:2 * COMBINE_TOK]
    o_ref[...] = x_ref[...] + g2_ref[pl.ds(b, 1), :] * f


def moe_combine(yb, x, slots_km, gates, modtab):
    per_b = T_LAT // COMBINE_TOK
    x_map = lambda i, s: ((i // per_b) * (S_ALL // COMBINE_TOK) + T_CTX // COMBINE_TOK + i % per_b, 0)
    return pl.pallas_call(
        _combine_kernel,
        grid_spec=pltpu.PrefetchScalarGridSpec(
            num_scalar_prefetch=1,
            grid=(N_TOK // COMBINE_TOK,),
            in_specs=[pl.BlockSpec(memory_space=pl.ANY),
                      pl.BlockSpec((COMBINE_TOK, D), x_map),
                      pl.BlockSpec((COMBINE_TOK, TOP_K), lambda i, s: (i, 0)),
                      pl.BlockSpec((8, D), lambda i, s: (0, 5))],
            out_specs=pl.BlockSpec((COMBINE_TOK, D), lambda i, s: (i, 0)),
            scratch_shapes=[pltpu.VMEM((2, COMBINE_TOK * TOP_K, D), F32), pltpu.SemaphoreType.DMA((2,))]),
        out_shape=jax.ShapeDtypeStruct((N_TOK, D), F32),
        compiler_params=_cparams(("arbitrary",)),
        name="moe_combine",
    )(slots_km, yb, x, gates, modtab)


def moe_routing(logits):
    top_v, top_e = lax.top_k(logits, TOP_K)
    gates = jax.nn.softmax(top_v, axis=-1)
    flat_e = top_e.reshape(-1)
    onehot = (flat_e[:, None] == jnp.arange(N_EXPERTS)[None, :]).astype(jnp.int32)
    rank = jnp.take_along_axis(jnp.cumsum(onehot, axis=0) - onehot, flat_e[:, None], axis=1)[:, 0]
    counts = onehot.sum(axis=0)
    padded = (counts + MOE_TM - 1) // MOE_TM * MOE_TM
    pend = jnp.cumsum(padded)
    slot = ((pend - padded)[flat_e] + rank).astype(jnp.int32)
    tok = jnp.arange(N_TOK, dtype=jnp.int32)
    tok_row = (tok // T_LAT) * S_ALL + T_CTX + tok % T_LAT
    slot_row = jnp.zeros((N_SLOT,), jnp.int32).at[slot].set(jnp.repeat(tok_row, TOP_K))
    n_used = (pend[-1] // MOE_TM).astype(jnp.int32)
    blk = jnp.arange(N_BLK)
    used = blk < n_used
    blk_e = jnp.sum(jnp.minimum(blk, n_used - 1)[:, None] * MOE_TM >= pend[None, :], axis=1).astype(jnp.int32)
    fresh = (used & ((blk == 0) | (blk_e != jnp.roll(blk_e, 1)))).astype(jnp.int32)
    run_idx = jnp.cumsum(fresh) - 1
    n_runs = fresh.sum()
    has = counts > 0
    ids = jnp.arange(N_EXPERTS)
    first_e = jnp.min(jnp.where(has, ids, N_EXPERTS))
    later = jnp.where(has[None, :] & (ids[None, :] > ids[:, None]), ids[None, :], N_EXPERTS).min(axis=1)
    next_of = jnp.where(later < N_EXPERTS, later, first_e)
    sched = (blk_e, fresh, run_idx.astype(jnp.int32), next_of[blk_e].astype(jnp.int32),
             (run_idx == n_runs - 1).astype(jnp.int32),
             jnp.where(used, jnp.clip(((pend - padded) + counts)[blk_e] - blk * MOE_TM, 0, MOE_TM), 0).astype(jnp.int32),
             jnp.stack([n_used, n_runs]).astype(jnp.int32))
    slots_km = slot.reshape(N_TOK // COMBINE_TOK, COMBINE_TOK, TOP_K).transpose(0, 2, 1).reshape(-1)
    return slot_row, sched, n_used.reshape(1), slots_km, gates


def _pad_cols(w, width):
    return jnp.pad(w, [(0, 0)] * (w.ndim - 1) + [(0, width - w.shape[-1])])


W_IN_SEGMENTS = ((C_Q, 0, 2048), (C_Z, 2064, 512), (C_XBC, 2576, 1024), (C_CQ, 3616, MLA_Q_RANK),
                 (C_CKV, 4064, MLA_KV_RANK), (C_KR, 4224, MLA_ROPE), (C_NAQ, 4288, 1536),
                 (C_GATE + G_I, 2048, 16), (C_GATE + G_DT, 3600, 16))
D_IN = 5824


def _w_in_layout_kernel(w_ref, o_ref):
    o_ref[...] = jnp.zeros_like(o_ref)
    for dst, src, width in W_IN_SEGMENTS:
        o_ref[0, :, dst:dst + width] = w_ref[0, :, src:src + width]


def layout_w_in(w):
    rows = 256
    return pl.pallas_call(
        _w_in_layout_kernel,
        grid=(DEPTH, D // rows),
        in_specs=[pl.BlockSpec((1, rows, D_IN), lambda l, i: (l, i, 0))],
        out_specs=pl.BlockSpec((1, rows, D_INP), lambda l, i: (l, i, 0)),
        out_shape=jax.ShapeDtypeStruct((DEPTH, D, D_INP), F32),
        compiler_params=_cparams(("arbitrary", "arbitrary")),
        name="w_in_layout",
    )(w)


def rope_tables():
    t = np.arange(T_LAT)
    n_freq = MLA_ROPE // 4
    freqs = ROPE_THETA ** (-jnp.arange(n_freq, dtype=F32) / n_freq)
    row = jnp.asarray(t // GRID_W, F32)
    col = jnp.asarray(t % GRID_W, F32)
    ang = jnp.concatenate([row[:, None] * freqs, col[:, None] * freqs], axis=-1)
    cos, sin = jnp.cos(ang), jnp.sin(ang)
    half = MLA_ROPE // 2
    zc = jnp.zeros((T_LAT, 128 - MLA_ROPE), F32)
    zh = jnp.zeros((T_LAT, half), F32)
    cs = jnp.concatenate([cos, cos, zc], axis=1)
    s1 = jnp.concatenate([-sin, zh, zc], axis=1)
    s2 = jnp.concatenate([zh, sin, zc], axis=1)
    ident = jnp.concatenate([jnp.ones((T_CTX, MLA_ROPE), F32), jnp.zeros((T_CTX, 128 - MLA_ROPE), F32)], axis=1)
    zeros = jnp.zeros((T_CTX, 128), F32)
    return (jnp.concatenate([ident, cs], axis=0), jnp.concatenate([zeros, s1], axis=0),
            jnp.concatenate([zeros, s2], axis=0))


def _gate_vectors(i_bias, f_bias, dt_bias, a_log):
    used = jnp.concatenate([i_bias.reshape(-1), f_bias.reshape(-1), dt_bias.reshape(-1)])
    alog = jnp.concatenate([jnp.zeros((G_DT,), F32), a_log.reshape(-1)])
    padc = lambda u: jnp.pad(u, (0, 128 - G_USED)).reshape(1, 128)
    return padc(used), used.reshape(G_USED, 1), padc(alog), alog.reshape(G_USED, 1)


def kernel(x, c, ctx, c_ctx, mod_w, mod_b, norm1, w_in, w_out, ml_i_bias, ml_f_bias, ml_norm, ssd_conv_w, ssd_conv_b, ssd_dt_bias, ssd_A_log, ssd_D, ssd_norm, mla_q_norm, mla_w_qb, mla_kv_norm, mla_w_kvb, mla_gq, mla_gk, na_gq, na_gk, na_rpb, norm2, ffn_w1, ffn_w3, ffn_w2, moe_router, moe_w1, moe_w3, moe_w2):
    xs = jnp.concatenate([ctx, x], axis=1).reshape(N_ROWS, D)
    c_all = jnp.concatenate([c, c_ctx[None, :], jnp.zeros((8 - NB - 1, D), F32)], axis=0)
    mod_all = modulation(c_all, mod_w, mod_b)
    tabs = rope_tables()
    w_in_p = layout_w_in(w_in)
    out = None
    for l in range(DEPTH):
        modtab = mod_all[l]
        h = rms_modulate(xs, norm1[l], modtab, 0)
        p = matmul(h, [w_in_p], tm=1536, tn=1024, layer=l, name="w_in")
        gates_t = p[:, C_GATE:C_GATE + G_USED].reshape(N_ROWS // CHUNK, CHUNK, G_USED).transpose(0, 2, 1)
        biasc, biasr, alogc, alogr = _gate_vectors(ml_i_bias[l], ml_f_bias[l], ssd_dt_bias[l], ssd_A_log[l])

        hf, hb = mlstm_scan(p, gates_t, biasc, biasr)
        ml = mlstm_finish(hf, hb, p, ml_norm[l])

        xbc = ssd_conv(p, ssd_conv_w[l], ssd_conv_b[l])
        yf, yb = ssd_scan(xbc, p, gates_t, biasc, biasr, alogc, alogr)
        ss = ssd_finish(yf, yb, xbc, p, ssd_D[l], ssd_norm[l])

        wq = jnp.pad(mla_w_qb[l].reshape(MLA_Q_RANK, MLA_HEADS, MLA_QK),
                     ((0, 512 - MLA_Q_RANK), (0, 0), (0, MLA_HP - MLA_QK))).reshape(512, MLA_HEADS * MLA_HP)
        wkv = jnp.pad(mla_w_kvb[l], ((0, 256 - MLA_KV_RANK), (0, 0)))
        pad1 = lambda u, w: jnp.pad(u, (0, w - u.shape[0])).reshape(1, w)
        q, k, v = mla_prep(p, wq, wkv, pad1(mla_q_norm[l], 512), pad1(mla_kv_norm[l], 256),
                           pad1(mla_gq[l], MLA_HP), pad1(mla_gk[l], MLA_HP), tabs)
        la = mla_attention(q, k, v)

        na = na_attention(p, na_gq[l], na_gk[l], na_bias_table(na_rpb[l]))

        xs = matmul([ml, ss, la, na], [w_out], mode="resid", res=xs, modtab=modtab, gate_col=2,
                    tm=1536, layer=l, name="w_out")

        if l % 2 == 0:
            h2 = rms_modulate(xs, norm2[l], modtab, 3)
            hid = matmul(h2, [ffn_w1[l // 2], ffn_w3[l // 2]], mode="swiglu", out_dtype=BF16, tm=1536,
                         name="ffn_up")
            xs = matmul(hid, [ffn_w2[l // 2]], mode="resid", res=xs, modtab=modtab, gate_col=5,
                        w_buffers=1, name="ffn_down")
        else:
            h2, logits = rms_modulate(xs, norm2[l], modtab, 3, router=_pad_cols(moe_router[l // 2], 128))
            lat = logits.reshape(NB, S_ALL, 128)[:, T_CTX:, :N_EXPERTS].reshape(N_TOK, N_EXPERTS)
            slot_row, sched, n_used, slots_km, gates = moe_routing(lat)
            xb = moe_gather(h2, slot_row, n_used)
            hid = grouped_matmul(xb, [moe_w1[l // 2], moe_w3[l // 2]], sched, mode="swiglu", out_dtype=BF16,
                                 tm=MOE_TM, tn=1024, name="moe_up")
            yb_ = grouped_matmul(hid, [moe_w2[l // 2]], sched, tm=MOE_TM, tn=512, name="moe_down")
            out = moe_combine(yb_, xs, slots_km, gates, modtab)
    return out.reshape(NB, T_LAT, D)
```

```python
from functools import partial

import numpy as np
import jax
import jax.numpy as jnp
from jax import lax
from jax.experimental import pallas as pl
from jax.experimental.pallas import tpu as pltpu

F32 = jnp.float32
BF16 = jnp.bfloat16
HI = lax.Precision.HIGHEST

D = 2048
NB = 4
T_LAT = 2048
T_CTX = 256
S_ALL = T_LAT + T_CTX
N_ROWS = NB * S_ALL
DEPTH = 2
GRID_W = 64
EPS = 1e-6
LOG2E = 1.4426950408889634

ROWBLK = 256
BLK_PER_B = S_ALL // ROWBLK
CHUNK = 128
CH_PER_B = S_ALL // CHUNK
CH_CTX = T_CTX // CHUNK

ML_HEADS, ML_HD = 4, 128
SSD_HEADS, SSD_HD, SSD_GROUPS, SSD_STATE, SSD_CONV = 8, 64, 2, 128, 5
SSD_W = SSD_HEADS * SSD_HD
MLA_HEADS, MLA_NOPE, MLA_ROPE, MLA_V = 4, 128, 64, 128
MLA_QK = MLA_NOPE + MLA_ROPE
MLA_Q_RANK, MLA_KV_RANK = 448, 160
MLA_HP = 256
ROPE_THETA = 10000.0
NA_HEADS, NA_HD, NA_KH, NA_KW = 4, 128, 8, 16
N_EXPERTS, TOP_K = 8, 2

C_Q, C_K, C_V, C_O = 0, 512, 1024, 1536
C_Z = 2048
C_XBC = 2560
C_CQ, C_CKV, C_KR = 3584, 4096, 4352
C_NAQ, C_NAK, C_NAV = 4480, 4992, 5504
C_GATE = 6016
D_INP = 6144
G_I, G_F, G_DT, G_USED = 0, 8, 16, 32

VMEM_LIMIT = 56 * 1024 * 1024


def _cparams(sem):
    return pltpu.CompilerParams(dimension_semantics=sem, vmem_limit_bytes=VMEM_LIMIT)


def _mod_index(blk256):
    return jnp.where(blk256 % BLK_PER_B == 0, NB, blk256 // BLK_PER_B)


def _dot_nt(a, b):
    return lax.dot_general(a, b, (((1,), (1,)), ((), ())), preferred_element_type=F32)


def _dot_tn(a, b):
    return lax.dot_general(a, b, (((0,), (0,)), ((), ())), preferred_element_type=F32)


def _silu(x):
    return x * jax.nn.sigmoid(x)


def _softplus(x):
    return jnp.maximum(x, 0.0) + jnp.log(1.0 + jnp.exp(-jnp.abs(x)))


def _mod_kernel(c_ref, w_ref, b_ref, o_ref):
    o_ref[0] = jnp.dot(_silu(c_ref[...]), w_ref[0], preferred_element_type=F32) + b_ref[0]


def modulation(c_all, mod_w, mod_b):
    tn = 1024
    n_out = mod_w.shape[-1]
    return pl.pallas_call(
        _mod_kernel,
        grid=(DEPTH, n_out // tn),
        in_specs=[pl.BlockSpec((8, D), lambda l, j: (0, 0)),
                  pl.BlockSpec((1, D, tn), lambda l, j: (l, 0, j)),
                  pl.BlockSpec((1, 1, tn), lambda l, j: (l, 0, j))],
        out_specs=pl.BlockSpec((1, 8, tn), lambda l, j: (l, 0, j)),
        out_shape=jax.ShapeDtypeStruct((DEPTH, 8, n_out), F32),
        compiler_params=_cparams(("arbitrary", "arbitrary")),
        name="modulation",
    )(c_all, mod_w, mod_b.reshape(DEPTH, 1, n_out))


ELT_ROWS = 768


def _rmsmod_kernel(x_ref, g_ref, sh_ref, sc_ref, *rest, with_router):
    for s in range(ELT_ROWS // ROWBLK):
        midx = _mod_index(pl.program_id(0) * (ELT_ROWS // ROWBLK) + s)
        rows = slice(s * ROWBLK, (s + 1) * ROWBLK)
        x = x_ref[rows, :]
        y = x * lax.rsqrt(jnp.mean(x * x, axis=-1, keepdims=True) + EPS) * g_ref[...]
        h = y * (1.0 + sc_ref[pl.ds(midx, 1), :]) + sh_ref[pl.ds(midx, 1), :]
        if with_router:
            r_ref, o_ref, lg_ref = rest
            o_ref[rows, :] = h
            lg_ref[rows, :] = jnp.dot(h, r_ref[...], precision=HI, preferred_element_type=F32)
        else:
            (o_ref,) = rest
            o_ref[rows, :] = h.astype(BF16)


def rms_modulate(x, g, modtab, which_shift, router=None):
    with_router = router is not None
    in_specs = [pl.BlockSpec((ELT_ROWS, D), lambda i: (i, 0)),
                pl.BlockSpec((1, D), lambda i: (0, 0)),
                pl.BlockSpec((8, D), lambda i: (0, which_shift)),
                pl.BlockSpec((8, D), lambda i: (0, which_shift + 1))]
    args = [x, g.reshape(1, D), modtab, modtab]
    if with_router:
        in_specs.append(pl.BlockSpec((D, 128), lambda i: (0, 0)))
        args.append(router)
        out_shape = [jax.ShapeDtypeStruct((N_ROWS, D), F32), jax.ShapeDtypeStruct((N_ROWS, 128), F32)]
        out_specs = [pl.BlockSpec((ELT_ROWS, D), lambda i: (i, 0)), pl.BlockSpec((ELT_ROWS, 128), lambda i: (i, 0))]
    else:
        out_shape = jax.ShapeDtypeStruct((N_ROWS, D), BF16)
        out_specs = pl.BlockSpec((ELT_ROWS, D), lambda i: (i, 0))
    return pl.pallas_call(
        partial(_rmsmod_kernel, with_router=with_router),
        grid=(N_ROWS // ELT_ROWS,),
        in_specs=in_specs, out_specs=out_specs, out_shape=out_shape,
        compiler_params=_cparams(("arbitrary",)),
        name="rms_modulate_router" if with_router else "rms_modulate",
    )(*args)


MM_COLS = 512


def _mm_compute(mode, a, w_tiles, o_ref, i=None, tm=None, res_ref=None, gate_ref=None, rows=slice(None),
                cast_from=None):
    tn = o_ref.shape[1]
    for c0 in range(0, tn, MM_COLS):
        cols = slice(c0, min(c0 + MM_COLS, tn))
        if cast_from is not None:
            for w, src in zip(w_tiles, cast_from):
                w[:, cols] = src[:, cols].astype(BF16)
        accs = [jnp.dot(a, w[:, cols], preferred_element_type=F32) for w in w_tiles]
        if mode == "plain":
            o_ref[rows, cols] = accs[0].astype(o_ref.dtype)
        elif mode == "swiglu":
            o_ref[rows, cols] = (_silu(accs[0]) * accs[1]).astype(o_ref.dtype)
        else:
            for s in range(tm // ROWBLK):
                midx = _mod_index(i * (tm // ROWBLK) + s)
                rows = slice(s * ROWBLK, (s + 1) * ROWBLK)
                o_ref[rows, cols] = res_ref[rows, cols] + gate_ref[pl.ds(midx, 1), cols] * accs[0][rows, :]


def _mm_kernel(*refs, mode, tm, na, stacked):
    nw = 2 if mode == "swiglu" else 1
    a_refs = refs[:na]
    w_refs = refs[na:na + nw]
    pos = na + nw
    res_ref = gate_ref = None
    if mode == "resid":
        res_ref, gate_ref = refs[pos:pos + 2]
        pos += 2
    o_ref = refs[pos]
    ws_refs = refs[pos + 1:]
    i = pl.program_id(1)

    @pl.when(i == 0)
    def _():
        for w_ref, ws_ref in zip(w_refs, ws_refs):
            ws_ref[...] = (w_ref[0] if stacked else w_ref[...]).astype(BF16)

    a = a_refs[0][...] if na == 1 else jnp.concatenate([a_ref[...] for a_ref in a_refs], axis=1)
    _mm_compute(mode, a, ws_refs, o_ref, i, tm, res_ref, gate_ref)


def matmul(a, ws, mode="plain", out_dtype=F32, res=None, modtab=None, gate_col=None,
           tm=768, tn=512, w_buffers=2, layer=None, name="matmul"):
    pieces = list(a) if isinstance(a, (list, tuple)) else [a]
    m = pieces[0].shape[0]
    kdim = sum(p.shape[1] for p in pieces)
    n = ws[0].shape[-1]
    wmode = {} if w_buffers == 2 else dict(pipeline_mode=pl.Buffered(w_buffers))
    in_specs = [pl.BlockSpec((tm, p.shape[1]), lambda j, i: (i, 0)) for p in pieces]
    if layer is None:
        in_specs += [pl.BlockSpec((kdim, tn), lambda j, i: (0, j), **wmode) for _ in ws]
    else:
        in_specs += [pl.BlockSpec((1, kdim, tn), lambda j, i: (layer, 0, j), **wmode) for _ in ws]
    args = [*pieces, *ws]
    if mode == "resid":
        in_specs += [pl.BlockSpec((tm, tn), lambda j, i: (i, j)),
                     pl.BlockSpec((8, tn), lambda j, i: (0, gate_col * (D // tn) + j))]
        args += [res, modtab]
    return pl.pallas_call(
        partial(_mm_kernel, mode=mode, tm=tm, na=len(pieces), stacked=layer is not None),
        grid=(n // tn, m // tm),
        in_specs=in_specs,
        out_specs=pl.BlockSpec((tm, tn), lambda j, i: (i, j)),
        out_shape=jax.ShapeDtypeStruct((m, n), out_dtype),
        scratch_shapes=[pltpu.VMEM((kdim, tn), BF16) for _ in ws],
        compiler_params=_cparams(("arbitrary", "arbitrary")),
        name=name,
    )(*args)


def _gmm_kernel(be_ref, fresh_ref, run_ref, nexte_ref, lastrun_ref, fill_ref, meta_ref, a_ref, *rest,
                mode, nw, tn):
    w_hbm = rest[:nw]
    o_ref = rest[nw]
    stage_ref, ws_ref, sem = rest[nw + 1:]
    j = pl.program_id(0)
    i = pl.program_id(1)
    n_runs = meta_ref[1]
    half = a_ref.shape[0] // 2

    def w_copy(widx, e, jj):
        col = pl.multiple_of(jj * tn, tn)
        return pltpu.make_async_copy(w_hbm[widx].at[e, :, pl.ds(col, tn)], stage_ref.at[widx], sem.at[widx])

    def await_weights():
        @pl.when(j * n_runs + run_ref[i] == 0)
        def _():
            for widx in range(nw):
                w_copy(widx, be_ref[i], j).start()

        for widx in range(nw):
            w_copy(widx, be_ref[i], j).wait()

    def prefetch_next():
        is_last = lastrun_ref[i] == 1

        @pl.when(jnp.logical_not(jnp.logical_and(is_last, j == pl.num_programs(0) - 1)))
        def _():
            jn = jnp.where(is_last, j + 1, j)
            for widx in range(nw):
                w_copy(widx, nexte_ref[i], jn).start()

    w_tiles = [ws_ref.at[widx] for widx in range(nw)]
    stage_tiles = [stage_ref.at[widx] for widx in range(nw)]
    fresh = fresh_ref[i] == 1
    fill = fill_ref[i]
    full = fill > half

    @pl.when(jnp.logical_and(fresh, full))
    def _():
        await_weights()
        _mm_compute(mode, a_ref[...], w_tiles, o_ref, cast_from=stage_tiles)
        prefetch_next()

    @pl.when(jnp.logical_and(fresh, jnp.logical_not(full)))
    def _():
        await_weights()
        for w, src in zip(w_tiles, stage_tiles):
            w[...] = src[...].astype(BF16)
        prefetch_next()

    @pl.when(jnp.logical_and(jnp.logical_not(fresh), full))
    def _():
        _mm_compute(mode, a_ref[...], w_tiles, o_ref)

    @pl.when(jnp.logical_and(fill > 0, fill <= half))
    def _():
        _mm_compute(mode, a_ref[0:half, :], w_tiles, o_ref, rows=slice(0, half))
        o_ref[half:, :] = jnp.zeros((half, o_ref.shape[1]), o_ref.dtype)

    @pl.when(fill == 0)
    def _():
        o_ref[...] = jnp.zeros_like(o_ref)


def grouped_matmul(a, ws, sched, mode="plain", out_dtype=F32, tm=512, tn=512, name="grouped_matmul"):
    m, kdim = a.shape
    n = ws[0].shape[-1]
    nw = len(ws)
    row = lambda j, i, be, fr, ru, ne, lr, fill, meta: jnp.minimum(i, meta[0] - 1)
    grid_spec = pltpu.PrefetchScalarGridSpec(
        num_scalar_prefetch=len(sched),
        grid=(n // tn, m // tm),
        in_specs=[pl.BlockSpec((tm, kdim), lambda *s: (row(*s), 0))] + [pl.BlockSpec(memory_space=pl.ANY)] * nw,
        out_specs=pl.BlockSpec((tm, tn), lambda j, i, *_: (i, j)),
        scratch_shapes=[pltpu.VMEM((nw, kdim, tn), F32), pltpu.VMEM((nw, kdim, tn), BF16),
                        pltpu.SemaphoreType.DMA((nw,))])
    return pl.pallas_call(
        partial(_gmm_kernel, mode=mode, nw=nw, tn=tn),
        grid_spec=grid_spec,
        out_shape=jax.ShapeDtypeStruct((m, n), out_dtype),
        compiler_params=_cparams(("arbitrary", "arbitrary")),
        name=name,
    )(*sched, a, *ws)


def _fwd_chunk(i):
    return i


def _bwd_chunk(i):
    return jnp.where(i < CH_CTX, CH_CTX - 1 - i, CH_PER_B + CH_CTX - 1 - i)


def _tri_masks():
    r = lax.broadcasted_iota(jnp.int32, (CHUNK, CHUNK), 0)
    c = lax.broadcasted_iota(jnp.int32, (CHUNK, CHUNK), 1)
    return c <= r, c >= r


def _mlstm_kernel(qf_ref, kf_ref, vf_ref, gcf_ref, grf_ref,
                  qb_ref, kb_ref, vb_ref, gcb_ref, grb_ref,
                  biasc_ref, biasr_ref, hf_ref, hb_ref, c_ref, n_ref, m_ref):
    @pl.when(pl.program_id(1) == 0)
    def _():
        c_ref[...] = jnp.zeros_like(c_ref)
        n_ref[...] = jnp.zeros_like(n_ref)
        m_ref[...] = jnp.zeros_like(m_ref)

    low, upp = _tri_masks()
    lowf, uppf = low.astype(F32), upp.astype(F32)
    streams = ((0, qf_ref, kf_ref, vf_ref, gcf_ref, grf_ref, hf_ref, upp, lowf, uppf, CHUNK - 1),
               (1, qb_ref, kb_ref, vb_ref, gcb_ref, grb_ref, hb_ref, low, uppf, lowf, 0))
    for d, q_ref, k_ref, v_ref, gc_ref, gr_ref, h_ref, mask_t, tri_c, tri_r, last in streams:
        ac = gc_ref[...] + biasc_ref[...]
        ar = gr_ref[0] + biasr_ref[...]
        lfc = -_softplus(-ac)
        lfr = -_softplus(-ar)
        bc_all = jnp.dot(tri_c, lfc, precision=HI, preferred_element_type=F32)
        br_all = jnp.dot(lfr, tri_r, precision=HI, preferred_element_type=F32)
        for h in range(ML_HEADS):
            s_idx = d * ML_HEADS + h
            ji, jf = G_I + s_idx, G_F + s_idx
            cols = slice(h * ML_HD, (h + 1) * ML_HD)
            q = q_ref[:, cols] * (ML_HD ** -0.5)
            k = k_ref[:, cols]
            vt = v_ref[:, cols].T
            b_row = br_all[jf:jf + 1, :]
            i_row = ar[ji:ji + 1, :]
            key_col = ac[:, ji:ji + 1] - bc_all[:, jf:jf + 1]
            cmat = c_ref[s_idx]
            nvec = n_ref[s_idx]
            m_prev = m_ref[s_idx][:, 0:1]
            dmat = jnp.where(mask_t, b_row + key_col, -1e30)
            inter = b_row + m_prev
            mt = jnp.maximum(inter, jnp.max(dmat, axis=0, keepdims=True))
            w_intra = jnp.exp(dmat - mt)
            w_state = jnp.exp(inter - mt)
            st = _dot_nt(k, q) * w_intra
            num = jnp.dot(vt, st, preferred_element_type=F32) + w_state * _dot_nt(cmat, q)
            den = jnp.sum(st, axis=0, keepdims=True) + w_state * _dot_nt(nvec, q)
            h_ref[:, cols] = (num / jnp.maximum(jnp.abs(den), jnp.exp(-mt))).T
            bl = b_row[:, last:last + 1]
            g = bl - b_row + i_row
            m_new = jnp.maximum(bl + m_prev, jnp.max(g, axis=1, keepdims=True))
            wg = jnp.exp(g - m_new)
            wc = jnp.exp(bl + m_prev - m_new)
            c_ref[s_idx] = wc * cmat + jnp.dot(vt * wg, k, preferred_element_type=F32)
            n_ref[s_idx] = wc * nvec + jnp.dot(wg, k, preferred_element_type=F32)
            m_ref[s_idx] = jnp.broadcast_to(m_new, (1, 128))


def mlstm_scan(p, gates_t, biasc, biasr):
    def rows(chunk_fn):
        return lambda b, i: b * CH_PER_B + chunk_fn(i)

    def stream_specs(chunk_fn):
        r = rows(chunk_fn)
        return [pl.BlockSpec((CHUNK, 512), lambda b, i: (r(b, i), C_Q // 512)),
                pl.BlockSpec((CHUNK, 512), lambda b, i: (r(b, i), C_K // 512)),
                pl.BlockSpec((CHUNK, 512), lambda b, i: (r(b, i), C_V // 512)),
                pl.BlockSpec((CHUNK, 128), lambda b, i: (r(b, i), C_GATE // 128)),
                pl.BlockSpec((1, G_USED, CHUNK), lambda b, i: (r(b, i), 0, 0))]

    rf, rb = rows(_fwd_chunk), rows(_bwd_chunk)
    n_streams = 2 * ML_HEADS
    return pl.pallas_call(
        _mlstm_kernel,
        grid=(NB, CH_PER_B),
        in_specs=stream_specs(_fwd_chunk) + stream_specs(_bwd_chunk) + [
            pl.BlockSpec((1, 128), lambda b, i: (0, 0)),
            pl.BlockSpec((G_USED, 1), lambda b, i: (0, 0))],
        out_specs=[pl.BlockSpec((CHUNK, 512), lambda b, i: (rf(b, i), 0)),
                   pl.BlockSpec((CHUNK, 512), lambda b, i: (rb(b, i), 0))],
        out_shape=[jax.ShapeDtypeStruct((N_ROWS, 512), F32)] * 2,
        scratch_shapes=[pltpu.VMEM((n_streams, ML_HD, ML_HD), F32),
                        pltpu.VMEM((n_streams, 1, ML_HD), F32),
                        pltpu.VMEM((n_streams, 1, 128), F32)],
        compiler_params=_cparams(("arbitrary", "arbitrary")),
        name="mlstm_scan",
    )(p, p, p, p, gates_t, p, p, p, p, gates_t, biasc, biasr)


def _mlstm_finish_kernel(hf_ref, hb_ref, o_ref, g_ref, out_ref):
    for h in range(ML_HEADS):
        cols = slice(h * ML_HD, (h + 1) * ML_HD)
        x = hf_ref[:, cols] + hb_ref[:, cols]
        y = x * lax.rsqrt(jnp.mean(x * x, axis=-1, keepdims=True) + EPS) * g_ref[:, cols]
        out_ref[:, cols] = (y * jax.nn.sigmoid(o_ref[:, cols])).astype(BF16)


def mlstm_finish(hf, hb, p, norm_g):
    blk = lambda c: pl.BlockSpec((ELT_ROWS, 512), lambda i: (i, c))
    return pl.pallas_call(
        _mlstm_finish_kernel,
        grid=(N_ROWS // ELT_ROWS,),
        in_specs=[blk(0), blk(0), blk(C_O // 512), pl.BlockSpec((1, 512), lambda i: (0, 0))],
        out_specs=blk(0),
        out_shape=jax.ShapeDtypeStruct((N_ROWS, 512), BF16),
        compiler_params=_cparams(("arbitrary",)),
        name="mlstm_finish",
    )(hf, hb, p, norm_g.reshape(1, 512))


_PAD = 8


def _ssd_conv_kernel(u_ref, w_ref, b_ref, o_ref, pad_ref):
    zeros = jnp.zeros((_PAD, u_ref.shape[1]), F32)
    segs = ((0, T_CTX, _PAD), (T_CTX, T_LAT, 2 * _PAD + T_CTX))
    pad_ref[0:_PAD] = zeros
    pad_ref[_PAD + T_CTX:2 * _PAD + T_CTX] = zeros
    pad_ref[2 * _PAD + S_ALL:3 * _PAD + S_ALL] = zeros
    for src, n, dst in segs:
        pad_ref[dst:dst + n] = u_ref[src:src + n]
    for src, n, dst in segs:
        acc = jnp.broadcast_to(b_ref[...], (n, u_ref.shape[1]))
        for j in range(SSD_CONV):
            lo = dst + j - SSD_CONV // 2
            acc = acc + w_ref[j:j + 1, :] * pad_ref[lo:lo + n]
        o_ref[src:src + n] = _silu(acc)


def ssd_conv(p, conv_w, conv_b):
    tc = 256
    return pl.pallas_call(
        _ssd_conv_kernel,
        grid=(NB, 1024 // tc),
        in_specs=[pl.BlockSpec((S_ALL, tc), lambda b, c: (b, C_XBC // tc + c)),
                  pl.BlockSpec((SSD_CONV, tc), lambda b, c: (0, c)),
                  pl.BlockSpec((1, tc), lambda b, c: (0, c))],
        out_specs=pl.BlockSpec((S_ALL, tc), lambda b, c: (b, c)),
        out_shape=jax.ShapeDtypeStruct((N_ROWS, 1024), F32),
        scratch_shapes=[pltpu.VMEM((S_ALL + 3 * _PAD, tc), F32)],
        compiler_params=_cparams(("arbitrary", "arbitrary")),
        name="ssd_conv",
    )(p, conv_w, conv_b.reshape(1, 1024))


def _ssd_kernel(xf_ref, bf_ref, cf_ref, gcf_ref, grf_ref,
                xb_ref, bb_ref, cb_ref, gcb_ref, grb_ref,
                biasc_ref, biasr_ref, alogc_ref, alogr_ref, yf_ref, yb_ref, st_ref):
    @pl.when(pl.program_id(1) == 0)
    def _():
        st_ref[...] = jnp.zeros_like(st_ref)

    low, upp = _tri_masks()
    lowf, uppf = low.astype(F32), upp.astype(F32)
    a_c = -jnp.exp(alogc_ref[...])
    a_r = -jnp.exp(alogr_ref[...])
    streams = ((0, xf_ref, bf_ref, cf_ref, gcf_ref, grf_ref, yf_ref, upp, lowf, uppf, CHUNK - 1),
               (1, xb_ref, bb_ref, cb_ref, gcb_ref, grb_ref, yb_ref, low, uppf, lowf, 0))
    for d, x_ref, b_ref, c_ref, gc_ref, gr_ref, y_ref, mask_t, tri_c, tri_r, last in streams:
        dtc_all = _softplus(gc_ref[...] + biasc_ref[...])
        dtr_all = _softplus(gr_ref[0] + biasr_ref[...])
        negc_all = -jnp.dot(tri_c, dtc_all * a_c, precision=HI, preferred_element_type=F32)
        cumr_all = jnp.dot(dtr_all * a_r, tri_r, precision=HI, preferred_element_type=F32)
        xt = x_ref[...].T
        yts = []
        for g in range(SSD_GROUPS):
            gcols = slice(g * SSD_STATE, (g + 1) * SSD_STATE)
            bm = b_ref[:, gcols]
            cm = c_ref[:, gcols]
            bc = _dot_nt(bm, cm)
            for hh in range(SSD_HEADS // SSD_GROUPS):
                h = g * (SSD_HEADS // SSD_GROUPS) + hh
                j = G_DT + d * SSD_HEADS + h
                s_idx = d * SSD_HEADS + h
                cum = cumr_all[j:j + 1, :]
                xd = xt[h * SSD_HD:(h + 1) * SSD_HD, :] * dtr_all[j:j + 1, :]
                st = st_ref[s_idx]
                w = bc * jnp.exp(jnp.where(mask_t, cum + negc_all[:, j:j + 1], -1e30))
                yts.append(jnp.dot(xd, w, preferred_element_type=F32) + _dot_nt(st, cm) * jnp.exp(cum))
                cl = cum[:, last:last + 1]
                st_ref[s_idx] = st * jnp.exp(cl) + jnp.dot(xd * jnp.exp(cl - cum), bm, preferred_element_type=F32)
        y_ref[...] = jnp.concatenate(yts, axis=0).T


def ssd_scan(xbc, p, gates_t, biasc, biasr, alogc, alogr):
    def rows(chunk_fn):
        return lambda b, i: b * CH_PER_B + chunk_fn(i)

    def stream_specs(chunk_fn):
        r = rows(chunk_fn)
        return [pl.BlockSpec((CHUNK, 512), lambda b, i: (r(b, i), 0)),
                pl.BlockSpec((CHUNK, 256), lambda b, i: (r(b, i), 2)),
                pl.BlockSpec((CHUNK, 256), lambda b, i: (r(b, i), 3)),
                pl.BlockSpec((CHUNK, 128), lambda b, i: (r(b, i), C_GATE // 128)),
                pl.BlockSpec((1, G_USED, CHUNK), lambda b, i: (r(b, i), 0, 0))]

    rf, rb = rows(_fwd_chunk), rows(_bwd_chunk)
    const = lambda shape: pl.BlockSpec(shape, lambda b, i: (0, 0))
    return pl.pallas_call(
        _ssd_kernel,
        grid=(NB, CH_PER_B),
        in_specs=stream_specs(_fwd_chunk) + stream_specs(_bwd_chunk) + [
            const((1, 128)), const((G_USED, 1)), const((1, 128)), const((G_USED, 1))],
        out_specs=[pl.BlockSpec((CHUNK, 512), lambda b, i: (rf(b, i), 0)),
                   pl.BlockSpec((CHUNK, 512), lambda b, i: (rb(b, i), 0))],
        out_shape=[jax.ShapeDtypeStruct((N_ROWS, 512), F32)] * 2,
        scratch_shapes=[pltpu.VMEM((2 * SSD_HEADS, SSD_HD, SSD_STATE), F32)],
        compiler_params=_cparams(("arbitrary", "arbitrary")),
        name="ssd_scan",
    )(xbc, xbc, xbc, p, gates_t, xbc, xbc, xbc, p, gates_t, biasc, biasr, alogc, alogr)


def _ssd_finish_kernel(yf_ref, yb_ref, xs_ref, z_ref, dsk_ref, g_ref, out_ref):
    y = yf_ref[...] + yb_ref[...] + dsk_ref[...] * xs_ref[...]
    y = y * _silu(z_ref[...])
    out_ref[...] = (y * lax.rsqrt(jnp.mean(y * y, axis=-1, keepdims=True) + EPS) * g_ref[...]).astype(BF16)


def ssd_finish(yf, yb, xbc, p, d_skip, norm_g):
    blk = lambda c: pl.BlockSpec((ELT_ROWS, 512), lambda i: (i, c))
    vec = pl.BlockSpec((1, 512), lambda i: (0, 0))
    return pl.pallas_call(
        _ssd_finish_kernel,
        grid=(N_ROWS // ELT_ROWS,),
        in_specs=[blk(0), blk(0), blk(0), blk(C_Z // 512), vec, vec],
        out_specs=blk(0),
        out_shape=jax.ShapeDtypeStruct((N_ROWS, 512), BF16),
        compiler_params=_cparams(("arbitrary",)),
        name="ssd_finish",
    )(yf, yb, xbc, p, jnp.repeat(d_skip, SSD_HD).reshape(1, 512), norm_g.reshape(1, 512))


def _rope_tile(r, cs_ref, s1_ref, s2_ref):
    return (r * cs_ref[...] + pltpu.roll(r, 128 - MLA_ROPE // 2, 1) * s1_ref[...]
            + pltpu.roll(r, MLA_ROPE // 2, 1) * s2_ref[...])


def _mla_prep_kernel(cq_ref, ckv_ref, kr_ref, wq_ref, wkv_ref, qn_ref, kvn_ref, gq_ref, gk_ref,
                     cs_ref, s1_ref, s2_ref, q_out, k_out, v_out):
    cq = cq_ref[...]
    cqn = cq * lax.rsqrt(jnp.sum(cq * cq, axis=-1, keepdims=True) / MLA_Q_RANK + EPS) * qn_ref[...]
    q_raw = jnp.dot(cqn, wq_ref[...], preferred_element_type=F32)
    ckv = ckv_ref[...]
    ckvn = ckv * lax.rsqrt(jnp.sum(ckv * ckv, axis=-1, keepdims=True) / MLA_KV_RANK + EPS) * kvn_ref[...]
    kv_raw = jnp.dot(ckvn, wkv_ref[...], preferred_element_type=F32)
    kr = kr_ref[...]
    kr_ss = jnp.sum(kr * kr, axis=-1, keepdims=True)
    for h in range(MLA_HEADS):
        base = h * MLA_HP
        qa = q_raw[:, base:base + 128]
        qb = q_raw[:, base + 128:base + 256]
        q_scale = lax.rsqrt((jnp.sum(qa * qa, axis=-1, keepdims=True)
                             + jnp.sum(qb * qb, axis=-1, keepdims=True)) / MLA_QK + EPS) * (MLA_QK ** -0.5 * LOG2E)
        q_out[:, base:base + 128] = (qa * q_scale * gq_ref[:, 0:128]).astype(BF16)
        q_out[:, base + 128:base + 256] = _rope_tile(qb * q_scale * gq_ref[:, 128:256],
                                                     cs_ref, s1_ref, s2_ref).astype(BF16)
        kn = kv_raw[:, base:base + 128]
        k_scale = lax.rsqrt((jnp.sum(kn * kn, axis=-1, keepdims=True) + kr_ss) / MLA_QK + EPS)
        k_out[:, base:base + 128] = (kn * k_scale * gk_ref[:, 0:128]).astype(BF16)
        k_out[:, base + 128:base + 256] = _rope_tile(kr * k_scale * gk_ref[:, 128:256],
                                                     cs_ref, s1_ref, s2_ref).astype(BF16)
        v_out[:, h * MLA_V:(h + 1) * MLA_V] = kv_raw[:, base + 128:base + 256].astype(BF16)


MLA_PREP_ROWS = 768


def mla_prep(p, wq, wkv, qn, kvn, gq, gk, rope_tabs):
    const = lambda shape: pl.BlockSpec(shape, lambda i: (0, 0))
    tab = pl.BlockSpec((MLA_PREP_ROWS, 128), lambda i: (i % (S_ALL // MLA_PREP_ROWS), 0))
    row = lambda w: pl.BlockSpec((MLA_PREP_ROWS, w), lambda i: (i, 0))
    return pl.pallas_call(
        _mla_prep_kernel,
        grid=(N_ROWS // MLA_PREP_ROWS,),
        in_specs=[pl.BlockSpec((MLA_PREP_ROWS, 512), lambda i: (i, C_CQ // 512)),
                  pl.BlockSpec((MLA_PREP_ROWS, 256), lambda i: (i, C_CKV // 256)),
                  pl.BlockSpec((MLA_PREP_ROWS, 128), lambda i: (i, C_KR // 128)),
                  const((512, MLA_HEADS * MLA_HP)), const((256, MLA_HEADS * MLA_HP)),
                  const((1, 512)), const((1, 256)), const((1, MLA_HP)), const((1, MLA_HP)),
                  tab, tab, tab],
        out_specs=[row(MLA_HEADS * MLA_HP), row(MLA_HEADS * MLA_HP), row(MLA_HEADS * MLA_V)],
        out_shape=[jax.ShapeDtypeStruct((N_ROWS, MLA_HEADS * MLA_HP), BF16),
                   jax.ShapeDtypeStruct((N_ROWS, MLA_HEADS * MLA_HP), BF16),
                   jax.ShapeDtypeStruct((N_ROWS, MLA_HEADS * MLA_V), BF16)],
        compiler_params=_cparams(("arbitrary",)),
        name="mla_prep",
    )(p, p, p, wq, wkv, qn, kvn, gq, gk, *rope_tabs)


def _softmax_pv(s_list, v_list):
    m = s_list[0].max(axis=-1, keepdims=True)
    for s in s_list[1:]:
        m = jnp.maximum(m, s.max(axis=-1, keepdims=True))
    ps = [jnp.exp2(s - m) for s in s_list]
    l = sum(p.sum(axis=-1, keepdims=True) for p in ps)
    o = sum(jnp.dot(p.astype(BF16), v, preferred_element_type=F32) for p, v in zip(ps, v_list))
    return o / l


MLA_KCHUNK = 512


def _mla_attn_kernel(q_ref, k_ref, v_ref, o_ref):
    q = q_ref[...]

    @pl.when(pl.program_id(2) == 0)
    def _():
        o_ref[...] = _softmax_pv([_dot_nt(q, k_ref[0:T_CTX, :])], [v_ref[0:T_CTX, :]]).astype(BF16)

    @pl.when(pl.program_id(2) > 0)
    def _():
        chunks = [slice(c0, min(c0 + MLA_KCHUNK, S_ALL)) for c0 in range(0, S_ALL, MLA_KCHUNK)]
        o_ref[...] = _softmax_pv([_dot_nt(q, k_ref[c, :]) for c in chunks], [v_ref[c, :] for c in chunks]).astype(BF16)


def mla_attention(q, k, v):
    return pl.pallas_call(
        _mla_attn_kernel,
        grid=(NB, MLA_HEADS, BLK_PER_B),
        in_specs=[pl.BlockSpec((ROWBLK, MLA_HP), lambda b, h, i: (b * BLK_PER_B + i, h)),
                  pl.BlockSpec((S_ALL, MLA_HP), lambda b, h, i: (b, h)),
                  pl.BlockSpec((S_ALL, MLA_V), lambda b, h, i: (b, h))],
        out_specs=pl.BlockSpec((ROWBLK, MLA_V), lambda b, h, i: (b * BLK_PER_B + i, h)),
        out_shape=jax.ShapeDtypeStruct((N_ROWS, MLA_HEADS * MLA_V), BF16),
        compiler_params=_cparams(("arbitrary", "arbitrary", "arbitrary")),
        name="mla_attention",
    )(q, k, v)


N_GRID_ROWS = T_LAT // GRID_W
NA_QROWS = ROWBLK // GRID_W
NA_GROUPS = N_GRID_ROWS // NA_QROWS
NA_WIN_ROWS = NA_KH + NA_QROWS - 1
NA_WIN = NA_WIN_ROWS * GRID_W


def _na_win_row(g):
    return jnp.clip(g * NA_QROWS - NA_KH // 2, 0, N_GRID_ROWS - NA_WIN_ROWS)


def _na_kernel(q_ref, k_ref, v_ref, gq_ref, gk_ref, bias_ref, o_ref, kn_ref, vb_ref):
    st = pl.program_id(2)

    @pl.when(st == 0)
    def _():
        k = k_ref[...]
        kn_ref[...] = (k * lax.rsqrt(jnp.mean(k * k, axis=-1, keepdims=True) + EPS) * gk_ref[...]).astype(BF16)
        vb_ref[...] = v_ref[...].astype(BF16)

    q = q_ref[...]
    q_scale = lax.rsqrt(jnp.mean(q * q, axis=-1, keepdims=True) + EPS) * (NA_HD ** -0.5 * LOG2E)
    qn = (q * q_scale * gq_ref[...]).astype(BF16)
    s_cx = _dot_nt(qn, kn_ref[0:T_CTX, :])
    v_cx = vb_ref[0:T_CTX, :]

    @pl.when(st == 0)
    def _():
        o_ref[...] = _softmax_pv([s_cx], [v_cx]).astype(BF16)

    @pl.when(st > 0)
    def _():
        start = pl.multiple_of(T_CTX + _na_win_row(st - 1) * GRID_W, GRID_W)
        s_nb = _dot_nt(qn, kn_ref[pl.ds(start, NA_WIN), :]) + bias_ref[0, 0]
        o_ref[...] = _softmax_pv([s_nb, s_cx], [vb_ref[pl.ds(start, NA_WIN), :], v_cx]).astype(BF16)


def _na_class(g):
    return jnp.where(g == 0, 0, jnp.where(g == NA_GROUPS - 1, 2, 1))


def na_attention(p, gq, gk, bias_tab):
    return pl.pallas_call(
        _na_kernel,
        grid=(NB, NA_HEADS, BLK_PER_B),
        in_specs=[pl.BlockSpec((ROWBLK, NA_HD), lambda b, h, st: (b * BLK_PER_B + st, C_NAQ // NA_HD + h)),
                  pl.BlockSpec((S_ALL, NA_HD), lambda b, h, st: (b, C_NAK // NA_HD + h)),
                  pl.BlockSpec((S_ALL, NA_HD), lambda b, h, st: (b, C_NAV // NA_HD + h)),
                  pl.BlockSpec((1, NA_HD), lambda b, h, st: (0, 0)),
                  pl.BlockSpec((1, NA_HD), lambda b, h, st: (0, 0)),
                  pl.BlockSpec((1, 1, ROWBLK, NA_WIN),
                               lambda b, h, st: (h, _na_class(jnp.maximum(st - 1, 0)), 0, 0))],
        out_specs=pl.BlockSpec((ROWBLK, NA_HD), lambda b, h, st: (b * BLK_PER_B + st, h)),
        out_shape=jax.ShapeDtypeStruct((N_ROWS, NA_HEADS * NA_HD), BF16),
        scratch_shapes=[pltpu.VMEM((S_ALL, NA_HD), BF16), pltpu.VMEM((S_ALL, NA_HD), BF16)],
        compiler_params=_cparams(("arbitrary", "arbitrary", "arbitrary")),
        name="na_attention",
    )(p, p, p, gq.reshape(1, NA_HD), gk.reshape(1, NA_HD), bias_tab)


def _na_window_tables():
    g_rep = np.array([0, 1, NA_GROUPS - 1])
    r = g_rep[:, None] * NA_QROWS + np.arange(NA_QROWS)[None, :]
    r0 = np.clip(r - NA_KH // 2, 0, N_GRID_ROWS - NA_KH)
    w0 = np.clip(g_rep * NA_QROWS - NA_KH // 2, 0, N_GRID_ROWS - NA_WIN_ROWS)
    kr = w0[:, None] + np.arange(NA_WIN_ROWS)[None, :]
    valid_r = (kr[:, None, :] >= r0[:, :, None]) & (kr[:, None, :] < r0[:, :, None] + NA_KH)
    dr = kr[:, None, :] - r[:, :, None] + NA_KH - 1
    return dr, valid_r


def _na_bias_kernel(toe_ref, o_ref):
    dr, valid_r = _na_window_tables()
    masked = jnp.full((GRID_W, GRID_W), -1e30, F32)
    for cls in range(dr.shape[0]):
        @pl.when(pl.program_id(1) == cls)
        def _(cls=cls):
            for a in range(NA_QROWS):
                for jw in range(NA_WIN_ROWS):
                    blk = toe_ref[0, int(dr[cls, a, jw])] if valid_r[cls, a, jw] else masked
                    o_ref[0, 0, a * GRID_W:(a + 1) * GRID_W, jw * GRID_W:(jw + 1) * GRID_W] = blk


def na_bias_table(rpb):
    c = np.arange(GRID_W)[:, None]
    kc = np.arange(GRID_W)[None, :]
    c0 = np.clip(c - NA_KW // 2, 0, GRID_W - NA_KW)
    valid_c = (kc >= c0) & (kc < c0 + NA_KW)
    dc = kc - c + NA_KW - 1
    onehot = (np.arange(2 * NA_KW - 1)[:, None, None] == dc[None]) & valid_c[None]
    toe = jnp.einsum('hrd,dck->hrck', rpb, jnp.asarray(onehot, F32), precision=HI)
    toe = jnp.where(valid_c[None, None], toe * LOG2E, -1e30)
    n_rows = 2 * NA_KH - 1
    return pl.pallas_call(
        _na_bias_kernel,
        grid=(NA_HEADS, 3),
        in_specs=[pl.BlockSpec((1, n_rows, GRID_W, GRID_W), lambda h, s: (h, 0, 0, 0))],
        out_specs=pl.BlockSpec((1, 1, ROWBLK, NA_WIN), lambda h, s: (h, s, 0, 0)),
        out_shape=jax.ShapeDtypeStruct((NA_HEADS, 3, ROWBLK, NA_WIN), F32),
        compiler_params=_cparams(("arbitrary", "arbitrary")),
        name="na_bias",
    )(toe)


N_TOK = NB * T_LAT
MOE_TM = 512
N_BLK = N_TOK * TOP_K // MOE_TM + N_EXPERTS
N_SLOT = N_BLK * MOE_TM
GATHER_ROWS = 256
COMBINE_TOK = 128


def _row_copies(n_rows, copy_fn):
    def start_all():
        def body(r8, carry):
            for u in range(8):
                copy_fn(r8 * 8 + u).start(priority=u % 2)
            return carry
        lax.fori_loop(0, n_rows // 8, body, 0)

    def wait_all():
        def body(r, carry):
            copy_fn(r).wait()
            return carry
        lax.fori_loop(0, n_rows, body, 0, unroll=8)

    return start_all, wait_all


def _gather_kernel(idx_ref, nu_ref, h_hbm, o_ref, buf_ref, sem):
    i = pl.program_id(0)
    n_steps = nu_ref[0] * (MOE_TM // GATHER_ROWS)

    def copies(step):
        slot = step % 2
        return _row_copies(GATHER_ROWS, lambda r: pltpu.make_async_copy(
            h_hbm.at[pl.ds(idx_ref[step * GATHER_ROWS + r], 1)], buf_ref.at[slot, pl.ds(r, 1)], sem.at[slot]))

    @pl.when(i == 0)
    def _():
        copies(i)[0]()

    @pl.when(i + 1 < n_steps)
    def _():
        copies(i + 1)[0]()

    @pl.when(i < n_steps)
    def _():
        copies(i)[1]()
        o_ref[...] = buf_ref[i % 2].astype(BF16)

    @pl.when(i >= n_steps)
    def _():
        o_ref[...] = jnp.zeros_like(o_ref)


def moe_gather(h2, slot_row, n_used):
    return pl.pallas_call(
        _gather_kernel,
        grid_spec=pltpu.PrefetchScalarGridSpec(
            num_scalar_prefetch=2,
            grid=(N_SLOT // GATHER_ROWS,),
            in_specs=[pl.BlockSpec(memory_space=pl.ANY)],
            out_specs=pl.BlockSpec((GATHER_ROWS, D), lambda i, idx, nu: (i, 0)),
            scratch_shapes=[pltpu.VMEM((2, GATHER_ROWS, D), F32), pltpu.SemaphoreType.DMA((2,))]),
        out_shape=jax.ShapeDtypeStruct((N_SLOT, D), BF16),
        compiler_params=_cparams(("arbitrary",)),
        name="moe_gather",
    )(slot_row, n_used, h2)


def _combine_kernel(slot_ref, y_hbm, x_ref, gate_ref, g2_ref, o_ref, buf_ref, sem):
    i = pl.program_id(0)

    def copies(step):
        slot = step % 2
        base = step * COMBINE_TOK * TOP_K
        return _row_copies(COMBINE_TOK * TOP_K, lambda r: pltpu.make_async_copy(
            y_hbm.at[pl.ds(slot_ref[base + r], 1)], buf_ref.at[slot, pl.ds(r, 1)], sem.at[slot]))

    @pl.when(i == 0)
    def _():
        copies(i)[0]()

    @pl.when(i + 1 < N_TOK // COMBINE_TOK)
    def _():
        copies(i + 1)[0]()

    copies(i)[1]()
    b = i // (T_LAT // COMBINE_TOK)
    rows = buf_ref[i % 2]
    f = gate_ref[:, 0:1] * rows[0:COMBINE_TOK] + gate_ref[:, 1:2] * rows[COMBINE_TOK:2 * COMBINE_TOK]
    o_ref[...] = x_ref[...] + g2_ref[pl.ds(b, 1), :] * f


def moe_combine(yb, x, slots_km, gates, modtab):
    per_b = T_LAT // COMBINE_TOK
    x_map = lambda i, s: ((i // per_b) * (S_ALL // COMBINE_TOK) + T_CTX // COMBINE_TOK + i % per_b, 0)
    return pl.pallas_call(
        _combine_kernel,
        grid_spec=pltpu.PrefetchScalarGridSpec(
            num_scalar_prefetch=1,
            grid=(N_TOK // COMBINE_TOK,),
            in_specs=[pl.BlockSpec(memory_space=pl.ANY),
                      pl.BlockSpec((COMBINE_TOK, D), x_map),
                      pl.BlockSpec((COMBINE_TOK, TOP_K), lambda i, s: (i, 0)),
                      pl.BlockSpec((8, D), lambda i, s: (0, 5))],
            out_specs=pl.BlockSpec((COMBINE_TOK, D), lambda i, s: (i, 0)),
            scratch_shapes=[pltpu.VMEM((2, COMBINE_TOK * TOP_K, D), F32), pltpu.SemaphoreType.DMA((2,))]),
        out_shape=jax.ShapeDtypeStruct((N_TOK, D), F32),
        compiler_params=_cparams(("arbitrary",)),
        name="moe_combine",
    )(slots_km, yb, x, gates, modtab)


def moe_routing(logits):
    top_v, top_e = lax.top_k(logits, TOP_K)
    gates = jax.nn.softmax(top_v, axis=-1)
    flat_e = top_e.reshape(-1)
    onehot = (flat_e[:, None] == jnp.arange(N_EXPERTS)[None, :]).astype(jnp.int32)
    rank = jnp.take_along_axis(jnp.cumsum(onehot, axis=0) - onehot, flat_e[:, None], axis=1)[:, 0]
    counts = onehot.sum(axis=0)
    padded = (counts + MOE_TM - 1) // MOE_TM * MOE_TM
    pend = jnp.cumsum(padded)
    slot = ((pend - padded)[flat_e] + rank).astype(jnp.int32)
    tok = jnp.arange(N_TOK, dtype=jnp.int32)
    tok_row = (tok // T_LAT) * S_ALL + T_CTX + tok % T_LAT
    slot_row = jnp.zeros((N_SLOT,), jnp.int32).at[slot].set(jnp.repeat(tok_row, TOP_K))
    n_used = (pend[-1] // MOE_TM).astype(jnp.int32)
    blk = jnp.arange(N_BLK)
    used = blk < n_used
    blk_e = jnp.sum(jnp.minimum(blk, n_used - 1)[:, None] * MOE_TM >= pend[None, :], axis=1).astype(jnp.int32)
    fresh = (used & ((blk == 0) | (blk_e != jnp.roll(blk_e, 1)))).astype(jnp.int32)
    run_idx = jnp.cumsum(fresh) - 1
    n_runs = fresh.sum()
    has = counts > 0
    ids = jnp.arange(N_EXPERTS)
    first_e = jnp.min(jnp.where(has, ids, N_EXPERTS))
    later = jnp.where(has[None, :] & (ids[None, :] > ids[:, None]), ids[None, :], N_EXPERTS).min(axis=1)
    next_of = jnp.where(later < N_EXPERTS, later, first_e)
    sched = (blk_e, fresh, run_idx.astype(jnp.int32), next_of[blk_e].astype(jnp.int32),
             (run_idx == n_runs - 1).astype(jnp.int32),
             jnp.where(used, jnp.clip(((pend - padded) + counts)[blk_e] - blk * MOE_TM, 0, MOE_TM), 0).astype(jnp.int32),
             jnp.stack([n_used, n_runs]).astype(jnp.int32))
    slots_km = slot.reshape(N_TOK // COMBINE_TOK, COMBINE_TOK, TOP_K).transpose(0, 2, 1).reshape(-1)
    return slot_row, sched, n_used.reshape(1), slots_km, gates


def _pad_cols(w, width):
    return jnp.pad(w, [(0, 0)] * (w.ndim - 1) + [(0, width - w.shape[-1])])


W_IN_SEGMENTS = ((C_Q, 0, 2048), (C_Z, 2064, 512), (C_XBC, 2576, 1024), (C_CQ, 3616, MLA_Q_RANK),
                 (C_CKV, 4064, MLA_KV_RANK), (C_KR, 4224, MLA_ROPE), (C_NAQ, 4288, 1536),
                 (C_GATE + G_I, 2048, 16), (C_GATE + G_DT, 3600, 16))
D_IN = 5824


def _w_in_layout_kernel(w_ref, o_ref):
    o_ref[...] = jnp.zeros_like(o_ref)
    for dst, src, width in W_IN_SEGMENTS:
        o_ref[0, :, dst:dst + width] = w_ref[0, :, src:src + width]


def layout_w_in(w):
    rows = 256
    return pl.pallas_call(
        _w_in_layout_kernel,
        grid=(DEPTH, D // rows),
        in_specs=[pl.BlockSpec((1, rows, D_IN), lambda l, i: (l, i, 0))],
        out_specs=pl.BlockSpec((1, rows, D_INP), lambda l, i: (l, i, 0)),
        out_shape=jax.ShapeDtypeStruct((DEPTH, D, D_INP), F32),
        compiler_params=_cparams(("arbitrary", "arbitrary")),
        name="w_in_layout",
    )(w)


def rope_tables():
    t = np.arange(T_LAT)
    n_freq = MLA_ROPE // 4
    freqs = ROPE_THETA ** (-jnp.arange(n_freq, dtype=F32) / n_freq)
    row = jnp.asarray(t // GRID_W, F32)
    col = jnp.asarray(t % GRID_W, F32)
    ang = jnp.concatenate([row[:, None] * freqs, col[:, None] * freqs], axis=-1)
    cos, sin = jnp.cos(ang), jnp.sin(ang)
    half = MLA_ROPE // 2
    zc = jnp.zeros((T_LAT, 128 - MLA_ROPE), F32)
    zh = jnp.zeros((T_LAT, half), F32)
    cs = jnp.concatenate([cos, cos, zc], axis=1)
    s1 = jnp.concatenate([-sin, zh, zc], axis=1)
    s2 = jnp.concatenate([zh, sin, zc], axis=1)
    ident = jnp.concatenate([jnp.ones((T_CTX, MLA_ROPE), F32), jnp.zeros((T_CTX, 128 - MLA_ROPE), F32)], axis=1)
    zeros = jnp.zeros((T_CTX, 128), F32)
    return (jnp.concatenate([ident, cs], axis=0), jnp.concatenate([zeros, s1], axis=0),
            jnp.concatenate([zeros, s2], axis=0))


def _gate_vectors(i_bias, f_bias, dt_bias, a_log):
    used = jnp.concatenate([i_bias.reshape(-1), f_bias.reshape(-1), dt_bias.reshape(-1)])
    alog = jnp.concatenate([jnp.zeros((G_DT,), F32), a_log.reshape(-1)])
    padc = lambda u: jnp.pad(u, (0, 128 - G_USED)).reshape(1, 128)
    return padc(used), used.reshape(G_USED, 1), padc(alog), alog.reshape(G_USED, 1)


def kernel(x, c, ctx, c_ctx, mod_w, mod_b, norm1, w_in, w_out, ml_i_bias, ml_f_bias, ml_norm, ssd_conv_w, ssd_conv_b, ssd_dt_bias, ssd_A_log, ssd_D, ssd_norm, mla_q_norm, mla_w_qb, mla_kv_norm, mla_w_kvb, mla_gq, mla_gk, na_gq, na_gk, na_rpb, norm2, ffn_w1, ffn_w3, ffn_w2, moe_router, moe_w1, moe_w3, moe_w2):
    xs = jnp.concatenate([ctx, x], axis=1).reshape(N_ROWS, D)
    c_all = jnp.concatenate([c, c_ctx[None, :], jnp.zeros((8 - NB - 1, D), F32)], axis=0)
    mod_all = modulation(c_all, mod_w, mod_b)
    tabs = rope_tables()
    w_in_p = layout_w_in(w_in)
    out = None
    for l in range(DEPTH):
        modtab = mod_all[l]
        h = rms_modulate(xs, norm1[l], modtab, 0)
        p = matmul(h, [w_in_p], tm=1536, tn=1024, layer=l, name="w_in")
        gates_t = p[:, C_GATE:C_GATE + G_USED].reshape(N_ROWS // CHUNK, CHUNK, G_USED).transpose(0, 2, 1)
        biasc, biasr, alogc, alogr = _gate_vectors(ml_i_bias[l], ml_f_bias[l], ssd_dt_bias[l], ssd_A_log[l])

        hf, hb = mlstm_scan(p, gates_t, biasc, biasr)
        ml = mlstm_finish(hf, hb, p, ml_norm[l])

        xbc = ssd_conv(p, ssd_conv_w[l], ssd_conv_b[l])
        yf, yb = ssd_scan(xbc, p, gates_t, biasc, biasr, alogc, alogr)
        ss = ssd_finish(yf, yb, xbc, p, ssd_D[l], ssd_norm[l])

        wq = jnp.pad(mla_w_qb[l].reshape(MLA_Q_RANK, MLA_HEADS, MLA_QK),
                     ((0, 512 - MLA_Q_RANK), (0, 0), (0, MLA_HP - MLA_QK))).reshape(512, MLA_HEADS * MLA_HP)
        wkv = jnp.pad(mla_w_kvb[l], ((0, 256 - MLA_KV_RANK), (0, 0)))
        pad1 = lambda u, w: jnp.pad(u, (0, w - u.shape[0])).reshape(1, w)
        q, k, v = mla_prep(p, wq, wkv, pad1(mla_q_norm[l], 512), pad1(mla_kv_norm[l], 256),
                           pad1(mla_gq[l], MLA_HP), pad1(mla_gk[l], MLA_HP), tabs)
        la = mla_attention(q, k, v)

        na = na_attention(p, na_gq[l], na_gk[l], na_bias_table(na_rpb[l]))

        xs = matmul([ml, ss, la, na], [w_out], mode="resid", res=xs, modtab=modtab, gate_col=2,
                    tm=1536, layer=l, name="w_out")

        if l % 2 == 0:
            h2 = rms_modulate(xs, norm2[l], modtab, 3)
            hid = matmul(h2, [ffn_w1[l // 2], ffn_w3[l // 2]], mode="swiglu", out_dtype=BF16, tm=1536,
                         name="ffn_up")
            xs = matmul(hid, [ffn_w2[l // 2]], mode="resid", res=xs, modtab=modtab, gate_col=5,
                        w_buffers=1, name="ffn_down")
        else:
            h2, logits = rms_modulate(xs, norm2[l], modtab, 3, router=_pad_cols(moe_router[l // 2], 128))
            lat = logits.reshape(NB, S_ALL, 128)[:, T_CTX:, :N_EXPERTS].reshape(N_TOK, N_EXPERTS)
            slot_row, sched, n_used, slots_km, gates = moe_routing(lat)
            xb = moe_gather(h2, slot_row, n_used)
            hid = grouped_matmul(xb, [moe_w1[l // 2], moe_w3[l // 2]], sched, mode="swiglu", out_dtype=BF16,
                                 tm=MOE_TM, tn=1024, name="moe_up")
            yb_ = grouped_matmul(hid, [moe_w2[l // 2]], sched, tm=MOE_TM, tn=512, name="moe_down")
            out = moe_combine(yb_, xs, slots_km, gates, modtab)
    return out.reshape(NB, T_LAT, D)
```

```python
from functools import partial

import numpy as np
import jax
import jax.numpy as jnp
from jax import lax
from jax.experimental import pallas as pl
from jax.experimental.pallas import tpu as pltpu

F32 = jnp.float32
BF16 = jnp.bfloat16
HI = lax.Precision.HIGHEST

D = 2048
NB = 4
T_LAT = 2048
T_CTX = 256
S_ALL = T_LAT + T_CTX
N_ROWS = NB * S_ALL
DEPTH = 2
GRID_W = 64
EPS = 1e-6
LOG2E = 1.4426950408889634

ROWBLK = 256
BLK_PER_B = S_ALL // ROWBLK
CHUNK = 128
CH_PER_B = S_ALL // CHUNK
CH_CTX = T_CTX // CHUNK

ML_HEADS, ML_HD = 4, 128
SSD_HEADS, SSD_HD, SSD_GROUPS, SSD_STATE, SSD_CONV = 8, 64, 2, 128, 5
SSD_W = SSD_HEADS * SSD_HD
MLA_HEADS, MLA_NOPE, MLA_ROPE, MLA_V = 4, 128, 64, 128
MLA_QK = MLA_NOPE + MLA_ROPE
MLA_Q_RANK, MLA_KV_RANK = 448, 160
MLA_HP = 256
ROPE_THETA = 10000.0
NA_HEADS, NA_HD, NA_KH, NA_KW = 4, 128, 8, 16
N_EXPERTS, TOP_K = 8, 2

C_Q, C_K, C_V, C_O = 0, 512, 1024, 1536
C_Z = 2048
C_XBC = 2560
C_CQ, C_CKV, C_KR = 3584, 4096, 4352
C_NAQ, C_NAK, C_NAV = 4480, 4992, 5504
C_GATE = 6016
D_INP = 6144
G_I, G_F, G_DT, G_USED = 0, 8, 16, 32

VMEM_LIMIT = 56 * 1024 * 1024


def _cparams(sem):
    return pltpu.CompilerParams(dimension_semantics=sem, vmem_limit_bytes=VMEM_LIMIT)


def _mod_index(blk256):
    return jnp.where(blk256 % BLK_PER_B == 0, NB, blk256 // BLK_PER_B)


def _dot_nt(a, b):
    return lax.dot_general(a, b, (((1,), (1,)), ((), ())), preferred_element_type=F32)


def _dot_tn(a, b):
    return lax.dot_general(a, b, (((0,), (0,)), ((), ())), preferred_element_type=F32)


def _silu(x):
    return x * jax.nn.sigmoid(x)


def _softplus(x):
    return jnp.maximum(x, 0.0) + jnp.log(1.0 + jnp.exp(-jnp.abs(x)))


def _mod_kernel(c_ref, w_ref, b_ref, o_ref):
    o_ref[0] = jnp.dot(_silu(c_ref[...]), w_ref[0], preferred_element_type=F32) + b_ref[0]


def modulation(c_all, mod_w, mod_b):
    tn = 1024
    n_out = mod_w.shape[-1]
    return pl.pallas_call(
        _mod_kernel,
        grid=(DEPTH, n_out // tn),
        in_specs=[pl.BlockSpec((8, D), lambda l, j: (0, 0)),
                  pl.BlockSpec((1, D, tn), lambda l, j: (l, 0, j)),
                  pl.BlockSpec((1, 1, tn), lambda l, j: (l, 0, j))],
        out_specs=pl.BlockSpec((1, 8, tn), lambda l, j: (l, 0, j)),
        out_shape=jax.ShapeDtypeStruct((DEPTH, 8, n_out), F32),
        compiler_params=_cparams(("arbitrary", "arbitrary")),
        name="modulation",
    )(c_all, mod_w, mod_b.reshape(DEPTH, 1, n_out))


ELT_ROWS = 768


def _rmsmod_kernel(x_ref, g_ref, sh_ref, sc_ref, *rest, with_router):
    for s in range(ELT_ROWS // ROWBLK):
        midx = _mod_index(pl.program_id(0) * (ELT_ROWS // ROWBLK) + s)
        rows = slice(s * ROWBLK, (s + 1) * ROWBLK)
        x = x_ref[rows, :]
        y = x * lax.rsqrt(jnp.mean(x * x, axis=-1, keepdims=True) + EPS) * g_ref[...]
        h = y * (1.0 + sc_ref[pl.ds(midx, 1), :]) + sh_ref[pl.ds(midx, 1), :]
        if with_router:
            r_ref, o_ref, lg_ref = rest
            o_ref[rows, :] = h
            lg_ref[rows, :] = jnp.dot(h, r_ref[...], precision=HI, preferred_element_type=F32)
        else:
            (o_ref,) = rest
            o_ref[rows, :] = h.astype(BF16)


def rms_modulate(x, g, modtab, which_shift, router=None):
    with_router = router is not None
    in_specs = [pl.BlockSpec((ELT_ROWS, D), lambda i: (i, 0)),
                pl.BlockSpec((1, D), lambda i: (0, 0)),
                pl.BlockSpec((8, D), lambda i: (0, which_shift)),
                pl.BlockSpec((8, D), lambda i: (0, which_shift + 1))]
    args = [x, g.reshape(1, D), modtab, modtab]
    if with_router:
        in_specs.append(pl.BlockSpec((D, 128), lambda i: (0, 0)))
        args.append(router)
        out_shape = [jax.ShapeDtypeStruct((N_ROWS, D), F32), jax.ShapeDtypeStruct((N_ROWS, 128), F32)]
        out_specs = [pl.BlockSpec((ELT_ROWS, D), lambda i: (i, 0)), pl.BlockSpec((ELT_ROWS, 128), lambda i: (i, 0))]
    else:
        out_shape = jax.ShapeDtypeStruct((N_ROWS, D), BF16)
        out_specs = pl.BlockSpec((ELT_ROWS, D), lambda i: (i, 0))
    return pl.pallas_call(
        partial(_rmsmod_kernel, with_router=with_router),
        grid=(N_ROWS // ELT_ROWS,),
        in_specs=in_specs, out_specs=out_specs, out_shape=out_shape,
        compiler_params=_cparams(("arbitrary",)),
        name="rms_modulate_router" if with_router else "rms_modulate",
    )(*args)


MM_COLS = 512


def _mm_compute(mode, a, w_tiles, o_ref, i=None, tm=None, res_ref=None, gate_ref=None, rows=slice(None)):
    tn = o_ref.shape[1]
    for c0 in range(0, tn, MM_COLS):
        cols = slice(c0, min(c0 + MM_COLS, tn))
        accs = [jnp.dot(a, w[:, cols], preferred_element_type=F32) for w in w_tiles]
        if mode == "plain":
            o_ref[rows, cols] = accs[0].astype(o_ref.dtype)
        elif mode == "swiglu":
            o_ref[rows, cols] = (_silu(accs[0]) * accs[1]).astype(o_ref.dtype)
        else:
            for s in range(tm // ROWBLK):
                midx = _mod_index(i * (tm // ROWBLK) + s)
                rows = slice(s * ROWBLK, (s + 1) * ROWBLK)
                o_ref[rows, cols] = res_ref[rows, cols] + gate_ref[pl.ds(midx, 1), cols] * accs[0][rows, :]


def _mm_kernel(*refs, mode, tm, na, stacked):
    nw = 2 if mode == "swiglu" else 1
    a_refs = refs[:na]
    w_refs = refs[na:na + nw]
    pos = na + nw
    res_ref = gate_ref = None
    if mode == "resid":
        res_ref, gate_ref = refs[pos:pos + 2]
        pos += 2
    o_ref = refs[pos]
    ws_refs = refs[pos + 1:]
    i = pl.program_id(1)

    @pl.when(i == 0)
    def _():
        for w_ref, ws_ref in zip(w_refs, ws_refs):
            ws_ref[...] = (w_ref[0] if stacked else w_ref[...]).astype(BF16)

    a = a_refs[0][...] if na == 1 else jnp.concatenate([a_ref[...] for a_ref in a_refs], axis=1)
    _mm_compute(mode, a, ws_refs, o_ref, i, tm, res_ref, gate_ref)


def matmul(a, ws, mode="plain", out_dtype=F32, res=None, modtab=None, gate_col=None,
           tm=768, tn=512, w_buffers=2, layer=None, name="matmul"):
    pieces = list(a) if isinstance(a, (list, tuple)) else [a]
    m = pieces[0].shape[0]
    kdim = sum(p.shape[1] for p in pieces)
    n = ws[0].shape[-1]
    wmode = {} if w_buffers == 2 else dict(pipeline_mode=pl.Buffered(w_buffers))
    in_specs = [pl.BlockSpec((tm, p.shape[1]), lambda j, i: (i, 0)) for p in pieces]
    if layer is None:
        in_specs += [pl.BlockSpec((kdim, tn), lambda j, i: (0, j), **wmode) for _ in ws]
    else:
        in_specs += [pl.BlockSpec((1, kdim, tn), lambda j, i: (layer, 0, j), **wmode) for _ in ws]
    args = [*pieces, *ws]
    if mode == "resid":
        in_specs += [pl.BlockSpec((tm, tn), lambda j, i: (i, j)),
                     pl.BlockSpec((8, tn), lambda j, i: (0, gate_col * (D // tn) + j))]
        args += [res, modtab]
    return pl.pallas_call(
        partial(_mm_kernel, mode=mode, tm=tm, na=len(pieces), stacked=layer is not None),
        grid=(n // tn, m // tm),
        in_specs=in_specs,
        out_specs=pl.BlockSpec((tm, tn), lambda j, i: (i, j)),
        out_shape=jax.ShapeDtypeStruct((m, n), out_dtype),
        scratch_shapes=[pltpu.VMEM((kdim, tn), BF16) for _ in ws],
        compiler_params=_cparams(("arbitrary", "arbitrary")),
        name=name,
    )(*args)


def _gmm_kernel(be_ref, fresh_ref, run_ref, nexte_ref, lastrun_ref, fill_ref, meta_ref, a_ref, *rest,
                mode, nw, tn):
    w_hbm = rest[:nw]
    o_ref = rest[nw]
    stage_ref, ws_ref, sem = rest[nw + 1:]
    j = pl.program_id(0)
    i = pl.program_id(1)
    n_runs = meta_ref[1]
    half = a_ref.shape[0] // 2

    def w_copy(widx, e, jj):
        col = pl.multiple_of(jj * tn, tn)
        return pltpu.make_async_copy(w_hbm[widx].at[e, :, pl.ds(col, tn)], stage_ref.at[widx], sem.at[widx])

    @pl.when(fresh_ref[i] == 1)
    def _():
        @pl.when(j * n_runs + run_ref[i] == 0)
        def _():
            for widx in range(nw):
                w_copy(widx, be_ref[i], j).start()

        for widx in range(nw):
            w_copy(widx, be_ref[i], j).wait()
            ws_ref[widx] = stage_ref[widx].astype(BF16)
        is_last = lastrun_ref[i] == 1

        @pl.when(jnp.logical_not(jnp.logical_and(is_last, j == pl.num_programs(0) - 1)))
        def _():
            jn = jnp.where(is_last, j + 1, j)
            for widx in range(nw):
                w_copy(widx, nexte_ref[i], jn).start()

    w_tiles = [ws_ref.at[widx] for widx in range(nw)]
    fill = fill_ref[i]

    @pl.when(fill > half)
    def _():
        _mm_compute(mode, a_ref[...], w_tiles, o_ref)

    @pl.when(jnp.logical_and(fill > 0, fill <= half))
    def _():
        _mm_compute(mode, a_ref[0:half, :], w_tiles, o_ref, rows=slice(0, half))
        o_ref[half:, :] = jnp.zeros((half, o_ref.shape[1]), o_ref.dtype)

    @pl.when(fill == 0)
    def _():
        o_ref[...] = jnp.zeros_like(o_ref)


def grouped_matmul(a, ws, sched, mode="plain", out_dtype=F32, tm=512, tn=512, name="grouped_matmul"):
    m, kdim = a.shape
    n = ws[0].shape[-1]
    nw = len(ws)
    row = lambda j, i, be, fr, ru, ne, lr, fill, meta: jnp.minimum(i, meta[0] - 1)
    grid_spec = pltpu.PrefetchScalarGridSpec(
        num_scalar_prefetch=len(sched),
        grid=(n // tn, m // tm),
        in_specs=[pl.BlockSpec((tm, kdim), lambda *s: (row(*s), 0))] + [pl.BlockSpec(memory_space=pl.ANY)] * nw,
        out_specs=pl.BlockSpec((tm, tn), lambda j, i, *_: (i, j)),
        scratch_shapes=[pltpu.VMEM((nw, kdim, tn), F32), pltpu.VMEM((nw, kdim, tn), BF16),
                        pltpu.SemaphoreType.DMA((nw,))])
    return pl.pallas_call(
        partial(_gmm_kernel, mode=mode, nw=nw, tn=tn),
        grid_spec=grid_spec,
        out_shape=jax.ShapeDtypeStruct((m, n), out_dtype),
        compiler_params=_cparams(("arbitrary", "arbitrary")),
        name=name,
    )(*sched, a, *ws)


def _fwd_chunk(i):
    return i


def _bwd_chunk(i):
    return jnp.where(i < CH_CTX, CH_CTX - 1 - i, CH_PER_B + CH_CTX - 1 - i)


def _tri_masks():
    r = lax.broadcasted_iota(jnp.int32, (CHUNK, CHUNK), 0)
    c = lax.broadcasted_iota(jnp.int32, (CHUNK, CHUNK), 1)
    return c <= r, c >= r


def _mlstm_kernel(qf_ref, kf_ref, vf_ref, gcf_ref, grf_ref,
                  qb_ref, kb_ref, vb_ref, gcb_ref, grb_ref,
                  biasc_ref, biasr_ref, hf_ref, hb_ref, c_ref, n_ref, m_ref):
    @pl.when(pl.program_id(1) == 0)
    def _():
        c_ref[...] = jnp.zeros_like(c_ref)
        n_ref[...] = jnp.zeros_like(n_ref)
        m_ref[...] = jnp.zeros_like(m_ref)

    low, upp = _tri_masks()
    lowf, uppf = low.astype(F32), upp.astype(F32)
    streams = ((0, qf_ref, kf_ref, vf_ref, gcf_ref, grf_ref, hf_ref, upp, lowf, uppf, CHUNK - 1),
               (1, qb_ref, kb_ref, vb_ref, gcb_ref, grb_ref, hb_ref, low, uppf, lowf, 0))
    for d, q_ref, k_ref, v_ref, gc_ref, gr_ref, h_ref, mask_t, tri_c, tri_r, last in streams:
        ac = gc_ref[...] + biasc_ref[...]
        ar = gr_ref[0] + biasr_ref[...]
        lfc = -_softplus(-ac)
        lfr = -_softplus(-ar)
        bc_all = jnp.dot(tri_c, lfc, precision=HI, preferred_element_type=F32)
        br_all = jnp.dot(lfr, tri_r, precision=HI, preferred_element_type=F32)
        for h in range(ML_HEADS):
            s_idx = d * ML_HEADS + h
            ji, jf = G_I + s_idx, G_F + s_idx
            cols = slice(h * ML_HD, (h + 1) * ML_HD)
            q = q_ref[:, cols] * (ML_HD ** -0.5)
            k = k_ref[:, cols]
            vt = v_ref[:, cols].T
            b_row = br_all[jf:jf + 1, :]
            i_row = ar[ji:ji + 1, :]
            key_col = ac[:, ji:ji + 1] - bc_all[:, jf:jf + 1]
            cmat = c_ref[s_idx]
            nvec = n_ref[s_idx]
            m_prev = m_ref[s_idx][:, 0:1]
            dmat = jnp.where(mask_t, b_row + key_col, -1e30)
            inter = b_row + m_prev
            mt = jnp.maximum(inter, jnp.max(dmat, axis=0, keepdims=True))
            w_intra = jnp.exp(dmat - mt)
            w_state = jnp.exp(inter - mt)
            st = _dot_nt(k, q) * w_intra
            num = jnp.dot(vt, st, preferred_element_type=F32) + w_state * _dot_nt(cmat, q)
            den = jnp.sum(st, axis=0, keepdims=True) + w_state * _dot_nt(nvec, q)
            h_ref[:, cols] = (num / jnp.maximum(jnp.abs(den), jnp.exp(-mt))).T
            bl = b_row[:, last:last + 1]
            g = bl - b_row + i_row
            m_new = jnp.maximum(bl + m_prev, jnp.max(g, axis=1, keepdims=True))
            wg = jnp.exp(g - m_new)
            wc = jnp.exp(bl + m_prev - m_new)
            c_ref[s_idx] = wc * cmat + jnp.dot(vt * wg, k, preferred_element_type=F32)
            n_ref[s_idx] = wc * nvec + jnp.dot(wg, k, preferred_element_type=F32)
            m_ref[s_idx] = jnp.broadcast_to(m_new, (1, 128))


def mlstm_scan(p, gates_t, biasc, biasr):
    def rows(chunk_fn):
        return lambda b, i: b * CH_PER_B + chunk_fn(i)

    def stream_specs(chunk_fn):
        r = rows(chunk_fn)
        return [pl.BlockSpec((CHUNK, 512), lambda b, i: (r(b, i), C_Q // 512)),
                pl.BlockSpec((CHUNK, 512), lambda b, i: (r(b, i), C_K // 512)),
                pl.BlockSpec((CHUNK, 512), lambda b, i: (r(b, i), C_V // 512)),
                pl.BlockSpec((CHUNK, 128), lambda b, i: (r(b, i), C_GATE // 128)),
                pl.BlockSpec((1, G_USED, CHUNK), lambda b, i: (r(b, i), 0, 0))]

    rf, rb = rows(_fwd_chunk), rows(_bwd_chunk)
    n_streams = 2 * ML_HEADS
    return pl.pallas_call(
        _mlstm_kernel,
        grid=(NB, CH_PER_B),
        in_specs=stream_specs(_fwd_chunk) + stream_specs(_bwd_chunk) + [
            pl.BlockSpec((1, 128), lambda b, i: (0, 0)),
            pl.BlockSpec((G_USED, 1), lambda b, i: (0, 0))],
        out_specs=[pl.BlockSpec((CHUNK, 512), lambda b, i: (rf(b, i), 0)),
                   pl.BlockSpec((CHUNK, 512), lambda b, i: (rb(b, i), 0))],
        out_shape=[jax.ShapeDtypeStruct((N_ROWS, 512), F32)] * 2,
        scratch_shapes=[pltpu.VMEM((n_streams, ML_HD, ML_HD), F32),
                        pltpu.VMEM((n_streams, 1, ML_HD), F32),
                        pltpu.VMEM((n_streams, 1, 128), F32)],
        compiler_params=_cparams(("arbitrary", "arbitrary")),
        name="mlstm_scan",
    )(p, p, p, p, gates_t, p, p, p, p, gates_t, biasc, biasr)


def _mlstm_finish_kernel(hf_ref, hb_ref, o_ref, g_ref, out_ref):
    for h in range(ML_HEADS):
        cols = slice(h * ML_HD, (h + 1) * ML_HD)
        x = hf_ref[:, cols] + hb_ref[:, cols]
        y = x * lax.rsqrt(jnp.mean(x * x, axis=-1, keepdims=True) + EPS) * g_ref[:, cols]
        out_ref[:, cols] = (y * jax.nn.sigmoid(o_ref[:, cols])).astype(BF16)


def mlstm_finish(hf, hb, p, norm_g):
    blk = lambda c: pl.BlockSpec((ELT_ROWS, 512), lambda i: (i, c))
    return pl.pallas_call(
        _mlstm_finish_kernel,
        grid=(N_ROWS // ELT_ROWS,),
        in_specs=[blk(0), blk(0), blk(C_O // 512), pl.BlockSpec((1, 512), lambda i: (0, 0))],
        out_specs=blk(0),
        out_shape=jax.ShapeDtypeStruct((N_ROWS, 512), BF16),
        compiler_params=_cparams(("arbitrary",)),
        name="mlstm_finish",
    )(hf, hb, p, norm_g.reshape(1, 512))


_PAD = 8


def _ssd_conv_kernel(u_ref, w_ref, b_ref, o_ref, pad_ref):
    zeros = jnp.zeros((_PAD, u_ref.shape[1]), F32)
    segs = ((0, T_CTX, _PAD), (T_CTX, T_LAT, 2 * _PAD + T_CTX))
    pad_ref[0:_PAD] = zeros
    pad_ref[_PAD + T_CTX:2 * _PAD + T_CTX] = zeros
    pad_ref[2 * _PAD + S_ALL:3 * _PAD + S_ALL] = zeros
    for src, n, dst in segs:
        pad_ref[dst:dst + n] = u_ref[src:src + n]
    for src, n, dst in segs:
        acc = jnp.broadcast_to(b_ref[...], (n, u_ref.shape[1]))
        for j in range(SSD_CONV):
            lo = dst + j - SSD_CONV // 2
            acc = acc + w_ref[j:j + 1, :] * pad_ref[lo:lo + n]
        o_ref[src:src + n] = _silu(acc)


def ssd_conv(p, conv_w, conv_b):
    tc = 256
    return pl.pallas_call(
        _ssd_conv_kernel,
        grid=(NB, 1024 // tc),
        in_specs=[pl.BlockSpec((S_ALL, tc), lambda b, c: (b, C_XBC // tc + c)),
                  pl.BlockSpec((SSD_CONV, tc), lambda b, c: (0, c)),
                  pl.BlockSpec((1, tc), lambda b, c: (0, c))],
        out_specs=pl.BlockSpec((S_ALL, tc), lambda b, c: (b, c)),
        out_shape=jax.ShapeDtypeStruct((N_ROWS, 1024), F32),
        scratch_shapes=[pltpu.VMEM((S_ALL + 3 * _PAD, tc), F32)],
        compiler_params=_cparams(("arbitrary", "arbitrary")),
        name="ssd_conv",
    )(p, conv_w, conv_b.reshape(1, 1024))


def _ssd_kernel(xf_ref, bf_ref, cf_ref, gcf_ref, grf_ref,
                xb_ref, bb_ref, cb_ref, gcb_ref, grb_ref,
                biasc_ref, biasr_ref, alogc_ref, alogr_ref, yf_ref, yb_ref, st_ref):
    @pl.when(pl.program_id(1) == 0)
    def _():
        st_ref[...] = jnp.zeros_like(st_ref)

    low, upp = _tri_masks()
    lowf, uppf = low.astype(F32), upp.astype(F32)
    a_c = -jnp.exp(alogc_ref[...])
    a_r = -jnp.exp(alogr_ref[...])
    streams = ((0, xf_ref, bf_ref, cf_ref, gcf_ref, grf_ref, yf_ref, upp, lowf, uppf, CHUNK - 1),
               (1, xb_ref, bb_ref, cb_ref, gcb_ref, grb_ref, yb_ref, low, uppf, lowf, 0))
    for d, x_ref, b_ref, c_ref, gc_ref, gr_ref, y_ref, mask_t, tri_c, tri_r, last in streams:
        dtc_all = _softplus(gc_ref[...] + biasc_ref[...])
        dtr_all = _softplus(gr_ref[0] + biasr_ref[...])
        negc_all = -jnp.dot(tri_c, dtc_all * a_c, precision=HI, preferred_element_type=F32)
        cumr_all = jnp.dot(dtr_all * a_r, tri_r, precision=HI, preferred_element_type=F32)
        xt = x_ref[...].T
        yts = []
        for g in range(SSD_GROUPS):
            gcols = slice(g * SSD_STATE, (g + 1) * SSD_STATE)
            bm = b_ref[:, gcols]
            cm = c_ref[:, gcols]
            bc = _dot_nt(bm, cm)
            for hh in range(SSD_HEADS // SSD_GROUPS):
                h = g * (SSD_HEADS // SSD_GROUPS) + hh
                j = G_DT + d * SSD_HEADS + h
                s_idx = d * SSD_HEADS + h
                cum = cumr_all[j:j + 1, :]
                xd = xt[h * SSD_HD:(h + 1) * SSD_HD, :] * dtr_all[j:j + 1, :]
                st = st_ref[s_idx]
                w = bc * jnp.exp(jnp.where(mask_t, cum + negc_all[:, j:j + 1], -1e30))
                yts.append(jnp.dot(xd, w, preferred_element_type=F32) + _dot_nt(st, cm) * jnp.exp(cum))
                cl = cum[:, last:last + 1]
                st_ref[s_idx] = st * jnp.exp(cl) + jnp.dot(xd * jnp.exp(cl - cum), bm, preferred_element_type=F32)
        y_ref[...] = jnp.concatenate(yts, axis=0).T


def ssd_scan(xbc, p, gates_t, biasc, biasr, alogc, alogr):
    def rows(chunk_fn):
        return lambda b, i: b * CH_PER_B + chunk_fn(i)

    def stream_specs(chunk_fn):
        r = rows(chunk_fn)
        return [pl.BlockSpec((CHUNK, 512), lambda b, i: (r(b, i), 0)),
                pl.BlockSpec((CHUNK, 256), lambda b, i: (r(b, i), 2)),
                pl.BlockSpec((CHUNK, 256), lambda b, i: (r(b, i), 3)),
                pl.BlockSpec((CHUNK, 128), lambda b, i: (r(b, i), C_GATE // 128)),
                pl.BlockSpec((1, G_USED, CHUNK), lambda b, i: (r(b, i), 0, 0))]

    rf, rb = rows(_fwd_chunk), rows(_bwd_chunk)
    const = lambda shape: pl.BlockSpec(shape, lambda b, i: (0, 0))
    return pl.pallas_call(
        _ssd_kernel,
        grid=(NB, CH_PER_B),
        in_specs=stream_specs(_fwd_chunk) + stream_specs(_bwd_chunk) + [
            const((1, 128)), const((G_USED, 1)), const((1, 128)), const((G_USED, 1))],
        out_specs=[pl.BlockSpec((CHUNK, 512), lambda b, i: (rf(b, i), 0)),
                   pl.BlockSpec((CHUNK, 512), lambda b, i: (rb(b, i), 0))],
        out_shape=[jax.ShapeDtypeStruct((N_ROWS, 512), F32)] * 2,
        scratch_shapes=[pltpu.VMEM((2 * SSD_HEADS, SSD_HD, SSD_STATE), F32)],
        compiler_params=_cparams(("arbitrary", "arbitrary")),
        name="ssd_scan",
    )(xbc, xbc, xbc, p, gates_t, xbc, xbc, xbc, p, gates_t, biasc, biasr, alogc, alogr)


def _ssd_finish_kernel(yf_ref, yb_ref, xs_ref, z_ref, dsk_ref, g_ref, out_ref):
    y = yf_ref[...] + yb_ref[...] + dsk_ref[...] * xs_ref[...]
    y = y * _silu(z_ref[...])
    out_ref[...] = (y * lax.rsqrt(jnp.mean(y * y, axis=-1, keepdims=True) + EPS) * g_ref[...]).astype(BF16)


def ssd_finish(yf, yb, xbc, p, d_skip, norm_g):
    blk = lambda c: pl.BlockSpec((ELT_ROWS, 512), lambda i: (i, c))
    vec = pl.BlockSpec((1, 512), lambda i: (0, 0))
    return pl.pallas_call(
        _ssd_finish_kernel,
        grid=(N_ROWS // ELT_ROWS,),
        in_specs=[blk(0), blk(0), blk(0), blk(C_Z // 512), vec, vec],
        out_specs=blk(0),
        out_shape=jax.ShapeDtypeStruct((N_ROWS, 512), BF16),
        compiler_params=_cparams(("arbitrary",)),
        name="ssd_finish",
    )(yf, yb, xbc, p, jnp.repeat(d_skip, SSD_HD).reshape(1, 512), norm_g.reshape(1, 512))


def _rope_tile(r, cs_ref, s1_ref, s2_ref):
    return (r * cs_ref[...] + pltpu.roll(r, 128 - MLA_ROPE // 2, 1) * s1_ref[...]
            + pltpu.roll(r, MLA_ROPE // 2, 1) * s2_ref[...])


def _mla_prep_kernel(cq_ref, ckv_ref, kr_ref, wq_ref, wkv_ref, qn_ref, kvn_ref, gq_ref, gk_ref,
                     cs_ref, s1_ref, s2_ref, q_out, k_out, v_out):
    cq = cq_ref[...]
    cqn = cq * lax.rsqrt(jnp.sum(cq * cq, axis=-1, keepdims=True) / MLA_Q_RANK + EPS) * qn_ref[...]
    q_raw = jnp.dot(cqn, wq_ref[...], preferred_element_type=F32)
    ckv = ckv_ref[...]
    ckvn = ckv * lax.rsqrt(jnp.sum(ckv * ckv, axis=-1, keepdims=True) / MLA_KV_RANK + EPS) * kvn_ref[...]
    kv_raw = jnp.dot(ckvn, wkv_ref[...], preferred_element_type=F32)
    kr = kr_ref[...]
    kr_ss = jnp.sum(kr * kr, axis=-1, keepdims=True)
    for h in range(MLA_HEADS):
        base = h * MLA_HP
        qa = q_raw[:, base:base + 128]
        qb = q_raw[:, base + 128:base + 256]
        q_scale = lax.rsqrt((jnp.sum(qa * qa, axis=-1, keepdims=True)
                             + jnp.sum(qb * qb, axis=-1, keepdims=True)) / MLA_QK + EPS) * (MLA_QK ** -0.5 * LOG2E)
        q_out[:, base:base + 128] = (qa * q_scale * gq_ref[:, 0:128]).astype(BF16)
        q_out[:, base + 128:base + 256] = _rope_tile(qb * q_scale * gq_ref[:, 128:256],
                                                     cs_ref, s1_ref, s2_ref).astype(BF16)
        kn = kv_raw[:, base:base + 128]
        k_scale = lax.rsqrt((jnp.sum(kn * kn, axis=-1, keepdims=True) + kr_ss) / MLA_QK + EPS)
        k_out[:, base:base + 128] = (kn * k_scale * gk_ref[:, 0:128]).astype(BF16)
        k_out[:, base + 128:base + 256] = _rope_tile(kr * k_scale * gk_ref[:, 128:256],
                                                     cs_ref, s1_ref, s2_ref).astype(BF16)
        v_out[:, h * MLA_V:(h + 1) * MLA_V] = kv_raw[:, base + 128:base + 256].astype(BF16)


MLA_PREP_ROWS = 768


def mla_prep(p, wq, wkv, qn, kvn, gq, gk, rope_tabs):
    const = lambda shape: pl.BlockSpec(shape, lambda i: (0, 0))
    tab = pl.BlockSpec((MLA_PREP_ROWS, 128), lambda i: (i % (S_ALL // MLA_PREP_ROWS), 0))
    row = lambda w: pl.BlockSpec((MLA_PREP_ROWS, w), lambda i: (i, 0))
    return pl.pallas_call(
        _mla_prep_kernel,
        grid=(N_ROWS // MLA_PREP_ROWS,),
        in_specs=[pl.BlockSpec((MLA_PREP_ROWS, 512), lambda i: (i, C_CQ // 512)),
                  pl.BlockSpec((MLA_PREP_ROWS, 256), lambda i: (i, C_CKV // 256)),
                  pl.BlockSpec((MLA_PREP_ROWS, 128), lambda i: (i, C_KR // 128)),
                  const((512, MLA_HEADS * MLA_HP)), const((256, MLA_HEADS * MLA_HP)),
                  const((1, 512)), const((1, 256)), const((1, MLA_HP)), const((1, MLA_HP)),
                  tab, tab, tab],
        out_specs=[row(MLA_HEADS * MLA_HP), row(MLA_HEADS * MLA_HP), row(MLA_HEADS * MLA_V)],
        out_shape=[jax.ShapeDtypeStruct((N_ROWS, MLA_HEADS * MLA_HP), BF16),
                   jax.ShapeDtypeStruct((N_ROWS, MLA_HEADS * MLA_HP), BF16),
                   jax.ShapeDtypeStruct((N_ROWS, MLA_HEADS * MLA_V), BF16)],
        compiler_params=_cparams(("arbitrary",)),
        name="mla_prep",
    )(p, p, p, wq, wkv, qn, kvn, gq, gk, *rope_tabs)


def _softmax_pv(s_list, v_list):
    m = s_list[0].max(axis=-1, keepdims=True)
    for s in s_list[1:]:
        m = jnp.maximum(m, s.max(axis=-1, keepdims=True))
    ps = [jnp.exp2(s - m) for s in s_list]
    l = sum(p.sum(axis=-1, keepdims=True) for p in ps)
    o = sum(jnp.dot(p.astype(BF16), v, preferred_element_type=F32) for p, v in zip(ps, v_list))
    return o / l


MLA_KCHUNK = 512


def _mla_attn_kernel(q_ref, k_ref, v_ref, o_ref):
    q = q_ref[...]

    @pl.when(pl.program_id(2) == 0)
    def _():
        o_ref[...] = _softmax_pv([_dot_nt(q, k_ref[0:T_CTX, :])], [v_ref[0:T_CTX, :]]).astype(BF16)

    @pl.when(pl.program_id(2) > 0)
    def _():
        chunks = [slice(c0, min(c0 + MLA_KCHUNK, S_ALL)) for c0 in range(0, S_ALL, MLA_KCHUNK)]
        o_ref[...] = _softmax_pv([_dot_nt(q, k_ref[c, :]) for c in chunks], [v_ref[c, :] for c in chunks]).astype(BF16)


def mla_attention(q, k, v):
    return pl.pallas_call(
        _mla_attn_kernel,
        grid=(NB, MLA_HEADS, BLK_PER_B),
        in_specs=[pl.BlockSpec((ROWBLK, MLA_HP), lambda b, h, i: (b * BLK_PER_B + i, h)),
                  pl.BlockSpec((S_ALL, MLA_HP), lambda b, h, i: (b, h)),
                  pl.BlockSpec((S_ALL, MLA_V), lambda b, h, i: (b, h))],
        out_specs=pl.BlockSpec((ROWBLK, MLA_V), lambda b, h, i: (b * BLK_PER_B + i, h)),
        out_shape=jax.ShapeDtypeStruct((N_ROWS, MLA_HEADS * MLA_V), BF16),
        compiler_params=_cparams(("arbitrary", "arbitrary", "arbitrary")),
        name="mla_attention",
    )(q, k, v)


N_GRID_ROWS = T_LAT // GRID_W
NA_QROWS = ROWBLK // GRID_W
NA_GROUPS = N_GRID_ROWS // NA_QROWS
NA_WIN_ROWS = NA_KH + NA_QROWS - 1
NA_WIN = NA_WIN_ROWS * GRID_W


def _na_win_row(g):
    return jnp.clip(g * NA_QROWS - NA_KH // 2, 0, N_GRID_ROWS - NA_WIN_ROWS)


def _na_kernel(q_ref, k_ref, v_ref, gq_ref, gk_ref, bias_ref, o_ref, kn_ref, vb_ref):
    st = pl.program_id(2)

    @pl.when(st == 0)
    def _():
        k = k_ref[...]
        kn_ref[...] = (k * lax.rsqrt(jnp.mean(k * k, axis=-1, keepdims=True) + EPS) * gk_ref[...]).astype(BF16)
        vb_ref[...] = v_ref[...].astype(BF16)

    q = q_ref[...]
    q_scale = lax.rsqrt(jnp.mean(q * q, axis=-1, keepdims=True) + EPS) * (NA_HD ** -0.5 * LOG2E)
    qn = (q * q_scale * gq_ref[...]).astype(BF16)
    s_cx = _dot_nt(qn, kn_ref[0:T_CTX, :])
    v_cx = vb_ref[0:T_CTX, :]

    @pl.when(st == 0)
    def _():
        o_ref[...] = _softmax_pv([s_cx], [v_cx]).astype(BF16)

    @pl.when(st > 0)
    def _():
        start = pl.multiple_of(T_CTX + _na_win_row(st - 1) * GRID_W, GRID_W)
        s_nb = _dot_nt(qn, kn_ref[pl.ds(start, NA_WIN), :]) + bias_ref[0, 0]
        o_ref[...] = _softmax_pv([s_nb, s_cx], [vb_ref[pl.ds(start, NA_WIN), :], v_cx]).astype(BF16)


def _na_class(g):
    return jnp.where(g == 0, 0, jnp.where(g == NA_GROUPS - 1, 2, 1))


def na_attention(p, gq, gk, bias_tab):
    return pl.pallas_call(
        _na_kernel,
        grid=(NB, NA_HEADS, BLK_PER_B),
        in_specs=[pl.BlockSpec((ROWBLK, NA_HD), lambda b, h, st: (b * BLK_PER_B + st, C_NAQ // NA_HD + h)),
                  pl.BlockSpec((S_ALL, NA_HD), lambda b, h, st: (b, C_NAK // NA_HD + h)),
                  pl.BlockSpec((S_ALL, NA_HD), lambda b, h, st: (b, C_NAV // NA_HD + h)),
                  pl.BlockSpec((1, NA_HD), lambda b, h, st: (0, 0)),
                  pl.BlockSpec((1, NA_HD), lambda b, h, st: (0, 0)),
                  pl.BlockSpec((1, 1, ROWBLK, NA_WIN),
                               lambda b, h, st: (h, _na_class(jnp.maximum(st - 1, 0)), 0, 0))],
        out_specs=pl.BlockSpec((ROWBLK, NA_HD), lambda b, h, st: (b * BLK_PER_B + st, h)),
        out_shape=jax.ShapeDtypeStruct((N_ROWS, NA_HEADS * NA_HD), BF16),
        scratch_shapes=[pltpu.VMEM((S_ALL, NA_HD), BF16), pltpu.VMEM((S_ALL, NA_HD), BF16)],
        compiler_params=_cparams(("arbitrary", "arbitrary", "arbitrary")),
        name="na_attention",
    )(p, p, p, gq.reshape(1, NA_HD), gk.reshape(1, NA_HD), bias_tab)


def _na_window_tables():
    g_rep = np.array([0, 1, NA_GROUPS - 1])
    r = g_rep[:, None] * NA_QROWS + np.arange(NA_QROWS)[None, :]
    r0 = np.clip(r - NA_KH // 2, 0, N_GRID_ROWS - NA_KH)
    w0 = np.clip(g_rep * NA_QROWS - NA_KH // 2, 0, N_GRID_ROWS - NA_WIN_ROWS)
    kr = w0[:, None] + np.arange(NA_WIN_ROWS)[None, :]
    valid_r = (kr[:, None, :] >= r0[:, :, None]) & (kr[:, None, :] < r0[:, :, None] + NA_KH)
    dr = kr[:, None, :] - r[:, :, None] + NA_KH - 1
    return dr, valid_r


def _na_bias_kernel(toe_ref, o_ref):
    dr, valid_r = _na_window_tables()
    masked = jnp.full((GRID_W, GRID_W), -1e30, F32)
    for cls in range(dr.shape[0]):
        @pl.when(pl.program_id(1) == cls)
        def _(cls=cls):
            for a in range(NA_QROWS):
                for jw in range(NA_WIN_ROWS):
                    blk = toe_ref[0, int(dr[cls, a, jw])] if valid_r[cls, a, jw] else masked
                    o_ref[0, 0, a * GRID_W:(a + 1) * GRID_W, jw * GRID_W:(jw + 1) * GRID_W] = blk


def na_bias_table(rpb):
    c = np.arange(GRID_W)[:, None]
    kc = np.arange(GRID_W)[None, :]
    c0 = np.clip(c - NA_KW // 2, 0, GRID_W - NA_KW)
    valid_c = (kc >= c0) & (kc < c0 + NA_KW)
    dc = kc - c + NA_KW - 1
    onehot = (np.arange(2 * NA_KW - 1)[:, None, None] == dc[None]) & valid_c[None]
    toe = jnp.einsum('hrd,dck->hrck', rpb, jnp.asarray(onehot, F32), precision=HI)
    toe = jnp.where(valid_c[None, None], toe * LOG2E, -1e30)
    n_rows = 2 * NA_KH - 1
    return pl.pallas_call(
        _na_bias_kernel,
        grid=(NA_HEADS, 3),
        in_specs=[pl.BlockSpec((1, n_rows, GRID_W, GRID_W), lambda h, s: (h, 0, 0, 0))],
        out_specs=pl.BlockSpec((1, 1, ROWBLK, NA_WIN), lambda h, s: (h, s, 0, 0)),
        out_shape=jax.ShapeDtypeStruct((NA_HEADS, 3, ROWBLK, NA_WIN), F32),
        compiler_params=_cparams(("arbitrary", "arbitrary")),
        name="na_bias",
    )(toe)


N_TOK = NB * T_LAT
MOE_TM = 512
N_BLK = N_TOK * TOP_K // MOE_TM + N_EXPERTS
N_SLOT = N_BLK * MOE_TM
GATHER_ROWS = 512
COMBINE_TOK = 256


def _row_copies(n_rows, copy_fn):
    def start_all():
        def body(r8, carry):
            for u in range(8):
                copy_fn(r8 * 8 + u).start(priority=u % 2)
            return carry
        lax.fori_loop(0, n_rows // 8, body, 0)

    def wait_all():
        def body(r, carry):
            copy_fn(r).wait()
            return carry
        lax.fori_loop(0, n_rows, body, 0, unroll=8)

    return start_all, wait_all


def _gather_kernel(idx_ref, nu_ref, h_hbm, o_ref, buf_ref, sem):
    i = pl.program_id(0)
    n_steps = nu_ref[0] * (MOE_TM // GATHER_ROWS)

    def copies(step):
        slot = step % 2
        return _row_copies(GATHER_ROWS, lambda r: pltpu.make_async_copy(
            h_hbm.at[pl.ds(idx_ref[step * GATHER_ROWS + r], 1)], buf_ref.at[slot, pl.ds(r, 1)], sem.at[slot]))

    @pl.when(i == 0)
    def _():
        copies(i)[0]()

    @pl.when(i + 1 < n_steps)
    def _():
        copies(i + 1)[0]()

    @pl.when(i < n_steps)
    def _():
        copies(i)[1]()
        o_ref[...] = buf_ref[i % 2].astype(BF16)

    @pl.when(i >= n_steps)
    def _():
        o_ref[...] = jnp.zeros_like(o_ref)


def moe_gather(h2, slot_row, n_used):
    return pl.pallas_call(
        _gather_kernel,
        grid_spec=pltpu.PrefetchScalarGridSpec(
            num_scalar_prefetch=2,
            grid=(N_SLOT // GATHER_ROWS,),
            in_specs=[pl.BlockSpec(memory_space=pl.ANY)],
            out_specs=pl.BlockSpec((GATHER_ROWS, D), lambda i, idx, nu: (i, 0)),
            scratch_shapes=[pltpu.VMEM((2, GATHER_ROWS, D), F32), pltpu.SemaphoreType.DMA((2,))]),
        out_shape=jax.ShapeDtypeStruct((N_SLOT, D), BF16),
        compiler_params=_cparams(("arbitrary",)),
        name="moe_gather",
    )(slot_row, n_used, h2)


def _combine_kernel(slot_ref, y_hbm, x_ref, gate_ref, g2_ref, o_ref, buf_ref, sem):
    i = pl.program_id(0)

    def copies(step):
        slot = step % 2
        base = step * COMBINE_TOK * TOP_K
        return _row_copies(COMBINE_TOK * TOP_K, lambda r: pltpu.make_async_copy(
            y_hbm.at[pl.ds(slot_ref[base + r], 1)], buf_ref.at[slot, pl.ds(r, 1)], sem.at[slot]))

    @pl.when(i == 0)
    def _():
        copies(i)[0]()

    @pl.when(i + 1 < N_TOK // COMBINE_TOK)
    def _():
        copies(i + 1)[0]()

    copies(i)[1]()
    b = i // (T_LAT // COMBINE_TOK)
    rows = buf_ref[i % 2]
    f = gate_ref[:, 0:1] * rows[0:COMBINE_TOK] + gate_ref[:, 1:2] * rows[COMBINE_TOK:2 * COMBINE_TOK]
    o_ref[...] = x_ref[...] + g2_ref[pl.ds(b, 1), :] * f


def moe_combine(yb, x, slots_km, gates, modtab):
    per_b = T_LAT // COMBINE_TOK
    x_map = lambda i, s: ((i // per_b) * (S_ALL // COMBINE_TOK) + T_CTX // COMBINE_TOK + i % per_b, 0)
    return pl.pallas_call(
        _combine_kernel,
        grid_spec=pltpu.PrefetchScalarGridSpec(
            num_scalar_prefetch=1,
            grid=(N_TOK // COMBINE_TOK,),
            in_specs=[pl.BlockSpec(memory_space=pl.ANY),
                      pl.BlockSpec((COMBINE_TOK, D), x_map),
                      pl.BlockSpec((COMBINE_TOK, TOP_K), lambda i, s: (i, 0)),
                      pl.BlockSpec((8, D), lambda i, s: (0, 5))],
            out_specs=pl.BlockSpec((COMBINE_TOK, D), lambda i, s: (i, 0)),
            scratch_shapes=[pltpu.VMEM((2, COMBINE_TOK * TOP_K, D), F32), pltpu.SemaphoreType.DMA((2,))]),
        out_shape=jax.ShapeDtypeStruct((N_TOK, D), F32),
        compiler_params=_cparams(("arbitrary",)),
        name="moe_combine",
    )(slots_km, yb, x, gates, modtab)


def moe_routing(logits):
    top_v, top_e = lax.top_k(logits, TOP_K)
    gates = jax.nn.softmax(top_v, axis=-1)
    flat_e = top_e.reshape(-1)
    onehot = (flat_e[:, None] == jnp.arange(N_EXPERTS)[None, :]).astype(jnp.int32)
    rank = jnp.take_along_axis(jnp.cumsum(onehot, axis=0) - onehot, flat_e[:, None], axis=1)[:, 0]
    counts = onehot.sum(axis=0)
    padded = (counts + MOE_TM - 1) // MOE_TM * MOE_TM
    pend = jnp.cumsum(padded)
    slot = ((pend - padded)[flat_e] + rank).astype(jnp.int32)
    tok = jnp.arange(N_TOK, dtype=jnp.int32)
    tok_row = (tok // T_LAT) * S_ALL + T_CTX + tok % T_LAT
    slot_row = jnp.zeros((N_SLOT,), jnp.int32).at[slot].set(jnp.repeat(tok_row, TOP_K))
    n_used = (pend[-1] // MOE_TM).astype(jnp.int32)
    blk = jnp.arange(N_BLK)
    used = blk < n_used
    blk_e = jnp.sum(jnp.minimum(blk, n_used - 1)[:, None] * MOE_TM >= pend[None, :], axis=1).astype(jnp.int32)
    fresh = (used & ((blk == 0) | (blk_e != jnp.roll(blk_e, 1)))).astype(jnp.int32)
    run_idx = jnp.cumsum(fresh) - 1
    n_runs = fresh.sum()
    has = counts > 0
    ids = jnp.arange(N_EXPERTS)
    first_e = jnp.min(jnp.where(has, ids, N_EXPERTS))
    later = jnp.where(has[None, :] & (ids[None, :] > ids[:, None]), ids[None, :], N_EXPERTS).min(axis=1)
    next_of = jnp.where(later < N_EXPERTS, later, first_e)
    sched = (blk_e, fresh, run_idx.astype(jnp.int32), next_of[blk_e].astype(jnp.int32),
             (run_idx == n_runs - 1).astype(jnp.int32),
             jnp.where(used, jnp.clip(((pend - padded) + counts)[blk_e] - blk * MOE_TM, 0, MOE_TM), 0).astype(jnp.int32),
             jnp.stack([n_used, n_runs]).astype(jnp.int32))
    slots_km = slot.reshape(N_TOK // COMBINE_TOK, COMBINE_TOK, TOP_K).transpose(0, 2, 1).reshape(-1)
    return slot_row, sched, n_used.reshape(1), slots_km, gates


def _pad_cols(w, width):
    return jnp.pad(w, [(0, 0)] * (w.ndim - 1) + [(0, width - w.shape[-1])])


W_IN_SEGMENTS = ((C_Q, 0, 2048), (C_Z, 2064, 512), (C_XBC, 2576, 1024), (C_CQ, 3616, MLA_Q_RANK),
                 (C_CKV, 4064, MLA_KV_RANK), (C_KR, 4224, MLA_ROPE), (C_NAQ, 4288, 1536),
                 (C_GATE + G_I, 2048, 16), (C_GATE + G_DT, 3600, 16))
D_IN = 5824


def _w_in_layout_kernel(w_ref, o_ref):
    o_ref[...] = jnp.zeros_like(o_ref)
    for dst, src, width in W_IN_SEGMENTS:
        o_ref[0, :, dst:dst + width] = w_ref[0, :, src:src + width]


def layout_w_in(w):
    rows = 256
    return pl.pallas_call(
        _w_in_layout_kernel,
        grid=(DEPTH, D // rows),
        in_specs=[pl.BlockSpec((1, rows, D_IN), lambda l, i: (l, i, 0))],
        out_specs=pl.BlockSpec((1, rows, D_INP), lambda l, i: (l, i, 0)),
        out_shape=jax.ShapeDtypeStruct((DEPTH, D, D_INP), F32),
        compiler_params=_cparams(("arbitrary", "arbitrary")),
        name="w_in_layout",
    )(w)


def rope_tables():
    t = np.arange(T_LAT)
    n_freq = MLA_ROPE // 4
    freqs = ROPE_THETA ** (-jnp.arange(n_freq, dtype=F32) / n_freq)
    row = jnp.asarray(t // GRID_W, F32)
    col = jnp.asarray(t % GRID_W, F32)
    ang = jnp.concatenate([row[:, None] * freqs, col[:, None] * freqs], axis=-1)
    cos, sin = jnp.cos(ang), jnp.sin(ang)
    half = MLA_ROPE // 2
    zc = jnp.zeros((T_LAT, 128 - MLA_ROPE), F32)
    zh = jnp.zeros((T_LAT, half), F32)
    cs = jnp.concatenate([cos, cos, zc], axis=1)
    s1 = jnp.concatenate([-sin, zh, zc], axis=1)
    s2 = jnp.concatenate([zh, sin, zc], axis=1)
    ident = jnp.concatenate([jnp.ones((T_CTX, MLA_ROPE), F32), jnp.zeros((T_CTX, 128 - MLA_ROPE), F32)], axis=1)
    zeros = jnp.zeros((T_CTX, 128), F32)
    return (jnp.concatenate([ident, cs], axis=0), jnp.concatenate([zeros, s1], axis=0),
            jnp.concatenate([zeros, s2], axis=0))


def _gate_vectors(i_bias, f_bias, dt_bias, a_log):
    used = jnp.concatenate([i_bias.reshape(-1), f_bias.reshape(-1), dt_bias.reshape(-1)])
    alog = jnp.concatenate([jnp.zeros((G_DT,), F32), a_log.reshape(-1)])
    padc = lambda u: jnp.pad(u, (0, 128 - G_USED)).reshape(1, 128)
    return padc(used), used.reshape(G_USED, 1), padc(alog), alog.reshape(G_USED, 1)


def kernel(x, c, ctx, c_ctx, mod_w, mod_b, norm1, w_in, w_out, ml_i_bias, ml_f_bias, ml_norm, ssd_conv_w, ssd_conv_b, ssd_dt_bias, ssd_A_log, ssd_D, ssd_norm, mla_q_norm, mla_w_qb, mla_kv_norm, mla_w_kvb, mla_gq, mla_gk, na_gq, na_gk, na_rpb, norm2, ffn_w1, ffn_w3, ffn_w2, moe_router, moe_w1, moe_w3, moe_w2):
    xs = jnp.concatenate([ctx, x], axis=1).reshape(N_ROWS, D)
    c_all = jnp.concatenate([c, c_ctx[None, :], jnp.zeros((8 - NB - 1, D), F32)], axis=0)
    mod_all = modulation(c_all, mod_w, mod_b)
    tabs = rope_tables()
    w_in_p = layout_w_in(w_in)
    out = None
    for l in range(DEPTH):
        modtab = mod_all[l]
        h = rms_modulate(xs, norm1[l], modtab, 0)
        p = matmul(h, [w_in_p], tm=1536, tn=1024, layer=l, name="w_in")
        gates_t = p[:, C_GATE:C_GATE + G_USED].reshape(N_ROWS // CHUNK, CHUNK, G_USED).transpose(0, 2, 1)
        biasc, biasr, alogc, alogr = _gate_vectors(ml_i_bias[l], ml_f_bias[l], ssd_dt_bias[l], ssd_A_log[l])

        hf, hb = mlstm_scan(p, gates_t, biasc, biasr)
        ml = mlstm_finish(hf, hb, p, ml_norm[l])

        xbc = ssd_conv(p, ssd_conv_w[l], ssd_conv_b[l])
        yf, yb = ssd_scan(xbc, p, gates_t, biasc, biasr, alogc, alogr)
        ss = ssd_finish(yf, yb, xbc, p, ssd_D[l], ssd_norm[l])

        wq = jnp.pad(mla_w_qb[l].reshape(MLA_Q_RANK, MLA_HEADS, MLA_QK),
                     ((0, 512 - MLA_Q_RANK), (0, 0), (0, MLA_HP - MLA_QK))).reshape(512, MLA_HEADS * MLA_HP)
        wkv = jnp.pad(mla_w_kvb[l], ((0, 256 - MLA_KV_RANK), (0, 0)))
        pad1 = lambda u, w: jnp.pad(u, (0, w - u.shape[0])).reshape(1, w)
        q, k, v = mla_prep(p, wq, wkv, pad1(mla_q_norm[l], 512), pad1(mla_kv_norm[l], 256),
                           pad1(mla_gq[l], MLA_HP), pad1(mla_gk[l], MLA_HP), tabs)
        la = mla_attention(q, k, v)

        na = na_attention(p, na_gq[l], na_gk[l], na_bias_table(na_rpb[l]))

        xs = matmul([ml, ss, la, na], [w_out], mode="resid", res=xs, modtab=modtab, gate_col=2,
                    tm=1536, layer=l, name="w_out")

        if l % 2 == 0:
            h2 = rms_modulate(xs, norm2[l], modtab, 3)
            hid = matmul(h2, [ffn_w1[l // 2], ffn_w3[l // 2]], mode="swiglu", out_dtype=BF16, tm=1536,
                         name="ffn_up")
            xs = matmul(hid, [ffn_w2[l // 2]], mode="resid", res=xs, modtab=modtab, gate_col=5,
                        w_buffers=1, name="ffn_down")
        else:
            h2, logits = rms_modulate(xs, norm2[l], modtab, 3, router=_pad_cols(moe_router[l // 2], 128))
            lat = logits.reshape(NB, S_ALL, 128)[:, T_CTX:, :N_EXPERTS].reshape(N_TOK, N_EXPERTS)
            slot_row, sched, n_used, slots_km, gates = moe_routing(lat)
            xb = moe_gather(h2, slot_row, n_used)
            hid = grouped_matmul(xb, [moe_w1[l // 2], moe_w3[l // 2]], sched, mode="swiglu", out_dtype=BF16,
                                 tm=MOE_TM, tn=1024, name="moe_up")
            yb_ = grouped_matmul(hid, [moe_w2[l // 2]], sched, tm=MOE_TM, tn=512, name="moe_down")
            out = moe_combine(yb_, xs, slots_km, gates, modtab)
    return out.reshape(NB, T_LAT, D)
```

```python
from functools import partial

import numpy as np
import jax
import jax.numpy as jnp
from jax import lax
from jax.experimental import pallas as pl
from jax.experimental.pallas import tpu as pltpu

F32 = jnp.float32
BF16 = jnp.bfloat16
HI = lax.Precision.HIGHEST

D = 2048
NB = 4
T_LAT = 2048
T_CTX = 256
S_ALL = T_LAT + T_CTX
N_ROWS = NB * S_ALL
DEPTH = 2
GRID_W = 64
EPS = 1e-6
LOG2E = 1.4426950408889634

ROWBLK = 256
BLK_PER_B = S_ALL // ROWBLK
CHUNK = 128
CH_PER_B = S_ALL // CHUNK
CH_CTX = T_CTX // CHUNK

ML_HEADS, ML_HD = 4, 128
SSD_HEADS, SSD_HD, SSD_GROUPS, SSD_STATE, SSD_CONV = 8, 64, 2, 128, 5
SSD_W = SSD_HEADS * SSD_HD
MLA_HEADS, MLA_NOPE, MLA_ROPE, MLA_V = 4, 128, 64, 128
MLA_QK = MLA_NOPE + MLA_ROPE
MLA_Q_RANK, MLA_KV_RANK = 448, 160
MLA_HP = 256
ROPE_THETA = 10000.0
NA_HEADS, NA_HD, NA_KH, NA_KW = 4, 128, 8, 16
N_EXPERTS, TOP_K = 8, 2

C_Q, C_K, C_V, C_O = 0, 512, 1024, 1536
C_Z = 2048
C_XBC = 2560
C_CQ, C_CKV, C_KR = 3584, 4096, 4352
C_NAQ, C_NAK, C_NAV = 4480, 4992, 5504
C_GATE = 6016
D_INP = 6144
G_I, G_F, G_DT, G_USED = 0, 8, 16, 32

VMEM_LIMIT = 56 * 1024 * 1024


def _cparams(sem):
    return pltpu.CompilerParams(dimension_semantics=sem, vmem_limit_bytes=VMEM_LIMIT)


def _mod_index(blk256):
    return jnp.where(blk256 % BLK_PER_B == 0, NB, blk256 // BLK_PER_B)


def _dot_nt(a, b):
    return lax.dot_general(a, b, (((1,), (1,)), ((), ())), preferred_element_type=F32)


def _dot_tn(a, b):
    return lax.dot_general(a, b, (((0,), (0,)), ((), ())), preferred_element_type=F32)


def _silu(x):
    return x * jax.nn.sigmoid(x)


def _softplus(x):
    return jnp.maximum(x, 0.0) + jnp.log(1.0 + jnp.exp(-jnp.abs(x)))


def _mod_kernel(c_ref, w_ref, b_ref, o_ref):
    o_ref[0] = jnp.dot(_silu(c_ref[...]), w_ref[0], preferred_element_type=F32) + b_ref[0]


def modulation(c_all, mod_w, mod_b):
    tn = 1024
    n_out = mod_w.shape[-1]
    return pl.pallas_call(
        _mod_kernel,
        grid=(DEPTH, n_out // tn),
        in_specs=[pl.BlockSpec((8, D), lambda l, j: (0, 0)),
                  pl.BlockSpec((1, D, tn), lambda l, j: (l, 0, j)),
                  pl.BlockSpec((1, 1, tn), lambda l, j: (l, 0, j))],
        out_specs=pl.BlockSpec((1, 8, tn), lambda l, j: (l, 0, j)),
        out_shape=jax.ShapeDtypeStruct((DEPTH, 8, n_out), F32),
        compiler_params=_cparams(("arbitrary", "arbitrary")),
        name="modulation",
    )(c_all, mod_w, mod_b.reshape(DEPTH, 1, n_out))


ELT_ROWS = 768


def _rmsmod_kernel(x_ref, g_ref, sh_ref, sc_ref, *rest, with_router):
    for s in range(ELT_ROWS // ROWBLK):
        midx = _mod_index(pl.program_id(0) * (ELT_ROWS // ROWBLK) + s)
        rows = slice(s * ROWBLK, (s + 1) * ROWBLK)
        x = x_ref[rows, :]
        y = x * lax.rsqrt(jnp.mean(x * x, axis=-1, keepdims=True) + EPS) * g_ref[...]
        h = y * (1.0 + sc_ref[pl.ds(midx, 1), :]) + sh_ref[pl.ds(midx, 1), :]
        if with_router:
            r_ref, o_ref, lg_ref = rest
            o_ref[rows, :] = h
            lg_ref[rows, :] = jnp.dot(h, r_ref[...], precision=HI, preferred_element_type=F32)
        else:
            (o_ref,) = rest
            o_ref[rows, :] = h.astype(BF16)


def rms_modulate(x, g, modtab, which_shift, router=None):
    with_router = router is not None
    in_specs = [pl.BlockSpec((ELT_ROWS, D), lambda i: (i, 0)),
                pl.BlockSpec((1, D), lambda i: (0, 0)),
                pl.BlockSpec((8, D), lambda i: (0, which_shift)),
                pl.BlockSpec((8, D), lambda i: (0, which_shift + 1))]
    args = [x, g.reshape(1, D), modtab, modtab]
    if with_router:
        in_specs.append(pl.BlockSpec((D, 128), lambda i: (0, 0)))
        args.append(router)
        out_shape = [jax.ShapeDtypeStruct((N_ROWS, D), F32), jax.ShapeDtypeStruct((N_ROWS, 128), F32)]
        out_specs = [pl.BlockSpec((ELT_ROWS, D), lambda i: (i, 0)), pl.BlockSpec((ELT_ROWS, 128), lambda i: (i, 0))]
    else:
        out_shape = jax.ShapeDtypeStruct((N_ROWS, D), BF16)
        out_specs = pl.BlockSpec((ELT_ROWS, D), lambda i: (i, 0))
    return pl.pallas_call(
        partial(_rmsmod_kernel, with_router=with_router),
        grid=(N_ROWS // ELT_ROWS,),
        in_specs=in_specs, out_specs=out_specs, out_shape=out_shape,
        compiler_params=_cparams(("arbitrary",)),
        name="rms_modulate_router" if with_router else "rms_modulate",
    )(*args)


MM_COLS = 512


def _mm_compute(mode, a, w_tiles, o_ref, i=None, tm=None, res_ref=None, gate_ref=None, rows=slice(None)):
    tn = o_ref.shape[1]
    for c0 in range(0, tn, MM_COLS):
        cols = slice(c0, min(c0 + MM_COLS, tn))
        accs = [jnp.dot(a, w[:, cols], preferred_element_type=F32) for w in w_tiles]
        if mode == "plain":
            o_ref[rows, cols] = accs[0].astype(o_ref.dtype)
        elif mode == "swiglu":
            o_ref[rows, cols] = (_silu(accs[0]) * accs[1]).astype(o_ref.dtype)
        else:
            for s in range(tm // ROWBLK):
                midx = _mod_index(i * (tm // ROWBLK) + s)
                rows = slice(s * ROWBLK, (s + 1) * ROWBLK)
                o_ref[rows, cols] = res_ref[rows, cols] + gate_ref[pl.ds(midx, 1), cols] * accs[0][rows, :]


def _mm_kernel(*refs, mode, tm, na, stacked):
    nw = 2 if mode == "swiglu" else 1
    a_refs = refs[:na]
    w_refs = refs[na:na + nw]
    pos = na + nw
    res_ref = gate_ref = None
    if mode == "resid":
        res_ref, gate_ref = refs[pos:pos + 2]
        pos += 2
    o_ref = refs[pos]
    ws_refs = refs[pos + 1:]
    i = pl.program_id(1)

    @pl.when(i == 0)
    def _():
        for w_ref, ws_ref in zip(w_refs, ws_refs):
            ws_ref[...] = (w_ref[0] if stacked else w_ref[...]).astype(BF16)

    a = a_refs[0][...] if na == 1 else jnp.concatenate([a_ref[...] for a_ref in a_refs], axis=1)
    _mm_compute(mode, a, ws_refs, o_ref, i, tm, res_ref, gate_ref)


def matmul(a, ws, mode="plain", out_dtype=F32, res=None, modtab=None, gate_col=None,
           tm=768, tn=512, w_buffers=2, layer=None, name="matmul"):
    pieces = list(a) if isinstance(a, (list, tuple)) else [a]
    m = pieces[0].shape[0]
    kdim = sum(p.shape[1] for p in pieces)
    n = ws[0].shape[-1]
    wmode = {} if w_buffers == 2 else dict(pipeline_mode=pl.Buffered(w_buffers))
    in_specs = [pl.BlockSpec((tm, p.shape[1]), lambda j, i: (i, 0)) for p in pieces]
    if layer is None:
        in_specs += [pl.BlockSpec((kdim, tn), lambda j, i: (0, j), **wmode) for _ in ws]
    else:
        in_specs += [pl.BlockSpec((1, kdim, tn), lambda j, i: (layer, 0, j), **wmode) for _ in ws]
    args = [*pieces, *ws]
    if mode == "resid":
        in_specs += [pl.BlockSpec((tm, tn), lambda j, i: (i, j)),
                     pl.BlockSpec((8, tn), lambda j, i: (0, gate_col * (D // tn) + j))]
        args += [res, modtab]
    return pl.pallas_call(
        partial(_mm_kernel, mode=mode, tm=tm, na=len(pieces), stacked=layer is not None),
        grid=(n // tn, m // tm),
        in_specs=in_specs,
        out_specs=pl.BlockSpec((tm, tn), lambda j, i: (i, j)),
        out_shape=jax.ShapeDtypeStruct((m, n), out_dtype),
        scratch_shapes=[pltpu.VMEM((kdim, tn), BF16) for _ in ws],
        compiler_params=_cparams(("arbitrary", "arbitrary")),
        name=name,
    )(*args)


def _gmm_kernel(be_ref, fresh_ref, run_ref, nexte_ref, lastrun_ref, fill_ref, meta_ref, a_ref, *rest,
                mode, nw, tn):
    w_hbm = rest[:nw]
    o_ref = rest[nw]
    stage_ref, ws_ref, sem = rest[nw + 1:]
    j = pl.program_id(0)
    i = pl.program_id(1)
    n_runs = meta_ref[1]
    half = a_ref.shape[0] // 2

    def w_copy(widx, e, jj):
        col = pl.multiple_of(jj * tn, tn)
        return pltpu.make_async_copy(w_hbm[widx].at[e, :, pl.ds(col, tn)], stage_ref.at[widx], sem.at[widx])

    @pl.when(fresh_ref[i] == 1)
    def _():
        @pl.when(j * n_runs + run_ref[i] == 0)
        def _():
            for widx in range(nw):
                w_copy(widx, be_ref[i], j).start()

        for widx in range(nw):
            w_copy(widx, be_ref[i], j).wait()
            ws_ref[widx] = stage_ref[widx].astype(BF16)
        is_last = lastrun_ref[i] == 1

        @pl.when(jnp.logical_not(jnp.logical_and(is_last, j == pl.num_programs(0) - 1)))
        def _():
            jn = jnp.where(is_last, j + 1, j)
            for widx in range(nw):
                w_copy(widx, nexte_ref[i], jn).start()

    w_tiles = [ws_ref.at[widx] for widx in range(nw)]
    fill = fill_ref[i]

    @pl.when(fill > half)
    def _():
        _mm_compute(mode, a_ref[...], w_tiles, o_ref)

    @pl.when(jnp.logical_and(fill > 0, fill <= half))
    def _():
        _mm_compute(mode, a_ref[0:half, :], w_tiles, o_ref, rows=slice(0, half))
        o_ref[half:, :] = jnp.zeros((half, o_ref.shape[1]), o_ref.dtype)

    @pl.when(fill == 0)
    def _():
        o_ref[...] = jnp.zeros_like(o_ref)


def grouped_matmul(a, ws, sched, mode="plain", out_dtype=F32, tm=512, tn=512, name="grouped_matmul"):
    m, kdim = a.shape
    n = ws[0].shape[-1]
    nw = len(ws)
    row = lambda j, i, be, fr, ru, ne, lr, fill, meta: jnp.minimum(i, meta[0] - 1)
    grid_spec = pltpu.PrefetchScalarGridSpec(
        num_scalar_prefetch=len(sched),
        grid=(n // tn, m // tm),
        in_specs=[pl.BlockSpec((tm, kdim), lambda *s: (row(*s), 0))] + [pl.BlockSpec(memory_space=pl.ANY)] * nw,
        out_specs=pl.BlockSpec((tm, tn), lambda j, i, *_: (i, j)),
        scratch_shapes=[pltpu.VMEM((nw, kdim, tn), F32), pltpu.VMEM((nw, kdim, tn), BF16),
                        pltpu.SemaphoreType.DMA((nw,))])
    return pl.pallas_call(
        partial(_gmm_kernel, mode=mode, nw=nw, tn=tn),
        grid_spec=grid_spec,
        out_shape=jax.ShapeDtypeStruct((m, n), out_dtype),
        compiler_params=_cparams(("arbitrary", "arbitrary")),
        name=name,
    )(*sched, a, *ws)


def _fwd_chunk(i):
    return i


def _bwd_chunk(i):
    return jnp.where(i < CH_CTX, CH_CTX - 1 - i, CH_PER_B + CH_CTX - 1 - i)


def _tri_masks():
    r = lax.broadcasted_iota(jnp.int32, (CHUNK, CHUNK), 0)
    c = lax.broadcasted_iota(jnp.int32, (CHUNK, CHUNK), 1)
    return c <= r, c >= r


def _mlstm_kernel(qf_ref, kf_ref, vf_ref, gcf_ref, grf_ref,
                  qb_ref, kb_ref, vb_ref, gcb_ref, grb_ref,
                  biasc_ref, biasr_ref, hf_ref, hb_ref, c_ref, n_ref, m_ref):
    @pl.when(pl.program_id(1) == 0)
    def _():
        c_ref[...] = jnp.zeros_like(c_ref)
        n_ref[...] = jnp.zeros_like(n_ref)
        m_ref[...] = jnp.zeros_like(m_ref)

    low, upp = _tri_masks()
    lowf, uppf = low.astype(F32), upp.astype(F32)
    streams = ((0, qf_ref, kf_ref, vf_ref, gcf_ref, grf_ref, hf_ref, upp, lowf, uppf, CHUNK - 1),
               (1, qb_ref, kb_ref, vb_ref, gcb_ref, grb_ref, hb_ref, low, uppf, lowf, 0))
    for d, q_ref, k_ref, v_ref, gc_ref, gr_ref, h_ref, mask_t, tri_c, tri_r, last in streams:
        ac = gc_ref[...] + biasc_ref[...]
        ar = gr_ref[0] + biasr_ref[...]
        lfc = -_softplus(-ac)
        lfr = -_softplus(-ar)
        bc_all = jnp.dot(tri_c, lfc, precision=HI, preferred_element_type=F32)
        br_all = jnp.dot(lfr, tri_r, precision=HI, preferred_element_type=F32)
        for h in range(ML_HEADS):
            s_idx = d * ML_HEADS + h
            ji, jf = G_I + s_idx, G_F + s_idx
            cols = slice(h * ML_HD, (h + 1) * ML_HD)
            q = q_ref[:, cols] * (ML_HD ** -0.5)
            k = k_ref[:, cols]
            vt = v_ref[:, cols].T
            b_row = br_all[jf:jf + 1, :]
            i_row = ar[ji:ji + 1, :]
            key_col = ac[:, ji:ji + 1] - bc_all[:, jf:jf + 1]
            cmat = c_ref[s_idx]
            nvec = n_ref[s_idx]
            m_prev = m_ref[s_idx][:, 0:1]
            dmat = jnp.where(mask_t, b_row + key_col, -1e30)
            inter = b_row + m_prev
            mt = jnp.maximum(inter, jnp.max(dmat, axis=0, keepdims=True))
            w_intra = jnp.exp(dmat - mt)
            w_state = jnp.exp(inter - mt)
            st = _dot_nt(k, q) * w_intra
            num = jnp.dot(vt, st, preferred_element_type=F32) + w_state * _dot_nt(cmat, q)
            den = jnp.sum(st, axis=0, keepdims=True) + w_state * _dot_nt(nvec, q)
            h_ref[:, cols] = (num / jnp.maximum(jnp.abs(den), jnp.exp(-mt))).T
            bl = b_row[:, last:last + 1]
            g = bl - b_row + i_row
            m_new = jnp.maximum(bl + m_prev, jnp.max(g, axis=1, keepdims=True))
            wg = jnp.exp(g - m_new)
            wc = jnp.exp(bl + m_prev - m_new)
            c_ref[s_idx] = wc * cmat + jnp.dot(vt * wg, k, preferred_element_type=F32)
            n_ref[s_idx] = wc * nvec + jnp.dot(wg, k, preferred_element_type=F32)
            m_ref[s_idx] = jnp.broadcast_to(m_new, (1, 128))


def mlstm_scan(p, gates_t, biasc, biasr):
    def rows(chunk_fn):
        return lambda b, i: b * CH_PER_B + chunk_fn(i)

    def stream_specs(chunk_fn):
        r = rows(chunk_fn)
        return [pl.BlockSpec((CHUNK, 512), lambda b, i: (r(b, i), C_Q // 512)),
                pl.BlockSpec((CHUNK, 512), lambda b, i: (r(b, i), C_K // 512)),
                pl.BlockSpec((CHUNK, 512), lambda b, i: (r(b, i), C_V // 512)),
                pl.BlockSpec((CHUNK, 128), lambda b, i: (r(b, i), C_GATE // 128)),
                pl.BlockSpec((1, G_USED, CHUNK), lambda b, i: (r(b, i), 0, 0))]

    rf, rb = rows(_fwd_chunk), rows(_bwd_chunk)
    n_streams = 2 * ML_HEADS
    return pl.pallas_call(
        _mlstm_kernel,
        grid=(NB, CH_PER_B),
        in_specs=stream_specs(_fwd_chunk) + stream_specs(_bwd_chunk) + [
            pl.BlockSpec((1, 128), lambda b, i: (0, 0)),
            pl.BlockSpec((G_USED, 1), lambda b, i: (0, 0))],
        out_specs=[pl.BlockSpec((CHUNK, 512), lambda b, i: (rf(b, i), 0)),
                   pl.BlockSpec((CHUNK, 512), lambda b, i: (rb(b, i), 0))],
        out_shape=[jax.ShapeDtypeStruct((N_ROWS, 512), F32)] * 2,
        scratch_shapes=[pltpu.VMEM((n_streams, ML_HD, ML_HD), F32),
                        pltpu.VMEM((n_streams, 1, ML_HD), F32),
                        pltpu.VMEM((n_streams, 1, 128), F32)],
        compiler_params=_cparams(("arbitrary", "arbitrary")),
        name="mlstm_scan",
    )(p, p, p, p, gates_t, p, p, p, p, gates_t, biasc, biasr)


def _mlstm_finish_kernel(hf_ref, hb_ref, o_ref, g_ref, out_ref):
    for h in range(ML_HEADS):
        cols = slice(h * ML_HD, (h + 1) * ML_HD)
        x = hf_ref[:, cols] + hb_ref[:, cols]
        y = x * lax.rsqrt(jnp.mean(x * x, axis=-1, keepdims=True) + EPS) * g_ref[:, cols]
        out_ref[:, cols] = (y * jax.nn.sigmoid(o_ref[:, cols])).astype(BF16)


def mlstm_finish(hf, hb, p, norm_g):
    blk = lambda c: pl.BlockSpec((ELT_ROWS, 512), lambda i: (i, c))
    return pl.pallas_call(
        _mlstm_finish_kernel,
        grid=(N_ROWS // ELT_ROWS,),
        in_specs=[blk(0), blk(0), blk(C_O // 512), pl.BlockSpec((1, 512), lambda i: (0, 0))],
        out_specs=blk(0),
        out_shape=jax.ShapeDtypeStruct((N_ROWS, 512), BF16),
        compiler_params=_cparams(("arbitrary",)),
        name="mlstm_finish",
    )(hf, hb, p, norm_g.reshape(1, 512))


_PAD = 8


def _ssd_conv_kernel(u_ref, w_ref, b_ref, o_ref, pad_ref):
    zeros = jnp.zeros((_PAD, u_ref.shape[1]), F32)
    segs = ((0, T_CTX, _PAD), (T_CTX, T_LAT, 2 * _PAD + T_CTX))
    pad_ref[0:_PAD] = zeros
    pad_ref[_PAD + T_CTX:2 * _PAD + T_CTX] = zeros
    pad_ref[2 * _PAD + S_ALL:3 * _PAD + S_ALL] = zeros
    for src, n, dst in segs:
        pad_ref[dst:dst + n] = u_ref[src:src + n]
    for src, n, dst in segs:
        acc = jnp.broadcast_to(b_ref[...], (n, u_ref.shape[1]))
        for j in range(SSD_CONV):
            lo = dst + j - SSD_CONV // 2
            acc = acc + w_ref[j:j + 1, :] * pad_ref[lo:lo + n]
        o_ref[src:src + n] = _silu(acc)


def ssd_conv(p, conv_w, conv_b):
    tc = 256
    return pl.pallas_call(
        _ssd_conv_kernel,
        grid=(NB, 1024 // tc),
        in_specs=[pl.BlockSpec((S_ALL, tc), lambda b, c: (b, C_XBC // tc + c)),
                  pl.BlockSpec((SSD_CONV, tc), lambda b, c: (0, c)),
                  pl.BlockSpec((1, tc), lambda b, c: (0, c))],
        out_specs=pl.BlockSpec((S_ALL, tc), lambda b, c: (b, c)),
        out_shape=jax.ShapeDtypeStruct((N_ROWS, 1024), F32),
        scratch_shapes=[pltpu.VMEM((S_ALL + 3 * _PAD, tc), F32)],
        compiler_params=_cparams(("arbitrary", "arbitrary")),
        name="ssd_conv",
    )(p, conv_w, conv_b.reshape(1, 1024))


def _ssd_kernel(xf_ref, bf_ref, cf_ref, gcf_ref, grf_ref,
                xb_ref, bb_ref, cb_ref, gcb_ref, grb_ref,
                biasc_ref, biasr_ref, alogc_ref, alogr_ref, yf_ref, yb_ref, st_ref):
    @pl.when(pl.program_id(1) == 0)
    def _():
        st_ref[...] = jnp.zeros_like(st_ref)

    low, upp = _tri_masks()
    lowf, uppf = low.astype(F32), upp.astype(F32)
    a_c = -jnp.exp(alogc_ref[...])
    a_r = -jnp.exp(alogr_ref[...])
    streams = ((0, xf_ref, bf_ref, cf_ref, gcf_ref, grf_ref, yf_ref, upp, lowf, uppf, CHUNK - 1),
               (1, xb_ref, bb_ref, cb_ref, gcb_ref, grb_ref, yb_ref, low, uppf, lowf, 0))
    for d, x_ref, b_ref, c_ref, gc_ref, gr_ref, y_ref, mask_t, tri_c, tri_r, last in streams:
        dtc_all = _softplus(gc_ref[...] + biasc_ref[...])
        dtr_all = _softplus(gr_ref[0] + biasr_ref[...])
        negc_all = -jnp.dot(tri_c, dtc_all * a_c, precision=HI, preferred_element_type=F32)
        cumr_all = jnp.dot(dtr_all * a_r, tri_r, precision=HI, preferred_element_type=F32)
        xt = x_ref[...].T
        yts = []
        for g in range(SSD_GROUPS):
            gcols = slice(g * SSD_STATE, (g + 1) * SSD_STATE)
            bm = b_ref[:, gcols]
            cm = c_ref[:, gcols]
            bc = _dot_nt(bm, cm)
            for hh in range(SSD_HEADS // SSD_GROUPS):
                h = g * (SSD_HEADS // SSD_GROUPS) + hh
                j = G_DT + d * SSD_HEADS + h
                s_idx = d * SSD_HEADS + h
                cum = cumr_all[j:j + 1, :]
                xd = xt[h * SSD_HD:(h + 1) * SSD_HD, :] * dtr_all[j:j + 1, :]
                st = st_ref[s_idx]
                w = bc * jnp.exp(jnp.where(mask_t, cum + negc_all[:, j:j + 1], -1e30))
                yts.append(jnp.dot(xd, w, preferred_element_type=F32) + _dot_nt(st, cm) * jnp.exp(cum))
                cl = cum[:, last:last + 1]
                st_ref[s_idx] = st * jnp.exp(cl) + jnp.dot(xd * jnp.exp(cl - cum), bm, preferred_element_type=F32)
        y_ref[...] = jnp.concatenate(yts, axis=0).T


def ssd_scan(xbc, p, gates_t, biasc, biasr, alogc, alogr):
    def rows(chunk_fn):
        return lambda b, i: b * CH_PER_B + chunk_fn(i)

    def stream_specs(chunk_fn):
        r = rows(chunk_fn)
        return [pl.BlockSpec((CHUNK, 512), lambda b, i: (r(b, i), 0)),
                pl.BlockSpec((CHUNK, 256), lambda b, i: (r(b, i), 2)),
                pl.BlockSpec((CHUNK, 256), lambda b, i: (r(b, i), 3)),
                pl.BlockSpec((CHUNK, 128), lambda b, i: (r(b, i), C_GATE // 128)),
                pl.BlockSpec((1, G_USED, CHUNK), lambda b, i: (r(b, i), 0, 0))]

    rf, rb = rows(_fwd_chunk), rows(_bwd_chunk)
    const = lambda shape: pl.BlockSpec(shape, lambda b, i: (0, 0))
    return pl.pallas_call(
        _ssd_kernel,
        grid=(NB, CH_PER_B),
        in_specs=stream_specs(_fwd_chunk) + stream_specs(_bwd_chunk) + [
            const((1, 128)), const((G_USED, 1)), const((1, 128)), const((G_USED, 1))],
        out_specs=[pl.BlockSpec((CHUNK, 512), lambda b, i: (rf(b, i), 0)),
                   pl.BlockSpec((CHUNK, 512), lambda b, i: (rb(b, i), 0))],
        out_shape=[jax.ShapeDtypeStruct((N_ROWS, 512), F32)] * 2,
        scratch_shapes=[pltpu.VMEM((2 * SSD_HEADS, SSD_HD, SSD_STATE), F32)],
        compiler_params=_cparams(("arbitrary", "arbitrary")),
        name="ssd_scan",
    )(xbc, xbc, xbc, p, gates_t, xbc, xbc, xbc, p, gates_t, biasc, biasr, alogc, alogr)


def _ssd_finish_kernel(yf_ref, yb_ref, xs_ref, z_ref, dsk_ref, g_ref, out_ref):
    y = yf_ref[...] + yb_ref[...] + dsk_ref[...] * xs_ref[...]
    y = y * _silu(z_ref[...])
    out_ref[...] = (y * lax.rsqrt(jnp.mean(y * y, axis=-1, keepdims=True) + EPS) * g_ref[...]).astype(BF16)


def ssd_finish(yf, yb, xbc, p, d_skip, norm_g):
    blk = lambda c: pl.BlockSpec((ELT_ROWS, 512), lambda i: (i, c))
    vec = pl.BlockSpec((1, 512), lambda i: (0, 0))
    return pl.pallas_call(
        _ssd_finish_kernel,
        grid=(N_ROWS // ELT_ROWS,),
        in_specs=[blk(0), blk(0), blk(0), blk(C_Z // 512), vec, vec],
        out_specs=blk(0),
        out_shape=jax.ShapeDtypeStruct((N_ROWS, 512), BF16),
        compiler_params=_cparams(("arbitrary",)),
        name="ssd_finish",
    )(yf, yb, xbc, p, jnp.repeat(d_skip, SSD_HD).reshape(1, 512), norm_g.reshape(1, 512))


def _rope_tile(r, cs_ref, s1_ref, s2_ref):
    return (r * cs_ref[...] + pltpu.roll(r, 128 - MLA_ROPE // 2, 1) * s1_ref[...]
            + pltpu.roll(r, MLA_ROPE // 2, 1) * s2_ref[...])


def _mla_prep_kernel(cq_ref, ckv_ref, kr_ref, wq_ref, wkv_ref, qn_ref, kvn_ref, gq_ref, gk_ref,
                     cs_ref, s1_ref, s2_ref, q_out, k_out, v_out):
    cq = cq_ref[...]
    cqn = cq * lax.rsqrt(jnp.sum(cq * cq, axis=-1, keepdims=True) / MLA_Q_RANK + EPS) * qn_ref[...]
    q_raw = jnp.dot(cqn, wq_ref[...], preferred_element_type=F32)
    ckv = ckv_ref[...]
    ckvn = ckv * lax.rsqrt(jnp.sum(ckv * ckv, axis=-1, keepdims=True) / MLA_KV_RANK + EPS) * kvn_ref[...]
    kv_raw = jnp.dot(ckvn, wkv_ref[...], preferred_element_type=F32)
    kr = kr_ref[...]
    kr_ss = jnp.sum(kr * kr, axis=-1, keepdims=True)
    for h in range(MLA_HEADS):
        base = h * MLA_HP
        qa = q_raw[:, base:base + 128]
        qb = q_raw[:, base + 128:base + 256]
        q_scale = lax.rsqrt((jnp.sum(qa * qa, axis=-1, keepdims=True)
                             + jnp.sum(qb * qb, axis=-1, keepdims=True)) / MLA_QK + EPS) * (MLA_QK ** -0.5 * LOG2E)
        q_out[:, base:base + 128] = (qa * q_scale * gq_ref[:, 0:128]).astype(BF16)
        q_out[:, base + 128:base + 256] = _rope_tile(qb * q_scale * gq_ref[:, 128:256],
                                                     cs_ref, s1_ref, s2_ref).astype(BF16)
        kn = kv_raw[:, base:base + 128]
        k_scale = lax.rsqrt((jnp.sum(kn * kn, axis=-1, keepdims=True) + kr_ss) / MLA_QK + EPS)
        k_out[:, base:base + 128] = (kn * k_scale * gk_ref[:, 0:128]).astype(BF16)
        k_out[:, base + 128:base + 256] = _rope_tile(kr * k_scale * gk_ref[:, 128:256],
                                                     cs_ref, s1_ref, s2_ref).astype(BF16)
        v_out[:, h * MLA_V:(h + 1) * MLA_V] = kv_raw[:, base + 128:base + 256].astype(BF16)


MLA_PREP_ROWS = 768


def mla_prep(p, wq, wkv, qn, kvn, gq, gk, rope_tabs):
    const = lambda shape: pl.BlockSpec(shape, lambda i: (0, 0))
    tab = pl.BlockSpec((MLA_PREP_ROWS, 128), lambda i: (i % (S_ALL // MLA_PREP_ROWS), 0))
    row = lambda w: pl.BlockSpec((MLA_PREP_ROWS, w), lambda i: (i, 0))
    return pl.pallas_call(
        _mla_prep_kernel,
        grid=(N_ROWS // MLA_PREP_ROWS,),
        in_specs=[pl.BlockSpec((MLA_PREP_ROWS, 512), lambda i: (i, C_CQ // 512)),
                  pl.BlockSpec((MLA_PREP_ROWS, 256), lambda i: (i, C_CKV // 256)),
                  pl.BlockSpec((MLA_PREP_ROWS, 128), lambda i: (i, C_KR // 128)),
                  const((512, MLA_HEADS * MLA_HP)), const((256, MLA_HEADS * MLA_HP)),
                  const((1, 512)), const((1, 256)), const((1, MLA_HP)), const((1, MLA_HP)),
                  tab, tab, tab],
        out_specs=[row(MLA_HEADS * MLA_HP), row(MLA_HEADS * MLA_HP), row(MLA_HEADS * MLA_V)],
        out_shape=[jax.ShapeDtypeStruct((N_ROWS, MLA_HEADS * MLA_HP), BF16),
                   jax.ShapeDtypeStruct((N_ROWS, MLA_HEADS * MLA_HP), BF16),
                   jax.ShapeDtypeStruct((N_ROWS, MLA_HEADS * MLA_V), BF16)],
        compiler_params=_cparams(("arbitrary",)),
        name="mla_prep",
    )(p, p, p, wq, wkv, qn, kvn, gq, gk, *rope_tabs)


def _softmax_pv(s_list, v_list):
    m = s_list[0].max(axis=-1, keepdims=True)
    for s in s_list[1:]:
        m = jnp.maximum(m, s.max(axis=-1, keepdims=True))
    ps = [jnp.exp2(s - m) for s in s_list]
    l = sum(p.sum(axis=-1, keepdims=True) for p in ps)
    o = sum(jnp.dot(p.astype(BF16), v, preferred_element_type=F32) for p, v in zip(ps, v_list))
    return o / l


MLA_KCHUNK = 512


def _mla_attn_kernel(q_ref, k_ref, v_ref, o_ref):
    q = q_ref[...]

    @pl.when(pl.program_id(2) == 0)
    def _():
        o_ref[...] = _softmax_pv([_dot_nt(q, k_ref[0:T_CTX, :])], [v_ref[0:T_CTX, :]]).astype(BF16)

    @pl.when(pl.program_id(2) > 0)
    def _():
        chunks = [slice(c0, min(c0 + MLA_KCHUNK, S_ALL)) for c0 in range(0, S_ALL, MLA_KCHUNK)]
        o_ref[...] = _softmax_pv([_dot_nt(q, k_ref[c, :]) for c in chunks], [v_ref[c, :] for c in chunks]).astype(BF16)


def mla_attention(q, k, v):
    return pl.pallas_call(
        _mla_attn_kernel,
        grid=(NB, MLA_HEADS, BLK_PER_B),
        in_specs=[pl.BlockSpec((ROWBLK, MLA_HP), lambda b, h, i: (b * BLK_PER_B + i, h)),
                  pl.BlockSpec((S_ALL, MLA_HP), lambda b, h, i: (b, h)),
                  pl.BlockSpec((S_ALL, MLA_V), lambda b, h, i: (b, h))],
        out_specs=pl.BlockSpec((ROWBLK, MLA_V), lambda b, h, i: (b * BLK_PER_B + i, h)),
        out_shape=jax.ShapeDtypeStruct((N_ROWS, MLA_HEADS * MLA_V), BF16),
        compiler_params=_cparams(("arbitrary", "arbitrary", "arbitrary")),
        name="mla_attention",
    )(q, k, v)


N_GRID_ROWS = T_LAT // GRID_W
NA_QROWS = ROWBLK // GRID_W
NA_GROUPS = N_GRID_ROWS // NA_QROWS
NA_WIN_ROWS = NA_KH + NA_QROWS - 1
NA_WIN = NA_WIN_ROWS * GRID_W


def _na_win_row(g):
    return jnp.clip(g * NA_QROWS - NA_KH // 2, 0, N_GRID_ROWS - NA_WIN_ROWS)


def _na_kernel(q_ref, k_ref, v_ref, gq_ref, gk_ref, bias_ref, o_ref, kn_ref, vb_ref):
    st = pl.program_id(2)

    @pl.when(st == 0)
    def _():
        k = k_ref[...]
        kn_ref[...] = (k * lax.rsqrt(jnp.mean(k * k, axis=-1, keepdims=True) + EPS) * gk_ref[...]).astype(BF16)
        vb_ref[...] = v_ref[...].astype(BF16)

    q = q_ref[...]
    q_scale = lax.rsqrt(jnp.mean(q * q, axis=-1, keepdims=True) + EPS) * (NA_HD ** -0.5 * LOG2E)
    qn = (q * q_scale * gq_ref[...]).astype(BF16)
    s_cx = _dot_nt(qn, kn_ref[0:T_CTX, :])
    v_cx = vb_ref[0:T_CTX, :]

    @pl.when(st == 0)
    def _():
        o_ref[...] = _softmax_pv([s_cx], [v_cx]).astype(BF16)

    @pl.when(st > 0)
    def _():
        start = pl.multiple_of(T_CTX + _na_win_row(st - 1) * GRID_W, GRID_W)
        s_nb = _dot_nt(qn, kn_ref[pl.ds(start, NA_WIN), :]) + bias_ref[0, 0]
        o_ref[...] = _softmax_pv([s_nb, s_cx], [vb_ref[pl.ds(start, NA_WIN), :], v_cx]).astype(BF16)


def _na_class(g):
    return jnp.where(g == 0, 0, jnp.where(g == NA_GROUPS - 1, 2, 1))


def na_attention(p, gq, gk, bias_tab):
    return pl.pallas_call(
        _na_kernel,
        grid=(NB, NA_HEADS, BLK_PER_B),
        in_specs=[pl.BlockSpec((ROWBLK, NA_HD), lambda b, h, st: (b * BLK_PER_B + st, C_NAQ // NA_HD + h)),
                  pl.BlockSpec((S_ALL, NA_HD), lambda b, h, st: (b, C_NAK // NA_HD + h)),
                  pl.BlockSpec((S_ALL, NA_HD), lambda b, h, st: (b, C_NAV // NA_HD + h)),
                  pl.BlockSpec((1, NA_HD), lambda b, h, st: (0, 0)),
                  pl.BlockSpec((1, NA_HD), lambda b, h, st: (0, 0)),
                  pl.BlockSpec((1, 1, ROWBLK, NA_WIN),
                               lambda b, h, st: (h, _na_class(jnp.maximum(st - 1, 0)), 0, 0))],
        out_specs=pl.BlockSpec((ROWBLK, NA_HD), lambda b, h, st: (b * BLK_PER_B + st, h)),
        out_shape=jax.ShapeDtypeStruct((N_ROWS, NA_HEADS * NA_HD), BF16),
        scratch_shapes=[pltpu.VMEM((S_ALL, NA_HD), BF16), pltpu.VMEM((S_ALL, NA_HD), BF16)],
        compiler_params=_cparams(("arbitrary", "arbitrary", "arbitrary")),
        name="na_attention",
    )(p, p, p, gq.reshape(1, NA_HD), gk.reshape(1, NA_HD), bias_tab)


def _na_window_tables():
    g_rep = np.array([0, 1, NA_GROUPS - 1])
    r = g_rep[:, None] * NA_QROWS + np.arange(NA_QROWS)[None, :]
    r0 = np.clip(r - NA_KH // 2, 0, N_GRID_ROWS - NA_KH)
    w0 = np.clip(g_rep * NA_QROWS - NA_KH // 2, 0, N_GRID_ROWS - NA_WIN_ROWS)
    kr = w0[:, None] + np.arange(NA_WIN_ROWS)[None, :]
    valid_r = (kr[:, None, :] >= r0[:, :, None]) & (kr[:, None, :] < r0[:, :, None] + NA_KH)
    dr = kr[:, None, :] - r[:, :, None] + NA_KH - 1
    return dr, valid_r


def _na_bias_kernel(toe_ref, o_ref):
    dr, valid_r = _na_window_tables()
    masked = jnp.full((GRID_W, GRID_W), -1e30, F32)
    for cls in range(dr.shape[0]):
        @pl.when(pl.program_id(1) == cls)
        def _(cls=cls):
            for a in range(NA_QROWS):
                for jw in range(NA_WIN_ROWS):
                    blk = toe_ref[0, int(dr[cls, a, jw])] if valid_r[cls, a, jw] else masked
                    o_ref[0, 0, a * GRID_W:(a + 1) * GRID_W, jw * GRID_W:(jw + 1) * GRID_W] = blk


def na_bias_table(rpb):
    c = np.arange(GRID_W)[:, None]
    kc = np.arange(GRID_W)[None, :]
    c0 = np.clip(c - NA_KW // 2, 0, GRID_W - NA_KW)
    valid_c = (kc >= c0) & (kc < c0 + NA_KW)
    dc = kc - c + NA_KW - 1
    onehot = (np.arange(2 * NA_KW - 1)[:, None, None] == dc[None]) & valid_c[None]
    toe = jnp.einsum('hrd,dck->hrck', rpb, jnp.asarray(onehot, F32), precision=HI)
    toe = jnp.where(valid_c[None, None], toe * LOG2E, -1e30)
    n_rows = 2 * NA_KH - 1
    return pl.pallas_call(
        _na_bias_kernel,
        grid=(NA_HEADS, 3),
        in_specs=[pl.BlockSpec((1, n_rows, GRID_W, GRID_W), lambda h, s: (h, 0, 0, 0))],
        out_specs=pl.BlockSpec((1, 1, ROWBLK, NA_WIN), lambda h, s: (h, s, 0, 0)),
        out_shape=jax.ShapeDtypeStruct((NA_HEADS, 3, ROWBLK, NA_WIN), F32),
        compiler_params=_cparams(("arbitrary", "arbitrary")),
        name="na_bias",
    )(toe)


N_TOK = NB * T_LAT
MOE_TM = 512
N_BLK = N_TOK * TOP_K // MOE_TM + N_EXPERTS
N_SLOT = N_BLK * MOE_TM
GATHER_ROWS = 512
COMBINE_TOK = 256


def _row_copies(n_rows, copy_fn):
    def start_all():
        def body(r8, carry):
            for u in range(8):
                copy_fn(r8 * 8 + u).start(priority=u % 2)
            return carry
        lax.fori_loop(0, n_rows // 8, body, 0)

    def wait_all():
        def body(r8, carry):
            for u in range(8):
                copy_fn(r8 * 8 + u).wait()
            return carry
        lax.fori_loop(0, n_rows // 8, body, 0)

    return start_all, wait_all


def _gather_kernel(idx_ref, fill_ref, nu_ref, h_hbm, o_ref, buf_ref, sem):
    i = pl.program_id(0)
    n_steps = nu_ref[0]

    def copies(step):
        slot = step % 2
        n_rows = (fill_ref[step] + 7) // 8 * 8
        return _row_copies(n_rows, lambda r: pltpu.make_async_copy(
            h_hbm.at[pl.ds(idx_ref[step * MOE_TM + r], 1)], buf_ref.at[slot, pl.ds(r, 1)], sem.at[slot]))

    @pl.when(i == 0)
    def _():
        buf_ref[...] = jnp.zeros_like(buf_ref)
        copies(i)[0]()

    @pl.when(i + 1 < n_steps)
    def _():
        copies(i + 1)[0]()

    @pl.when(i < n_steps)
    def _():
        copies(i)[1]()
        o_ref[...] = buf_ref[i % 2].astype(BF16)

    @pl.when(i >= n_steps)
    def _():
        o_ref[...] = jnp.zeros_like(o_ref)


def moe_gather(h2, slot_row, fill, n_used):
    return pl.pallas_call(
        _gather_kernel,
        grid_spec=pltpu.PrefetchScalarGridSpec(
            num_scalar_prefetch=3,
            grid=(N_BLK,),
            in_specs=[pl.BlockSpec(memory_space=pl.ANY)],
            out_specs=pl.BlockSpec((MOE_TM, D), lambda i, idx, fill, nu: (i, 0)),
            scratch_shapes=[pltpu.VMEM((2, MOE_TM, D), F32), pltpu.SemaphoreType.DMA((2,))]),
        out_shape=jax.ShapeDtypeStruct((N_SLOT, D), BF16),
        compiler_params=_cparams(("arbitrary",)),
        name="moe_gather",
    )(slot_row, fill, n_used, h2)


def _combine_kernel(slot_ref, y_hbm, x_ref, gate_ref, g2_ref, o_ref, buf_ref, sem):
    i = pl.program_id(0)

    def copies(step):
        slot = step % 2
        base = step * COMBINE_TOK * TOP_K
        return _row_copies(COMBINE_TOK * TOP_K, lambda r: pltpu.make_async_copy(
            y_hbm.at[pl.ds(slot_ref[base + r], 1)], buf_ref.at[slot, pl.ds(r, 1)], sem.at[slot]))

    @pl.when(i == 0)
    def _():
        copies(i)[0]()

    @pl.when(i + 1 < N_TOK // COMBINE_TOK)
    def _():
        copies(i + 1)[0]()

    copies(i)[1]()
    b = i // (T_LAT // COMBINE_TOK)
    rows = buf_ref[i % 2]
    f = gate_ref[:, 0:1] * rows[0:COMBINE_TOK] + gate_ref[:, 1:2] * rows[COMBINE_TOK:2 * COMBINE_TOK]
    o_ref[...] = x_ref[...] + g2_ref[pl.ds(b, 1), :] * f


def moe_combine(yb, x, slots_km, gates, modtab):
    per_b = T_LAT // COMBINE_TOK
    x_map = lambda i, s: ((i // per_b) * (S_ALL // COMBINE_TOK) + T_CTX // COMBINE_TOK + i % per_b, 0)
    return pl.pallas_call(
        _combine_kernel,
        grid_spec=pltpu.PrefetchScalarGridSpec(
            num_scalar_prefetch=1,
            grid=(N_TOK // COMBINE_TOK,),
            in_specs=[pl.BlockSpec(memory_space=pl.ANY),
                      pl.BlockSpec((COMBINE_TOK, D), x_map),
                      pl.BlockSpec((COMBINE_TOK, TOP_K), lambda i, s: (i, 0)),
                      pl.BlockSpec((8, D), lambda i, s: (0, 5))],
            out_specs=pl.BlockSpec((COMBINE_TOK, D), lambda i, s: (i, 0)),
            scratch_shapes=[pltpu.VMEM((2, COMBINE_TOK * TOP_K, D), F32), pltpu.SemaphoreType.DMA((2,))]),
        out_shape=jax.ShapeDtypeStruct((N_TOK, D), F32),
        compiler_params=_cparams(("arbitrary",)),
        name="moe_combine",
    )(slots_km, yb, x, gates, modtab)


def moe_routing(logits):
    top_v, top_e = lax.top_k(logits, TOP_K)
    gates = jax.nn.softmax(top_v, axis=-1)
    flat_e = top_e.reshape(-1)
    onehot = (flat_e[:, None] == jnp.arange(N_EXPERTS)[None, :]).astype(jnp.int32)
    rank = jnp.take_along_axis(jnp.cumsum(onehot, axis=0) - onehot, flat_e[:, None], axis=1)[:, 0]
    counts = onehot.sum(axis=0)
    padded = (counts + MOE_TM - 1) // MOE_TM * MOE_TM
    pend = jnp.cumsum(padded)
    slot = ((pend - padded)[flat_e] + rank).astype(jnp.int32)
    tok = jnp.arange(N_TOK, dtype=jnp.int32)
    tok_row = (tok // T_LAT) * S_ALL + T_CTX + tok % T_LAT
    slot_row = jnp.zeros((N_SLOT,), jnp.int32).at[slot].set(jnp.repeat(tok_row, TOP_K))
    n_used = (pend[-1] // MOE_TM).astype(jnp.int32)
    blk = jnp.arange(N_BLK)
    used = blk < n_used
    blk_e = jnp.sum(jnp.minimum(blk, n_used - 1)[:, None] * MOE_TM >= pend[None, :], axis=1).astype(jnp.int32)
    fresh = (used & ((blk == 0) | (blk_e != jnp.roll(blk_e, 1)))).astype(jnp.int32)
    run_idx = jnp.cumsum(fresh) - 1
    n_runs = fresh.sum()
    has = counts > 0
    ids = jnp.arange(N_EXPERTS)
    first_e = jnp.min(jnp.where(has, ids, N_EXPERTS))
    later = jnp.where(has[None, :] & (ids[None, :] > ids[:, None]), ids[None, :], N_EXPERTS).min(axis=1)
    next_of = jnp.where(later < N_EXPERTS, later, first_e)
    sched = (blk_e, fresh, run_idx.astype(jnp.int32), next_of[blk_e].astype(jnp.int32),
             (run_idx == n_runs - 1).astype(jnp.int32),
             jnp.where(used, jnp.clip(((pend - padded) + counts)[blk_e] - blk * MOE_TM, 0, MOE_TM), 0).astype(jnp.int32),
             jnp.stack([n_used, n_runs]).astype(jnp.int32))
    slots_km = slot.reshape(N_TOK // COMBINE_TOK, COMBINE_TOK, TOP_K).transpose(0, 2, 1).reshape(-1)
    return slot_row, sched, n_used.reshape(1), slots_km, gates


def _pad_cols(w, width):
    return jnp.pad(w, [(0, 0)] * (w.ndim - 1) + [(0, width - w.shape[-1])])


W_IN_SEGMENTS = ((C_Q, 0, 2048), (C_Z, 2064, 512), (C_XBC, 2576, 1024), (C_CQ, 3616, MLA_Q_RANK),
                 (C_CKV, 4064, MLA_KV_RANK), (C_KR, 4224, MLA_ROPE), (C_NAQ, 4288, 1536),
                 (C_GATE + G_I, 2048, 16), (C_GATE + G_DT, 3600, 16))
D_IN = 5824


def _w_in_layout_kernel(w_ref, o_ref):
    o_ref[...] = jnp.zeros_like(o_ref)
    for dst, src, width in W_IN_SEGMENTS:
        o_ref[0, :, dst:dst + width] = w_ref[0, :, src:src + width]


def layout_w_in(w):
    rows = 256
    return pl.pallas_call(
        _w_in_layout_kernel,
        grid=(DEPTH, D // rows),
        in_specs=[pl.BlockSpec((1, rows, D_IN), lambda l, i: (l, i, 0))],
        out_specs=pl.BlockSpec((1, rows, D_INP), lambda l, i: (l, i, 0)),
        out_shape=jax.ShapeDtypeStruct((DEPTH, D, D_INP), F32),
        compiler_params=_cparams(("arbitrary", "arbitrary")),
        name="w_in_layout",
    )(w)


def rope_tables():
    t = np.arange(T_LAT)
    n_freq = MLA_ROPE // 4
    freqs = ROPE_THETA ** (-jnp.arange(n_freq, dtype=F32) / n_freq)
    row = jnp.asarray(t // GRID_W, F32)
    col = jnp.asarray(t % GRID_W, F32)
    ang = jnp.concatenate([row[:, None] * freqs, col[:, None] * freqs], axis=-1)
    cos, sin = jnp.cos(ang), jnp.sin(ang)
    half = MLA_ROPE // 2
    zc = jnp.zeros((T_LAT, 128 - MLA_ROPE), F32)
    zh = jnp.zeros((T_LAT, half), F32)
    cs = jnp.concatenate([cos, cos, zc], axis=1)
    s1 = jnp.concatenate([-sin, zh, zc], axis=1)
    s2 = jnp.concatenate([zh, sin, zc], axis=1)
    ident = jnp.concatenate([jnp.ones((T_CTX, MLA_ROPE), F32), jnp.zeros((T_CTX, 128 - MLA_ROPE), F32)], axis=1)
    zeros = jnp.zeros((T_CTX, 128), F32)
    return (jnp.concatenate([ident, cs], axis=0), jnp.concatenate([zeros, s1], axis=0),
            jnp.concatenate([zeros, s2], axis=0))


def _gate_vectors(i_bias, f_bias, dt_bias, a_log):
    used = jnp.concatenate([i_bias.reshape(-1), f_bias.reshape(-1), dt_bias.reshape(-1)])
    alog = jnp.concatenate([jnp.zeros((G_DT,), F32), a_log.reshape(-1)])
    padc = lambda u: jnp.pad(u, (0, 128 - G_USED)).reshape(1, 128)
    return padc(used), used.reshape(G_USED, 1), padc(alog), alog.reshape(G_USED, 1)


def kernel(x, c, ctx, c_ctx, mod_w, mod_b, norm1, w_in, w_out, ml_i_bias, ml_f_bias, ml_norm, ssd_conv_w, ssd_conv_b, ssd_dt_bias, ssd_A_log, ssd_D, ssd_norm, mla_q_norm, mla_w_qb, mla_kv_norm, mla_w_kvb, mla_gq, mla_gk, na_gq, na_gk, na_rpb, norm2, ffn_w1, ffn_w3, ffn_w2, moe_router, moe_w1, moe_w3, moe_w2):
    xs = jnp.concatenate([ctx, x], axis=1).reshape(N_ROWS, D)
    c_all = jnp.concatenate([c, c_ctx[None, :], jnp.zeros((8 - NB - 1, D), F32)], axis=0)
    mod_all = modulation(c_all, mod_w, mod_b)
    tabs = rope_tables()
    w_in_p = layout_w_in(w_in)
    out = None
    for l in range(DEPTH):
        modtab = mod_all[l]
        h = rms_modulate(xs, norm1[l], modtab, 0)
        p = matmul(h, [w_in_p], tm=1536, tn=1024, layer=l, name="w_in")
        gates_t = p[:, C_GATE:C_GATE + G_USED].reshape(N_ROWS // CHUNK, CHUNK, G_USED).transpose(0, 2, 1)
        biasc, biasr, alogc, alogr = _gate_vectors(ml_i_bias[l], ml_f_bias[l], ssd_dt_bias[l], ssd_A_log[l])

        hf, hb = mlstm_scan(p, gates_t, biasc, biasr)
        ml = mlstm_finish(hf, hb, p, ml_norm[l])

        xbc = ssd_conv(p, ssd_conv_w[l], ssd_conv_b[l])
        yf, yb = ssd_scan(xbc, p, gates_t, biasc, biasr, alogc, alogr)
        ss = ssd_finish(yf, yb, xbc, p, ssd_D[l], ssd_norm[l])

        wq = jnp.pad(mla_w_qb[l].reshape(MLA_Q_RANK, MLA_HEADS, MLA_QK),
                     ((0, 512 - MLA_Q_RANK), (0, 0), (0, MLA_HP - MLA_QK))).reshape(512, MLA_HEADS * MLA_HP)
        wkv = jnp.pad(mla_w_kvb[l], ((0, 256 - MLA_KV_RANK), (0, 0)))
        pad1 = lambda u, w: jnp.pad(u, (0, w - u.shape[0])).reshape(1, w)
        q, k, v = mla_prep(p, wq, wkv, pad1(mla_q_norm[l], 512), pad1(mla_kv_norm[l], 256),
                           pad1(mla_gq[l], MLA_HP), pad1(mla_gk[l], MLA_HP), tabs)
        la = mla_attention(q, k, v)

        na = na_attention(p, na_gq[l], na_gk[l], na_bias_table(na_rpb[l]))

        xs = matmul([ml, ss, la, na], [w_out], mode="resid", res=xs, modtab=modtab, gate_col=2,
                    tm=1536, layer=l, name="w_out")

        if l % 2 == 0:
            h2 = rms_modulate(xs, norm2[l], modtab, 3)
            hid = matmul(h2, [ffn_w1[l // 2], ffn_w3[l // 2]], mode="swiglu", out_dtype=BF16, tm=1536,
                         name="ffn_up")
            xs = matmul(hid, [ffn_w2[l // 2]], mode="resid", res=xs, modtab=modtab, gate_col=5,
                        w_buffers=1, name="ffn_down")
        else:
            h2, logits = rms_modulate(xs, norm2[l], modtab, 3, router=_pad_cols(moe_router[l // 2], 128))
            lat = logits.reshape(NB, S_ALL, 128)[:, T_CTX:, :N_EXPERTS].reshape(N_TOK, N_EXPERTS)
            slot_row, sched, n_used, slots_km, gates = moe_routing(lat)
            xb = moe_gather(h2, slot_row, sched[5], n_used)
            hid = grouped_matmul(xb, [moe_w1[l // 2], moe_w3[l // 2]], sched, mode="swiglu", out_dtype=BF16,
                                 tm=MOE_TM, tn=1024, name="moe_up")
            yb_ = grouped_matmul(hid, [moe_w2[l // 2]], sched, tm=MOE_TM, tn=512, name="moe_down")
            out = moe_combine(yb_, xs, slots_km, gates, modtab)
    return out.reshape(NB, T_LAT, D)
```
